```python
import jax, jax.numpy as jnp
from jax import lax
import numpy as np

D_MODEL = 1024
BATCH = 8
SEQ = 4096
DEPTH = 1

GRID_W = 64
CTX_LEN = 256
EPS = 1e-6
ROPE_THETA = 10000.0
Q_BLOCK = 128

GQA_HEADS = 8
GQA_KV_HEADS = 2
GQA_GROUP = GQA_HEADS // GQA_KV_HEADS
GQA_HEAD_DIM = 128
GQA_Q_W = GQA_HEADS * GQA_HEAD_DIM
GQA_KV_W = GQA_KV_HEADS * GQA_HEAD_DIM

MLA_HEADS = 8
MLA_Q_RANK = 256
MLA_KV_RANK = 128
MLA_NOPE_DIM = 128
MLA_ROPE_DIM = 64
MLA_V_DIM = 128
MLA_QK_DIM = MLA_NOPE_DIM + MLA_ROPE_DIM
MLA_KVA_W = MLA_KV_RANK + MLA_ROPE_DIM
MLA_V_W = MLA_HEADS * MLA_V_DIM

N_BRANCH = 2
OFF_GQA_K = GQA_Q_W
OFF_GQA_V = OFF_GQA_K + GQA_KV_W
OFF_MLA_QA = OFF_GQA_V + GQA_KV_W
OFF_MLA_KVA = OFF_MLA_QA + MLA_Q_RANK
OFF_GATE = OFF_MLA_KVA + MLA_KVA_W
IN_WIDTH = OFF_GATE + N_BRANCH * D_MODEL

N_EXPERTS = 32
TOP_K = 4
D_EXPERT = 1024
SWIGLU_LIMIT = 7.0
SWIGLU_ALPHA = 1.702
EXPERT_BLOCK = 256

kernel_name = 'hybrid_gqa_mla_moe_dit_layer'


def _rms_norm(x, g):
    xf = x.astype(jnp.float32)
    y = xf * lax.rsqrt(jnp.mean(xf * xf, axis=-1, keepdims=True) + EPS)
    return (y * g.astype(jnp.float32)).astype(x.dtype)


def _modulate(x, g, shift, scale):
    return _rms_norm(x, g) * (1 + scale) + shift


def _rope_1d(x, pos):
    d = x.shape[-1]
    inv_freq = ROPE_THETA ** (-jnp.arange(0, d, 2, dtype=jnp.float32) / d)
    ang = pos.astype(jnp.float32)[:, None] * inv_freq[None, :]
    cos = jnp.cos(ang)[None, :, None, :].astype(x.dtype)
    sin = jnp.sin(ang)[None, :, None, :].astype(x.dtype)
    x1, x2 = x[..., : d // 2], x[..., d // 2:]
    return jnp.concatenate([x1 * cos - x2 * sin, x1 * sin + x2 * cos], axis=-1)


def _axial_rope(x, row, col):
    half = x.shape[-1] // 2
    return jnp.concatenate([_rope_1d(x[..., :half], row), _rope_1d(x[..., half:], col)], axis=-1)


def _rope_tail(x, pos):
    return jnp.concatenate([x[..., :MLA_NOPE_DIM], _axial_rope(x[..., MLA_NOPE_DIM:], *pos)], axis=-1)


def _blocked_attention(q, k, v, scale):
    b, lq, kvh, g, dk = q.shape
    nb = lq // Q_BLOCK
    qb = jnp.moveaxis(q.reshape(b, nb, Q_BLOCK, kvh, g, dk), 1, 0)

    def one_block(q_blk):
        s = jnp.einsum('bqkgd,btkd->bkgqt', q_blk, k, preferred_element_type=jnp.float32) * scale
        p = jax.nn.softmax(s, axis=-1).astype(v.dtype)
        return jnp.einsum('bkgqt,btkv->bqkgv', p, v)

    o = lax.map(one_block, qb)
    return jnp.moveaxis(o, 0, 1).reshape(b, lq, kvh * g * v.shape[-1])


def _split_proj(proj):
    return (proj[..., :OFF_GQA_K], proj[..., OFF_GQA_K:OFF_GQA_V], proj[..., OFF_GQA_V:OFF_MLA_QA],
            proj[..., OFF_MLA_QA:OFF_MLA_KVA], proj[..., OFF_MLA_KVA:OFF_GATE], proj[..., OFF_GATE:])


def _gqa_q(pq, q_norm_g, pos):
    b, l, _ = pq.shape
    q = _rms_norm(pq.reshape(b, l, GQA_HEADS, GQA_HEAD_DIM), q_norm_g)
    if pos is not None:
        q = _axial_rope(q, *pos)
    return q.reshape(b, l, GQA_KV_HEADS, GQA_GROUP, GQA_HEAD_DIM)


def _gqa_kv(pk, pv, k_norm_g, pos):
    b, l, _ = pk.shape
    k = _rms_norm(pk.reshape(b, l, GQA_KV_HEADS, GQA_HEAD_DIM), k_norm_g)
    if pos is not None:
        k = _axial_rope(k, *pos)
    return k, pv.reshape(b, l, GQA_KV_HEADS, GQA_HEAD_DIM)


def _mla_q(pqa, p, pos):
    b, l, _ = pqa.shape
    q = (_rms_norm(pqa, p['mla_q_a_norm']) @ p['mla_w_qb']).reshape(b, l, MLA_HEADS, MLA_QK_DIM)
    q = _rms_norm(q, p['mla_q_norm'])
    if pos is not None:
        q = _rope_tail(q, pos)
    return q[:, :, :, None, :]


def _mla_kv(pkva, p, pos):
    b, l, _ = pkva.shape
    c_kv, k_rope = pkva[..., :MLA_KV_RANK], pkva[..., MLA_KV_RANK:]
    kv = (_rms_norm(c_kv, p['mla_kv_a_norm']) @ p['mla_w_kvb']).reshape(b, l, MLA_HEADS, MLA_NOPE_DIM + MLA_V_DIM)
    k_nope, v = kv[..., :MLA_NOPE_DIM], kv[..., MLA_NOPE_DIM:]
    k_rope = jnp.broadcast_to(k_rope[:, :, None, :], (b, l, MLA_HEADS, MLA_ROPE_DIM))
    k = _rms_norm(jnp.concatenate([k_nope, k_rope], axis=-1), p['mla_k_norm'])
    if pos is not None:
        k = _rope_tail(k, pos)
    return k, v


def _merge(o_a, o_b, gate_logits, p):
    g = jax.nn.sigmoid(gate_logits)
    y = g[..., :D_MODEL] * (o_a @ p['w_o_gqa']) + g[..., D_MODEL:] * (o_b @ p['w_o_mla'])
    return y @ p['w_out']


def _mixer(h, hc, p, pos, with_ctx_queries):
    q_a, k_a, v_a, qa_b, kva_b, gate_l = _split_proj(h @ p['w_in'])
    qc_a, kc_a_raw, vc_a_raw, qac_b, kvac_b, gate_c = _split_proj(hc @ p['w_in'])
    kc_a, vc_a = _gqa_kv(kc_a_raw, vc_a_raw, p['gqa_k_norm'], None)
    kc_b, vc_b = _mla_kv(kvac_b, p, None)
    kl_a, vl_a = _gqa_kv(k_a, v_a, p['gqa_k_norm'], pos)
    kl_b, vl_b = _mla_kv(kva_b, p, pos)
    o_a = _blocked_attention(_gqa_q(q_a, p['gqa_q_norm'], pos),
                             jnp.concatenate([kc_a, kl_a], axis=1), jnp.concatenate([vc_a, vl_a], axis=1),
                             GQA_HEAD_DIM ** -0.5)
    o_b = _blocked_attention(_mla_q(qa_b, p, pos),
                             jnp.concatenate([kc_b, kl_b], axis=1), jnp.concatenate([vc_b, vl_b], axis=1),
                             MLA_QK_DIM ** -0.5)
    y = _merge(o_a, o_b, gate_l, p)
    if with_ctx_queries:
        oc_a = _blocked_attention(_gqa_q(qc_a, p['gqa_q_norm'], None), kc_a, vc_a, GQA_HEAD_DIM ** -0.5)
        oc_b = _blocked_attention(_mla_q(qac_b, p, None), kc_b, vc_b, MLA_QK_DIM ** -0.5)
        return y, _merge(oc_a, oc_b, gate_c, p)
    return y, None


def _moe(h, p):
    b, l, d = h.shape
    t = h.reshape(-1, d)
    n_tok = t.shape[0]
    logits = (t @ p['router_w'] + p['router_b']).astype(jnp.float32)
    top_val, top_idx = lax.top_k(logits, TOP_K)
    gates = jax.nn.softmax(top_val, axis=-1)
    n_assign = n_tok * TOP_K
    e = top_idx.reshape(-1)
    tok = jnp.repeat(jnp.arange(n_tok, dtype=jnp.int32), TOP_K)
    order = jnp.argsort(e)
    e_s, tok_s, g_s = e[order], tok[order], gates.reshape(-1)[order]
    counts = jnp.bincount(e, length=N_EXPERTS)
    starts = jnp.cumsum(counts) - counts
    padded = (counts + EXPERT_BLOCK - 1) // EXPERT_BLOCK * EXPERT_BLOCK
    pad_starts = jnp.cumsum(padded) - padded
    pad_ends = pad_starts + padded
    dest = pad_starts[e_s] + (jnp.arange(n_assign, dtype=jnp.int32) - starts[e_s])
    n_blocks = -(-n_assign // EXPERT_BLOCK) + N_EXPERTS
    buf = jnp.zeros((n_blocks * EXPERT_BLOCK, d), h.dtype).at[dest].set(t[tok_s])
    block_e = jnp.minimum(jnp.searchsorted(pad_ends, jnp.arange(n_blocks) * EXPERT_BLOCK, side='right'),
                          N_EXPERTS - 1)

    def expert_block(args):
        xb, eid = args
        gu = xb @ p['expert_w1'][eid] + p['expert_b1'][eid]
        x_glu, x_lin = gu[..., :D_EXPERT], gu[..., D_EXPERT:]
        x_glu = jnp.minimum(x_glu, SWIGLU_LIMIT)
        x_lin = jnp.clip(x_lin, -SWIGLU_LIMIT, SWIGLU_LIMIT)
        act = x_glu * jax.nn.sigmoid(SWIGLU_ALPHA * x_glu) * (x_lin + 1)
        return act @ p['expert_w2'][eid] + p['expert_b2'][eid]

    y = lax.map(expert_block, (buf.reshape(n_blocks, EXPERT_BLOCK, d), block_e)).reshape(-1, d)
    y = y[dest] * g_s[:, None].astype(h.dtype)
    return jax.ops.segment_sum(y, tok_s, num_segments=n_tok).reshape(b, l, d)


def _layer(x, xc, c, c_ctx, p, pos, update_ctx):
    mod = (jax.nn.silu(c) @ p['ada_w'] + p['ada_b'])[:, None, :]
    mod_c = jax.nn.silu(c_ctx) @ p['ada_w'] + p['ada_b']
    sh1, sc1, g1, sh2, sc2, g2 = jnp.split(mod, 6, axis=-1)
    csh1, csc1, cg1, csh2, csc2, cg2 = jnp.split(mod_c, 6, axis=-1)
    h = _modulate(x, p['norm_mix'], sh1, sc1)
    hc = _modulate(xc, p['norm_mix'], csh1, csc1)
    y, yc = _mixer(h, hc, p, pos, update_ctx)
    x = x + g1 * y
    x = x + g2 * _moe(_modulate(x, p['norm_ffn'], sh2, sc2), p)
    if update_ctx:
        xc = xc + cg1 * yc
        xc = xc + cg2 * _moe(_modulate(xc, p['norm_ffn'], csh2, csc2), p)
    return x, xc


def setup_inputs(seed: int = 0) -> dict:
    key = jax.random.key(seed)
    ks = jax.random.split(key, 26)
    f32 = jnp.float32
    D = D_MODEL

    def nrm(k, shape, scale):
        return jax.random.normal(k, shape, f32) * scale

    def gain(k, shape):
        return 1.0 + 0.02 * jax.random.normal(k, shape, f32)

    return {
        'x': nrm(ks[0], (BATCH, SEQ, D), 1.0),
        'c': nrm(ks[1], (BATCH, D), 1.0),
        'ctx': nrm(ks[2], (BATCH, CTX_LEN, D), 1.0),
        'c_ctx': nrm(ks[3], (D,), 1.0),
        'ada_w': nrm(ks[4], (DEPTH, D, 6 * D), 0.5 * D ** -0.5),
        'ada_b': nrm(ks[5], (DEPTH, 6 * D), 0.02),
        'norm_mix': gain(ks[6], (DEPTH, D)),
        'norm_ffn': gain(ks[7], (DEPTH, D)),
        'w_in': nrm(ks[8], (DEPTH, D, IN_WIDTH), D ** -0.5),
        'gqa_q_norm': gain(ks[9], (DEPTH, GQA_HEAD_DIM)),
        'gqa_k_norm': gain(ks[10], (DEPTH, GQA_HEAD_DIM)),
        'mla_q_a_norm': gain(ks[11], (DEPTH, MLA_Q_RANK)),
        'mla_kv_a_norm': gain(ks[12], (DEPTH, MLA_KV_RANK)),
        'mla_w_qb': nrm(ks[13], (DEPTH, MLA_Q_RANK, MLA_HEADS * MLA_QK_DIM), MLA_Q_RANK ** -0.5),
        'mla_w_kvb': nrm(ks[14], (DEPTH, MLA_KV_RANK, MLA_HEADS * (MLA_NOPE_DIM + MLA_V_DIM)), MLA_KV_RANK ** -0.5),
        'mla_q_norm': gain(ks[15], (DEPTH, MLA_QK_DIM)),
        'mla_k_norm': gain(ks[16], (DEPTH, MLA_QK_DIM)),
        'w_o_gqa': nrm(ks[17], (DEPTH, GQA_Q_W, D), GQA_Q_W ** -0.5),
        'w_o_mla': nrm(ks[18], (DEPTH, MLA_V_W, D), MLA_V_W ** -0.5),
        'w_out': nrm(ks[19], (DEPTH, D, D), D ** -0.5),
        'router_w': nrm(ks[20], (DEPTH, D, N_EXPERTS), D ** -0.5),
        'router_b': nrm(ks[21], (DEPTH, N_EXPERTS), 0.01),
        'expert_w1': nrm(ks[22], (DEPTH, N_EXPERTS, D, 2 * D_EXPERT), D ** -0.5),
        'expert_b1': nrm(ks[23], (DEPTH, N_EXPERTS, 2 * D_EXPERT), 0.02),
        'expert_w2': nrm(ks[24], (DEPTH, N_EXPERTS, D_EXPERT, D), D_EXPERT ** -0.5),
        'expert_b2': nrm(ks[25], (DEPTH, N_EXPERTS, D), 0.02),
    }


def reference(x, c, ctx, c_ctx, ada_w, ada_b, norm_mix, norm_ffn, w_in, gqa_q_norm, gqa_k_norm,
              mla_q_a_norm, mla_kv_a_norm, mla_w_qb, mla_w_kvb, mla_q_norm, mla_k_norm,
              w_o_gqa, w_o_mla, w_out, router_w, router_b, expert_w1, expert_b1, expert_w2, expert_b2):
    seq_len = x.shape[1]
    n_rows = seq_len // GRID_W
    row = jnp.broadcast_to(jnp.arange(n_rows, dtype=jnp.int32)[:, None], (n_rows, GRID_W)).reshape(-1)
    col = jnp.broadcast_to(jnp.arange(GRID_W, dtype=jnp.int32)[None, :], (n_rows, GRID_W)).reshape(-1)
    pos = (row, col)
    xc = ctx
    for l in range(DEPTH):
        p = {
            'ada_w': ada_w[l], 'ada_b': ada_b[l], 'norm_mix': norm_mix[l], 'norm_ffn': norm_ffn[l],
            'w_in': w_in[l], 'gqa_q_norm': gqa_q_norm[l], 'gqa_k_norm': gqa_k_norm[l],
            'mla_q_a_norm': mla_q_a_norm[l], 'mla_kv_a_norm': mla_kv_a_norm[l],
            'mla_w_qb': mla_w_qb[l], 'mla_w_kvb': mla_w_kvb[l],
            'mla_q_norm': mla_q_norm[l], 'mla_k_norm': mla_k_norm[l],
            'w_o_gqa': w_o_gqa[l], 'w_o_mla': w_o_mla[l], 'w_out': w_out[l],
            'router_w': router_w[l], 'router_b': router_b[l],
            'expert_w1': expert_w1[l], 'expert_b1': expert_b1[l],
            'expert_w2': expert_w2[l], 'expert_b2': expert_b2[l],
        }
        x, xc = _layer(x, xc, c, c_ctx, p, pos, l < DEPTH - 1)
    return x
```

```python
import functools
import math

import jax
import jax.numpy as jnp
from jax import lax
from jax.experimental import pallas as pl
from jax.experimental.pallas import tpu as pltpu

D_MODEL = 1024
GRID_W = 64
EPS = 1e-6
ROPE_THETA = 10000.0

GQA_HEADS = 8
GQA_KV_HEADS = 2
GQA_GROUP = GQA_HEADS // GQA_KV_HEADS
GQA_HEAD_DIM = 128
GQA_Q_W = GQA_HEADS * GQA_HEAD_DIM
GQA_KV_W = GQA_KV_HEADS * GQA_HEAD_DIM

MLA_HEADS = 8
MLA_Q_RANK = 256
MLA_KV_RANK = 128
MLA_NOPE_DIM = 128
MLA_ROPE_DIM = 64
MLA_V_DIM = 128
MLA_QK_DIM = MLA_NOPE_DIM + MLA_ROPE_DIM
MLA_HEAD_PAD = 256
MLA_V_W = MLA_HEADS * MLA_V_DIM

OFF_GQA_K = GQA_Q_W
OFF_GQA_V = OFF_GQA_K + GQA_KV_W
OFF_MLA_QA = OFF_GQA_V + GQA_KV_W
OFF_MLA_KVA = OFF_MLA_QA + MLA_Q_RANK
OFF_GATE = OFF_MLA_KVA + MLA_KV_RANK + MLA_ROPE_DIM

N_EXPERTS = 32
TOP_K = 4
D_EXPERT = 1024
SWIGLU_LIMIT = 7.0
SWIGLU_ALPHA = 1.702
EXPERT_BLOCK = 256

LANES = 128
NEG_INF = float("-inf")

BF16 = jnp.bfloat16
F32 = jnp.float32


def _dot(a, b):
    return jnp.dot(a, b, preferred_element_type=F32)


def _dot_nt(a, b):
    return lax.dot_general(a, b, (((1,), (1,)), ((), ())), preferred_element_type=F32)


def _split_bf16(a):
    hi = a.astype(BF16)
    lo = (a - hi.astype(F32)).astype(BF16)
    return hi, lo


def _rms(x, g, n):
    ms = jnp.sum(x * x, axis=-1, keepdims=True) * (1.0 / n)
    return x * lax.rsqrt(ms + EPS) * g


def _swap_halves(x, k):
    lane = lax.broadcasted_iota(jnp.int32, x.shape, 1)
    return jnp.where((lane & k) != 0, pltpu.roll(x, k, 1), pltpu.roll(x, LANES - k, 1))


def _rope(x, cos, sin_signed, k):
    return x * cos + _swap_halves(x, k) * sin_signed


def _ada_kernel(c_ref, w_ref, b_ref, o_ref):
    c = c_ref[...]
    s = c * (1.0 / (1.0 + jnp.exp(-c)))
    s_hi, s_lo = _split_bf16(s)
    w_hi, w_lo = _split_bf16(w_ref[...])
    o_ref[...] = _dot(s_hi, w_hi) + (_dot(s_hi, w_lo) + _dot(s_lo, w_hi)) + b_ref[...]


def _ada_mod(cc, ada_w, ada_b):
    n = ada_w.shape[1]
    tn = 1024
    return pl.pallas_call(
        _ada_kernel,
        out_shape=jax.ShapeDtypeStruct((cc.shape[0], n), F32),
        grid=(n // tn,),
        in_specs=[
            pl.BlockSpec((cc.shape[0], D_MODEL), lambda j: (0, 0)),
            pl.BlockSpec((D_MODEL, tn), lambda j: (0, j)),
            pl.BlockSpec((1, tn), lambda j: (0, j)),
        ],
        out_specs=pl.BlockSpec((cc.shape[0], tn), lambda j: (0, j)),
        compiler_params=pltpu.CompilerParams(dimension_semantics=("parallel",)),
        name="ada_mod",
    )(cc, ada_w, ada_b.reshape(1, n))


def _proj_kernel(*refs, with_q):
    if with_q:
        (x_ref, mod_ref, nmix_ref, cg_ref, sg_ref, cm_ref, sm_ref,
         wq_ref, wk_ref, wv_ref, wqa_ref, wckv_ref, wkr_ref, wg_ref, wqb_ref, wkb_ref, wvb_ref,
         gq_ref, gk_ref, gqa_ref, gkva_ref, gmq_ref, gmk_ref,
         q_ref, k_ref, v_ref, qm_ref, km_ref, vm_ref, gate_ref) = refs
    else:
        (x_ref, mod_ref, nmix_ref, cg_ref, sg_ref, cm_ref, sm_ref,
         wk_ref, wv_ref, wckv_ref, wkr_ref, wkb_ref, wvb_ref,
         gk_ref, gkva_ref, gmk_ref,
         k_ref, v_ref, km_ref, vm_ref) = refs

    x = x_ref[0]
    mod = mod_ref[0]
    shift, scale = mod[0:1, :], mod[1:2, :]
    h = _rms(x, nmix_ref[...], D_MODEL) * (1.0 + scale) + shift
    hb = h.astype(BF16)
    cg, sg = cg_ref[...], sg_ref[...]
    cm, sm = cm_ref[...], sm_ref[...]

    kk = _dot(hb, wk_ref[...])
    for j in range(GQA_KV_HEADS):
        sl = slice(j * GQA_HEAD_DIM, (j + 1) * GQA_HEAD_DIM)
        kn = _rms(kk[:, sl], gk_ref[...], GQA_HEAD_DIM)
        k_ref[0, :, sl] = _rope(kn, cg, sg, 32).astype(BF16)
    v_ref[0] = _dot(hb, wv_ref[...]).astype(BF16)

    ckv = _rms(_dot(hb, wckv_ref[...]), gkva_ref[...], MLA_KV_RANK).astype(BF16)
    vm_ref[0] = _dot(ckv, wvb_ref[...]).astype(BF16)
    knope = _dot(ckv, wkb_ref[...])
    kr = _dot(hb, wkr_ref[...])
    gmk = gmk_ref[...]
    g_nope, g_rope = gmk[:, :MLA_NOPE_DIM], gmk[:, MLA_NOPE_DIM:]
    ssq_r = jnp.sum(kr * kr, axis=-1, keepdims=True)
    kr_roped = _rope(kr * g_rope, cm, sm, 16)
    for j in range(MLA_HEADS):
        kn = knope[:, j * MLA_NOPE_DIM:(j + 1) * MLA_NOPE_DIM]
        ms = (jnp.sum(kn * kn, axis=-1, keepdims=True) + ssq_r) * (1.0 / MLA_QK_DIM)
        r = lax.rsqrt(ms + EPS)
        base = j * MLA_HEAD_PAD
        km_ref[0, :, base:base + MLA_NOPE_DIM] = (kn * r * g_nope).astype(BF16)
        km_ref[0, :, base + MLA_NOPE_DIM:base + MLA_HEAD_PAD] = (kr_roped * r).astype(BF16)

    if not with_q:
        return

    qq = _dot(hb, wq_ref[...])
    q_scale = GQA_HEAD_DIM ** -0.5
    for j in range(GQA_HEADS):
        sl = slice(j * GQA_HEAD_DIM, (j + 1) * GQA_HEAD_DIM)
        qn = _rms(qq[:, sl], gq_ref[...], GQA_HEAD_DIM)
        q_ref[0, :, sl] = (_rope(qn, cg, sg, 32) * q_scale).astype(BF16)

    qa = _rms(_dot(hb, wqa_ref[...]), gqa_ref[...], MLA_Q_RANK).astype(BF16)
    q2 = _dot(qa, wqb_ref[...])
    gmq = gmq_ref[...]
    gq_nope, gq_rope = gmq[:, :MLA_NOPE_DIM], gmq[:, MLA_NOPE_DIM:]
    m_scale = MLA_QK_DIM ** -0.5
    for j in range(MLA_HEADS):
        base = j * MLA_HEAD_PAD
        qn = q2[:, base:base + MLA_NOPE_DIM]
        qr = q2[:, base + MLA_NOPE_DIM:base + MLA_HEAD_PAD]
        ms = (jnp.sum(qn * qn, axis=-1, keepdims=True) + jnp.sum(qr * qr, axis=-1, keepdims=True)) * (1.0 / MLA_QK_DIM)
        r = lax.rsqrt(ms + EPS)
        qm_ref[0, :, base:base + MLA_NOPE_DIM] = (qn * r * gq_nope * m_scale).astype(BF16)
        qm_ref[0, :, base + MLA_NOPE_DIM:base + MLA_HEAD_PAD] = (_rope(qr * r * gq_rope, cm, sm, 16) * m_scale).astype(BF16)

    gl = _dot(hb, wg_ref[...])
    gate_ref[0] = (1.0 / (1.0 + jnp.exp(-gl))).astype(BF16)


def _proj(x, mod3, mod_row_of_batch, tables, w, with_q, tm):
    b, l, _ = x.shape
    cg, sg, cm, sm = tables
    const = lambda shape: pl.BlockSpec(shape, lambda bi, i: (0,) * len(shape))
    tab = pl.BlockSpec((tm, LANES), lambda bi, i: (i, 0))
    in_specs = [
        pl.BlockSpec((1, tm, D_MODEL), lambda bi, i: (bi, i, 0)),
        pl.BlockSpec((1, 6, D_MODEL), lambda bi, i: (mod_row_of_batch(bi), 0, 0)),
        const((1, D_MODEL)), tab, tab, tab, tab,
    ]
    if with_q:
        weights = [w["wq"], w["wk"], w["wv"], w["wqa"], w["wckv"], w["wkr"], w["wg"], w["wqb"], w["wkb"], w["wvb"],
                   w["gq"], w["gk"], w["gqa"], w["gkva"], w["gmq"], w["gmk"]]
    else:
        weights = [w["wk"], w["wv"], w["wckv"], w["wkr"], w["wkb"], w["wvb"], w["gk"], w["gkva"], w["gmk"]]
    in_specs += [const(a.shape) for a in weights]

    def out(width):
        return jax.ShapeDtypeStruct((b, l, width), BF16), pl.BlockSpec((1, tm, width), lambda bi, i: (bi, i, 0))

    outs = [out(GQA_KV_W), out(GQA_KV_W), out(MLA_HEADS * MLA_HEAD_PAD), out(MLA_V_W)]
    if with_q:
        outs = [out(GQA_Q_W)] + outs[:2] + [out(MLA_HEADS * MLA_HEAD_PAD)] + outs[2:] + [out(2 * D_MODEL)]
    return pl.pallas_call(
        functools.partial(_proj_kernel, with_q=with_q),
        out_shape=[o[0] for o in outs],
        grid=(b, l // tm),
        in_specs=in_specs,
        out_specs=[o[1] for o in outs],
        compiler_params=pltpu.CompilerParams(dimension_semantics=("parallel", "parallel"),
                                             vmem_limit_bytes=56 * 1024 * 1024),
        name="proj_latent" if with_q else "proj_ctx",
    )(x, mod3, w["norm_mix"], cg, sg, cm, sm, *weights)


def _attn_kernel(q_ref, kl_ref, vl_ref, kc_ref, vc_ref, o_ref, *, tk, n_lat):
    q = q_ref[0]
    tq = q.shape[0]
    dv = vl_ref.shape[-1]

    def update(carry, k, v):
        m, l, acc = carry
        s = _dot_nt(q, k)
        m_new = jnp.maximum(m, jnp.max(s, axis=-1, keepdims=True))
        alpha = jnp.exp(m - m_new)
        p = jnp.exp(s - m_new)
        l = alpha * l + jnp.sum(p, axis=-1, keepdims=True)
        acc = alpha * acc + _dot(p.astype(BF16), v)
        return m_new, l, acc

    def body(j, carry):
        off = pl.multiple_of(j * tk, tk)
        return update(carry, kl_ref[0, pl.ds(off, tk), :], vl_ref[0, pl.ds(off, tk), :])

    init = (jnp.full((tq, 1), NEG_INF, F32), jnp.zeros((tq, 1), F32), jnp.zeros((tq, dv), F32))
    carry = lax.fori_loop(0, n_lat // tk, body, init)
    m, l, acc = update(carry, kc_ref[0], vc_ref[0])
    o_ref[0] = (acc / l).astype(BF16)


def _attention(q, k_lat, v_lat, k_ctx, v_ctx, n_heads, group, d_qk, dv, tq, tk, name):
    b, l, _ = q.shape
    lc = k_ctx.shape[1]
    return pl.pallas_call(
        functools.partial(_attn_kernel, tk=tk, n_lat=l),
        out_shape=jax.ShapeDtypeStruct((b, l, n_heads * dv), BF16),
        grid=(b, n_heads, l // tq),
        in_specs=[
            pl.BlockSpec((1, tq, d_qk), lambda bi, h, i: (bi, i, h)),
            pl.BlockSpec((1, l, d_qk), lambda bi, h, i: (bi, 0, h // group)),
            pl.BlockSpec((1, l, dv), lambda bi, h, i: (bi, 0, h // group)),
            pl.BlockSpec((1, lc, d_qk), lambda bi, h, i: (bi, 0, h // group)),
            pl.BlockSpec((1, lc, dv), lambda bi, h, i: (bi, 0, h // group)),
        ],
        out_specs=pl.BlockSpec((1, tq, dv), lambda bi, h, i: (bi, i, h)),
        compiler_params=pltpu.CompilerParams(dimension_semantics=("parallel", "parallel", "parallel"),
                                             vmem_limit_bytes=56 * 1024 * 1024),
        name=name,
    )(q, k_lat, v_lat, k_ctx, v_ctx)


def _merge_kernel(oa_ref, ob_ref, gate_ref, x_ref, mod_ref, woa_ref, wob_ref, wout_ref, nffn_ref,
                  rwh_ref, rwl_ref, rb_ref, x1_ref, h2_ref, eidx_ref, egate_ref):
    mod = mod_ref[0]
    g1, shift2, scale2 = mod[2:3, :], mod[3:4, :], mod[4:5, :]
    ya = _dot(oa_ref[0], woa_ref[...])
    yb = _dot(ob_ref[0], wob_ref[...])
    g = gate_ref[0].astype(F32)
    y = g[:, :D_MODEL] * ya + g[:, D_MODEL:] * yb
    z = _dot(y.astype(BF16), wout_ref[...])
    x1 = x_ref[0] + g1 * z
    x1_ref[0] = x1
    h2 = _rms(x1, nffn_ref[...], D_MODEL) * (1.0 + scale2) + shift2
    h2_ref[0] = h2

    h_hi, h_lo = _split_bf16(h2)
    logits = _dot(h_hi, rwh_ref[...]) + (_dot(h_hi, rwl_ref[...]) + _dot(h_lo, rwh_ref[...])) + rb_ref[...]
    lane = lax.broadcasted_iota(jnp.int32, logits.shape, 1).astype(F32)
    cur = jnp.where(lane < N_EXPERTS, logits, NEG_INF)
    vals, idxs = [], []
    for _ in range(TOP_K):
        mx = jnp.max(cur, axis=-1, keepdims=True)
        ix = jnp.min(jnp.where(cur == mx, lane, float(LANES)), axis=-1, keepdims=True)
        vals.append(mx)
        idxs.append(ix)
        cur = jnp.where(lane == ix, NEG_INF, cur)
    ex = [jnp.exp(v - vals[0]) for v in vals]
    den = ex[0] + ex[1] + ex[2] + ex[3]
    eidx = jnp.zeros(logits.shape, F32)
    egate = jnp.zeros(logits.shape, F32)
    for k in range(TOP_K):
        eidx = jnp.where(lane == k, idxs[k], eidx)
        egate = jnp.where(lane == k, ex[k] / den, egate)
    eidx_ref[0] = eidx.astype(jnp.int32)
    egate_ref[0] = egate


def _merge(o_a, o_b, gates, x, mod3, w, tm):
    b, l, _ = x.shape
    const = lambda shape: pl.BlockSpec(shape, lambda bi, i: (0,) * len(shape))
    tok = lambda width: pl.BlockSpec((1, tm, width), lambda bi, i: (bi, i, 0))
    return pl.pallas_call(
        _merge_kernel,
        out_shape=[jax.ShapeDtypeStruct((b, l, D_MODEL), F32), jax.ShapeDtypeStruct((b, l, D_MODEL), F32),
                   jax.ShapeDtypeStruct((b, l, LANES), jnp.int32), jax.ShapeDtypeStruct((b, l, LANES), F32)],
        grid=(b, l // tm),
        in_specs=[tok(GQA_Q_W), tok(MLA_V_W), tok(2 * D_MODEL), tok(D_MODEL),
                  pl.BlockSpec((1, 6, D_MODEL), lambda bi, i: (bi, 0, 0)),
                  const((GQA_Q_W, D_MODEL)), const((MLA_V_W, D_MODEL)), const((D_MODEL, D_MODEL)), const((1, D_MODEL)),
                  const((D_MODEL, LANES)), const((D_MODEL, LANES)), const((1, LANES))],
        out_specs=[tok(D_MODEL), tok(D_MODEL), tok(LANES), tok(LANES)],
        compiler_params=pltpu.CompilerParams(dimension_semantics=("parallel", "parallel"),
                                             vmem_limit_bytes=56 * 1024 * 1024),
        name="merge_router",
    )(o_a, o_b, gates, x, mod3, w["woa"], w["wob"], w["wout"], w["norm_ffn"], w["rw_hi"], w["rw_lo"], w["rb"])


def _onehots(idx, lane):
    return [lane == idx[:, k:k + 1] for k in range(TOP_K)]


def _route_kernel(eidx_ref, rank_ref, cnt_ref, carry_ref):
    @pl.when(pl.program_id(0) == 0)
    def _():
        carry_ref[...] = jnp.zeros_like(carry_ref)

    idx = eidx_ref[...]
    tm = idx.shape[0]
    lane = lax.broadcasted_iota(jnp.int32, idx.shape, 1)
    oh = _onehots(idx, lane)
    total = jnp.zeros(idx.shape, F32)
    for k in range(TOP_K):
        total = total + jnp.where(oh[k], 1.0, 0.0)
    row = lax.broadcasted_iota(jnp.int32, (tm, tm), 0)
    col = lax.broadcasted_iota(jnp.int32, (tm, tm), 1)
    tri = jnp.where(row > col, 1.0, 0.0).astype(BF16)
    before = _dot(tri, total.astype(BF16)) + carry_ref[0:1, :]
    rank = jnp.zeros(idx.shape, F32)
    for k in range(TOP_K):
        rk = jnp.sum(jnp.where(oh[k], before, 0.0), axis=-1, keepdims=True)
        rank = jnp.where(lane == k, rk, rank)
    rank_ref[...] = rank.astype(jnp.int32)
    carry_ref[0:1, :] = carry_ref[0:1, :] + jnp.sum(total, axis=0, keepdims=True)
    cnt_ref[...] = carry_ref[...]


def _route(eidx, tm):
    t = eidx.shape[0]
    return pl.pallas_call(
        _route_kernel,
        out_shape=[jax.ShapeDtypeStruct((t, LANES), jnp.int32), jax.ShapeDtypeStruct((8, LANES), F32)],
        grid=(t // tm,),
        in_specs=[pl.BlockSpec((tm, LANES), lambda i: (i, 0))],
        out_specs=[pl.BlockSpec((tm, LANES), lambda i: (i, 0)), pl.BlockSpec((8, LANES), lambda i: (0, 0))],
        scratch_shapes=[pltpu.VMEM((8, LANES), F32)],
        compiler_params=pltpu.CompilerParams(dimension_semantics=("arbitrary",)),
        name="route_rank",
    )(eidx)


def _dest_kernel(eidx_ref, rank_ref, start_ref, dest_ref):
    idx = eidx_ref[...]
    lane = lax.broadcasted_iota(jnp.int32, idx.shape, 1)
    oh = _onehots(idx, lane)
    start = start_ref[0:1, :]
    dest = jnp.zeros(idx.shape, F32)
    for k in range(TOP_K):
        sk = jnp.sum(jnp.where(oh[k], start, 0.0), axis=-1, keepdims=True)
        dest = jnp.where(lane == k, sk, dest)
    dest_ref[...] = dest.astype(jnp.int32) + jnp.where(lane < TOP_K, rank_ref[...], 0)


def _dest(eidx, rank, start, tm):
    t = eidx.shape[0]
    blk = pl.BlockSpec((tm, LANES), lambda i: (i, 0))
    return pl.pallas_call(
        _dest_kernel,
        out_shape=jax.ShapeDtypeStruct((t, LANES), jnp.int32),
        grid=(t // tm,),
        in_specs=[blk, blk, pl.BlockSpec((8, LANES), lambda i: (0, 0))],
        out_specs=blk,
        compiler_params=pltpu.CompilerParams(dimension_semantics=("parallel",)),
        name="route_dest",
    )(eidx, rank, start)


def _row_copy(src_ref, s, dst_ref, d, sem):
    return pltpu.make_async_copy(src_ref.at[pl.ds(s, 1)], dst_ref.at[pl.ds(d, 1)], sem)


def _dispatch_kernel(dest_ref, h_ref, buf_in_ref, buf_ref, sem):
    del buf_in_ref
    tm = h_ref.shape[0]

    def start(t, c):
        for k in range(TOP_K):
            _row_copy(h_ref, t, buf_ref, dest_ref[0, 0, t * TOP_K + k], sem).start()
        return c

    def wait(t, c):
        for k in range(TOP_K):
            _row_copy(h_ref, 0, buf_ref, 0, sem).wait()
        return c

    lax.fori_loop(0, tm, start, 0)
    lax.fori_loop(0, tm, wait, 0)


def _dispatch(dest_s, h2, buf0, tm):
    t = h2.shape[0]
    return pl.pallas_call(
        _dispatch_kernel,
        out_shape=jax.ShapeDtypeStruct(buf0.shape, buf0.dtype),
        grid=(t // tm,),
        in_specs=[pl.BlockSpec((1, 1, tm * TOP_K), lambda i: (i, 0, 0), memory_space=pltpu.SMEM),
                  pl.BlockSpec((tm, D_MODEL), lambda i: (i, 0)),
                  pl.BlockSpec(memory_space=pl.ANY)],
        out_specs=pl.BlockSpec(memory_space=pl.ANY),
        scratch_shapes=[pltpu.SemaphoreType.DMA(())],
        input_output_aliases={2: 0},
        compiler_params=pltpu.CompilerParams(dimension_semantics=("arbitrary",), has_side_effects=True),
        name="moe_dispatch",
    )(dest_s, h2, buf0)


def _combine_kernel(dest_ref, y_ref, x1_ref, egate_ref, mod_ref, o_ref, stage_ref, sem):
    tm = x1_ref.shape[0]

    def start(t, c):
        for k in range(TOP_K):
            _row_copy(y_ref, dest_ref[0, 0, t * TOP_K + k], stage_ref.at[k], t, sem).start()
        return c

    def wait(t, c):
        for k in range(TOP_K):
            _row_copy(y_ref, 0, stage_ref.at[k], 0, sem).wait()
        return c

    lax.fori_loop(0, tm, start, 0)
    lax.fori_loop(0, tm, wait, 0)
    g2 = mod_ref[0][5:6, :]
    eg = egate_ref[...]
    moe = stage_ref[0] * eg[:, 0:1]
    for k in range(1, TOP_K):
        moe = moe + stage_ref[k] * eg[:, k:k + 1]
    o_ref[...] = x1_ref[...] + g2 * moe


def _combine(dest_s, y, x1, egate, mod3, tm, l):
    t = x1.shape[0]
    per_batch = l // tm
    return pl.pallas_call(
        _combine_kernel,
        out_shape=jax.ShapeDtypeStruct((t, D_MODEL), F32),
        grid=(t // tm,),
        in_specs=[pl.BlockSpec((1, 1, tm * TOP_K), lambda i: (i, 0, 0), memory_space=pltpu.SMEM),
                  pl.BlockSpec(memory_space=pl.ANY),
                  pl.BlockSpec((tm, D_MODEL), lambda i: (i, 0)),
                  pl.BlockSpec((tm, LANES), lambda i: (i, 0)),
                  pl.BlockSpec((1, 6, D_MODEL), lambda i: (i // per_batch, 0, 0))],
        out_specs=pl.BlockSpec((tm, D_MODEL), lambda i: (i, 0)),
        scratch_shapes=[pltpu.VMEM((TOP_K, tm, D_MODEL), F32), pltpu.SemaphoreType.DMA(())],
        compiler_params=pltpu.CompilerParams(dimension_semantics=("arbitrary",)),
        name="moe_combine",
    )(dest_s, y, x1, egate, mod3)


def _expert_kernel(be_ref, nused_ref, x_ref, w1_ref, b1_ref, w2_ref, b2_ref, y_ref):
    del be_ref

    @pl.when(pl.program_id(0) < nused_ref[0])
    def _():
        xb = x_ref[...].astype(BF16)
        gu = _dot(xb, w1_ref[0]) + b1_ref[0]
        glu = jnp.minimum(gu[:, :D_EXPERT], SWIGLU_LIMIT)
        lin = jnp.clip(gu[:, D_EXPERT:], -SWIGLU_LIMIT, SWIGLU_LIMIT)
        act = glu * (1.0 / (1.0 + jnp.exp(-SWIGLU_ALPHA * glu))) * (lin + 1.0)
        y_ref[...] = _dot(act.astype(BF16), w2_ref[0]) + b2_ref[0]

    @pl.when(pl.program_id(0) >= nused_ref[0])
    def _():
        y_ref[...] = jnp.zeros_like(y_ref)


def _experts(block_e, nused, buf, w1, b1, w2, b2):
    nb = buf.shape[0] // EXPERT_BLOCK
    row = lambda i, be, nu: (jnp.minimum(i, nu[0] - 1), 0)
    out_row = lambda i, be, nu: (i, 0)
    exp3 = lambda i, be, nu: (be[jnp.minimum(i, nu[0] - 1)], 0, 0)
    return pl.pallas_call(
        _expert_kernel,
        out_shape=jax.ShapeDtypeStruct((buf.shape[0], D_MODEL), F32),
        grid_spec=pltpu.PrefetchScalarGridSpec(
            num_scalar_prefetch=2,
            grid=(nb,),
            in_specs=[pl.BlockSpec((EXPERT_BLOCK, D_MODEL), row),
                      pl.BlockSpec((1, D_MODEL, 2 * D_EXPERT), exp3),
                      pl.BlockSpec((1, 1, 2 * D_EXPERT), exp3),
                      pl.BlockSpec((1, D_EXPERT, D_MODEL), exp3),
                      pl.BlockSpec((1, 1, D_MODEL), exp3)],
            out_specs=pl.BlockSpec((EXPERT_BLOCK, D_MODEL), out_row)),
        compiler_params=pltpu.CompilerParams(dimension_semantics=("arbitrary",),
                                             vmem_limit_bytes=56 * 1024 * 1024),
        name="moe_experts",
    )(block_e, nused, buf, w1, b1, w2, b2)


def _rope_tables(seq_len):
    pos = jnp.arange(seq_len, dtype=jnp.int32)
    row = (pos // GRID_W).astype(F32)[:, None]
    col = (pos % GRID_W).astype(F32)[:, None]

    def table(d_axis, pad):
        inv_freq = ROPE_THETA ** (-jnp.arange(0, d_axis, 2, dtype=F32) / d_axis)
        ar, ac = row * inv_freq[None, :], col * inv_freq[None, :]
        cos = jnp.concatenate([jnp.cos(ar), jnp.cos(ar), jnp.cos(ac), jnp.cos(ac)], axis=-1)
        sin = jnp.concatenate([-jnp.sin(ar), jnp.sin(ar), -jnp.sin(ac), jnp.sin(ac)], axis=-1)
        if pad:
            cos = jnp.concatenate([cos, jnp.ones((seq_len, pad), F32)], axis=-1)
            sin = jnp.concatenate([sin, jnp.zeros((seq_len, pad), F32)], axis=-1)
        return cos, sin

    cg, sg = table(GQA_HEAD_DIM // 2, 0)
    cm, sm = table(MLA_ROPE_DIM // 2, LANES - MLA_ROPE_DIM)
    return cg, sg, cm, sm


def _pad_lanes(a, width):
    return jnp.pad(a, [(0, 0)] * (a.ndim - 1) + [(0, width - a.shape[-1])])


def _prep_weights(p):
    w_in = p["w_in"]
    w = {"norm_mix": p["norm_mix"].reshape(1, D_MODEL), "norm_ffn": p["norm_ffn"].reshape(1, D_MODEL)}
    w["wq"] = w_in[:, :OFF_GQA_K].astype(BF16)
    w["wk"] = w_in[:, OFF_GQA_K:OFF_GQA_V].astype(BF16)
    w["wv"] = w_in[:, OFF_GQA_V:OFF_MLA_QA].astype(BF16)
    w["wqa"] = w_in[:, OFF_MLA_QA:OFF_MLA_KVA].astype(BF16)
    w["wckv"] = w_in[:, OFF_MLA_KVA:OFF_MLA_KVA + MLA_KV_RANK].astype(BF16)
    w["wkr"] = _pad_lanes(w_in[:, OFF_MLA_KVA + MLA_KV_RANK:OFF_GATE], LANES).astype(BF16)
    w["wg"] = w_in[:, OFF_GATE:].astype(BF16)
    wqb = p["mla_w_qb"].reshape(MLA_Q_RANK, MLA_HEADS, MLA_QK_DIM)
    w["wqb"] = _pad_lanes(wqb, MLA_HEAD_PAD).reshape(MLA_Q_RANK, MLA_HEADS * MLA_HEAD_PAD).astype(BF16)
    wkvb = p["mla_w_kvb"].reshape(MLA_KV_RANK, MLA_HEADS, MLA_NOPE_DIM + MLA_V_DIM)
    w["wkb"] = wkvb[:, :, :MLA_NOPE_DIM].reshape(MLA_KV_RANK, MLA_HEADS * MLA_NOPE_DIM).astype(BF16)
    w["wvb"] = wkvb[:, :, MLA_NOPE_DIM:].reshape(MLA_KV_RANK, MLA_V_W).astype(BF16)
    w["gq"] = p["gqa_q_norm"].reshape(1, GQA_HEAD_DIM)
    w["gk"] = p["gqa_k_norm"].reshape(1, GQA_HEAD_DIM)
    w["gqa"] = p["mla_q_a_norm"].reshape(1, MLA_Q_RANK)
    w["gkva"] = p["mla_kv_a_norm"].reshape(1, MLA_KV_RANK)
    w["gmq"] = _pad_lanes(p["mla_q_norm"].reshape(1, MLA_QK_DIM), MLA_HEAD_PAD)
    w["gmk"] = _pad_lanes(p["mla_k_norm"].reshape(1, MLA_QK_DIM), MLA_HEAD_PAD)
    w["woa"] = p["w_o_gqa"].astype(BF16)
    w["wob"] = p["w_o_mla"].astype(BF16)
    w["wout"] = p["w_out"].astype(BF16)
    rw = _pad_lanes(p["router_w"], LANES)
    w["rw_hi"] = rw.astype(BF16)
    w["rw_lo"] = (rw - w["rw_hi"].astype(F32)).astype(BF16)
    w["rb"] = _pad_lanes(p["router_b"].reshape(1, N_EXPERTS), LANES)
    return w


def _layer(x, c, ctx, c_ctx, p):
    b, l, _ = x.shape
    lc = ctx.shape[1]
    t = b * l
    w = _prep_weights(p)

    n_mod_rows = -(-(b + 1) // 8) * 8
    cc = jnp.zeros((n_mod_rows, D_MODEL), F32).at[:b].set(c).at[b].set(c_ctx)
    mod3 = _ada_mod(cc, p["ada_w"], p["ada_b"]).reshape(n_mod_rows, 6, D_MODEL)

    tables = _rope_tables(l)
    ident = (jnp.ones((lc, LANES), F32), jnp.zeros((lc, LANES), F32)) * 2
    tm = min(256, l)
    q_a, k_a, v_a, q_m, k_m, v_m, gates = _proj(x, mod3, lambda bi: bi, tables, w, True, tm)
    kc_a, vc_a, kc_m, vc_m = _proj(ctx, mod3, lambda bi: b, ident, w, False, min(256, lc))

    tq, tk = min(512, l), min(512, l)
    o_a = _attention(q_a, k_a, v_a, kc_a, vc_a, GQA_HEADS, GQA_GROUP, GQA_HEAD_DIM, GQA_HEAD_DIM, tq, tk, "attn_gqa")
    o_m = _attention(q_m, k_m, v_m, kc_m, vc_m, MLA_HEADS, 1, MLA_HEAD_PAD, MLA_V_DIM, tq, tk, "attn_mla")

    x1, h2, eidx, egate = _merge(o_a, o_m, gates, x, mod3, w, tm)
    x1, h2 = x1.reshape(t, D_MODEL), h2.reshape(t, D_MODEL)
    eidx, egate = eidx.reshape(t, LANES), egate.reshape(t, LANES)

    tr = min(512, t)
    rank, cnt = _route(eidx, tr)
    counts = cnt[0, :N_EXPERTS].astype(jnp.int32)
    padded = (counts + EXPERT_BLOCK - 1) // EXPERT_BLOCK * EXPERT_BLOCK
    pad_ends = jnp.cumsum(padded)
    pad_starts = pad_ends - padded
    n_blocks = -(-(t * TOP_K) // EXPERT_BLOCK) + N_EXPERTS
    block_e = jnp.minimum(jnp.searchsorted(pad_ends, jnp.arange(n_blocks, dtype=jnp.int32) * EXPERT_BLOCK, side="right"),
                          N_EXPERTS - 1).astype(jnp.int32)
    nused = (pad_ends[-1:] // EXPERT_BLOCK).astype(jnp.int32)
    start = jnp.zeros((8, LANES), F32).at[0, :N_EXPERTS].set(pad_starts.astype(F32))
    dest = _dest(eidx, rank, start, tr)

    td = min(256, t)
    dest_s = dest[:, :TOP_K].reshape(t // td, 1, td * TOP_K)
    buf = _dispatch(dest_s, h2, jnp.zeros((n_blocks * EXPERT_BLOCK, D_MODEL), F32), td)
    y = _experts(block_e, nused, buf, p["expert_w1"].astype(BF16), p["expert_b1"].reshape(N_EXPERTS, 1, -1),
                 p["expert_w2"].astype(BF16), p["expert_b2"].reshape(N_EXPERTS, 1, -1))
    out = _combine(dest_s, y, x1, egate, mod3, td, l)
    return out.reshape(b, l, D_MODEL)


def kernel(x, c, ctx, c_ctx, ada_w, ada_b, norm_mix, norm_ffn, w_in, gqa_q_norm, gqa_k_norm, mla_q_a_norm, mla_kv_a_norm, mla_w_qb, mla_w_kvb, mla_q_norm, mla_k_norm, w_o_gqa, w_o_mla, w_out, router_w, router_b, expert_w1, expert_b1, expert_w2, expert_b2):
    assert ada_w.shape[0] == 1, "single-layer problem: the context stream is never updated"
    p = {
        "ada_w": ada_w[0], "ada_b": ada_b[0], "norm_mix": norm_mix[0], "norm_ffn": norm_ffn[0],
        "w_in": w_in[0], "gqa_q_norm": gqa_q_norm[0], "gqa_k_norm": gqa_k_norm[0],
        "mla_q_a_norm": mla_q_a_norm[0], "mla_kv_a_norm": mla_kv_a_norm[0],
        "mla_w_qb": mla_w_qb[0], "mla_w_kvb": mla_w_kvb[0],
        "mla_q_norm": mla_q_norm[0], "mla_k_norm": mla_k_norm[0],
        "w_o_gqa": w_o_gqa[0], "w_o_mla": w_o_mla[0], "w_out": w_out[0],
        "router_w": router_w[0], "router_b": router_b[0],
        "expert_w1": expert_w1[0], "expert_b1": expert_b1[0],
        "expert_w2": expert_w2[0], "expert_b2": expert_b2[0],
    }
    return _layer(x, c, ctx, c_ctx, p)
```

```python
import functools
import math

import jax
import jax.numpy as jnp
from jax import lax
from jax.experimental import pallas as pl
from jax.experimental.pallas import tpu as pltpu

D_MODEL = 1024
GRID_W = 64
EPS = 1e-6
ROPE_THETA = 10000.0

GQA_HEADS = 8
GQA_KV_HEADS = 2
GQA_GROUP = GQA_HEADS // GQA_KV_HEADS
GQA_HEAD_DIM = 128
GQA_Q_W = GQA_HEADS * GQA_HEAD_DIM
GQA_KV_W = GQA_KV_HEADS * GQA_HEAD_DIM

MLA_HEADS = 8
MLA_Q_RANK = 256
MLA_KV_RANK = 128
MLA_NOPE_DIM = 128
MLA_ROPE_DIM = 64
MLA_V_DIM = 128
MLA_QK_DIM = MLA_NOPE_DIM + MLA_ROPE_DIM
MLA_HEAD_PAD = 256
MLA_V_W = MLA_HEADS * MLA_V_DIM

OFF_GQA_K = GQA_Q_W
OFF_GQA_V = OFF_GQA_K + GQA_KV_W
OFF_MLA_QA = OFF_GQA_V + GQA_KV_W
OFF_MLA_KVA = OFF_MLA_QA + MLA_Q_RANK
OFF_GATE = OFF_MLA_KVA + MLA_KV_RANK + MLA_ROPE_DIM

N_EXPERTS = 32
TOP_K = 4
D_EXPERT = 1024
SWIGLU_LIMIT = 7.0
SWIGLU_ALPHA = 1.702
EXPERT_BLOCK = 256

LANES = 128
NEG_INF = float("-inf")
LOG2_E = math.log2(math.e)

BF16 = jnp.bfloat16
F32 = jnp.float32


def _dot(a, b):
    return jnp.dot(a, b, preferred_element_type=F32)


def _dot_nt(a, b):
    return lax.dot_general(a, b, (((1,), (1,)), ((), ())), preferred_element_type=F32)


def _split_bf16(a):
    hi = a.astype(BF16)
    lo = (a - hi.astype(F32)).astype(BF16)
    return hi, lo


def _rms(x, g, n):
    ms = jnp.sum(x * x, axis=-1, keepdims=True) * (1.0 / n)
    return x * lax.rsqrt(ms + EPS) * g


def _swap_halves(x, k):
    lane = lax.broadcasted_iota(jnp.int32, x.shape, 1)
    return jnp.where((lane & k) != 0, pltpu.roll(x, k, 1), pltpu.roll(x, LANES - k, 1))


def _rope(x, cos, sin_signed, k):
    return x * cos + _swap_halves(x, k) * sin_signed


def _ada_kernel(c_ref, w_ref, b_ref, o_ref):
    c = c_ref[...]
    s = c * (1.0 / (1.0 + jnp.exp(-c)))
    s_hi, s_lo = _split_bf16(s)
    w_hi, w_lo = _split_bf16(w_ref[...])
    o_ref[...] = _dot(s_hi, w_hi) + (_dot(s_hi, w_lo) + _dot(s_lo, w_hi)) + b_ref[...]


def _ada_mod(cc, ada_w, ada_b):
    n = ada_w.shape[1]
    tn = 1024
    return pl.pallas_call(
        _ada_kernel,
        out_shape=jax.ShapeDtypeStruct((cc.shape[0], n), F32),
        grid=(n // tn,),
        in_specs=[
            pl.BlockSpec((cc.shape[0], D_MODEL), lambda j: (0, 0)),
            pl.BlockSpec((D_MODEL, tn), lambda j: (0, j)),
            pl.BlockSpec((1, tn), lambda j: (0, j)),
        ],
        out_specs=pl.BlockSpec((cc.shape[0], tn), lambda j: (0, j)),
        compiler_params=pltpu.CompilerParams(dimension_semantics=("parallel",)),
        name="ada_mod",
    )(cc, ada_w, ada_b.reshape(1, n))


def _proj_kernel(*refs, with_q):
    if with_q:
        (x_ref, mod_ref, nmix_ref, cg_ref, sg_ref, cm_ref, sm_ref,
         wq_ref, wk_ref, wv_ref, wqa_ref, wckv_ref, wkr_ref, wg_ref, wqb_ref, wkb_ref, wvb_ref,
         gq_ref, gk_ref, gqa_ref, gkva_ref, gmq_ref, gmk_ref,
         q_ref, k_ref, v_ref, qm_ref, km_ref, vm_ref, gate_ref) = refs
    else:
        (x_ref, mod_ref, nmix_ref, cg_ref, sg_ref, cm_ref, sm_ref,
         wk_ref, wv_ref, wckv_ref, wkr_ref, wkb_ref, wvb_ref,
         gk_ref, gkva_ref, gmk_ref,
         k_ref, v_ref, km_ref, vm_ref) = refs

    x = x_ref[0]
    mod = mod_ref[0]
    shift, scale = mod[0:1, :], mod[1:2, :]
    h = _rms(x, nmix_ref[...], D_MODEL) * (1.0 + scale) + shift
    hb = h.astype(BF16)
    cg, sg = cg_ref[...], sg_ref[...]
    cm, sm = cm_ref[...], sm_ref[...]

    kk = _dot(hb, wk_ref[...])
    for j in range(GQA_KV_HEADS):
        sl = slice(j * GQA_HEAD_DIM, (j + 1) * GQA_HEAD_DIM)
        kn = _rms(kk[:, sl], gk_ref[...], GQA_HEAD_DIM)
        k_ref[0, :, sl] = _rope(kn, cg, sg, 32).astype(BF16)
    v_ref[0] = _dot_nt(wv_ref[...], hb).astype(BF16)

    ckv = _rms(_dot(hb, wckv_ref[...]), gkva_ref[...], MLA_KV_RANK).astype(BF16)
    vm_ref[0] = _dot_nt(wvb_ref[...], ckv).astype(BF16)
    knope = _dot(ckv, wkb_ref[...])
    kr = _dot(hb, wkr_ref[...])
    gmk = gmk_ref[...]
    g_nope, g_rope = gmk[:, :MLA_NOPE_DIM], gmk[:, MLA_NOPE_DIM:]
    ssq_r = jnp.sum(kr * kr, axis=-1, keepdims=True)
    kr_roped = _rope(kr * g_rope, cm, sm, 16)
    for j in range(MLA_HEADS):
        kn = knope[:, j * MLA_NOPE_DIM:(j + 1) * MLA_NOPE_DIM]
        ms = (jnp.sum(kn * kn, axis=-1, keepdims=True) + ssq_r) * (1.0 / MLA_QK_DIM)
        r = lax.rsqrt(ms + EPS)
        base = j * MLA_HEAD_PAD
        km_ref[0, :, base:base + MLA_NOPE_DIM] = (kn * r * g_nope).astype(BF16)
        km_ref[0, :, base + MLA_NOPE_DIM:base + MLA_HEAD_PAD] = (kr_roped * r).astype(BF16)

    if not with_q:
        return

    qq = _dot(hb, wq_ref[...])
    q_scale = GQA_HEAD_DIM ** -0.5 * LOG2_E
    for j in range(GQA_HEADS):
        sl = slice(j * GQA_HEAD_DIM, (j + 1) * GQA_HEAD_DIM)
        qn = _rms(qq[:, sl], gq_ref[...], GQA_HEAD_DIM)
        q_ref[0, :, sl] = (_rope(qn, cg, sg, 32) * q_scale).astype(BF16)

    qa = _rms(_dot(hb, wqa_ref[...]), gqa_ref[...], MLA_Q_RANK).astype(BF16)
    q2 = _dot(qa, wqb_ref[...])
    gmq = gmq_ref[...]
    gq_nope, gq_rope = gmq[:, :MLA_NOPE_DIM], gmq[:, MLA_NOPE_DIM:]
    m_scale = MLA_QK_DIM ** -0.5 * LOG2_E
    for j in range(MLA_HEADS):
        base = j * MLA_HEAD_PAD
        qn = q2[:, base:base + MLA_NOPE_DIM]
        qr = q2[:, base + MLA_NOPE_DIM:base + MLA_HEAD_PAD]
        ms = (jnp.sum(qn * qn, axis=-1, keepdims=True) + jnp.sum(qr * qr, axis=-1, keepdims=True)) * (1.0 / MLA_QK_DIM)
        r = lax.rsqrt(ms + EPS)
        qm_ref[0, :, base:base + MLA_NOPE_DIM] = (qn * r * gq_nope * m_scale).astype(BF16)
        qm_ref[0, :, base + MLA_NOPE_DIM:base + MLA_HEAD_PAD] = (_rope(qr * r * gq_rope, cm, sm, 16) * m_scale).astype(BF16)

    gl = _dot(hb, wg_ref[...])
    gate_ref[0] = (1.0 / (1.0 + jnp.exp(-gl))).astype(BF16)


def _proj(x, mod3, mod_row_of_batch, tables, w, with_q, tm):
    b, l, _ = x.shape
    cg, sg, cm, sm = tables
    const = lambda shape: pl.BlockSpec(shape, lambda bi, i: (0,) * len(shape))
    tab = pl.BlockSpec((tm, LANES), lambda bi, i: (i, 0))
    in_specs = [
        pl.BlockSpec((1, tm, D_MODEL), lambda bi, i: (bi, i, 0)),
        pl.BlockSpec((1, 6, D_MODEL), lambda bi, i: (mod_row_of_batch(bi), 0, 0)),
        const((1, D_MODEL)), tab, tab, tab, tab,
    ]
    if with_q:
        weights = [w["wq"], w["wk"], w["wv"], w["wqa"], w["wckv"], w["wkr"], w["wg"], w["wqb"], w["wkb"], w["wvb"],
                   w["gq"], w["gk"], w["gqa"], w["gkva"], w["gmq"], w["gmk"]]
    else:
        weights = [w["wk"], w["wv"], w["wckv"], w["wkr"], w["wkb"], w["wvb"], w["gk"], w["gkva"], w["gmk"]]
    in_specs += [const(a.shape) for a in weights]

    def out(width):
        return jax.ShapeDtypeStruct((b, l, width), BF16), pl.BlockSpec((1, tm, width), lambda bi, i: (bi, i, 0))

    def out_t(width):
        return jax.ShapeDtypeStruct((b, width, l), BF16), pl.BlockSpec((1, width, tm), lambda bi, i: (bi, 0, i))

    outs = [out(GQA_KV_W), out_t(GQA_KV_W), out(MLA_HEADS * MLA_HEAD_PAD), out_t(MLA_V_W)]
    if with_q:
        outs = [out(GQA_Q_W)] + outs[:2] + [out(MLA_HEADS * MLA_HEAD_PAD)] + outs[2:] + [out(2 * D_MODEL)]
    return pl.pallas_call(
        functools.partial(_proj_kernel, with_q=with_q),
        out_shape=[o[0] for o in outs],
        grid=(b, l // tm),
        in_specs=in_specs,
        out_specs=[o[1] for o in outs],
        compiler_params=pltpu.CompilerParams(dimension_semantics=("parallel", "parallel"),
                                             vmem_limit_bytes=56 * 1024 * 1024),
        name="proj_latent" if with_q else "proj_ctx",
    )(x, mod3, w["norm_mix"], cg, sg, cm, sm, *weights)


def _attn_kernel(q_ref, kl_ref, vl_ref, kc_ref, vc_ref, o_ref, *, tk, n_lat):
    q = q_ref[0]
    tq = q.shape[0]
    dv = vl_ref.shape[1]

    def update(carry, s, vt):
        m, l, acc = carry
        m_new = jnp.maximum(m, jnp.max(s, axis=0, keepdims=True))
        alpha = jnp.exp2(m - m_new)
        p = jnp.exp2(s - m_new)
        l = alpha * l + jnp.sum(p, axis=0, keepdims=True)
        return m_new, l, alpha * acc + _dot(vt, p.astype(BF16))

    n = n_lat // tk
    keys = [kl_ref[0, j * tk:(j + 1) * tk, :] for j in range(n)] + [kc_ref[0]]
    vals = [vl_ref[0, :, j * tk:(j + 1) * tk] for j in range(n)] + [vc_ref[0]]
    carry = (jnp.full((1, tq), NEG_INF, F32), jnp.zeros((1, tq), F32), jnp.zeros((dv, tq), F32))
    s = _dot_nt(keys[0], q)
    for j in range(n + 1):
        s_next = _dot_nt(keys[j + 1], q) if j < n else None
        carry = update(carry, s, vals[j])
        s = s_next
    m, l, acc = carry
    o_ref[0] = (acc / l).T.astype(BF16)


def _attention(q, k_lat, v_lat, k_ctx, v_ctx, n_heads, group, d_qk, dv, tq, tk, name):
    b, l, _ = q.shape
    lc = k_ctx.shape[1]
    return pl.pallas_call(
        functools.partial(_attn_kernel, tk=tk, n_lat=l),
        out_shape=jax.ShapeDtypeStruct((b, l, n_heads * dv), BF16),
        grid=(b, n_heads, l // tq),
        in_specs=[
            pl.BlockSpec((1, tq, d_qk), lambda bi, h, i: (bi, i, h)),
            pl.BlockSpec((1, l, d_qk), lambda bi, h, i: (bi, 0, h // group)),
            pl.BlockSpec((1, dv, l), lambda bi, h, i: (bi, h // group, 0)),
            pl.BlockSpec((1, lc, d_qk), lambda bi, h, i: (bi, 0, h // group)),
            pl.BlockSpec((1, dv, lc), lambda bi, h, i: (bi, h // group, 0)),
        ],
        out_specs=pl.BlockSpec((1, tq, dv), lambda bi, h, i: (bi, i, h)),
        compiler_params=pltpu.CompilerParams(dimension_semantics=("parallel", "parallel", "parallel"),
                                             vmem_limit_bytes=56 * 1024 * 1024),
        name=name,
    )(q, k_lat, v_lat, k_ctx, v_ctx)


def _merge_kernel(oa_ref, ob_ref, gate_ref, x_ref, mod_ref, woa_ref, wob_ref, wout_ref, nffn_ref,
                  rwh_ref, rwl_ref, rb_ref, x1_ref, h2_ref, eidx_ref, egate_ref):
    mod = mod_ref[0]
    g1, shift2, scale2 = mod[2:3, :], mod[3:4, :], mod[4:5, :]
    ya = _dot(oa_ref[0], woa_ref[...])
    yb = _dot(ob_ref[0], wob_ref[...])
    g = gate_ref[0].astype(F32)
    y = g[:, :D_MODEL] * ya + g[:, D_MODEL:] * yb
    z = _dot(y.astype(BF16), wout_ref[...])
    x1 = x_ref[0] + g1 * z
    x1_ref[0] = x1
    h2 = _rms(x1, nffn_ref[...], D_MODEL) * (1.0 + scale2) + shift2
    h2_ref[0] = h2

    h_hi, h_lo = _split_bf16(h2)
    logits = _dot(h_hi, rwh_ref[...]) + (_dot(h_hi, rwl_ref[...]) + _dot(h_lo, rwh_ref[...])) + rb_ref[...]
    lane = lax.broadcasted_iota(jnp.int32, logits.shape, 1).astype(F32)
    cur = jnp.where(lane < N_EXPERTS, logits, NEG_INF)
    vals, idxs = [], []
    for _ in range(TOP_K):
        mx = jnp.max(cur, axis=-1, keepdims=True)
        ix = jnp.min(jnp.where(cur == mx, lane, float(LANES)), axis=-1, keepdims=True)
        vals.append(mx)
        idxs.append(ix)
        cur = jnp.where(lane == ix, NEG_INF, cur)
    ex = [jnp.exp(v - vals[0]) for v in vals]
    den = ex[0] + ex[1] + ex[2] + ex[3]
    eidx = jnp.zeros(logits.shape, F32)
    egate = jnp.zeros(logits.shape, F32)
    for k in range(TOP_K):
        eidx = jnp.where(lane == k, idxs[k], eidx)
        egate = jnp.where(lane == k, ex[k] / den, egate)
    eidx_ref[0] = eidx.astype(jnp.int32)
    egate_ref[0] = egate


def _merge(o_a, o_b, gates, x, mod3, w, tm):
    b, l, _ = x.shape
    const = lambda shape: pl.BlockSpec(shape, lambda bi, i: (0,) * len(shape))
    tok = lambda width: pl.BlockSpec((1, tm, width), lambda bi, i: (bi, i, 0))
    return pl.pallas_call(
        _merge_kernel,
        out_shape=[jax.ShapeDtypeStruct((b, l, D_MODEL), F32), jax.ShapeDtypeStruct((b, l, D_MODEL), F32),
                   jax.ShapeDtypeStruct((b, l, LANES), jnp.int32), jax.ShapeDtypeStruct((b, l, LANES), F32)],
        grid=(b, l // tm),
        in_specs=[tok(GQA_Q_W), tok(MLA_V_W), tok(2 * D_MODEL), tok(D_MODEL),
                  pl.BlockSpec((1, 6, D_MODEL), lambda bi, i: (bi, 0, 0)),
                  const((GQA_Q_W, D_MODEL)), const((MLA_V_W, D_MODEL)), const((D_MODEL, D_MODEL)), const((1, D_MODEL)),
                  const((D_MODEL, LANES)), const((D_MODEL, LANES)), const((1, LANES))],
        out_specs=[tok(D_MODEL), tok(D_MODEL), tok(LANES), tok(LANES)],
        compiler_params=pltpu.CompilerParams(dimension_semantics=("parallel", "parallel"),
                                             vmem_limit_bytes=56 * 1024 * 1024),
        name="merge_router",
    )(o_a, o_b, gates, x, mod3, w["woa"], w["wob"], w["wout"], w["norm_ffn"], w["rw_hi"], w["rw_lo"], w["rb"])


def _onehots(idx, lane):
    return [lane == idx[:, k:k + 1] for k in range(TOP_K)]


def _route_kernel(eidx_ref, rank_ref, cnt_ref, carry_ref):
    @pl.when(pl.program_id(0) == 0)
    def _():
        carry_ref[...] = jnp.zeros_like(carry_ref)

    idx = eidx_ref[...]
    tm = idx.shape[0]
    lane = lax.broadcasted_iota(jnp.int32, idx.shape, 1)
    oh = _onehots(idx, lane)
    total = jnp.zeros(idx.shape, F32)
    for k in range(TOP_K):
        total = total + jnp.where(oh[k], 1.0, 0.0)
    row = lax.broadcasted_iota(jnp.int32, (tm, tm), 0)
    col = lax.broadcasted_iota(jnp.int32, (tm, tm), 1)
    tri = jnp.where(row > col, 1.0, 0.0).astype(BF16)
    before = _dot(tri, total.astype(BF16)) + carry_ref[0:1, :]
    rank = jnp.zeros(idx.shape, F32)
    for k in range(TOP_K):
        rk = jnp.sum(jnp.where(oh[k], before, 0.0), axis=-1, keepdims=True)
        rank = jnp.where(lane == k, rk, rank)
    rank_ref[...] = rank.astype(jnp.int32)
    carry_ref[0:1, :] = carry_ref[0:1, :] + jnp.sum(total, axis=0, keepdims=True)
    cnt_ref[...] = carry_ref[...]


def _route(eidx, tm):
    t = eidx.shape[0]
    return pl.pallas_call(
        _route_kernel,
        out_shape=[jax.ShapeDtypeStruct((t, LANES), jnp.int32), jax.ShapeDtypeStruct((8, LANES), F32)],
        grid=(t // tm,),
        in_specs=[pl.BlockSpec((tm, LANES), lambda i: (i, 0))],
        out_specs=[pl.BlockSpec((tm, LANES), lambda i: (i, 0)), pl.BlockSpec((8, LANES), lambda i: (0, 0))],
        scratch_shapes=[pltpu.VMEM((8, LANES), F32)],
        compiler_params=pltpu.CompilerParams(dimension_semantics=("arbitrary",)),
        name="route_rank",
    )(eidx)


def _dest_kernel(eidx_ref, rank_ref, start_ref, dest_ref):
    idx = eidx_ref[...]
    lane = lax.broadcasted_iota(jnp.int32, idx.shape, 1)
    oh = _onehots(idx, lane)
    start = start_ref[0:1, :]
    dest = jnp.zeros(idx.shape, F32)
    for k in range(TOP_K):
        sk = jnp.sum(jnp.where(oh[k], start, 0.0), axis=-1, keepdims=True)
        dest = jnp.where(lane == k, sk, dest)
    dest_ref[...] = dest.astype(jnp.int32) + jnp.where(lane < TOP_K, rank_ref[...], 0)


def _dest(eidx, rank, start, tm):
    t = eidx.shape[0]
    blk = pl.BlockSpec((tm, LANES), lambda i: (i, 0))
    return pl.pallas_call(
        _dest_kernel,
        out_shape=jax.ShapeDtypeStruct((t, LANES), jnp.int32),
        grid=(t // tm,),
        in_specs=[blk, blk, pl.BlockSpec((8, LANES), lambda i: (0, 0))],
        out_specs=blk,
        compiler_params=pltpu.CompilerParams(dimension_semantics=("parallel",)),
        name="route_dest",
    )(eidx, rank, start)


def _row_copy(src_ref, s, dst_ref, d, sem):
    return pltpu.make_async_copy(src_ref.at[pl.ds(s, 1)], dst_ref.at[pl.ds(d, 1)], sem)


def _dispatch_kernel(dest_ref, h_ref, buf_in_ref, buf_ref, sem):
    del buf_in_ref
    tm = h_ref.shape[0]

    def start(t, c):
        for k in range(TOP_K):
            _row_copy(h_ref, t, buf_ref, dest_ref[0, 0, t * TOP_K + k], sem).start()
        return c

    def wait(t, c):
        for k in range(TOP_K):
            _row_copy(h_ref, 0, buf_ref, 0, sem).wait()
        return c

    lax.fori_loop(0, tm, start, 0)
    lax.fori_loop(0, tm, wait, 0)


def _dispatch(dest_s, h2, buf0, tm):
    t = h2.shape[0]
    return pl.pallas_call(
        _dispatch_kernel,
        out_shape=jax.ShapeDtypeStruct(buf0.shape, buf0.dtype),
        grid=(t // tm,),
        in_specs=[pl.BlockSpec((1, 1, tm * TOP_K), lambda i: (i, 0, 0), memory_space=pltpu.SMEM),
                  pl.BlockSpec((tm, D_MODEL), lambda i: (i, 0)),
                  pl.BlockSpec(memory_space=pl.ANY)],
        out_specs=pl.BlockSpec(memory_space=pl.ANY),
        scratch_shapes=[pltpu.SemaphoreType.DMA(())],
        input_output_aliases={2: 0},
        compiler_params=pltpu.CompilerParams(dimension_semantics=("arbitrary",), has_side_effects=True),
        name="moe_dispatch",
    )(dest_s, h2, buf0)


def _combine_kernel(dest_ref, y_ref, x1_ref, egate_ref, mod_ref, o_ref, stage_ref, sem):
    tm = x1_ref.shape[0]

    def start(t, c):
        for k in range(TOP_K):
            _row_copy(y_ref, dest_ref[0, 0, t * TOP_K + k], stage_ref.at[k], t, sem).start()
        return c

    def wait(t, c):
        for k in range(TOP_K):
            _row_copy(y_ref, 0, stage_ref.at[k], 0, sem).wait()
        return c

    lax.fori_loop(0, tm, start, 0)
    lax.fori_loop(0, tm, wait, 0)
    g2 = mod_ref[0][5:6, :]
    eg = egate_ref[...]
    moe = stage_ref[0] * eg[:, 0:1]
    for k in range(1, TOP_K):
        moe = moe + stage_ref[k] * eg[:, k:k + 1]
    o_ref[...] = x1_ref[...] + g2 * moe


def _combine(dest_s, y, x1, egate, mod3, tm, l):
    t = x1.shape[0]
    per_batch = l // tm
    return pl.pallas_call(
        _combine_kernel,
        out_shape=jax.ShapeDtypeStruct((t, D_MODEL), F32),
        grid=(t // tm,),
        in_specs=[pl.BlockSpec((1, 1, tm * TOP_K), lambda i: (i, 0, 0), memory_space=pltpu.SMEM),
                  pl.BlockSpec(memory_space=pl.ANY),
                  pl.BlockSpec((tm, D_MODEL), lambda i: (i, 0)),
                  pl.BlockSpec((tm, LANES), lambda i: (i, 0)),
                  pl.BlockSpec((1, 6, D_MODEL), lambda i: (i // per_batch, 0, 0))],
        out_specs=pl.BlockSpec((tm, D_MODEL), lambda i: (i, 0)),
        scratch_shapes=[pltpu.VMEM((TOP_K, tm, D_MODEL), F32), pltpu.SemaphoreType.DMA(())],
        compiler_params=pltpu.CompilerParams(dimension_semantics=("arbitrary",)),
        name="moe_combine",
    )(dest_s, y, x1, egate, mod3)


def _expert_kernel(be_ref, nused_ref, x_ref, w1_ref, b1_ref, w2_ref, b2_ref, y_ref):
    del be_ref

    @pl.when(pl.program_id(0) < nused_ref[0])
    def _():
        xb = x_ref[...].astype(BF16)
        gu = _dot(xb, w1_ref[0]) + b1_ref[0]
        glu = jnp.minimum(gu[:, :D_EXPERT], SWIGLU_LIMIT)
        lin = jnp.clip(gu[:, D_EXPERT:], -SWIGLU_LIMIT, SWIGLU_LIMIT)
        act = glu * (1.0 / (1.0 + jnp.exp(-SWIGLU_ALPHA * glu))) * (lin + 1.0)
        y_ref[...] = _dot(act.astype(BF16), w2_ref[0]) + b2_ref[0]

    @pl.when(pl.program_id(0) >= nused_ref[0])
    def _():
        y_ref[...] = jnp.zeros_like(y_ref)


def _experts(block_e, nused, buf, w1, b1, w2, b2):
    nb = buf.shape[0] // EXPERT_BLOCK
    row = lambda i, be, nu: (jnp.minimum(i, nu[0] - 1), 0)
    out_row = lambda i, be, nu: (i, 0)
    exp3 = lambda i, be, nu: (be[jnp.minimum(i, nu[0] - 1)], 0, 0)
    return pl.pallas_call(
        _expert_kernel,
        out_shape=jax.ShapeDtypeStruct((buf.shape[0], D_MODEL), F32),
        grid_spec=pltpu.PrefetchScalarGridSpec(
            num_scalar_prefetch=2,
            grid=(nb,),
            in_specs=[pl.BlockSpec((EXPERT_BLOCK, D_MODEL), row),
                      pl.BlockSpec((1, D_MODEL, 2 * D_EXPERT), exp3),
                      pl.BlockSpec((1, 1, 2 * D_EXPERT), exp3),
                      pl.BlockSpec((1, D_EXPERT, D_MODEL), exp3),
                      pl.BlockSpec((1, 1, D_MODEL), exp3)],
            out_specs=pl.BlockSpec((EXPERT_BLOCK, D_MODEL), out_row)),
        compiler_params=pltpu.CompilerParams(dimension_semantics=("arbitrary",),
                                             vmem_limit_bytes=56 * 1024 * 1024),
        name="moe_experts",
    )(block_e, nused, buf, w1, b1, w2, b2)


def _rope_tables(seq_len):
    pos = jnp.arange(seq_len, dtype=jnp.int32)
    row = (pos // GRID_W).astype(F32)[:, None]
    col = (pos % GRID_W).astype(F32)[:, None]

    def table(d_axis, pad):
        inv_freq = ROPE_THETA ** (-jnp.arange(0, d_axis, 2, dtype=F32) / d_axis)
        ar, ac = row * inv_freq[None, :], col * inv_freq[None, :]
        cos = jnp.concatenate([jnp.cos(ar), jnp.cos(ar), jnp.cos(ac), jnp.cos(ac)], axis=-1)
        sin = jnp.concatenate([-jnp.sin(ar), jnp.sin(ar), -jnp.sin(ac), jnp.sin(ac)], axis=-1)
        if pad:
            cos = jnp.concatenate([cos, jnp.ones((seq_len, pad), F32)], axis=-1)
            sin = jnp.concatenate([sin, jnp.zeros((seq_len, pad), F32)], axis=-1)
        return cos, sin

    cg, sg = table(GQA_HEAD_DIM // 2, 0)
    cm, sm = table(MLA_ROPE_DIM // 2, LANES - MLA_ROPE_DIM)
    return cg, sg, cm, sm


def _pad_lanes(a, width):
    return jnp.pad(a, [(0, 0)] * (a.ndim - 1) + [(0, width - a.shape[-1])])


def _prep_weights(p):
    w_in = p["w_in"]
    w = {"norm_mix": p["norm_mix"].reshape(1, D_MODEL), "norm_ffn": p["norm_ffn"].reshape(1, D_MODEL)}
    w["wq"] = w_in[:, :OFF_GQA_K].astype(BF16)
    w["wk"] = w_in[:, OFF_GQA_K:OFF_GQA_V].astype(BF16)
    w["wv"] = w_in[:, OFF_GQA_V:OFF_MLA_QA].T.astype(BF16)
    w["wqa"] = w_in[:, OFF_MLA_QA:OFF_MLA_KVA].astype(BF16)
    w["wckv"] = w_in[:, OFF_MLA_KVA:OFF_MLA_KVA + MLA_KV_RANK].astype(BF16)
    w["wkr"] = _pad_lanes(w_in[:, OFF_MLA_KVA + MLA_KV_RANK:OFF_GATE], LANES).astype(BF16)
    w["wg"] = w_in[:, OFF_GATE:].astype(BF16)
    wqb = p["mla_w_qb"].reshape(MLA_Q_RANK, MLA_HEADS, MLA_QK_DIM)
    w["wqb"] = _pad_lanes(wqb, MLA_HEAD_PAD).reshape(MLA_Q_RANK, MLA_HEADS * MLA_HEAD_PAD).astype(BF16)
    wkvb = p["mla_w_kvb"].reshape(MLA_KV_RANK, MLA_HEADS, MLA_NOPE_DIM + MLA_V_DIM)
    w["wkb"] = wkvb[:, :, :MLA_NOPE_DIM].reshape(MLA_KV_RANK, MLA_HEADS * MLA_NOPE_DIM).astype(BF16)
    w["wvb"] = wkvb[:, :, MLA_NOPE_DIM:].reshape(MLA_KV_RANK, MLA_V_W).T.astype(BF16)
    w["gq"] = p["gqa_q_norm"].reshape(1, GQA_HEAD_DIM)
    w["gk"] = p["gqa_k_norm"].reshape(1, GQA_HEAD_DIM)
    w["gqa"] = p["mla_q_a_norm"].reshape(1, MLA_Q_RANK)
    w["gkva"] = p["mla_kv_a_norm"].reshape(1, MLA_KV_RANK)
    w["gmq"] = _pad_lanes(p["mla_q_norm"].reshape(1, MLA_QK_DIM), MLA_HEAD_PAD)
    w["gmk"] = _pad_lanes(p["mla_k_norm"].reshape(1, MLA_QK_DIM), MLA_HEAD_PAD)
    w["woa"] = p["w_o_gqa"].astype(BF16)
    w["wob"] = p["w_o_mla"].astype(BF16)
    w["wout"] = p["w_out"].astype(BF16)
    rw = _pad_lanes(p["router_w"], LANES)
    w["rw_hi"] = rw.astype(BF16)
    w["rw_lo"] = (rw - w["rw_hi"].astype(F32)).astype(BF16)
    w["rb"] = _pad_lanes(p["router_b"].reshape(1, N_EXPERTS), LANES)
    return w


def _layer(x, c, ctx, c_ctx, p):
    b, l, _ = x.shape
    lc = ctx.shape[1]
    t = b * l
    w = _prep_weights(p)

    n_mod_rows = -(-(b + 1) // 8) * 8
    cc = jnp.zeros((n_mod_rows, D_MODEL), F32).at[:b].set(c).at[b].set(c_ctx)
    mod3 = _ada_mod(cc, p["ada_w"], p["ada_b"]).reshape(n_mod_rows, 6, D_MODEL)

    tables = _rope_tables(l)
    ident = (jnp.ones((lc, LANES), F32), jnp.zeros((lc, LANES), F32)) * 2
    tm = min(256, l)
    q_a, k_a, v_a, q_m, k_m, v_m, gates = _proj(x, mod3, lambda bi: bi, tables, w, True, tm)
    kc_a, vc_a, kc_m, vc_m = _proj(ctx, mod3, lambda bi: b, ident, w, False, min(256, lc))

    tq, tk = min(512, l), min(1024, l)
    o_a = _attention(q_a, k_a, v_a, kc_a, vc_a, GQA_HEADS, GQA_GROUP, GQA_HEAD_DIM, GQA_HEAD_DIM, tq, tk, "attn_gqa")
    o_m = _attention(q_m, k_m, v_m, kc_m, vc_m, MLA_HEADS, 1, MLA_HEAD_PAD, MLA_V_DIM, tq, tk, "attn_mla")

    x1, h2, eidx, egate = _merge(o_a, o_m, gates, x, mod3, w, tm)
    x1, h2 = x1.reshape(t, D_MODEL), h2.reshape(t, D_MODEL)
    eidx, egate = eidx.reshape(t, LANES), egate.reshape(t, LANES)

    tr = min(512, t)
    rank, cnt = _route(eidx, tr)
    counts = cnt[0, :N_EXPERTS].astype(jnp.int32)
    padded = (counts + EXPERT_BLOCK - 1) // EXPERT_BLOCK * EXPERT_BLOCK
    pad_ends = jnp.cumsum(padded)
    pad_starts = pad_ends - padded
    n_blocks = -(-(t * TOP_K) // EXPERT_BLOCK) + N_EXPERTS
    block_row = jnp.arange(n_blocks, dtype=jnp.int32) * EXPERT_BLOCK
    block_e = jnp.minimum(jnp.sum(pad_ends[None, :] <= block_row[:, None], axis=1), N_EXPERTS - 1).astype(jnp.int32)
    nused = (pad_ends[-1:] // EXPERT_BLOCK).astype(jnp.int32)
    start = jnp.zeros((8, LANES), F32).at[0, :N_EXPERTS].set(pad_starts.astype(F32))
    dest = _dest(eidx, rank, start, tr)

    td = min(256, t)
    dest_s = dest[:, :TOP_K].reshape(t // td, 1, td * TOP_K)
    buf = _dispatch(dest_s, h2, jnp.zeros((n_blocks * EXPERT_BLOCK, D_MODEL), F32), td)
    y = _experts(block_e, nused, buf, p["expert_w1"].astype(BF16), p["expert_b1"].reshape(N_EXPERTS, 1, -1),
                 p["expert_w2"].astype(BF16), p["expert_b2"].reshape(N_EXPERTS, 1, -1))
    out = _combine(dest_s, y, x1, egate, mod3, td, l)
    return out.reshape(b, l, D_MODEL)


def kernel(x, c, ctx, c_ctx, ada_w, ada_b, norm_mix, norm_ffn, w_in, gqa_q_norm, gqa_k_norm, mla_q_a_norm, mla_kv_a_norm, mla_w_qb, mla_w_kvb, mla_q_norm, mla_k_norm, w_o_gqa, w_o_mla, w_out, router_w, router_b, expert_w1, expert_b1, expert_w2, expert_b2):
    assert ada_w.shape[0] == 1, "single-layer problem: the context stream is never updated"
    p = {
        "ada_w": ada_w[0], "ada_b": ada_b[0], "norm_mix": norm_mix[0], "norm_ffn": norm_ffn[0],
        "w_in": w_in[0], "gqa_q_norm": gqa_q_norm[0], "gqa_k_norm": gqa_k_norm[0],
        "mla_q_a_norm": mla_q_a_norm[0], "mla_kv_a_norm": mla_kv_a_norm[0],
        "mla_w_qb": mla_w_qb[0], "mla_w_kvb": mla_w_kvb[0],
        "mla_q_norm": mla_q_norm[0], "mla_k_norm": mla_k_norm[0],
        "w_o_gqa": w_o_gqa[0], "w_o_mla": w_o_mla[0], "w_out": w_out[0],
        "router_w": router_w[0], "router_b": router_b[0],
        "expert_w1": expert_w1[0], "expert_b1": expert_b1[0],
        "expert_w2": expert_w2[0], "expert_b2": expert_b2[0],
    }
    return _layer(x, c, ctx, c_ctx, p)
```

```python
import functools
import math

import jax
import jax.numpy as jnp
from jax import lax
from jax.experimental import pallas as pl
from jax.experimental.pallas import tpu as pltpu

D_MODEL = 1024
GRID_W = 64
EPS = 1e-6
ROPE_THETA = 10000.0

GQA_HEADS = 8
GQA_KV_HEADS = 2
GQA_GROUP = GQA_HEADS // GQA_KV_HEADS
GQA_HEAD_DIM = 128
GQA_Q_W = GQA_HEADS * GQA_HEAD_DIM
GQA_KV_W = GQA_KV_HEADS * GQA_HEAD_DIM

MLA_HEADS = 8
MLA_Q_RANK = 256
MLA_KV_RANK = 128
MLA_NOPE_DIM = 128
MLA_ROPE_DIM = 64
MLA_V_DIM = 128
MLA_QK_DIM = MLA_NOPE_DIM + MLA_ROPE_DIM
MLA_HEAD_PAD = 256
MLA_V_W = MLA_HEADS * MLA_V_DIM

OFF_GQA_K = GQA_Q_W
OFF_GQA_V = OFF_GQA_K + GQA_KV_W
OFF_MLA_QA = OFF_GQA_V + GQA_KV_W
OFF_MLA_KVA = OFF_MLA_QA + MLA_Q_RANK
OFF_GATE = OFF_MLA_KVA + MLA_KV_RANK + MLA_ROPE_DIM

N_EXPERTS = 32
TOP_K = 4
D_EXPERT = 1024
SWIGLU_LIMIT = 7.0
SWIGLU_ALPHA = 1.702
EXPERT_BLOCK = 256

LANES = 128
NEG_INF = float("-inf")
LOG2_E = math.log2(math.e)

BF16 = jnp.bfloat16
F32 = jnp.float32


def _dot(a, b):
    return jnp.dot(a, b, preferred_element_type=F32)


def _dot_nt(a, b):
    return lax.dot_general(a, b, (((1,), (1,)), ((), ())), preferred_element_type=F32)


def _split_bf16(a):
    hi = a.astype(BF16)
    lo = (a - hi.astype(F32)).astype(BF16)
    return hi, lo


def _rms(x, g, n):
    ms = jnp.sum(x * x, axis=-1, keepdims=True) * (1.0 / n)
    return x * lax.rsqrt(ms + EPS) * g


def _swap_halves(x, k):
    lane = lax.broadcasted_iota(jnp.int32, x.shape, 1)
    return jnp.where((lane & k) != 0, pltpu.roll(x, k, 1), pltpu.roll(x, LANES - k, 1))


def _rope(x, cos, sin_signed, k):
    return x * cos + _swap_halves(x, k) * sin_signed


def _ada_kernel(c_ref, w_ref, b_ref, o_ref):
    c = c_ref[...]
    s = c * (1.0 / (1.0 + jnp.exp(-c)))
    s_hi, s_lo = _split_bf16(s)
    w_hi, w_lo = _split_bf16(w_ref[...])
    o_ref[...] = _dot(s_hi, w_hi) + (_dot(s_hi, w_lo) + _dot(s_lo, w_hi)) + b_ref[...]


def _ada_mod(cc, ada_w, ada_b):
    n = ada_w.shape[1]
    tn = 1024
    return pl.pallas_call(
        _ada_kernel,
        out_shape=jax.ShapeDtypeStruct((cc.shape[0], n), F32),
        grid=(n // tn,),
        in_specs=[
            pl.BlockSpec((cc.shape[0], D_MODEL), lambda j: (0, 0)),
            pl.BlockSpec((D_MODEL, tn), lambda j: (0, j)),
            pl.BlockSpec((1, tn), lambda j: (0, j)),
        ],
        out_specs=pl.BlockSpec((cc.shape[0], tn), lambda j: (0, j)),
        compiler_params=pltpu.CompilerParams(dimension_semantics=("parallel",)),
        name="ada_mod",
    )(cc, ada_w, ada_b.reshape(1, n))


def _proj_kernel(*refs, with_q):
    if with_q:
        (x_ref, mod_ref, nmix_ref, cg_ref, sg_ref, cm_ref, sm_ref,
         wq_ref, wk_ref, wv_ref, wqa_ref, wckv_ref, wkr_ref, wg_ref, wqb_ref, wkb_ref, wvb_ref,
         gq_ref, gk_ref, gqa_ref, gkva_ref, gmq_ref, gmk_ref,
         q_ref, k_ref, v_ref, qm_ref, km_ref, vm_ref, gate_ref) = refs
    else:
        (x_ref, mod_ref, nmix_ref, cg_ref, sg_ref, cm_ref, sm_ref,
         wk_ref, wv_ref, wckv_ref, wkr_ref, wkb_ref, wvb_ref,
         gk_ref, gkva_ref, gmk_ref,
         k_ref, v_ref, km_ref, vm_ref) = refs

    x = x_ref[0]
    mod = mod_ref[0]
    shift, scale = mod[0:1, :], mod[1:2, :]
    h = _rms(x, nmix_ref[...], D_MODEL) * (1.0 + scale) + shift
    hb = h.astype(BF16)
    cg, sg = cg_ref[...], sg_ref[...]
    cm, sm = cm_ref[...], sm_ref[...]

    kk = _dot(hb, wk_ref[...])
    for j in range(GQA_KV_HEADS):
        sl = slice(j * GQA_HEAD_DIM, (j + 1) * GQA_HEAD_DIM)
        kn = _rms(kk[:, sl], gk_ref[...], GQA_HEAD_DIM)
        k_ref[0, :, sl] = _rope(kn, cg, sg, 32).astype(BF16)
    v_ref[0] = _dot_nt(wv_ref[...], hb).astype(BF16)

    ckv = _rms(_dot(hb, wckv_ref[...]), gkva_ref[...], MLA_KV_RANK).astype(BF16)
    vm_ref[0] = _dot_nt(wvb_ref[...], ckv).astype(BF16)
    knope = _dot(ckv, wkb_ref[...])
    kr = _dot(hb, wkr_ref[...])
    gmk = gmk_ref[...]
    g_nope, g_rope = gmk[:, :MLA_NOPE_DIM], gmk[:, MLA_NOPE_DIM:]
    ssq_r = jnp.sum(kr * kr, axis=-1, keepdims=True)
    kr_roped = _rope(kr * g_rope, cm, sm, 16)
    for j in range(MLA_HEADS):
        kn = knope[:, j * MLA_NOPE_DIM:(j + 1) * MLA_NOPE_DIM]
        ms = (jnp.sum(kn * kn, axis=-1, keepdims=True) + ssq_r) * (1.0 / MLA_QK_DIM)
        r = lax.rsqrt(ms + EPS)
        base = j * MLA_HEAD_PAD
        km_ref[0, :, base:base + MLA_NOPE_DIM] = (kn * r * g_nope).astype(BF16)
        km_ref[0, :, base + MLA_NOPE_DIM:base + MLA_HEAD_PAD] = (kr_roped * r).astype(BF16)

    if not with_q:
        return

    qq = _dot(hb, wq_ref[...])
    q_scale = GQA_HEAD_DIM ** -0.5 * LOG2_E
    for j in range(GQA_HEADS):
        sl = slice(j * GQA_HEAD_DIM, (j + 1) * GQA_HEAD_DIM)
        qn = _rms(qq[:, sl], gq_ref[...], GQA_HEAD_DIM)
        q_ref[0, :, sl] = (_rope(qn, cg, sg, 32) * q_scale).astype(BF16)

    qa = _rms(_dot(hb, wqa_ref[...]), gqa_ref[...], MLA_Q_RANK).astype(BF16)
    q2 = _dot(qa, wqb_ref[...])
    gmq = gmq_ref[...]
    gq_nope, gq_rope = gmq[:, :MLA_NOPE_DIM], gmq[:, MLA_NOPE_DIM:]
    m_scale = MLA_QK_DIM ** -0.5 * LOG2_E
    for j in range(MLA_HEADS):
        base = j * MLA_HEAD_PAD
        qn = q2[:, base:base + MLA_NOPE_DIM]
        qr = q2[:, base + MLA_NOPE_DIM:base + MLA_HEAD_PAD]
        ms = (jnp.sum(qn * qn, axis=-1, keepdims=True) + jnp.sum(qr * qr, axis=-1, keepdims=True)) * (1.0 / MLA_QK_DIM)
        r = lax.rsqrt(ms + EPS)
        qm_ref[0, :, base:base + MLA_NOPE_DIM] = (qn * r * gq_nope * m_scale).astype(BF16)
        qm_ref[0, :, base + MLA_NOPE_DIM:base + MLA_HEAD_PAD] = (_rope(qr * r * gq_rope, cm, sm, 16) * m_scale).astype(BF16)

    gl = _dot(hb, wg_ref[...])
    gate_ref[0] = (1.0 / (1.0 + jnp.exp(-gl))).astype(BF16)


def _proj(x, mod3, mod_row_of_batch, tables, w, with_q, tm):
    b, l, _ = x.shape
    cg, sg, cm, sm = tables
    const = lambda shape: pl.BlockSpec(shape, lambda bi, i: (0,) * len(shape), pipeline_mode=pl.Buffered(1))
    tab = pl.BlockSpec((tm, LANES), lambda bi, i: (i, 0))
    in_specs = [
        pl.BlockSpec((1, tm, D_MODEL), lambda bi, i: (bi, i, 0)),
        pl.BlockSpec((1, 6, D_MODEL), lambda bi, i: (mod_row_of_batch(bi), 0, 0)),
        const((1, D_MODEL)), tab, tab, tab, tab,
    ]
    if with_q:
        weights = [w["wq"], w["wk"], w["wv"], w["wqa"], w["wckv"], w["wkr"], w["wg"], w["wqb"], w["wkb"], w["wvb"],
                   w["gq"], w["gk"], w["gqa"], w["gkva"], w["gmq"], w["gmk"]]
    else:
        weights = [w["wk"], w["wv"], w["wckv"], w["wkr"], w["wkb"], w["wvb"], w["gk"], w["gkva"], w["gmk"]]
    in_specs += [const(a.shape) for a in weights]

    def out(width):
        return jax.ShapeDtypeStruct((b, l, width), BF16), pl.BlockSpec((1, tm, width), lambda bi, i: (bi, i, 0))

    def out_t(width):
        return jax.ShapeDtypeStruct((b, width, l), BF16), pl.BlockSpec((1, width, tm), lambda bi, i: (bi, 0, i))

    outs = [out(GQA_KV_W), out_t(GQA_KV_W), out(MLA_HEADS * MLA_HEAD_PAD), out_t(MLA_V_W)]
    if with_q:
        outs = [out(GQA_Q_W)] + outs[:2] + [out(MLA_HEADS * MLA_HEAD_PAD)] + outs[2:] + [out(2 * D_MODEL)]
    return pl.pallas_call(
        functools.partial(_proj_kernel, with_q=with_q),
        out_shape=[o[0] for o in outs],
        grid=(b, l // tm),
        in_specs=in_specs,
        out_specs=[o[1] for o in outs],
        compiler_params=pltpu.CompilerParams(dimension_semantics=("parallel", "parallel"),
                                             vmem_limit_bytes=56 * 1024 * 1024),
        name="proj_latent" if with_q else "proj_ctx",
    )(x, mod3, w["norm_mix"], cg, sg, cm, sm, *weights)


def _attn_kernel(q_ref, kl_ref, vl_ref, kc_ref, vc_ref, o_ref, *, tk, n_lat, bounded):
    q = q_ref[0]
    tq = q.shape[0]
    dv = vl_ref.shape[1]
    n = n_lat // tk

    if bounded:
        l = jnp.zeros((1, tq), F32)
        acc = jnp.zeros((dv, tq), F32)
        for k, vt in [(kl_ref[0, j * tk:(j + 1) * tk, :], vl_ref[0, :, j * tk:(j + 1) * tk]) for j in range(n)] + [
                (kc_ref[0], vc_ref[0])]:
            p = jnp.exp2(_dot_nt(k, q))
            l = l + jnp.sum(p, axis=0, keepdims=True)
            acc = acc + _dot(vt, p.astype(BF16))
        o_ref[0] = (acc / l).T.astype(BF16)
        return

    def update(carry, s, vt):
        m, l, acc = carry
        m_new = jnp.maximum(m, jnp.max(s, axis=0, keepdims=True))
        alpha = jnp.exp2(m - m_new)
        p = jnp.exp2(s - m_new)
        l = alpha * l + jnp.sum(p, axis=0, keepdims=True)
        return m_new, l, alpha * acc + _dot(vt, p.astype(BF16))

    n = n_lat // tk
    keys = [kl_ref[0, j * tk:(j + 1) * tk, :] for j in range(n)] + [kc_ref[0]]
    vals = [vl_ref[0, :, j * tk:(j + 1) * tk] for j in range(n)] + [vc_ref[0]]
    carry = (jnp.full((1, tq), NEG_INF, F32), jnp.zeros((1, tq), F32), jnp.zeros((dv, tq), F32))
    s = _dot_nt(keys[0], q)
    for j in range(n + 1):
        s_next = _dot_nt(keys[j + 1], q) if j < n else None
        carry = update(carry, s, vals[j])
        s = s_next
    m, l, acc = carry
    o_ref[0] = (acc / l).T.astype(BF16)


SOFTMAX_SAFE_EXPONENT = 56.0


def _score_bound(gain_q, gain_k, dim):
    return dim * jnp.max(jnp.abs(gain_q)) * jnp.max(jnp.abs(gain_k)) * (dim ** -0.5 * LOG2_E) * 1.02


def _attention(q, k_lat, v_lat, k_ctx, v_ctx, n_heads, group, d_qk, dv, tq, tk, name, score_bound):
    run = functools.partial(_attention_call, q, k_lat, v_lat, k_ctx, v_ctx, n_heads, group, d_qk, dv, tq, tk, name)
    return lax.cond(score_bound <= SOFTMAX_SAFE_EXPONENT, lambda: run(True), lambda: run(False))


def _attention_call(q, k_lat, v_lat, k_ctx, v_ctx, n_heads, group, d_qk, dv, tq, tk, name, bounded):
    b, l, _ = q.shape
    lc = k_ctx.shape[1]
    name = name + ("_bounded" if bounded else "_online")
    return pl.pallas_call(
        functools.partial(_attn_kernel, tk=tk, n_lat=l, bounded=bounded),
        out_shape=jax.ShapeDtypeStruct((b, l, n_heads * dv), BF16),
        grid=(b, n_heads, l // tq),
        in_specs=[
            pl.BlockSpec((1, tq, d_qk), lambda bi, h, i: (bi, i, h)),
            pl.BlockSpec((1, l, d_qk), lambda bi, h, i: (bi, 0, h // group)),
            pl.BlockSpec((1, dv, l), lambda bi, h, i: (bi, h // group, 0)),
            pl.BlockSpec((1, lc, d_qk), lambda bi, h, i: (bi, 0, h // group)),
            pl.BlockSpec((1, dv, lc), lambda bi, h, i: (bi, h // group, 0)),
        ],
        out_specs=pl.BlockSpec((1, tq, dv), lambda bi, h, i: (bi, i, h)),
        compiler_params=pltpu.CompilerParams(dimension_semantics=("parallel", "parallel", "parallel"),
                                             vmem_limit_bytes=56 * 1024 * 1024),
        name=name,
    )(q, k_lat, v_lat, k_ctx, v_ctx)


def _merge_kernel(oa_ref, ob_ref, gate_ref, x_ref, mod_ref, woa_ref, wob_ref, wout_ref, nffn_ref,
                  rwh_ref, rwl_ref, rb_ref, x1_ref, h2_ref, eidx_ref, egate_ref):
    mod = mod_ref[0]
    g1, shift2, scale2 = mod[2:3, :], mod[3:4, :], mod[4:5, :]
    ya = _dot(oa_ref[0], woa_ref[...])
    yb = _dot(ob_ref[0], wob_ref[...])
    g = gate_ref[0].astype(F32)
    y = g[:, :D_MODEL] * ya + g[:, D_MODEL:] * yb
    z = _dot(y.astype(BF16), wout_ref[...])
    x1 = x_ref[0] + g1 * z
    x1_ref[0] = x1
    h2 = _rms(x1, nffn_ref[...], D_MODEL) * (1.0 + scale2) + shift2
    h2_ref[0] = h2.astype(BF16)

    h_hi, h_lo = _split_bf16(h2)
    logits = _dot(h_hi, rwh_ref[...]) + (_dot(h_hi, rwl_ref[...]) + _dot(h_lo, rwh_ref[...])) + rb_ref[...]
    lane = lax.broadcasted_iota(jnp.int32, logits.shape, 1).astype(F32)
    cur = jnp.where(lane < N_EXPERTS, logits, NEG_INF)
    vals, idxs = [], []
    for _ in range(TOP_K):
        mx = jnp.max(cur, axis=-1, keepdims=True)
        ix = jnp.min(jnp.where(cur == mx, lane, float(LANES)), axis=-1, keepdims=True)
        vals.append(mx)
        idxs.append(ix)
        cur = jnp.where(lane == ix, NEG_INF, cur)
    ex = [jnp.exp(v - vals[0]) for v in vals]
    den = ex[0] + ex[1] + ex[2] + ex[3]
    eidx = jnp.zeros(logits.shape, F32)
    egate = jnp.zeros(logits.shape, F32)
    for k in range(TOP_K):
        eidx = jnp.where(lane == k, idxs[k], eidx)
        egate = jnp.where(lane == k, ex[k] / den, egate)
    eidx_ref[0] = eidx.astype(jnp.int32)
    egate_ref[0] = egate


def _merge(o_a, o_b, gates, x, mod3, w, tm):
    b, l, _ = x.shape
    const = lambda shape: pl.BlockSpec(shape, lambda bi, i: (0,) * len(shape), pipeline_mode=pl.Buffered(1))
    tok = lambda width: pl.BlockSpec((1, tm, width), lambda bi, i: (bi, i, 0))
    return pl.pallas_call(
        _merge_kernel,
        out_shape=[jax.ShapeDtypeStruct((b, l, D_MODEL), F32), jax.ShapeDtypeStruct((b, l, D_MODEL), BF16),
                   jax.ShapeDtypeStruct((b, l, LANES), jnp.int32), jax.ShapeDtypeStruct((b, l, LANES), F32)],
        grid=(b, l // tm),
        in_specs=[tok(GQA_Q_W), tok(MLA_V_W), tok(2 * D_MODEL), tok(D_MODEL),
                  pl.BlockSpec((1, 6, D_MODEL), lambda bi, i: (bi, 0, 0)),
                  const((GQA_Q_W, D_MODEL)), const((MLA_V_W, D_MODEL)), const((D_MODEL, D_MODEL)), const((1, D_MODEL)),
                  const((D_MODEL, LANES)), const((D_MODEL, LANES)), const((1, LANES))],
        out_specs=[tok(D_MODEL), tok(D_MODEL), tok(LANES), tok(LANES)],
        compiler_params=pltpu.CompilerParams(dimension_semantics=("parallel", "parallel"),
                                             vmem_limit_bytes=56 * 1024 * 1024),
        name="merge_router",
    )(o_a, o_b, gates, x, mod3, w["woa"], w["wob"], w["wout"], w["norm_ffn"], w["rw_hi"], w["rw_lo"], w["rb"])


def _onehots(idx, lane):
    return [lane == idx[:, k:k + 1] for k in range(TOP_K)]


SEG_ALIGN = 8


def _route_kernel(eidx_ref, rank_ref, before_ref, tcnt_ref, cnt_ref, carry_ref):
    @pl.when(pl.program_id(0) == 0)
    def _():
        carry_ref[...] = jnp.zeros_like(carry_ref)

    before_ref[0] = carry_ref[...]
    idx = eidx_ref[...]
    tm = idx.shape[0]
    lane = lax.broadcasted_iota(jnp.int32, idx.shape, 1)
    oh = _onehots(idx, lane)
    total = jnp.zeros(idx.shape, F32)
    for k in range(TOP_K):
        total = total + jnp.where(oh[k], 1.0, 0.0)
    row = lax.broadcasted_iota(jnp.int32, (tm, tm), 0)
    col = lax.broadcasted_iota(jnp.int32, (tm, tm), 1)
    tri = jnp.where(row > col, 1.0, 0.0).astype(BF16)
    before = _dot(tri, total.astype(BF16)) + carry_ref[0:1, :]
    rank = jnp.zeros(idx.shape, F32)
    for k in range(TOP_K):
        rk = jnp.sum(jnp.where(oh[k], before, 0.0), axis=-1, keepdims=True)
        rank = jnp.where(lane == k, rk, rank)
    rank_ref[...] = rank.astype(jnp.int32)
    tile_cnt = jnp.sum(total, axis=0, keepdims=True)
    tcnt_ref[0] = jnp.broadcast_to(tile_cnt, carry_ref.shape)
    seg = jnp.floor((tile_cnt + (SEG_ALIGN - 1)) * (1.0 / SEG_ALIGN)) * SEG_ALIGN
    carry_ref[...] = carry_ref[...] + seg
    cnt_ref[...] = carry_ref[...]


def _route(eidx, tm):
    t = eidx.shape[0]
    return pl.pallas_call(
        _route_kernel,
        out_shape=[jax.ShapeDtypeStruct((t, LANES), jnp.int32), jax.ShapeDtypeStruct((t // tm, 8, LANES), F32),
                   jax.ShapeDtypeStruct((t // tm, 8, LANES), F32), jax.ShapeDtypeStruct((8, LANES), F32)],
        grid=(t // tm,),
        in_specs=[pl.BlockSpec((tm, LANES), lambda i: (i, 0))],
        out_specs=[pl.BlockSpec((tm, LANES), lambda i: (i, 0)), pl.BlockSpec((1, 8, LANES), lambda i: (i, 0, 0)),
                   pl.BlockSpec((1, 8, LANES), lambda i: (i, 0, 0)), pl.BlockSpec((8, LANES), lambda i: (0, 0))],
        scratch_shapes=[pltpu.VMEM((8, LANES), F32)],
        compiler_params=pltpu.CompilerParams(dimension_semantics=("arbitrary",)),
        name="route_rank",
    )(eidx)


def _dest_kernel(eidx_ref, rank_ref, egate_ref, start_ref, before_ref, dest_ref, col_ref, dest_t_ref, col_t_ref,
                 gate_t_ref):
    idx = eidx_ref[...]
    lane = lax.broadcasted_iota(jnp.int32, idx.shape, 1)
    oh = _onehots(idx, lane)
    start = start_ref[0:1, :]
    before = before_ref[0][0:1, :]
    rank = rank_ref[...].astype(F32)
    dest = jnp.zeros(idx.shape, F32)
    col = jnp.full(idx.shape, -1.0, F32)
    for k in range(TOP_K):
        sk = jnp.sum(jnp.where(oh[k], start, 0.0), axis=-1, keepdims=True)
        bk = jnp.sum(jnp.where(oh[k], before, 0.0), axis=-1, keepdims=True)
        rk = rank[:, k:k + 1]
        local = rk - bk
        ck = jnp.where(local < WIN, idx[:, k:k + 1].astype(F32) * WIN + local, -1.0)
        dest = jnp.where(lane == k, sk + rk, dest)
        col = jnp.where(lane == k, ck, col)
    dest_ref[...] = dest.astype(jnp.int32)
    col_ref[...] = col.astype(jnp.int32)
    dest_t_ref[0] = dest.T[0:8, :].astype(jnp.int32)
    col_t_ref[0] = col.T[0:8, :].astype(jnp.int32)
    gate_t_ref[0] = egate_ref[...].T[0:8, :]


def _dest(eidx, rank, egate, start, before, tm):
    t = eidx.shape[0]
    blk = pl.BlockSpec((tm, LANES), lambda i: (i, 0))
    blk_t = pl.BlockSpec((1, 8, tm), lambda i: (i, 0, 0))
    tok = jax.ShapeDtypeStruct((t, LANES), jnp.int32)
    tok_t = jax.ShapeDtypeStruct((t // tm, 8, tm), jnp.int32)
    return pl.pallas_call(
        _dest_kernel,
        out_shape=[tok, tok, tok_t, tok_t, jax.ShapeDtypeStruct((t // tm, 8, tm), F32)],
        grid=(t // tm,),
        in_specs=[blk, blk, blk, pl.BlockSpec((8, LANES), lambda i: (0, 0)),
                  pl.BlockSpec((1, 8, LANES), lambda i: (i, 0, 0))],
        out_specs=[blk, blk, blk_t, blk_t, blk_t],
        compiler_params=pltpu.CompilerParams(dimension_semantics=("parallel",)),
        name="route_dest",
    )(eidx, rank, egate, start, before)


MOE_TILE = 256
WIN = 64
N_MAIN_ROWS = N_EXPERTS * WIN
MAIN_CHUNK = 512
WIN_PER_STACK = MOE_TILE // WIN
MAX_WINDOWS = MOE_TILE * TOP_K // WIN
WORKLIST_LEN = -(-(MAX_WINDOWS + WIN_PER_STACK) // WIN_PER_STACK) * WIN_PER_STACK
HALF = D_MODEL // 2


def _pack_bf16_pair(v):
    return pltpu.bitcast(pltpu.pack_elementwise([v[:, HALF:], v[:, :HALF]], packed_dtype=BF16), jnp.uint32)


def _unpack_bf16_pair(u):
    hi = pltpu.unpack_elementwise(u, index=1, packed_dtype=BF16, unpacked_dtype=F32).astype(BF16)
    lo = pltpu.unpack_elementwise(u, index=0, packed_dtype=BF16, unpacked_dtype=F32).astype(BF16)
    return hi, lo


def _build_worklist(ts_ref, tc_ref, wl_ref, junk_row):
    base = pl.program_id(0) * N_EXPERTS

    def per_expert(e, n):
        first = ts_ref[base + e]
        n_win = (tc_ref[base + e] + (WIN - 1)) // WIN

        def per_window(wi, n):
            wl_ref[n] = first + wi * WIN
            return n + 1

        return lax.fori_loop(1, n_win, per_window, n)

    n = lax.fori_loop(0, N_EXPERTS, per_expert, 0)
    n_stacks = (n + (WIN_PER_STACK - 1)) // WIN_PER_STACK

    def pad(j, c):
        wl_ref[j] = junk_row + (j % WIN_PER_STACK) * WIN
        return c

    lax.fori_loop(n, n_stacks * WIN_PER_STACK, pad, 0)
    return n_stacks


def _seg_aligned(row):
    return row if isinstance(row, int) else pl.multiple_of(row, SEG_ALIGN)


def _stack_row_ids(wl_ref, stack, shape, axis):
    pos = lax.broadcasted_iota(jnp.int32, shape, axis)
    last = WIN_PER_STACK - 1
    row = wl_ref[stack * WIN_PER_STACK + last] + (pos - last * WIN)
    for wi in range(last - 1, -1, -1):
        row = jnp.where(pos < (wi + 1) * WIN, wl_ref[stack * WIN_PER_STACK + wi] + (pos - wi * WIN), row)
    return row


def _select_rows(row_id, id_t, gate_t):
    shape = (row_id.shape[0], id_t.shape[1])
    sel = jnp.zeros(shape, F32)
    gsel = jnp.zeros(shape, F32)
    for k in range(TOP_K):
        hit = row_id == id_t[k:k + 1, :]
        sel = jnp.where(hit, 1.0, sel)
        gsel = jnp.where(hit, gate_t[k:k + 1, :], gsel)
    return sel.astype(BF16), jnp.sum(gsel, axis=1, keepdims=True)


MAIN_PENDING = 2


TAIL_ROWS = WIN + EXPERT_BLOCK


def _dispatch_kernel(ts_ref, tc_ref, ov_ref, re_ref, h_ref, col_t_ref, dest_t_ref, gate_t_ref,
                     buf_ref, gs_ref, wl_ref, pend_ref, main_x, main_g, stage_x, stage_g, main_sem, sem, *, junk_row):
    i = pl.program_id(0)
    base = i * N_EXPERTS

    @pl.when(i == 0)
    def _():
        pend_ref[0] = 0
        pend_ref[1] = 0
        pend_ref[MAIN_PENDING] = 0
        main_x[0:TAIL_ROWS, :] = jnp.zeros((TAIL_ROWS, HALF), jnp.uint32)
        main_g[0:TAIL_ROWS, :] = jnp.zeros((TAIL_ROWS, LANES), F32)
        cps = []
        for e in range(N_EXPERTS):
            rows = pl.ds(pl.multiple_of(jnp.maximum(re_ref[e] - TAIL_ROWS, 0), SEG_ALIGN), TAIL_ROWS)
            cps.append(pltpu.make_async_copy(main_x.at[pl.ds(0, TAIL_ROWS)], buf_ref.at[rows], main_sem))
            cps.append(pltpu.make_async_copy(main_g.at[pl.ds(0, TAIL_ROWS)], gs_ref.at[rows], main_sem))
        for cp in cps:
            cp.start()
        for cp in cps:
            cp.wait()

        def zero_block(blk, c):
            rows = pl.ds(pl.multiple_of(blk * EXPERT_BLOCK, EXPERT_BLOCK), EXPERT_BLOCK)
            cx = pltpu.make_async_copy(main_x.at[pl.ds(0, EXPERT_BLOCK)], buf_ref.at[rows], main_sem)
            cg = pltpu.make_async_copy(main_g.at[pl.ds(0, EXPERT_BLOCK)], gs_ref.at[rows], main_sem)
            cx.start()
            cg.start()
            cx.wait()
            cg.wait()
            return c

        lax.fori_loop(re_ref[N_EXPERTS - 1] // EXPERT_BLOCK, (junk_row + EXPERT_BLOCK) // EXPERT_BLOCK, zero_block, 0)

    gate_t = gate_t_ref[0]
    h = h_ref[...]

    def main_copies(first_row):
        cps = []
        for e in range(N_EXPERTS):
            rows = pl.ds(_seg_aligned(first_row(e)), WIN)
            cps.append(pltpu.make_async_copy(main_x.at[pl.ds(e * WIN, WIN)], buf_ref.at[rows], main_sem))
            cps.append(pltpu.make_async_copy(main_g.at[pl.ds(e * WIN, WIN)], gs_ref.at[rows], main_sem))
        return cps

    def drain_main():
        @pl.when(pend_ref[MAIN_PENDING] == 1)
        def _():
            for cp in main_copies(lambda e: 0):
                cp.wait()
            pend_ref[MAIN_PENDING] = 0

    col_t = col_t_ref[0]
    pieces = []
    for c in range(N_MAIN_ROWS // MAIN_CHUNK):
        row_id = lax.broadcasted_iota(jnp.int32, (MAIN_CHUNK, 1), 0) + c * MAIN_CHUNK
        sel, gate_row = _select_rows(row_id, col_t, gate_t)
        pieces.append((_pack_bf16_pair(_dot(sel, h)), gate_row))

    def window_copies(slot, first_row):
        cps = []
        for wi in range(WIN_PER_STACK):
            rows = pl.ds(_seg_aligned(first_row(wi)), WIN)
            cps.append(pltpu.make_async_copy(stage_x.at[slot, pl.ds(wi * WIN, WIN)], buf_ref.at[rows], sem.at[slot]))
            cps.append(pltpu.make_async_copy(stage_g.at[slot, pl.ds(wi * WIN, WIN)], gs_ref.at[rows], sem.at[slot]))
        return cps

    def drain(slot):
        @pl.when(pend_ref[slot] == 1)
        def _():
            for cp in window_copies(slot, lambda wi: 0):
                cp.wait()
            pend_ref[slot] = 0

    drain_main()
    drain(0)
    drain(1)
    for c, (packed, gate_row) in enumerate(pieces):
        main_x[c * MAIN_CHUNK:(c + 1) * MAIN_CHUNK, :] = packed
        main_g[c * MAIN_CHUNK:(c + 1) * MAIN_CHUNK, :] = jnp.broadcast_to(gate_row, (MAIN_CHUNK, LANES))
    for cp in main_copies(lambda e: ts_ref[base + e]):
        cp.start()
    pend_ref[MAIN_PENDING] = 1

    @pl.when(ov_ref[i] > 0)
    def _():
        n_stacks = _build_worklist(ts_ref, tc_ref, wl_ref, junk_row)
        dest_t = dest_t_ref[0]

        def stack_body(s, c):
            slot = s & 1
            sel, gate_row = _select_rows(_stack_row_ids(wl_ref, s, (MOE_TILE, 1), 0), dest_t, gate_t)
            rows = _dot(sel, h)
            drain(slot)
            stage_x[slot] = _pack_bf16_pair(rows)
            stage_g[slot] = jnp.broadcast_to(gate_row, (MOE_TILE, LANES))
            for cp in window_copies(slot, lambda wi: wl_ref[s * WIN_PER_STACK + wi]):
                cp.start()
            pend_ref[slot] = 1
            return c

        lax.fori_loop(0, n_stacks, stack_body, 0)

    @pl.when(i == pl.num_programs(0) - 1)
    def _():
        drain_main()
        drain(0)
        drain(1)


def _moe_grid_spec(n_tiles, in_specs, out_specs, scratch_shapes):
    return pltpu.PrefetchScalarGridSpec(num_scalar_prefetch=4, grid=(n_tiles,), in_specs=in_specs,
                                        out_specs=out_specs, scratch_shapes=scratch_shapes)


def _dispatch(tile_start, tile_cnt, tile_ovf, region_ends, h2, col_t, dest_t, gate_t, n_rows):
    n_tiles = h2.shape[0] // MOE_TILE
    junk_row = n_rows - EXPERT_BLOCK
    any_spec = pl.BlockSpec(memory_space=pl.ANY)
    tok_t = pl.BlockSpec((1, 8, MOE_TILE), lambda i, *_: (i, 0, 0))
    return pl.pallas_call(
        functools.partial(_dispatch_kernel, junk_row=junk_row),
        out_shape=[jax.ShapeDtypeStruct((n_rows, HALF), jnp.uint32), jax.ShapeDtypeStruct((n_rows, LANES), F32)],
        grid_spec=_moe_grid_spec(
            n_tiles,
            [pl.BlockSpec((MOE_TILE, D_MODEL), lambda i, *_: (i, 0)), tok_t, tok_t, tok_t],
            [any_spec, any_spec],
            [pltpu.SMEM((WORKLIST_LEN,), jnp.int32), pltpu.SMEM((3,), jnp.int32),
             pltpu.VMEM((N_MAIN_ROWS, HALF), jnp.uint32), pltpu.VMEM((N_MAIN_ROWS, LANES), F32),
             pltpu.VMEM((2, MOE_TILE, HALF), jnp.uint32), pltpu.VMEM((2, MOE_TILE, LANES), F32),
             pltpu.SemaphoreType.DMA(()), pltpu.SemaphoreType.DMA((2,))]),
        compiler_params=pltpu.CompilerParams(dimension_semantics=("arbitrary",),
                                             vmem_limit_bytes=56 * 1024 * 1024),
        name="moe_dispatch",
    )(tile_start, tile_cnt, tile_ovf, region_ends, h2, col_t, dest_t, gate_t)


def _select_cols(ids, col_id):
    sel = jnp.zeros((ids.shape[0], col_id.shape[1]), F32)
    for k in range(TOP_K):
        sel = jnp.where(ids[:, k:k + 1] == col_id, 1.0, sel)
    return sel.astype(BF16)


def _combine_kernel(ts_ref, tc_ref, ov_ref, re_ref, col_ref, dest_ref, x1_ref, mod_ref, y_ref, o_ref,
                    wl_ref, main_y, stage_y, acc_ref, main_sem, sem, *, junk_row):
    del re_ref
    i = pl.program_id(0)
    slot_i = i & 1

    def main_copies(slot, first_row):
        return [pltpu.make_async_copy(y_ref.at[pl.ds(_seg_aligned(first_row(e)), WIN)],
                                      main_y.at[slot, pl.ds(e * WIN, WIN)], main_sem.at[slot])
                for e in range(N_EXPERTS)]

    def fetch_main(tile, slot):
        for cp in main_copies(slot, lambda e: ts_ref[tile * N_EXPERTS + e]):
            cp.start()

    @pl.when(i == 0)
    def _():
        fetch_main(0, 0)

    @pl.when(i + 1 < pl.num_programs(0))
    def _():
        fetch_main(i + 1, 1 - slot_i)

    for cp in main_copies(slot_i, lambda e: 0):
        cp.wait()

    col = col_ref[...]
    sel = jnp.concatenate(
        [_select_cols(col, lax.broadcasted_iota(jnp.int32, (1, MAIN_CHUNK), 1) + c * MAIN_CHUNK)
         for c in range(N_MAIN_ROWS // MAIN_CHUNK)], axis=1)
    y_hi, y_lo = _unpack_bf16_pair(main_y[slot_i])
    acc_ref[:, :HALF] = _dot(sel, y_hi)
    acc_ref[:, HALF:] = _dot(sel, y_lo)

    @pl.when(ov_ref[i] > 0)
    def _():
        n_stacks = _build_worklist(ts_ref, tc_ref, wl_ref, junk_row)
        dest = dest_ref[...]

        def window_copies(slot, first_row):
            return [pltpu.make_async_copy(y_ref.at[pl.ds(_seg_aligned(first_row(wi)), WIN)],
                                          stage_y.at[slot, pl.ds(wi * WIN, WIN)], sem.at[slot])
                    for wi in range(WIN_PER_STACK)]

        def fetch(s, slot):
            for cp in window_copies(slot, lambda wi: wl_ref[s * WIN_PER_STACK + wi]):
                cp.start()

        @pl.when(n_stacks > 0)
        def _():
            fetch(0, 0)

        def stack_body(s, c):
            slot = s & 1

            @pl.when(s + 1 < n_stacks)
            def _():
                fetch(s + 1, 1 - slot)

            for cp in window_copies(slot, lambda wi: 0):
                cp.wait()
            sel_o = _select_cols(dest, _stack_row_ids(wl_ref, s, (1, MOE_TILE), 1))
            o_hi, o_lo = _unpack_bf16_pair(stage_y[slot])
            acc_ref[:, :HALF] += _dot(sel_o, o_hi)
            acc_ref[:, HALF:] += _dot(sel_o, o_lo)
            return c

        lax.fori_loop(0, n_stacks, stack_body, 0)

    g2 = mod_ref[0][5:6, :]
    o_ref[...] = x1_ref[...] + g2 * acc_ref[...]


def _combine(tile_start, tile_cnt, tile_ovf, region_ends, col, dest, x1, mod3, y, l):
    t = x1.shape[0]
    per_batch = l // MOE_TILE
    junk_row = y.shape[0] - EXPERT_BLOCK
    tok = lambda width: pl.BlockSpec((MOE_TILE, width), lambda i, *_: (i, 0))
    return pl.pallas_call(
        functools.partial(_combine_kernel, junk_row=junk_row),
        out_shape=jax.ShapeDtypeStruct((t, D_MODEL), F32),
        grid_spec=_moe_grid_spec(
            t // MOE_TILE,
            [tok(LANES), tok(LANES), tok(D_MODEL),
             pl.BlockSpec((1, 6, D_MODEL), lambda i, *_: (i // per_batch, 0, 0)),
             pl.BlockSpec(memory_space=pl.ANY)],
            tok(D_MODEL),
            [pltpu.SMEM((WORKLIST_LEN,), jnp.int32), pltpu.VMEM((2, N_MAIN_ROWS, HALF), jnp.uint32),
             pltpu.VMEM((2, MOE_TILE, HALF), jnp.uint32), pltpu.VMEM((MOE_TILE, D_MODEL), F32),
             pltpu.SemaphoreType.DMA((2,)), pltpu.SemaphoreType.DMA((2,))]),
        compiler_params=pltpu.CompilerParams(dimension_semantics=("arbitrary",),
                                             vmem_limit_bytes=56 * 1024 * 1024),
        name="moe_combine",
    )(tile_start, tile_cnt, tile_ovf, region_ends, col, dest, x1, mod3, y)


def _expert_kernel(be_ref, nused_ref, x_ref, g_ref, w1_ref, b1_ref, w2_ref, b2_ref, y_ref, w1b_ref, w2b_ref):
    i = pl.program_id(0)
    live = i < nused_ref[0]

    @pl.when(live & ((i == 0) | (be_ref[i] != be_ref[jnp.maximum(i - 1, 0)])))
    def _():
        w1b_ref[...] = w1_ref[0].astype(BF16)
        w2b_ref[...] = w2_ref[0].astype(BF16)

    @pl.when(live)
    def _():
        x_hi, x_lo = _unpack_bf16_pair(x_ref[...])
        gu = _dot(x_hi, w1b_ref[:HALF, :]) + _dot(x_lo, w1b_ref[HALF:, :]) + b1_ref[0]
        glu = jnp.minimum(gu[:, :D_EXPERT], SWIGLU_LIMIT)
        lin = jnp.clip(gu[:, D_EXPERT:], -SWIGLU_LIMIT, SWIGLU_LIMIT)
        act = glu * (1.0 / (1.0 + jnp.exp(-SWIGLU_ALPHA * glu))) * (lin + 1.0)
        y = (_dot(act.astype(BF16), w2b_ref[...]) + b2_ref[0]) * g_ref[:, 0:1]
        y_ref[...] = _pack_bf16_pair(y)

    @pl.when(pl.program_id(0) >= nused_ref[0])
    def _():
        y_ref[...] = jnp.zeros_like(y_ref)


def _experts(block_e, nused, buf, gs, w1, b1, w2, b2):
    nb = buf.shape[0] // EXPERT_BLOCK
    row = lambda i, be, nu: (jnp.minimum(i, nu[0] - 1), 0)
    out_row = lambda i, be, nu: (i, 0)
    exp3 = lambda i, be, nu: (be[jnp.minimum(i, nu[0] - 1)], 0, 0)
    return pl.pallas_call(
        _expert_kernel,
        out_shape=jax.ShapeDtypeStruct((buf.shape[0], HALF), jnp.uint32),
        grid_spec=pltpu.PrefetchScalarGridSpec(
            num_scalar_prefetch=2,
            grid=(nb,),
            in_specs=[pl.BlockSpec((EXPERT_BLOCK, HALF), row),
                      pl.BlockSpec((EXPERT_BLOCK, LANES), row),
                      pl.BlockSpec((1, D_MODEL, 2 * D_EXPERT), exp3),
                      pl.BlockSpec((1, 1, 2 * D_EXPERT), exp3),
                      pl.BlockSpec((1, D_EXPERT, D_MODEL), exp3),
                      pl.BlockSpec((1, 1, D_MODEL), exp3)],
            out_specs=pl.BlockSpec((EXPERT_BLOCK, HALF), out_row),
            scratch_shapes=[pltpu.VMEM((D_MODEL, 2 * D_EXPERT), BF16), pltpu.VMEM((D_EXPERT, D_MODEL), BF16)]),
        compiler_params=pltpu.CompilerParams(dimension_semantics=("arbitrary",),
                                             vmem_limit_bytes=56 * 1024 * 1024),
        name="moe_experts",
    )(block_e, nused, buf, gs, w1, b1, w2, b2)


def _rope_tables(seq_len):
    pos = jnp.arange(seq_len, dtype=jnp.int32)
    row = (pos // GRID_W).astype(F32)[:, None]
    col = (pos % GRID_W).astype(F32)[:, None]

    def table(d_axis, pad):
        inv_freq = ROPE_THETA ** (-jnp.arange(0, d_axis, 2, dtype=F32) / d_axis)
        ar, ac = row * inv_freq[None, :], col * inv_freq[None, :]
        cos = jnp.concatenate([jnp.cos(ar), jnp.cos(ar), jnp.cos(ac), jnp.cos(ac)], axis=-1)
        sin = jnp.concatenate([-jnp.sin(ar), jnp.sin(ar), -jnp.sin(ac), jnp.sin(ac)], axis=-1)
        if pad:
            cos = jnp.concatenate([cos, jnp.ones((seq_len, pad), F32)], axis=-1)
            sin = jnp.concatenate([sin, jnp.zeros((seq_len, pad), F32)], axis=-1)
        return cos, sin

    cg, sg = table(GQA_HEAD_DIM // 2, 0)
    cm, sm = table(MLA_ROPE_DIM // 2, LANES - MLA_ROPE_DIM)
    return cg, sg, cm, sm


def _pad_lanes(a, width):
    return jnp.pad(a, [(0, 0)] * (a.ndim - 1) + [(0, width - a.shape[-1])])


def _prep_weights(p):
    w_in = p["w_in"]
    w = {"norm_mix": p["norm_mix"].reshape(1, D_MODEL), "norm_ffn": p["norm_ffn"].reshape(1, D_MODEL)}
    w["wq"] = w_in[:, :OFF_GQA_K].astype(BF16)
    w["wk"] = w_in[:, OFF_GQA_K:OFF_GQA_V].astype(BF16)
    w["wv"] = w_in[:, OFF_GQA_V:OFF_MLA_QA].T.astype(BF16)
    w["wqa"] = w_in[:, OFF_MLA_QA:OFF_MLA_KVA].astype(BF16)
    w["wckv"] = w_in[:, OFF_MLA_KVA:OFF_MLA_KVA + MLA_KV_RANK].astype(BF16)
    w["wkr"] = _pad_lanes(w_in[:, OFF_MLA_KVA + MLA_KV_RANK:OFF_GATE], LANES).astype(BF16)
    w["wg"] = w_in[:, OFF_GATE:].astype(BF16)
    wqb = p["mla_w_qb"].reshape(MLA_Q_RANK, MLA_HEADS, MLA_QK_DIM)
    w["wqb"] = _pad_lanes(wqb, MLA_HEAD_PAD).reshape(MLA_Q_RANK, MLA_HEADS * MLA_HEAD_PAD).astype(BF16)
    wkvb = p["mla_w_kvb"].reshape(MLA_KV_RANK, MLA_HEADS, MLA_NOPE_DIM + MLA_V_DIM)
    w["wkb"] = wkvb[:, :, :MLA_NOPE_DIM].reshape(MLA_KV_RANK, MLA_HEADS * MLA_NOPE_DIM).astype(BF16)
    w["wvb"] = wkvb[:, :, MLA_NOPE_DIM:].reshape(MLA_KV_RANK, MLA_V_W).T.astype(BF16)
    w["gq"] = p["gqa_q_norm"].reshape(1, GQA_HEAD_DIM)
    w["gk"] = p["gqa_k_norm"].reshape(1, GQA_HEAD_DIM)
    w["gqa"] = p["mla_q_a_norm"].reshape(1, MLA_Q_RANK)
    w["gkva"] = p["mla_kv_a_norm"].reshape(1, MLA_KV_RANK)
    w["gmq"] = _pad_lanes(p["mla_q_norm"].reshape(1, MLA_QK_DIM), MLA_HEAD_PAD)
    w["gmk"] = _pad_lanes(p["mla_k_norm"].reshape(1, MLA_QK_DIM), MLA_HEAD_PAD)
    w["woa"] = p["w_o_gqa"].astype(BF16)
    w["wob"] = p["w_o_mla"].astype(BF16)
    w["wout"] = p["w_out"].astype(BF16)
    rw = _pad_lanes(p["router_w"], LANES)
    w["rw_hi"] = rw.astype(BF16)
    w["rw_lo"] = (rw - w["rw_hi"].astype(F32)).astype(BF16)
    w["rb"] = _pad_lanes(p["router_b"].reshape(1, N_EXPERTS), LANES)
    return w


def _layer(x, c, ctx, c_ctx, p):
    b, l, _ = x.shape
    lc = ctx.shape[1]
    t = b * l
    w = _prep_weights(p)

    n_mod_rows = -(-(b + 1) // 8) * 8
    cc = jnp.zeros((n_mod_rows, D_MODEL), F32).at[:b].set(c).at[b].set(c_ctx)
    mod3 = _ada_mod(cc, p["ada_w"], p["ada_b"]).reshape(n_mod_rows, 6, D_MODEL)

    tables = _rope_tables(l)
    ident = (jnp.ones((lc, LANES), F32), jnp.zeros((lc, LANES), F32)) * 2
    tm = min(512, l)
    q_a, k_a, v_a, q_m, k_m, v_m, gates = _proj(x, mod3, lambda bi: bi, tables, w, True, tm)
    kc_a, vc_a, kc_m, vc_m = _proj(ctx, mod3, lambda bi: b, ident, w, False, min(256, lc))

    tq, tk = min(1024, l), min(1024, l)
    o_a = _attention(q_a, k_a, v_a, kc_a, vc_a, GQA_HEADS, GQA_GROUP, GQA_HEAD_DIM, GQA_HEAD_DIM, tq, tk, "attn_gqa",
                     _score_bound(p["gqa_q_norm"], p["gqa_k_norm"], GQA_HEAD_DIM))
    o_m = _attention(q_m, k_m, v_m, kc_m, vc_m, MLA_HEADS, 1, MLA_HEAD_PAD, MLA_V_DIM, tq, tk, "attn_mla",
                     _score_bound(p["mla_q_norm"], p["mla_k_norm"], MLA_QK_DIM))

    x1, h2, eidx, egate = _merge(o_a, o_m, gates, x, mod3, w, tm)
    x1, h2 = x1.reshape(t, D_MODEL), h2.reshape(t, D_MODEL)
    eidx, egate = eidx.reshape(t, LANES), egate.reshape(t, LANES)

    assert t % MOE_TILE == 0
    n_tiles = t // MOE_TILE
    rank, before_raw, tcnt, cnt = _route(eidx, MOE_TILE)
    seg_rows = cnt[0, :N_EXPERTS].astype(jnp.int32)
    before = before_raw[:, 0, :N_EXPERTS].astype(jnp.int32)
    region = (seg_rows + WIN + EXPERT_BLOCK - 1) // EXPERT_BLOCK * EXPERT_BLOCK
    region_ends = jnp.cumsum(region)
    region_starts = region_ends - region
    max_rows = t * TOP_K + n_tiles * N_EXPERTS * (SEG_ALIGN - 1) + N_EXPERTS * (WIN + EXPERT_BLOCK - 1)
    n_blocks = max_rows // EXPERT_BLOCK + 1
    n_rows = n_blocks * EXPERT_BLOCK
    block_row = jnp.arange(n_blocks, dtype=jnp.int32) * EXPERT_BLOCK
    block_e = jnp.minimum(jnp.sum(region_ends[None, :] <= block_row[:, None], axis=1), N_EXPERTS - 1).astype(jnp.int32)
    nused = (region_ends[-1:] // EXPERT_BLOCK).astype(jnp.int32)
    tile_start = (region_starts[None, :] + before).reshape(-1)
    tile_cnt = tcnt[:, 0, :N_EXPERTS].astype(jnp.int32)
    tile_ovf = jnp.sum(jnp.maximum((tile_cnt + WIN - 1) // WIN - 1, 0), axis=1).astype(jnp.int32)
    tile_cnt = tile_cnt.reshape(-1)
    start = jnp.zeros((8, LANES), F32).at[0, :N_EXPERTS].set(region_starts.astype(F32))
    dest, col, dest_t, col_t, gate_t = _dest(eidx, rank, egate, start, before_raw, MOE_TILE)

    region_ends = region_ends.astype(jnp.int32)
    buf, gs = _dispatch(tile_start, tile_cnt, tile_ovf, region_ends, h2, col_t, dest_t, gate_t, n_rows)
    y = _experts(block_e, nused, buf, gs, p["expert_w1"], p["expert_b1"].reshape(N_EXPERTS, 1, -1),
                 p["expert_w2"], p["expert_b2"].reshape(N_EXPERTS, 1, -1))
    out = _combine(tile_start, tile_cnt, tile_ovf, region_ends, col, dest, x1, mod3, y, l)
    return out.reshape(b, l, D_MODEL)


def kernel(x, c, ctx, c_ctx, ada_w, ada_b, norm_mix, norm_ffn, w_in, gqa_q_norm, gqa_k_norm, mla_q_a_norm, mla_kv_a_norm, mla_w_qb, mla_w_kvb, mla_q_norm, mla_k_norm, w_o_gqa, w_o_mla, w_out, router_w, router_b, expert_w1, expert_b1, expert_w2, expert_b2):
    assert ada_w.shape[0] == 1, "single-layer problem: the context stream is never updated"
    p = {
        "ada_w": ada_w[0], "ada_b": ada_b[0], "norm_mix": norm_mix[0], "norm_ffn": norm_ffn[0],
        "w_in": w_in[0], "gqa_q_norm": gqa_q_norm[0], "gqa_k_norm": gqa_k_norm[0],
        "mla_q_a_norm": mla_q_a_norm[0], "mla_kv_a_norm": mla_kv_a_norm[0],
        "mla_w_qb": mla_w_qb[0], "mla_w_kvb": mla_w_kvb[0],
        "mla_q_norm": mla_q_norm[0], "mla_k_norm": mla_k_norm[0],
        "w_o_gqa": w_o_gqa[0], "w_o_mla": w_o_mla[0], "w_out": w_out[0],
        "router_w": router_w[0], "router_b": router_b[0],
        "expert_w1": expert_w1[0], "expert_b1": expert_b1[0],
        "expert_w2": expert_w2[0], "expert_b2": expert_b2[0],
    }
    return _layer(x, c, ctx, c_ctx, p)
```

```python
import functools
import math

import jax
import jax.numpy as jnp
from jax import lax
from jax.experimental import pallas as pl
from jax.experimental.pallas import tpu as pltpu

D_MODEL = 1024
GRID_W = 64
EPS = 1e-6
ROPE_THETA = 10000.0

GQA_HEADS = 8
GQA_KV_HEADS = 2
GQA_GROUP = GQA_HEADS // GQA_KV_HEADS
GQA_HEAD_DIM = 128
GQA_Q_W = GQA_HEADS * GQA_HEAD_DIM
GQA_KV_W = GQA_KV_HEADS * GQA_HEAD_DIM

MLA_HEADS = 8
MLA_Q_RANK = 256
MLA_KV_RANK = 128
MLA_NOPE_DIM = 128
MLA_ROPE_DIM = 64
MLA_V_DIM = 128
MLA_QK_DIM = MLA_NOPE_DIM + MLA_ROPE_DIM
MLA_HEAD_PAD = 256
MLA_V_W = MLA_HEADS * MLA_V_DIM

OFF_GQA_K = GQA_Q_W
OFF_GQA_V = OFF_GQA_K + GQA_KV_W
OFF_MLA_QA = OFF_GQA_V + GQA_KV_W
OFF_MLA_KVA = OFF_MLA_QA + MLA_Q_RANK
OFF_GATE = OFF_MLA_KVA + MLA_KV_RANK + MLA_ROPE_DIM

N_EXPERTS = 32
TOP_K = 4
D_EXPERT = 1024
SWIGLU_LIMIT = 7.0
SWIGLU_ALPHA = 1.702
EXPERT_BLOCK = 512

LANES = 128
NEG_INF = float("-inf")
LOG2_E = math.log2(math.e)

BF16 = jnp.bfloat16
F32 = jnp.float32


def _dot(a, b):
    return jnp.dot(a, b, preferred_element_type=F32)


def _dot_nt(a, b):
    return lax.dot_general(a, b, (((1,), (1,)), ((), ())), preferred_element_type=F32)


def _split_bf16(a):
    hi = a.astype(BF16)
    lo = (a - hi.astype(F32)).astype(BF16)
    return hi, lo


def _rms(x, g, n):
    ms = jnp.sum(x * x, axis=-1, keepdims=True) * (1.0 / n)
    return x * lax.rsqrt(ms + EPS) * g


def _swap_halves(x, k):
    lane = lax.broadcasted_iota(jnp.int32, x.shape, 1)
    return jnp.where((lane & k) != 0, pltpu.roll(x, k, 1), pltpu.roll(x, LANES - k, 1))


def _rope(x, cos, sin_signed, k):
    return x * cos + _swap_halves(x, k) * sin_signed


def _ada_kernel(c_ref, w_ref, b_ref, o_ref):
    c = c_ref[...]
    s = c * (1.0 / (1.0 + jnp.exp(-c)))
    s_hi, s_lo = _split_bf16(s)
    w_hi, w_lo = _split_bf16(w_ref[...])
    o_ref[...] = _dot(s_hi, w_hi) + (_dot(s_hi, w_lo) + _dot(s_lo, w_hi)) + b_ref[...]


def _ada_mod(cc, ada_w, ada_b):
    n = ada_w.shape[1]
    tn = 1024
    return pl.pallas_call(
        _ada_kernel,
        out_shape=jax.ShapeDtypeStruct((cc.shape[0], n), F32),
        grid=(n // tn,),
        in_specs=[
            pl.BlockSpec((cc.shape[0], D_MODEL), lambda j: (0, 0)),
            pl.BlockSpec((D_MODEL, tn), lambda j: (0, j)),
            pl.BlockSpec((1, tn), lambda j: (0, j)),
        ],
        out_specs=pl.BlockSpec((cc.shape[0], tn), lambda j: (0, j)),
        compiler_params=pltpu.CompilerParams(dimension_semantics=("parallel",)),
        name="ada_mod",
    )(cc, ada_w, ada_b.reshape(1, n))


def _proj_kernel(*refs, with_q):
    if with_q:
        (x_ref, mod_ref, nmix_ref, cg_ref, sg_ref, cm_ref, sm_ref,
         wq_ref, wk_ref, wv_ref, wqa_ref, wckv_ref, wkr_ref, wg_ref, wqb_ref, wkb_ref, wvb_ref,
         gq_ref, gk_ref, gqa_ref, gkva_ref, gmq_ref, gmk_ref,
         q_ref, k_ref, v_ref, qm_ref, km_ref, vm_ref, gate_ref) = refs
    else:
        (x_ref, mod_ref, nmix_ref, cg_ref, sg_ref, cm_ref, sm_ref,
         wk_ref, wv_ref, wckv_ref, wkr_ref, wkb_ref, wvb_ref,
         gk_ref, gkva_ref, gmk_ref,
         k_ref, v_ref, km_ref, vm_ref) = refs

    x = x_ref[0]
    mod = mod_ref[0]
    shift, scale = mod[0:1, :], mod[1:2, :]
    h = _rms(x, nmix_ref[...], D_MODEL) * (1.0 + scale) + shift
    hb = h.astype(BF16)
    cg, sg = cg_ref[...], sg_ref[...]
    cm, sm = cm_ref[...], sm_ref[...]

    kk = _dot(hb, wk_ref[...])
    for j in range(GQA_KV_HEADS):
        sl = slice(j * GQA_HEAD_DIM, (j + 1) * GQA_HEAD_DIM)
        kn = _rms(kk[:, sl], gk_ref[...], GQA_HEAD_DIM)
        k_ref[0, :, sl] = _rope(kn, cg, sg, 32).astype(BF16)
    v_ref[0] = _dot_nt(wv_ref[...], hb).astype(BF16)

    ckv = _rms(_dot(hb, wckv_ref[...]), gkva_ref[...], MLA_KV_RANK).astype(BF16)
    vm_ref[0] = _dot_nt(wvb_ref[...], ckv).astype(BF16)
    knope = _dot(ckv, wkb_ref[...])
    kr = _dot(hb, wkr_ref[...])
    gmk = gmk_ref[...]
    g_nope, g_rope = gmk[:, :MLA_NOPE_DIM], gmk[:, MLA_NOPE_DIM:]
    ssq_r = jnp.sum(kr * kr, axis=-1, keepdims=True)
    kr_roped = _rope(kr * g_rope, cm, sm, 16)
    for j in range(MLA_HEADS):
        kn = knope[:, j * MLA_NOPE_DIM:(j + 1) * MLA_NOPE_DIM]
        ms = (jnp.sum(kn * kn, axis=-1, keepdims=True) + ssq_r) * (1.0 / MLA_QK_DIM)
        r = lax.rsqrt(ms + EPS)
        base = j * MLA_HEAD_PAD
        km_ref[0, :, base:base + MLA_NOPE_DIM] = (kn * r * g_nope).astype(BF16)
        km_ref[0, :, base + MLA_NOPE_DIM:base + MLA_HEAD_PAD] = (kr_roped * r).astype(BF16)

    if not with_q:
        return

    qq = _dot(hb, wq_ref[...])
    q_scale = GQA_HEAD_DIM ** -0.5 * LOG2_E
    for j in range(GQA_HEADS):
        sl = slice(j * GQA_HEAD_DIM, (j + 1) * GQA_HEAD_DIM)
        qn = _rms(qq[:, sl], gq_ref[...], GQA_HEAD_DIM)
        q_ref[0, :, sl] = (_rope(qn, cg, sg, 32) * q_scale).astype(BF16)

    qa = _rms(_dot(hb, wqa_ref[...]), gqa_ref[...], MLA_Q_RANK).astype(BF16)
    q2 = _dot(qa, wqb_ref[...])
    gmq = gmq_ref[...]
    gq_nope, gq_rope = gmq[:, :MLA_NOPE_DIM], gmq[:, MLA_NOPE_DIM:]
    m_scale = MLA_QK_DIM ** -0.5 * LOG2_E
    for j in range(MLA_HEADS):
        base = j * MLA_HEAD_PAD
        qn = q2[:, base:base + MLA_NOPE_DIM]
        qr = q2[:, base + MLA_NOPE_DIM:base + MLA_HEAD_PAD]
        ms = (jnp.sum(qn * qn, axis=-1, keepdims=True) + jnp.sum(qr * qr, axis=-1, keepdims=True)) * (1.0 / MLA_QK_DIM)
        r = lax.rsqrt(ms + EPS)
        qm_ref[0, :, base:base + MLA_NOPE_DIM] = (qn * r * gq_nope * m_scale).astype(BF16)
        qm_ref[0, :, base + MLA_NOPE_DIM:base + MLA_HEAD_PAD] = (_rope(qr * r * gq_rope, cm, sm, 16) * m_scale).astype(BF16)

    gl = _dot(hb, wg_ref[...])
    gate_ref[0] = (1.0 / (1.0 + jnp.exp(-gl))).astype(BF16)


def _proj(x, mod3, mod_row_of_batch, tables, w, with_q, tm):
    b, l, _ = x.shape
    cg, sg, cm, sm = tables
    const = lambda shape: pl.BlockSpec(shape, lambda bi, i: (0,) * len(shape), pipeline_mode=pl.Buffered(1))
    tab = pl.BlockSpec((tm, LANES), lambda bi, i: (i, 0))
    in_specs = [
        pl.BlockSpec((1, tm, D_MODEL), lambda bi, i: (bi, i, 0)),
        pl.BlockSpec((1, 6, D_MODEL), lambda bi, i: (mod_row_of_batch(bi), 0, 0)),
        const((1, D_MODEL)), tab, tab, tab, tab,
    ]
    if with_q:
        weights = [w["wq"], w["wk"], w["wv"], w["wqa"], w["wckv"], w["wkr"], w["wg"], w["wqb"], w["wkb"], w["wvb"],
                   w["gq"], w["gk"], w["gqa"], w["gkva"], w["gmq"], w["gmk"]]
    else:
        weights = [w["wk"], w["wv"], w["wckv"], w["wkr"], w["wkb"], w["wvb"], w["gk"], w["gkva"], w["gmk"]]
    in_specs += [const(a.shape) for a in weights]

    def out(width):
        return jax.ShapeDtypeStruct((b, l, width), BF16), pl.BlockSpec((1, tm, width), lambda bi, i: (bi, i, 0))

    def out_t(width):
        return jax.ShapeDtypeStruct((b, width, l), BF16), pl.BlockSpec((1, width, tm), lambda bi, i: (bi, 0, i))

    outs = [out(GQA_KV_W), out_t(GQA_KV_W), out(MLA_HEADS * MLA_HEAD_PAD), out_t(MLA_V_W)]
    if with_q:
        outs = [out(GQA_Q_W)] + outs[:2] + [out(MLA_HEADS * MLA_HEAD_PAD)] + outs[2:] + [out(2 * D_MODEL)]
    return pl.pallas_call(
        functools.partial(_proj_kernel, with_q=with_q),
        out_shape=[o[0] for o in outs],
        grid=(b, l // tm),
        in_specs=in_specs,
        out_specs=[o[1] for o in outs],
        compiler_params=pltpu.CompilerParams(dimension_semantics=("parallel", "parallel"),
                                             vmem_limit_bytes=56 * 1024 * 1024),
        name="proj_latent" if with_q else "proj_ctx",
    )(x, mod3, w["norm_mix"], cg, sg, cm, sm, *weights)


def _attn_kernel(q_ref, kl_ref, vl_ref, kc_ref, vc_ref, o_ref, *, tk, n_lat, bounded):
    q = q_ref[0]
    tq = q.shape[0]
    dv = vl_ref.shape[1]
    n = n_lat // tk

    if bounded:
        l = jnp.zeros((1, tq), F32)
        acc = jnp.zeros((dv, tq), F32)
        for k, vt in [(kl_ref[0, j * tk:(j + 1) * tk, :], vl_ref[0, :, j * tk:(j + 1) * tk]) for j in range(n)] + [
                (kc_ref[0], vc_ref[0])]:
            p = jnp.exp2(_dot_nt(k, q))
            l = l + jnp.sum(p, axis=0, keepdims=True)
            acc = acc + _dot(vt, p.astype(BF16))
        o_ref[0] = (acc / l).T.astype(BF16)
        return

    def update(carry, s, vt):
        m, l, acc = carry
        m_new = jnp.maximum(m, jnp.max(s, axis=0, keepdims=True))
        alpha = jnp.exp2(m - m_new)
        p = jnp.exp2(s - m_new)
        l = alpha * l + jnp.sum(p, axis=0, keepdims=True)
        return m_new, l, alpha * acc + _dot(vt, p.astype(BF16))

    n = n_lat // tk
    keys = [kl_ref[0, j * tk:(j + 1) * tk, :] for j in range(n)] + [kc_ref[0]]
    vals = [vl_ref[0, :, j * tk:(j + 1) * tk] for j in range(n)] + [vc_ref[0]]
    carry = (jnp.full((1, tq), NEG_INF, F32), jnp.zeros((1, tq), F32), jnp.zeros((dv, tq), F32))
    s = _dot_nt(keys[0], q)
    for j in range(n + 1):
        s_next = _dot_nt(keys[j + 1], q) if j < n else None
        carry = update(carry, s, vals[j])
        s = s_next
    m, l, acc = carry
    o_ref[0] = (acc / l).T.astype(BF16)


SOFTMAX_SAFE_EXPONENT = 56.0


def _score_bound(gain_q, gain_k, dim):
    return dim * jnp.max(jnp.abs(gain_q)) * jnp.max(jnp.abs(gain_k)) * (dim ** -0.5 * LOG2_E) * 1.02


def _attention(q, k_lat, v_lat, k_ctx, v_ctx, n_heads, group, d_qk, dv, tq, tk, name, score_bound):
    run = functools.partial(_attention_call, q, k_lat, v_lat, k_ctx, v_ctx, n_heads, group, d_qk, dv, tq, tk, name)
    return lax.cond(score_bound <= SOFTMAX_SAFE_EXPONENT, lambda: run(True), lambda: run(False))


def _attention_call(q, k_lat, v_lat, k_ctx, v_ctx, n_heads, group, d_qk, dv, tq, tk, name, bounded):
    b, l, _ = q.shape
    lc = k_ctx.shape[1]
    name = name + ("_bounded" if bounded else "_online")
    return pl.pallas_call(
        functools.partial(_attn_kernel, tk=tk, n_lat=l, bounded=bounded),
        out_shape=jax.ShapeDtypeStruct((b, l, n_heads * dv), BF16),
        grid=(b, n_heads, l // tq),
        in_specs=[
            pl.BlockSpec((1, tq, d_qk), lambda bi, h, i: (bi, i, h)),
            pl.BlockSpec((1, l, d_qk), lambda bi, h, i: (bi, 0, h // group)),
            pl.BlockSpec((1, dv, l), lambda bi, h, i: (bi, h // group, 0)),
            pl.BlockSpec((1, lc, d_qk), lambda bi, h, i: (bi, 0, h // group)),
            pl.BlockSpec((1, dv, lc), lambda bi, h, i: (bi, h // group, 0)),
        ],
        out_specs=pl.BlockSpec((1, tq, dv), lambda bi, h, i: (bi, i, h)),
        compiler_params=pltpu.CompilerParams(dimension_semantics=("parallel", "parallel", "parallel"),
                                             vmem_limit_bytes=56 * 1024 * 1024),
        name=name,
    )(q, k_lat, v_lat, k_ctx, v_ctx)


def _merge_kernel(oa_ref, ob_ref, gate_ref, x_ref, mod_ref, woa_ref, wob_ref, wout_ref, nffn_ref,
                  rwh_ref, rwl_ref, rb_ref, x1_ref, h2_ref, eidx_ref, egate_ref):
    mod = mod_ref[0]
    g1, shift2, scale2 = mod[2:3, :], mod[3:4, :], mod[4:5, :]
    ya = _dot(oa_ref[0], woa_ref[...])
    yb = _dot(ob_ref[0], wob_ref[...])
    g = gate_ref[0].astype(F32)
    y = g[:, :D_MODEL] * ya + g[:, D_MODEL:] * yb
    z = _dot(y.astype(BF16), wout_ref[...])
    x1 = x_ref[0] + g1 * z
    x1_ref[0] = x1
    h2 = _rms(x1, nffn_ref[...], D_MODEL) * (1.0 + scale2) + shift2
    h2_ref[0] = h2.astype(BF16)

    h_hi, h_lo = _split_bf16(h2)
    logits = _dot(h_hi, rwh_ref[...]) + (_dot(h_hi, rwl_ref[...]) + _dot(h_lo, rwh_ref[...])) + rb_ref[...]
    lane = lax.broadcasted_iota(jnp.int32, logits.shape, 1).astype(F32)
    cur = jnp.where(lane < N_EXPERTS, logits, NEG_INF)
    vals, idxs = [], []
    for _ in range(TOP_K):
        mx = jnp.max(cur, axis=-1, keepdims=True)
        ix = jnp.min(jnp.where(cur == mx, lane, float(LANES)), axis=-1, keepdims=True)
        vals.append(mx)
        idxs.append(ix)
        cur = jnp.where(lane == ix, NEG_INF, cur)
    ex = [jnp.exp(v - vals[0]) for v in vals]
    den = ex[0] + ex[1] + ex[2] + ex[3]
    eidx = jnp.zeros(logits.shape, F32)
    egate = jnp.zeros(logits.shape, F32)
    for k in range(TOP_K):
        eidx = jnp.where(lane == k, idxs[k], eidx)
        egate = jnp.where(lane == k, ex[k] / den, egate)
    eidx_ref[0] = eidx.astype(jnp.int32)
    egate_ref[0] = egate


def _merge(o_a, o_b, gates, x, mod3, w, tm):
    b, l, _ = x.shape
    const = lambda shape: pl.BlockSpec(shape, lambda bi, i: (0,) * len(shape), pipeline_mode=pl.Buffered(1))
    tok = lambda width: pl.BlockSpec((1, tm, width), lambda bi, i: (bi, i, 0))
    return pl.pallas_call(
        _merge_kernel,
        out_shape=[jax.ShapeDtypeStruct((b, l, D_MODEL), F32), jax.ShapeDtypeStruct((b, l, D_MODEL), BF16),
                   jax.ShapeDtypeStruct((b, l, LANES), jnp.int32), jax.ShapeDtypeStruct((b, l, LANES), F32)],
        grid=(b, l // tm),
        in_specs=[tok(GQA_Q_W), tok(MLA_V_W), tok(2 * D_MODEL), tok(D_MODEL),
                  pl.BlockSpec((1, 6, D_MODEL), lambda bi, i: (bi, 0, 0)),
                  const((GQA_Q_W, D_MODEL)), const((MLA_V_W, D_MODEL)), const((D_MODEL, D_MODEL)), const((1, D_MODEL)),
                  const((D_MODEL, LANES)), const((D_MODEL, LANES)), const((1, LANES))],
        out_specs=[tok(D_MODEL), tok(D_MODEL), tok(LANES), tok(LANES)],
        compiler_params=pltpu.CompilerParams(dimension_semantics=("parallel", "parallel"),
                                             vmem_limit_bytes=56 * 1024 * 1024),
        name="merge_router",
    )(o_a, o_b, gates, x, mod3, w["woa"], w["wob"], w["wout"], w["norm_ffn"], w["rw_hi"], w["rw_lo"], w["rb"])


def _onehots(idx, lane):
    return [lane == idx[:, k:k + 1] for k in range(TOP_K)]


SEG_ALIGN = 8


def _route_kernel(eidx_ref, rank_ref, before_ref, tcnt_ref, cnt_ref, carry_ref):
    @pl.when(pl.program_id(0) == 0)
    def _():
        carry_ref[...] = jnp.zeros_like(carry_ref)

    before_ref[0] = carry_ref[...]
    idx = eidx_ref[...]
    tm = idx.shape[0]
    lane = lax.broadcasted_iota(jnp.int32, idx.shape, 1)
    oh = _onehots(idx, lane)
    total = jnp.zeros(idx.shape, F32)
    for k in range(TOP_K):
        total = total + jnp.where(oh[k], 1.0, 0.0)
    row = lax.broadcasted_iota(jnp.int32, (tm, tm), 0)
    col = lax.broadcasted_iota(jnp.int32, (tm, tm), 1)
    tri = jnp.where(row > col, 1.0, 0.0).astype(BF16)
    before = _dot(tri, total.astype(BF16)) + carry_ref[0:1, :]
    rank = jnp.zeros(idx.shape, F32)
    for k in range(TOP_K):
        rk = jnp.sum(jnp.where(oh[k], before, 0.0), axis=-1, keepdims=True)
        rank = jnp.where(lane == k, rk, rank)
    rank_ref[...] = rank.astype(jnp.int32)
    tile_cnt = jnp.sum(total, axis=0, keepdims=True)
    tcnt_ref[0] = jnp.broadcast_to(tile_cnt, carry_ref.shape)
    seg = jnp.floor((tile_cnt + (SEG_ALIGN - 1)) * (1.0 / SEG_ALIGN)) * SEG_ALIGN
    carry_ref[...] = carry_ref[...] + seg
    cnt_ref[...] = carry_ref[...]


def _route(eidx, tm):
    t = eidx.shape[0]
    return pl.pallas_call(
        _route_kernel,
        out_shape=[jax.ShapeDtypeStruct((t, LANES), jnp.int32), jax.ShapeDtypeStruct((t // tm, 8, LANES), F32),
                   jax.ShapeDtypeStruct((t // tm, 8, LANES), F32), jax.ShapeDtypeStruct((8, LANES), F32)],
        grid=(t // tm,),
        in_specs=[pl.BlockSpec((tm, LANES), lambda i: (i, 0))],
        out_specs=[pl.BlockSpec((tm, LANES), lambda i: (i, 0)), pl.BlockSpec((1, 8, LANES), lambda i: (i, 0, 0)),
                   pl.BlockSpec((1, 8, LANES), lambda i: (i, 0, 0)), pl.BlockSpec((8, LANES), lambda i: (0, 0))],
        scratch_shapes=[pltpu.VMEM((8, LANES), F32)],
        compiler_params=pltpu.CompilerParams(dimension_semantics=("arbitrary",)),
        name="route_rank",
    )(eidx)


def _dest_kernel(eidx_ref, rank_ref, egate_ref, start_ref, before_ref, dest_ref, col_ref, dest_t_ref, col_t_ref,
                 gate_t_ref):
    idx = eidx_ref[...]
    lane = lax.broadcasted_iota(jnp.int32, idx.shape, 1)
    oh = _onehots(idx, lane)
    start = start_ref[0:1, :]
    before = before_ref[0][0:1, :]
    rank = rank_ref[...].astype(F32)
    dest = jnp.zeros(idx.shape, F32)
    col = jnp.full(idx.shape, -1.0, F32)
    for k in range(TOP_K):
        sk = jnp.sum(jnp.where(oh[k], start, 0.0), axis=-1, keepdims=True)
        bk = jnp.sum(jnp.where(oh[k], before, 0.0), axis=-1, keepdims=True)
        rk = rank[:, k:k + 1]
        local = rk - bk
        ck = jnp.where(local < WIN, idx[:, k:k + 1].astype(F32) * WIN + local, -1.0)
        dest = jnp.where(lane == k, sk + rk, dest)
        col = jnp.where(lane == k, ck, col)
    dest_ref[...] = dest.astype(jnp.int32)
    col_ref[...] = col.astype(jnp.int32)
    dest_t_ref[0] = dest.T[0:8, :].astype(jnp.int32)
    col_t_ref[0] = col.T[0:8, :].astype(jnp.int32)
    gate_t_ref[0] = egate_ref[...].T[0:8, :]


def _dest(eidx, rank, egate, start, before, tm):
    t = eidx.shape[0]
    blk = pl.BlockSpec((tm, LANES), lambda i: (i, 0))
    blk_t = pl.BlockSpec((1, 8, tm), lambda i: (i, 0, 0))
    tok = jax.ShapeDtypeStruct((t, LANES), jnp.int32)
    tok_t = jax.ShapeDtypeStruct((t // tm, 8, tm), jnp.int32)
    return pl.pallas_call(
        _dest_kernel,
        out_shape=[tok, tok, tok_t, tok_t, jax.ShapeDtypeStruct((t // tm, 8, tm), F32)],
        grid=(t // tm,),
        in_specs=[blk, blk, blk, pl.BlockSpec((8, LANES), lambda i: (0, 0)),
                  pl.BlockSpec((1, 8, LANES), lambda i: (i, 0, 0))],
        out_specs=[blk, blk, blk_t, blk_t, blk_t],
        compiler_params=pltpu.CompilerParams(dimension_semantics=("parallel",)),
        name="route_dest",
    )(eidx, rank, egate, start, before)


MOE_TILE = 256
WIN = 48
N_MAIN_ROWS = N_EXPERTS * WIN
MAIN_CHUNK = 512
WIN_PER_STACK = MOE_TILE // WIN
MAX_WINDOWS = MOE_TILE * TOP_K // WIN
WORKLIST_LEN = -(-(MAX_WINDOWS + WIN_PER_STACK) // WIN_PER_STACK) * WIN_PER_STACK
HALF = D_MODEL // 2


def _pack_bf16_pair(v):
    return pltpu.bitcast(pltpu.pack_elementwise([v[:, HALF:], v[:, :HALF]], packed_dtype=BF16), jnp.uint32)


def _unpack_bf16_pair(u):
    hi = pltpu.unpack_elementwise(u, index=1, packed_dtype=BF16, unpacked_dtype=F32).astype(BF16)
    lo = pltpu.unpack_elementwise(u, index=0, packed_dtype=BF16, unpacked_dtype=F32).astype(BF16)
    return hi, lo


def _build_worklist(ts_ref, tc_ref, wl_ref, junk_row):
    base = pl.program_id(0) * N_EXPERTS

    def per_expert(e, n):
        first = ts_ref[base + e]
        n_win = (tc_ref[base + e] + (WIN - 1)) // WIN

        def per_window(wi, n):
            wl_ref[n] = first + wi * WIN
            return n + 1

        return lax.fori_loop(1, n_win, per_window, n)

    n = lax.fori_loop(0, N_EXPERTS, per_expert, 0)
    n_stacks = (n + (WIN_PER_STACK - 1)) // WIN_PER_STACK

    def pad(j, c):
        wl_ref[j] = junk_row + (j % WIN_PER_STACK) * WIN
        return c

    lax.fori_loop(n, n_stacks * WIN_PER_STACK, pad, 0)
    return n_stacks


def _seg_aligned(row):
    return row if isinstance(row, int) else pl.multiple_of(row, SEG_ALIGN)


def _stack_row_ids(wl_ref, stack, shape, axis):
    pos = lax.broadcasted_iota(jnp.int32, shape, axis)
    row = jnp.full(shape, -1, jnp.int32)
    for wi in range(WIN_PER_STACK - 1, -1, -1):
        row = jnp.where(pos < (wi + 1) * WIN, wl_ref[stack * WIN_PER_STACK + wi] + (pos - wi * WIN), row)
    return row


def _select_rows(row_id, id_t, gate_t):
    shape = (row_id.shape[0], id_t.shape[1])
    sel = jnp.zeros(shape, F32)
    gsel = jnp.zeros(shape, F32)
    for k in range(TOP_K):
        hit = row_id == id_t[k:k + 1, :]
        sel = jnp.where(hit, 1.0, sel)
        gsel = jnp.where(hit, gate_t[k:k + 1, :], gsel)
    return sel.astype(BF16), jnp.sum(gsel, axis=1, keepdims=True)


MAIN_PENDING = 2


TAIL_ROWS = WIN + EXPERT_BLOCK


def _dispatch_kernel(ts_ref, tc_ref, ov_ref, re_ref, h_ref, col_t_ref, dest_t_ref, gate_t_ref,
                     buf_ref, gs_ref, wl_ref, pend_ref, main_x, main_g, stage_x, stage_g, main_sem, sem, *, junk_row):
    i = pl.program_id(0)
    base = i * N_EXPERTS

    @pl.when(i == 0)
    def _():
        pend_ref[0] = 0
        pend_ref[1] = 0
        pend_ref[MAIN_PENDING] = 0
        main_x[0:TAIL_ROWS, :] = jnp.zeros((TAIL_ROWS, HALF), jnp.uint32)
        main_g[0:TAIL_ROWS, :] = jnp.zeros((TAIL_ROWS, LANES), F32)
        cps = []
        for e in range(N_EXPERTS):
            rows = pl.ds(pl.multiple_of(jnp.maximum(re_ref[e] - TAIL_ROWS, 0), SEG_ALIGN), TAIL_ROWS)
            cps.append(pltpu.make_async_copy(main_x.at[pl.ds(0, TAIL_ROWS)], buf_ref.at[rows], main_sem))
            cps.append(pltpu.make_async_copy(main_g.at[pl.ds(0, TAIL_ROWS)], gs_ref.at[rows], main_sem))
        for cp in cps:
            cp.start()
        for cp in cps:
            cp.wait()

        def zero_block_copies(blk):
            rows = pl.ds(pl.multiple_of(blk * EXPERT_BLOCK, EXPERT_BLOCK), EXPERT_BLOCK)
            return (pltpu.make_async_copy(main_x.at[pl.ds(0, EXPERT_BLOCK)], buf_ref.at[rows], main_sem),
                    pltpu.make_async_copy(main_g.at[pl.ds(0, EXPERT_BLOCK)], gs_ref.at[rows], main_sem))

        def start_zero(blk, c):
            for cp in zero_block_copies(blk):
                cp.start()
            return c

        def wait_zero(blk, c):
            for cp in zero_block_copies(blk):
                cp.wait()
            return c

        first_free, n_blocks = re_ref[N_EXPERTS - 1] // EXPERT_BLOCK, (junk_row + EXPERT_BLOCK) // EXPERT_BLOCK
        lax.fori_loop(first_free, n_blocks, start_zero, 0)
        lax.fori_loop(first_free, n_blocks, wait_zero, 0)

    gate_t = gate_t_ref[0]
    h = h_ref[...]

    def main_copies(first_row):
        cps = []
        for e in range(N_EXPERTS):
            rows = pl.ds(_seg_aligned(first_row(e)), WIN)
            cps.append(pltpu.make_async_copy(main_x.at[pl.ds(e * WIN, WIN)], buf_ref.at[rows], main_sem))
            cps.append(pltpu.make_async_copy(main_g.at[pl.ds(e * WIN, WIN)], gs_ref.at[rows], main_sem))
        return cps

    def drain_main():
        @pl.when(pend_ref[MAIN_PENDING] == 1)
        def _():
            for cp in main_copies(lambda e: 0):
                cp.wait()
            pend_ref[MAIN_PENDING] = 0

    col_t = col_t_ref[0]
    pieces = []
    for c in range(N_MAIN_ROWS // MAIN_CHUNK):
        row_id = lax.broadcasted_iota(jnp.int32, (MAIN_CHUNK, 1), 0) + c * MAIN_CHUNK
        sel, gate_row = _select_rows(row_id, col_t, gate_t)
        pieces.append((_pack_bf16_pair(_dot(sel, h)), gate_row))

    def window_copies(slot, first_row):
        cps = []
        for wi in range(WIN_PER_STACK):
            rows = pl.ds(_seg_aligned(first_row(wi)), WIN)
            cps.append(pltpu.make_async_copy(stage_x.at[slot, pl.ds(wi * WIN, WIN)], buf_ref.at[rows], sem.at[slot]))
            cps.append(pltpu.make_async_copy(stage_g.at[slot, pl.ds(wi * WIN, WIN)], gs_ref.at[rows], sem.at[slot]))
        return cps

    def drain(slot):
        @pl.when(pend_ref[slot] == 1)
        def _():
            for cp in window_copies(slot, lambda wi: 0):
                cp.wait()
            pend_ref[slot] = 0

    drain_main()
    drain(0)
    drain(1)
    for c, (packed, gate_row) in enumerate(pieces):
        main_x[c * MAIN_CHUNK:(c + 1) * MAIN_CHUNK, :] = packed
        main_g[c * MAIN_CHUNK:(c + 1) * MAIN_CHUNK, :] = jnp.broadcast_to(gate_row, (MAIN_CHUNK, LANES))
    for n, cp in enumerate(main_copies(lambda e: ts_ref[base + e])):
        cp.start(priority=(n // 2) % 2)
    pend_ref[MAIN_PENDING] = 1

    @pl.when(ov_ref[i] > 0)
    def _():
        n_stacks = _build_worklist(ts_ref, tc_ref, wl_ref, junk_row)
        dest_t = dest_t_ref[0]

        def stack_body(s, c):
            slot = s & 1
            sel, gate_row = _select_rows(_stack_row_ids(wl_ref, s, (MOE_TILE, 1), 0), dest_t, gate_t)
            rows = _dot(sel, h)
            drain(slot)
            stage_x[slot] = _pack_bf16_pair(rows)
            stage_g[slot] = jnp.broadcast_to(gate_row, (MOE_TILE, LANES))
            for cp in window_copies(slot, lambda wi: wl_ref[s * WIN_PER_STACK + wi]):
                cp.start()
            pend_ref[slot] = 1
            return c

        lax.fori_loop(0, n_stacks, stack_body, 0)

    @pl.when(i == pl.num_programs(0) - 1)
    def _():
        drain_main()
        drain(0)
        drain(1)


def _moe_grid_spec(n_tiles, in_specs, out_specs, scratch_shapes):
    return pltpu.PrefetchScalarGridSpec(num_scalar_prefetch=4, grid=(n_tiles,), in_specs=in_specs,
                                        out_specs=out_specs, scratch_shapes=scratch_shapes)


def _dispatch(tile_start, tile_cnt, tile_ovf, region_ends, h2, col_t, dest_t, gate_t, n_rows):
    n_tiles = h2.shape[0] // MOE_TILE
    junk_row = n_rows - EXPERT_BLOCK
    any_spec = pl.BlockSpec(memory_space=pl.ANY)
    tok_t = pl.BlockSpec((1, 8, MOE_TILE), lambda i, *_: (i, 0, 0))
    return pl.pallas_call(
        functools.partial(_dispatch_kernel, junk_row=junk_row),
        out_shape=[jax.ShapeDtypeStruct((n_rows, HALF), jnp.uint32), jax.ShapeDtypeStruct((n_rows, LANES), F32)],
        grid_spec=_moe_grid_spec(
            n_tiles,
            [pl.BlockSpec((MOE_TILE, D_MODEL), lambda i, *_: (i, 0)), tok_t, tok_t, tok_t],
            [any_spec, any_spec],
            [pltpu.SMEM((WORKLIST_LEN,), jnp.int32), pltpu.SMEM((3,), jnp.int32),
             pltpu.VMEM((N_MAIN_ROWS, HALF), jnp.uint32), pltpu.VMEM((N_MAIN_ROWS, LANES), F32),
             pltpu.VMEM((2, MOE_TILE, HALF), jnp.uint32), pltpu.VMEM((2, MOE_TILE, LANES), F32),
             pltpu.SemaphoreType.DMA(()), pltpu.SemaphoreType.DMA((2,))]),
        compiler_params=pltpu.CompilerParams(dimension_semantics=("arbitrary",),
                                             vmem_limit_bytes=56 * 1024 * 1024),
        name="moe_dispatch",
    )(tile_start, tile_cnt, tile_ovf, region_ends, h2, col_t, dest_t, gate_t)


def _select_cols(ids, col_id):
    sel = jnp.zeros((ids.shape[0], col_id.shape[1]), F32)
    for k in range(TOP_K):
        sel = jnp.where(ids[:, k:k + 1] == col_id, 1.0, sel)
    return sel.astype(BF16)


def _combine_kernel(ts_ref, tc_ref, ov_ref, re_ref, col_ref, dest_ref, x1_ref, mod_ref, y_ref, o_ref,
                    wl_ref, main_y, stage_y, acc_ref, main_sem, sem, *, junk_row):
    del re_ref
    i = pl.program_id(0)
    slot_i = i & 1

    def main_copies(slot, first_row):
        return [pltpu.make_async_copy(y_ref.at[pl.ds(_seg_aligned(first_row(e)), WIN)],
                                      main_y.at[slot, pl.ds(e * WIN, WIN)], main_sem.at[slot])
                for e in range(N_EXPERTS)]

    def fetch_main(tile, slot):
        for n, cp in enumerate(main_copies(slot, lambda e: ts_ref[tile * N_EXPERTS + e])):
            cp.start(priority=n % 2)

    @pl.when(i == 0)
    def _():
        fetch_main(0, 0)

    @pl.when(i + 1 < pl.num_programs(0))
    def _():
        fetch_main(i + 1, 1 - slot_i)

    for cp in main_copies(slot_i, lambda e: 0):
        cp.wait()

    col = col_ref[...]
    sel = jnp.concatenate(
        [_select_cols(col, lax.broadcasted_iota(jnp.int32, (1, MAIN_CHUNK), 1) + c * MAIN_CHUNK)
         for c in range(N_MAIN_ROWS // MAIN_CHUNK)], axis=1)
    y_hi, y_lo = _unpack_bf16_pair(main_y[slot_i])
    acc_ref[:, :HALF] = _dot(sel, y_hi)
    acc_ref[:, HALF:] = _dot(sel, y_lo)

    @pl.when(ov_ref[i] > 0)
    def _():
        n_stacks = _build_worklist(ts_ref, tc_ref, wl_ref, junk_row)
        dest = dest_ref[...]
        if WIN_PER_STACK * WIN < MOE_TILE:
            for slot in range(2):
                stage_y[slot, WIN_PER_STACK * WIN:, :] = jnp.zeros((MOE_TILE - WIN_PER_STACK * WIN, HALF), jnp.uint32)

        def window_copies(slot, first_row):
            return [pltpu.make_async_copy(y_ref.at[pl.ds(_seg_aligned(first_row(wi)), WIN)],
                                          stage_y.at[slot, pl.ds(wi * WIN, WIN)], sem.at[slot])
                    for wi in range(WIN_PER_STACK)]

        def fetch(s, slot):
            for cp in window_copies(slot, lambda wi: wl_ref[s * WIN_PER_STACK + wi]):
                cp.start()

        @pl.when(n_stacks > 0)
        def _():
            fetch(0, 0)

        def stack_body(s, c):
            slot = s & 1

            @pl.when(s + 1 < n_stacks)
            def _():
                fetch(s + 1, 1 - slot)

            for cp in window_copies(slot, lambda wi: 0):
                cp.wait()
            sel_o = _select_cols(dest, _stack_row_ids(wl_ref, s, (1, MOE_TILE), 1))
            o_hi, o_lo = _unpack_bf16_pair(stage_y[slot])
            acc_ref[:, :HALF] += _dot(sel_o, o_hi)
            acc_ref[:, HALF:] += _dot(sel_o, o_lo)
            return c

        lax.fori_loop(0, n_stacks, stack_body, 0)

    g2 = mod_ref[0][5:6, :]
    o_ref[...] = x1_ref[...] + g2 * acc_ref[...]


def _combine(tile_start, tile_cnt, tile_ovf, region_ends, col, dest, x1, mod3, y, l):
    t = x1.shape[0]
    per_batch = l // MOE_TILE
    junk_row = y.shape[0] - EXPERT_BLOCK
    tok = lambda width: pl.BlockSpec((MOE_TILE, width), lambda i, *_: (i, 0))
    return pl.pallas_call(
        functools.partial(_combine_kernel, junk_row=junk_row),
        out_shape=jax.ShapeDtypeStruct((t, D_MODEL), F32),
        grid_spec=_moe_grid_spec(
            t // MOE_TILE,
            [tok(LANES), tok(LANES), tok(D_MODEL),
             pl.BlockSpec((1, 6, D_MODEL), lambda i, *_: (i // per_batch, 0, 0)),
             pl.BlockSpec(memory_space=pl.ANY)],
            tok(D_MODEL),
            [pltpu.SMEM((WORKLIST_LEN,), jnp.int32), pltpu.VMEM((2, N_MAIN_ROWS, HALF), jnp.uint32),
             pltpu.VMEM((2, MOE_TILE, HALF), jnp.uint32), pltpu.VMEM((MOE_TILE, D_MODEL), F32),
             pltpu.SemaphoreType.DMA((2,)), pltpu.SemaphoreType.DMA((2,))]),
        compiler_params=pltpu.CompilerParams(dimension_semantics=("arbitrary",),
                                             vmem_limit_bytes=56 * 1024 * 1024),
        name="moe_combine",
    )(tile_start, tile_cnt, tile_ovf, region_ends, col, dest, x1, mod3, y)


def _expert_kernel(be_ref, nused_ref, x_ref, g_ref, w1_ref, b1_ref, w2_ref, b2_ref, y_ref, w1b_ref, w2b_ref):
    i = pl.program_id(0)
    live = i < nused_ref[0]

    @pl.when(live & ((i == 0) | (be_ref[i] != be_ref[jnp.maximum(i - 1, 0)])))
    def _():
        w1b_ref[...] = w1_ref[0].astype(BF16)
        w2b_ref[...] = w2_ref[0].astype(BF16)

    @pl.when(live)
    def _():
        x_hi, x_lo = _unpack_bf16_pair(x_ref[...])
        gu = _dot(x_hi, w1b_ref[:HALF, :]) + _dot(x_lo, w1b_ref[HALF:, :]) + b1_ref[0]
        glu = jnp.minimum(gu[:, :D_EXPERT], SWIGLU_LIMIT)
        lin = jnp.clip(gu[:, D_EXPERT:], -SWIGLU_LIMIT, SWIGLU_LIMIT)
        act = glu * (1.0 / (1.0 + jnp.exp(-SWIGLU_ALPHA * glu))) * (lin + 1.0)
        y = (_dot(act.astype(BF16), w2b_ref[...]) + b2_ref[0]) * g_ref[:, 0:1]
        y_ref[...] = _pack_bf16_pair(y)

    @pl.when(pl.program_id(0) >= nused_ref[0])
    def _():
        y_ref[...] = jnp.zeros_like(y_ref)


def _experts(block_e, nused, buf, gs, w1, b1, w2, b2):
    nb = buf.shape[0] // EXPERT_BLOCK
    row = lambda i, be, nu: (jnp.minimum(i, nu[0] - 1), 0)
    out_row = lambda i, be, nu: (i, 0)
    exp3 = lambda i, be, nu: (be[jnp.minimum(i, nu[0] - 1)], 0, 0)
    return pl.pallas_call(
        _expert_kernel,
        out_shape=jax.ShapeDtypeStruct((buf.shape[0], HALF), jnp.uint32),
        grid_spec=pltpu.PrefetchScalarGridSpec(
            num_scalar_prefetch=2,
            grid=(nb,),
            in_specs=[pl.BlockSpec((EXPERT_BLOCK, HALF), row),
                      pl.BlockSpec((EXPERT_BLOCK, LANES), row),
                      pl.BlockSpec((1, D_MODEL, 2 * D_EXPERT), exp3),
                      pl.BlockSpec((1, 1, 2 * D_EXPERT), exp3),
                      pl.BlockSpec((1, D_EXPERT, D_MODEL), exp3),
                      pl.BlockSpec((1, 1, D_MODEL), exp3)],
            out_specs=pl.BlockSpec((EXPERT_BLOCK, HALF), out_row),
            scratch_shapes=[pltpu.VMEM((D_MODEL, 2 * D_EXPERT), BF16), pltpu.VMEM((D_EXPERT, D_MODEL), BF16)]),
        compiler_params=pltpu.CompilerParams(dimension_semantics=("arbitrary",),
                                             vmem_limit_bytes=56 * 1024 * 1024),
        name="moe_experts",
    )(block_e, nused, buf, gs, w1, b1, w2, b2)


def _rope_tables(seq_len):
    pos = jnp.arange(seq_len, dtype=jnp.int32)
    row = (pos // GRID_W).astype(F32)[:, None]
    col = (pos % GRID_W).astype(F32)[:, None]

    def table(d_axis, pad):
        inv_freq = ROPE_THETA ** (-jnp.arange(0, d_axis, 2, dtype=F32) / d_axis)
        ar, ac = row * inv_freq[None, :], col * inv_freq[None, :]
        cos = jnp.concatenate([jnp.cos(ar), jnp.cos(ar), jnp.cos(ac), jnp.cos(ac)], axis=-1)
        sin = jnp.concatenate([-jnp.sin(ar), jnp.sin(ar), -jnp.sin(ac), jnp.sin(ac)], axis=-1)
        if pad:
            cos = jnp.concatenate([cos, jnp.ones((seq_len, pad), F32)], axis=-1)
            sin = jnp.concatenate([sin, jnp.zeros((seq_len, pad), F32)], axis=-1)
        return cos, sin

    cg, sg = table(GQA_HEAD_DIM // 2, 0)
    cm, sm = table(MLA_ROPE_DIM // 2, LANES - MLA_ROPE_DIM)
    return cg, sg, cm, sm


def _pad_lanes(a, width):
    return jnp.pad(a, [(0, 0)] * (a.ndim - 1) + [(0, width - a.shape[-1])])


def _prep_weights(p):
    w_in = p["w_in"]
    w = {"norm_mix": p["norm_mix"].reshape(1, D_MODEL), "norm_ffn": p["norm_ffn"].reshape(1, D_MODEL)}
    w["wq"] = w_in[:, :OFF_GQA_K].astype(BF16)
    w["wk"] = w_in[:, OFF_GQA_K:OFF_GQA_V].astype(BF16)
    w["wv"] = w_in[:, OFF_GQA_V:OFF_MLA_QA].T.astype(BF16)
    w["wqa"] = w_in[:, OFF_MLA_QA:OFF_MLA_KVA].astype(BF16)
    w["wckv"] = w_in[:, OFF_MLA_KVA:OFF_MLA_KVA + MLA_KV_RANK].astype(BF16)
    w["wkr"] = _pad_lanes(w_in[:, OFF_MLA_KVA + MLA_KV_RANK:OFF_GATE], LANES).astype(BF16)
    w["wg"] = w_in[:, OFF_GATE:].astype(BF16)
    wqb = p["mla_w_qb"].reshape(MLA_Q_RANK, MLA_HEADS, MLA_QK_DIM)
    w["wqb"] = _pad_lanes(wqb, MLA_HEAD_PAD).reshape(MLA_Q_RANK, MLA_HEADS * MLA_HEAD_PAD).astype(BF16)
    wkvb = p["mla_w_kvb"].reshape(MLA_KV_RANK, MLA_HEADS, MLA_NOPE_DIM + MLA_V_DIM)
    w["wkb"] = wkvb[:, :, :MLA_NOPE_DIM].reshape(MLA_KV_RANK, MLA_HEADS * MLA_NOPE_DIM).astype(BF16)
    w["wvb"] = wkvb[:, :, MLA_NOPE_DIM:].reshape(MLA_KV_RANK, MLA_V_W).T.astype(BF16)
    w["gq"] = p["gqa_q_norm"].reshape(1, GQA_HEAD_DIM)
    w["gk"] = p["gqa_k_norm"].reshape(1, GQA_HEAD_DIM)
    w["gqa"] = p["mla_q_a_norm"].reshape(1, MLA_Q_RANK)
    w["gkva"] = p["mla_kv_a_norm"].reshape(1, MLA_KV_RANK)
    w["gmq"] = _pad_lanes(p["mla_q_norm"].reshape(1, MLA_QK_DIM), MLA_HEAD_PAD)
    w["gmk"] = _pad_lanes(p["mla_k_norm"].reshape(1, MLA_QK_DIM), MLA_HEAD_PAD)
    w["woa"] = p["w_o_gqa"].astype(BF16)
    w["wob"] = p["w_o_mla"].astype(BF16)
    w["wout"] = p["w_out"].astype(BF16)
    rw = _pad_lanes(p["router_w"], LANES)
    w["rw_hi"] = rw.astype(BF16)
    w["rw_lo"] = (rw - w["rw_hi"].astype(F32)).astype(BF16)
    w["rb"] = _pad_lanes(p["router_b"].reshape(1, N_EXPERTS), LANES)
    return w


def _layer(x, c, ctx, c_ctx, p):
    b, l, _ = x.shape
    lc = ctx.shape[1]
    t = b * l
    w = _prep_weights(p)

    n_mod_rows = -(-(b + 1) // 8) * 8
    cc = jnp.zeros((n_mod_rows, D_MODEL), F32).at[:b].set(c).at[b].set(c_ctx)
    mod3 = _ada_mod(cc, p["ada_w"], p["ada_b"]).reshape(n_mod_rows, 6, D_MODEL)

    tables = _rope_tables(l)
    ident = (jnp.ones((lc, LANES), F32), jnp.zeros((lc, LANES), F32)) * 2
    tm = min(512, l)
    q_a, k_a, v_a, q_m, k_m, v_m, gates = _proj(x, mod3, lambda bi: bi, tables, w, True, tm)
    kc_a, vc_a, kc_m, vc_m = _proj(ctx, mod3, lambda bi: b, ident, w, False, min(256, lc))

    tq, tk = min(2048, l), min(1024, l)
    o_a = _attention(q_a, k_a, v_a, kc_a, vc_a, GQA_HEADS, GQA_GROUP, GQA_HEAD_DIM, GQA_HEAD_DIM, tq, tk, "attn_gqa",
                     _score_bound(p["gqa_q_norm"], p["gqa_k_norm"], GQA_HEAD_DIM))
    o_m = _attention(q_m, k_m, v_m, kc_m, vc_m, MLA_HEADS, 1, MLA_HEAD_PAD, MLA_V_DIM, tq, tk, "attn_mla",
                     _score_bound(p["mla_q_norm"], p["mla_k_norm"], MLA_QK_DIM))

    x1, h2, eidx, egate = _merge(o_a, o_m, gates, x, mod3, w, tm)
    x1, h2 = x1.reshape(t, D_MODEL), h2.reshape(t, D_MODEL)
    eidx, egate = eidx.reshape(t, LANES), egate.reshape(t, LANES)

    assert t % MOE_TILE == 0
    n_tiles = t // MOE_TILE
    rank, before_raw, tcnt, cnt = _route(eidx, MOE_TILE)
    seg_rows = cnt[0, :N_EXPERTS].astype(jnp.int32)
    before = before_raw[:, 0, :N_EXPERTS].astype(jnp.int32)
    region = (seg_rows + WIN + EXPERT_BLOCK - 1) // EXPERT_BLOCK * EXPERT_BLOCK
    region_ends = jnp.cumsum(region)
    region_starts = region_ends - region
    max_rows = t * TOP_K + n_tiles * N_EXPERTS * (SEG_ALIGN - 1) + N_EXPERTS * (WIN + EXPERT_BLOCK - 1)
    n_blocks = max_rows // EXPERT_BLOCK + 1
    n_rows = n_blocks * EXPERT_BLOCK
    block_row = jnp.arange(n_blocks, dtype=jnp.int32) * EXPERT_BLOCK
    block_e = jnp.minimum(jnp.sum(region_ends[None, :] <= block_row[:, None], axis=1), N_EXPERTS - 1).astype(jnp.int32)
    nused = (region_ends[-1:] // EXPERT_BLOCK).astype(jnp.int32)
    tile_start = (region_starts[None, :] + before).reshape(-1)
    tile_cnt = tcnt[:, 0, :N_EXPERTS].astype(jnp.int32)
    tile_ovf = jnp.sum(jnp.maximum((tile_cnt + WIN - 1) // WIN - 1, 0), axis=1).astype(jnp.int32)
    tile_cnt = tile_cnt.reshape(-1)
    start = jnp.zeros((8, LANES), F32).at[0, :N_EXPERTS].set(region_starts.astype(F32))
    dest, col, dest_t, col_t, gate_t = _dest(eidx, rank, egate, start, before_raw, MOE_TILE)

    region_ends = region_ends.astype(jnp.int32)
    buf, gs = _dispatch(tile_start, tile_cnt, tile_ovf, region_ends, h2, col_t, dest_t, gate_t, n_rows)
    y = _experts(block_e, nused, buf, gs, p["expert_w1"], p["expert_b1"].reshape(N_EXPERTS, 1, -1),
                 p["expert_w2"], p["expert_b2"].reshape(N_EXPERTS, 1, -1))
    out = _combine(tile_start, tile_cnt, tile_ovf, region_ends, col, dest, x1, mod3, y, l)
    return out.reshape(b, l, D_MODEL)


def kernel(x, c, ctx, c_ctx, ada_w, ada_b, norm_mix, norm_ffn, w_in, gqa_q_norm, gqa_k_norm, mla_q_a_norm, mla_kv_a_norm, mla_w_qb, mla_w_kvb, mla_q_norm, mla_k_norm, w_o_gqa, w_o_mla, w_out, router_w, router_b, expert_w1, expert_b1, expert_w2, expert_b2):
    assert ada_w.shape[0] == 1, "single-layer problem: the context stream is never updated"
    p = {
        "ada_w": ada_w[0], "ada_b": ada_b[0], "norm_mix": norm_mix[0], "norm_ffn": norm_ffn[0],
        "w_in": w_in[0], "gqa_q_norm": gqa_q_norm[0], "gqa_k_norm": gqa_k_norm[0],
        "mla_q_a_norm": mla_q_a_norm[0], "mla_kv_a_norm": mla_kv_a_norm[0],
        "mla_w_qb": mla_w_qb[0], "mla_w_kvb": mla_w_kvb[0],
        "mla_q_norm": mla_q_norm[0], "mla_k_norm": mla_k_norm[0],
        "w_o_gqa": w_o_gqa[0], "w_o_mla": w_o_mla[0], "w_out": w_out[0],
        "router_w": router_w[0], "router_b": router_b[0],
        "expert_w1": expert_w1[0], "expert_b1": expert_b1[0],
        "expert_w2": expert_w2[0], "expert_b2": expert_b2[0],
    }
    return _layer(x, c, ctx, c_ctx, p)
```

```python
import functools
import math

import jax
import jax.numpy as jnp
from jax import lax
from jax.experimental import pallas as pl
from jax.experimental.pallas import tpu as pltpu

D_MODEL = 1024
GRID_W = 64
EPS = 1e-6
ROPE_THETA = 10000.0

GQA_HEADS = 8
GQA_KV_HEADS = 2
GQA_GROUP = GQA_HEADS // GQA_KV_HEADS
GQA_HEAD_DIM = 128
GQA_Q_W = GQA_HEADS * GQA_HEAD_DIM
GQA_KV_W = GQA_KV_HEADS * GQA_HEAD_DIM

MLA_HEADS = 8
MLA_Q_RANK = 256
MLA_KV_RANK = 128
MLA_NOPE_DIM = 128
MLA_ROPE_DIM = 64
MLA_V_DIM = 128
MLA_QK_DIM = MLA_NOPE_DIM + MLA_ROPE_DIM
MLA_HEAD_PAD = 256
MLA_V_W = MLA_HEADS * MLA_V_DIM

OFF_GQA_K = GQA_Q_W
OFF_GQA_V = OFF_GQA_K + GQA_KV_W
OFF_MLA_QA = OFF_GQA_V + GQA_KV_W
OFF_MLA_KVA = OFF_MLA_QA + MLA_Q_RANK
OFF_GATE = OFF_MLA_KVA + MLA_KV_RANK + MLA_ROPE_DIM

N_EXPERTS = 32
TOP_K = 4
D_EXPERT = 1024
SWIGLU_LIMIT = 7.0
SWIGLU_ALPHA = 1.702
EXPERT_BLOCK = 512

LANES = 128
NEG_INF = float("-inf")
LOG2_E = math.log2(math.e)

BF16 = jnp.bfloat16
F32 = jnp.float32


def _dot(a, b):
    return jnp.dot(a, b, preferred_element_type=F32)


def _dot_nt(a, b):
    return lax.dot_general(a, b, (((1,), (1,)), ((), ())), preferred_element_type=F32)


def _split_bf16(a):
    hi = a.astype(BF16)
    lo = (a - hi.astype(F32)).astype(BF16)
    return hi, lo


def _rms(x, g, n):
    ms = jnp.sum(x * x, axis=-1, keepdims=True) * (1.0 / n)
    return x * lax.rsqrt(ms + EPS) * g


def _swap_halves(x, k):
    lane = lax.broadcasted_iota(jnp.int32, x.shape, 1)
    return jnp.where((lane & k) != 0, pltpu.roll(x, k, 1), pltpu.roll(x, LANES - k, 1))


def _rope(x, cos, sin_signed, k):
    return x * cos + _swap_halves(x, k) * sin_signed


def _ada_kernel(c_ref, w_ref, b_ref, o_ref):
    c = c_ref[...]
    s = c * (1.0 / (1.0 + jnp.exp(-c)))
    s_hi, s_lo = _split_bf16(s)
    w_hi, w_lo = _split_bf16(w_ref[...])
    o_ref[...] = _dot(s_hi, w_hi) + (_dot(s_hi, w_lo) + _dot(s_lo, w_hi)) + b_ref[...]


def _ada_mod(cc, ada_w, ada_b):
    n = ada_w.shape[1]
    tn = 1024
    return pl.pallas_call(
        _ada_kernel,
        out_shape=jax.ShapeDtypeStruct((cc.shape[0], n), F32),
        grid=(n // tn,),
        in_specs=[
            pl.BlockSpec((cc.shape[0], D_MODEL), lambda j: (0, 0)),
            pl.BlockSpec((D_MODEL, tn), lambda j: (0, j)),
            pl.BlockSpec((1, tn), lambda j: (0, j)),
        ],
        out_specs=pl.BlockSpec((cc.shape[0], tn), lambda j: (0, j)),
        compiler_params=pltpu.CompilerParams(dimension_semantics=("parallel",)),
        name="ada_mod",
    )(cc, ada_w, ada_b.reshape(1, n))


def _proj_kernel(*refs, with_q):
    if with_q:
        (x_ref, mod_ref, nmix_ref, cg_ref, sg_ref, cm_ref, sm_ref,
         wq_ref, wk_ref, wv_ref, wqa_ref, wckv_ref, wkr_ref, wg_ref, wqb_ref, wkb_ref, wvb_ref,
         gq_ref, gk_ref, gqa_ref, gkva_ref, gmq_ref, gmk_ref,
         q_ref, k_ref, v_ref, qm_ref, km_ref, vm_ref, gate_ref) = refs
    else:
        (x_ref, mod_ref, nmix_ref, cg_ref, sg_ref, cm_ref, sm_ref,
         wk_ref, wv_ref, wckv_ref, wkr_ref, wkb_ref, wvb_ref,
         gk_ref, gkva_ref, gmk_ref,
         k_ref, v_ref, km_ref, vm_ref) = refs

    x = x_ref[0]
    mod = mod_ref[0]
    shift, scale = mod[0:1, :], mod[1:2, :]
    h = _rms(x, nmix_ref[...], D_MODEL) * (1.0 + scale) + shift
    hb = h.astype(BF16)
    cg, sg = cg_ref[...], sg_ref[...]
    cm, sm = cm_ref[...], sm_ref[...]

    kk = _dot(hb, wk_ref[...])
    for j in range(GQA_KV_HEADS):
        sl = slice(j * GQA_HEAD_DIM, (j + 1) * GQA_HEAD_DIM)
        kn = _rms(kk[:, sl], gk_ref[...], GQA_HEAD_DIM)
        k_ref[0, :, sl] = _rope(kn, cg, sg, 32).astype(BF16)
    v_ref[0] = _dot_nt(wv_ref[...], hb).astype(BF16)

    ckv = _rms(_dot(hb, wckv_ref[...]), gkva_ref[...], MLA_KV_RANK).astype(BF16)
    vm_ref[0] = _dot_nt(wvb_ref[...], ckv).astype(BF16)
    knope = _dot(ckv, wkb_ref[...])
    kr = _dot(hb, wkr_ref[...])
    gmk = gmk_ref[...]
    g_nope, g_rope = gmk[:, :MLA_NOPE_DIM], gmk[:, MLA_NOPE_DIM:]
    ssq_r = jnp.sum(kr * kr, axis=-1, keepdims=True)
    kr_roped = _rope(kr * g_rope, cm, sm, 16)
    for j in range(MLA_HEADS):
        kn = knope[:, j * MLA_NOPE_DIM:(j + 1) * MLA_NOPE_DIM]
        ms = (jnp.sum(kn * kn, axis=-1, keepdims=True) + ssq_r) * (1.0 / MLA_QK_DIM)
        r = lax.rsqrt(ms + EPS)
        base = j * MLA_HEAD_PAD
        km_ref[0, :, base:base + MLA_NOPE_DIM] = (kn * r * g_nope).astype(BF16)
        km_ref[0, :, base + MLA_NOPE_DIM:base + MLA_HEAD_PAD] = (kr_roped * r).astype(BF16)

    if not with_q:
        return

    qq = _dot(hb, wq_ref[...])
    q_scale = GQA_HEAD_DIM ** -0.5 * LOG2_E
    for j in range(GQA_HEADS):
        sl = slice(j * GQA_HEAD_DIM, (j + 1) * GQA_HEAD_DIM)
        qn = _rms(qq[:, sl], gq_ref[...], GQA_HEAD_DIM)
        q_ref[0, :, sl] = (_rope(qn, cg, sg, 32) * q_scale).astype(BF16)

    qa = _rms(_dot(hb, wqa_ref[...]), gqa_ref[...], MLA_Q_RANK).astype(BF16)
    q2 = _dot(qa, wqb_ref[...])
    gmq = gmq_ref[...]
    gq_nope, gq_rope = gmq[:, :MLA_NOPE_DIM], gmq[:, MLA_NOPE_DIM:]
    m_scale = MLA_QK_DIM ** -0.5 * LOG2_E
    for j in range(MLA_HEADS):
        base = j * MLA_HEAD_PAD
        qn = q2[:, base:base + MLA_NOPE_DIM]
        qr = q2[:, base + MLA_NOPE_DIM:base + MLA_HEAD_PAD]
        ms = (jnp.sum(qn * qn, axis=-1, keepdims=True) + jnp.sum(qr * qr, axis=-1, keepdims=True)) * (1.0 / MLA_QK_DIM)
        r = lax.rsqrt(ms + EPS)
        qm_ref[0, :, base:base + MLA_NOPE_DIM] = (qn * r * gq_nope * m_scale).astype(BF16)
        qm_ref[0, :, base + MLA_NOPE_DIM:base + MLA_HEAD_PAD] = (_rope(qr * r * gq_rope, cm, sm, 16) * m_scale).astype(BF16)

    gl = _dot(hb, wg_ref[...])
    gate_ref[0] = (1.0 / (1.0 + jnp.exp(-gl))).astype(BF16)


def _proj(x, mod3, mod_row_of_batch, tables, w, with_q, tm):
    b, l, _ = x.shape
    cg, sg, cm, sm = tables
    const = lambda shape: pl.BlockSpec(shape, lambda bi, i: (0,) * len(shape), pipeline_mode=pl.Buffered(1))
    tab = pl.BlockSpec((tm, LANES), lambda bi, i: (i, 0))
    in_specs = [
        pl.BlockSpec((1, tm, D_MODEL), lambda bi, i: (bi, i, 0)),
        pl.BlockSpec((1, 6, D_MODEL), lambda bi, i: (mod_row_of_batch(bi), 0, 0)),
        const((1, D_MODEL)), tab, tab, tab, tab,
    ]
    if with_q:
        weights = [w["wq"], w["wk"], w["wv"], w["wqa"], w["wckv"], w["wkr"], w["wg"], w["wqb"], w["wkb"], w["wvb"],
                   w["gq"], w["gk"], w["gqa"], w["gkva"], w["gmq"], w["gmk"]]
    else:
        weights = [w["wk"], w["wv"], w["wckv"], w["wkr"], w["wkb"], w["wvb"], w["gk"], w["gkva"], w["gmk"]]
    in_specs += [const(a.shape) for a in weights]

    def out(width):
        return jax.ShapeDtypeStruct((b, l, width), BF16), pl.BlockSpec((1, tm, width), lambda bi, i: (bi, i, 0))

    def out_t(width):
        return jax.ShapeDtypeStruct((b, width, l), BF16), pl.BlockSpec((1, width, tm), lambda bi, i: (bi, 0, i))

    outs = [out(GQA_KV_W), out_t(GQA_KV_W), out(MLA_HEADS * MLA_HEAD_PAD), out_t(MLA_V_W)]
    if with_q:
        outs = [out(GQA_Q_W)] + outs[:2] + [out(MLA_HEADS * MLA_HEAD_PAD)] + outs[2:] + [out(2 * D_MODEL)]
    return pl.pallas_call(
        functools.partial(_proj_kernel, with_q=with_q),
        out_shape=[o[0] for o in outs],
        grid=(b, l // tm),
        in_specs=in_specs,
        out_specs=[o[1] for o in outs],
        compiler_params=pltpu.CompilerParams(dimension_semantics=("parallel", "parallel"),
                                             vmem_limit_bytes=56 * 1024 * 1024),
        name="proj_latent" if with_q else "proj_ctx",
    )(x, mod3, w["norm_mix"], cg, sg, cm, sm, *weights)


def _attn_kernel(q_ref, kl_ref, vl_ref, kc_ref, vc_ref, o_ref, *, tk, n_lat, bounded):
    q = q_ref[0]
    tq = q.shape[0]
    dv = vl_ref.shape[1]
    n = n_lat // tk

    if bounded:
        l = jnp.zeros((1, tq), F32)
        acc = jnp.zeros((dv, tq), F32)
        for k, vt in [(kl_ref[0, j * tk:(j + 1) * tk, :], vl_ref[0, :, j * tk:(j + 1) * tk]) for j in range(n)] + [
                (kc_ref[0], vc_ref[0])]:
            p = jnp.exp2(_dot_nt(k, q))
            l = l + jnp.sum(p, axis=0, keepdims=True)
            acc = acc + _dot(vt, p.astype(BF16))
        o_ref[0] = (acc / l).T.astype(BF16)
        return

    def update(carry, s, vt):
        m, l, acc = carry
        m_new = jnp.maximum(m, jnp.max(s, axis=0, keepdims=True))
        alpha = jnp.exp2(m - m_new)
        p = jnp.exp2(s - m_new)
        l = alpha * l + jnp.sum(p, axis=0, keepdims=True)
        return m_new, l, alpha * acc + _dot(vt, p.astype(BF16))

    n = n_lat // tk
    keys = [kl_ref[0, j * tk:(j + 1) * tk, :] for j in range(n)] + [kc_ref[0]]
    vals = [vl_ref[0, :, j * tk:(j + 1) * tk] for j in range(n)] + [vc_ref[0]]
    carry = (jnp.full((1, tq), NEG_INF, F32), jnp.zeros((1, tq), F32), jnp.zeros((dv, tq), F32))
    s = _dot_nt(keys[0], q)
    for j in range(n + 1):
        s_next = _dot_nt(keys[j + 1], q) if j < n else None
        carry = update(carry, s, vals[j])
        s = s_next
    m, l, acc = carry
    o_ref[0] = (acc / l).T.astype(BF16)


SOFTMAX_SAFE_EXPONENT = 56.0


def _score_bound(gain_q, gain_k, dim):
    return dim * jnp.max(jnp.abs(gain_q)) * jnp.max(jnp.abs(gain_k)) * (dim ** -0.5 * LOG2_E) * 1.02


def _attention(q, k_lat, v_lat, k_ctx, v_ctx, n_heads, group, d_qk, dv, tq, tk, name, score_bound):
    run = functools.partial(_attention_call, q, k_lat, v_lat, k_ctx, v_ctx, n_heads, group, d_qk, dv, tq, tk, name)
    return lax.cond(score_bound <= SOFTMAX_SAFE_EXPONENT, lambda: run(True), lambda: run(False))


def _attention_call(q, k_lat, v_lat, k_ctx, v_ctx, n_heads, group, d_qk, dv, tq, tk, name, bounded):
    b, l, _ = q.shape
    lc = k_ctx.shape[1]
    name = name + ("_bounded" if bounded else "_online")
    return pl.pallas_call(
        functools.partial(_attn_kernel, tk=tk, n_lat=l, bounded=bounded),
        out_shape=jax.ShapeDtypeStruct((b, l, n_heads * dv), BF16),
        grid=(b, n_heads, l // tq),
        in_specs=[
            pl.BlockSpec((1, tq, d_qk), lambda bi, h, i: (bi, i, h)),
            pl.BlockSpec((1, l, d_qk), lambda bi, h, i: (bi, 0, h // group)),
            pl.BlockSpec((1, dv, l), lambda bi, h, i: (bi, h // group, 0)),
            pl.BlockSpec((1, lc, d_qk), lambda bi, h, i: (bi, 0, h // group)),
            pl.BlockSpec((1, dv, lc), lambda bi, h, i: (bi, h // group, 0)),
        ],
        out_specs=pl.BlockSpec((1, tq, dv), lambda bi, h, i: (bi, i, h)),
        compiler_params=pltpu.CompilerParams(dimension_semantics=("parallel", "parallel", "parallel"),
                                             vmem_limit_bytes=56 * 1024 * 1024),
        name=name,
    )(q, k_lat, v_lat, k_ctx, v_ctx)


def _merge_kernel(oa_ref, ob_ref, gate_ref, x_ref, mod_ref, woa_ref, wob_ref, wout_ref, nffn_ref,
                  rwh_ref, rwl_ref, rb_ref, x1_ref, h2_ref, eidx_ref, egate_ref):
    mod = mod_ref[0]
    g1, shift2, scale2 = mod[2:3, :], mod[3:4, :], mod[4:5, :]
    ya = _dot(oa_ref[0], woa_ref[...])
    yb = _dot(ob_ref[0], wob_ref[...])
    g = gate_ref[0].astype(F32)
    y = g[:, :D_MODEL] * ya + g[:, D_MODEL:] * yb
    z = _dot(y.astype(BF16), wout_ref[...])
    x1 = x_ref[0] + g1 * z
    x1_ref[0] = x1
    h2 = _rms(x1, nffn_ref[...], D_MODEL) * (1.0 + scale2) + shift2
    h2_ref[0] = h2.astype(BF16)

    h_hi, h_lo = _split_bf16(h2)
    logits = _dot(h_hi, rwh_ref[...]) + (_dot(h_hi, rwl_ref[...]) + _dot(h_lo, rwh_ref[...])) + rb_ref[...]
    lane = lax.broadcasted_iota(jnp.int32, logits.shape, 1).astype(F32)
    cur = jnp.where(lane < N_EXPERTS, logits, NEG_INF)
    vals, idxs = [], []
    for _ in range(TOP_K):
        mx = jnp.max(cur, axis=-1, keepdims=True)
        ix = jnp.min(jnp.where(cur == mx, lane, float(LANES)), axis=-1, keepdims=True)
        vals.append(mx)
        idxs.append(ix)
        cur = jnp.where(lane == ix, NEG_INF, cur)
    ex = [jnp.exp(v - vals[0]) for v in vals]
    den = ex[0] + ex[1] + ex[2] + ex[3]
    eidx = jnp.zeros(logits.shape, F32)
    egate = jnp.zeros(logits.shape, F32)
    for k in range(TOP_K):
        eidx = jnp.where(lane == k, idxs[k], eidx)
        egate = jnp.where(lane == k, ex[k] / den, egate)
    eidx_ref[0] = eidx.astype(jnp.int32)
    egate_ref[0] = egate


def _merge(o_a, o_b, gates, x, mod3, w, tm):
    b, l, _ = x.shape
    const = lambda shape: pl.BlockSpec(shape, lambda bi, i: (0,) * len(shape), pipeline_mode=pl.Buffered(1))
    tok = lambda width: pl.BlockSpec((1, tm, width), lambda bi, i: (bi, i, 0))
    return pl.pallas_call(
        _merge_kernel,
        out_shape=[jax.ShapeDtypeStruct((b, l, D_MODEL), F32), jax.ShapeDtypeStruct((b, l, D_MODEL), BF16),
                   jax.ShapeDtypeStruct((b, l, LANES), jnp.int32), jax.ShapeDtypeStruct((b, l, LANES), F32)],
        grid=(b, l // tm),
        in_specs=[tok(GQA_Q_W), tok(MLA_V_W), tok(2 * D_MODEL), tok(D_MODEL),
                  pl.BlockSpec((1, 6, D_MODEL), lambda bi, i: (bi, 0, 0)),
                  const((GQA_Q_W, D_MODEL)), const((MLA_V_W, D_MODEL)), const((D_MODEL, D_MODEL)), const((1, D_MODEL)),
                  const((D_MODEL, LANES)), const((D_MODEL, LANES)), const((1, LANES))],
        out_specs=[tok(D_MODEL), tok(D_MODEL), tok(LANES), tok(LANES)],
        compiler_params=pltpu.CompilerParams(dimension_semantics=("parallel", "parallel"),
                                             vmem_limit_bytes=56 * 1024 * 1024),
        name="merge_router",
    )(o_a, o_b, gates, x, mod3, w["woa"], w["wob"], w["wout"], w["norm_ffn"], w["rw_hi"], w["rw_lo"], w["rb"])


def _onehots(idx, lane):
    return [lane == idx[:, k:k + 1] for k in range(TOP_K)]


SEG_ALIGN = 16


def _route_kernel(eidx_ref, rank_ref, before_ref, tcnt_ref, cnt_ref, carry_ref):
    @pl.when(pl.program_id(0) == 0)
    def _():
        carry_ref[...] = jnp.zeros_like(carry_ref)

    before_ref[0] = carry_ref[...]
    idx = eidx_ref[...]
    tm = idx.shape[0]
    lane = lax.broadcasted_iota(jnp.int32, idx.shape, 1)
    oh = _onehots(idx, lane)
    total = jnp.zeros(idx.shape, F32)
    for k in range(TOP_K):
        total = total + jnp.where(oh[k], 1.0, 0.0)
    row = lax.broadcasted_iota(jnp.int32, (tm, tm), 0)
    col = lax.broadcasted_iota(jnp.int32, (tm, tm), 1)
    tri = jnp.where(row > col, 1.0, 0.0).astype(BF16)
    before = _dot(tri, total.astype(BF16)) + carry_ref[0:1, :]
    rank = jnp.where(lane < TOP_K, jnp.take_along_axis(before, idx, axis=1), 0.0)
    rank_ref[...] = rank.astype(jnp.int32)
    tile_cnt = jnp.sum(total, axis=0, keepdims=True)
    tcnt_ref[0] = jnp.broadcast_to(tile_cnt, carry_ref.shape)
    seg = jnp.floor((tile_cnt + (SEG_ALIGN - 1)) * (1.0 / SEG_ALIGN)) * SEG_ALIGN
    carry_ref[...] = carry_ref[...] + seg
    cnt_ref[...] = carry_ref[...]


def _route(eidx, tm):
    t = eidx.shape[0]
    return pl.pallas_call(
        _route_kernel,
        out_shape=[jax.ShapeDtypeStruct((t, LANES), jnp.int32), jax.ShapeDtypeStruct((t // tm, 8, LANES), F32),
                   jax.ShapeDtypeStruct((t // tm, 8, LANES), F32), jax.ShapeDtypeStruct((8, LANES), F32)],
        grid=(t // tm,),
        in_specs=[pl.BlockSpec((tm, LANES), lambda i: (i, 0))],
        out_specs=[pl.BlockSpec((tm, LANES), lambda i: (i, 0)), pl.BlockSpec((1, 8, LANES), lambda i: (i, 0, 0)),
                   pl.BlockSpec((1, 8, LANES), lambda i: (i, 0, 0)), pl.BlockSpec((8, LANES), lambda i: (0, 0))],
        scratch_shapes=[pltpu.VMEM((8, LANES), F32)],
        compiler_params=pltpu.CompilerParams(dimension_semantics=("arbitrary",)),
        name="route_rank",
    )(eidx)


def _dest_kernel(eidx_ref, rank_ref, egate_ref, start_ref, before_ref, dest_ref, col_ref, dest_t_ref, col_t_ref,
                 gate_t_ref):
    idx = eidx_ref[...]
    valid = lax.broadcasted_iota(jnp.int32, idx.shape, 1) < TOP_K
    region_start = jnp.take_along_axis(jnp.broadcast_to(start_ref[0:1, :], idx.shape), idx, axis=1)
    seg_start = jnp.take_along_axis(jnp.broadcast_to(before_ref[0][0:1, :], idx.shape), idx, axis=1)
    rank = rank_ref[...].astype(F32)
    local = rank - seg_start
    dest = jnp.where(valid, region_start + rank, 0.0)
    col = jnp.where(valid & (local < WIN), idx.astype(F32) * WIN + local, -1.0)
    dest_ref[...] = dest.astype(jnp.int32)
    col_ref[...] = col.astype(jnp.int32)
    dest_t_ref[0] = dest.T[0:8, :].astype(jnp.int32)
    col_t_ref[0] = col.T[0:8, :].astype(jnp.int32)
    gate_t_ref[0] = egate_ref[...].T[0:8, :]


def _dest(eidx, rank, egate, start, before, tm):
    t = eidx.shape[0]
    blk = pl.BlockSpec((tm, LANES), lambda i: (i, 0))
    blk_t = pl.BlockSpec((1, 8, tm), lambda i: (i, 0, 0))
    tok = jax.ShapeDtypeStruct((t, LANES), jnp.int32)
    tok_t = jax.ShapeDtypeStruct((t // tm, 8, tm), jnp.int32)
    return pl.pallas_call(
        _dest_kernel,
        out_shape=[tok, tok, tok_t, tok_t, jax.ShapeDtypeStruct((t // tm, 8, tm), F32)],
        grid=(t // tm,),
        in_specs=[blk, blk, blk, pl.BlockSpec((8, LANES), lambda i: (0, 0)),
                  pl.BlockSpec((1, 8, LANES), lambda i: (i, 0, 0))],
        out_specs=[blk, blk, blk_t, blk_t, blk_t],
        compiler_params=pltpu.CompilerParams(dimension_semantics=("parallel",)),
        name="route_dest",
    )(eidx, rank, egate, start, before)


MOE_TILE = 256
WIN = 48
N_MAIN_ROWS = N_EXPERTS * WIN
MAIN_CHUNK = 512
WIN_PER_STACK = MOE_TILE // WIN
MAX_WINDOWS = MOE_TILE * TOP_K // WIN
WORKLIST_LEN = -(-(MAX_WINDOWS + WIN_PER_STACK) // WIN_PER_STACK) * WIN_PER_STACK


def _build_worklist(ts_ref, tc_ref, wl_ref, junk_row):
    base = pl.program_id(0) * N_EXPERTS

    def per_expert(e, n):
        first = ts_ref[base + e]
        n_win = (tc_ref[base + e] + (WIN - 1)) // WIN

        def per_window(wi, n):
            wl_ref[n] = first + wi * WIN
            return n + 1

        return lax.fori_loop(1, n_win, per_window, n)

    n = lax.fori_loop(0, N_EXPERTS, per_expert, 0)
    n_stacks = (n + (WIN_PER_STACK - 1)) // WIN_PER_STACK

    def pad(j, c):
        wl_ref[j] = junk_row + (j % WIN_PER_STACK) * WIN
        return c

    lax.fori_loop(n, n_stacks * WIN_PER_STACK, pad, 0)
    return n_stacks


def _seg_aligned(row):
    return row if isinstance(row, int) else pl.multiple_of(row, SEG_ALIGN)


def _stack_row_ids(wl_ref, stack, shape, axis):
    pos = lax.broadcasted_iota(jnp.int32, shape, axis)
    row = jnp.full(shape, -1, jnp.int32)
    for wi in range(WIN_PER_STACK - 1, -1, -1):
        row = jnp.where(pos < (wi + 1) * WIN, wl_ref[stack * WIN_PER_STACK + wi] + (pos - wi * WIN), row)
    return row


def _select_rows(row_id, id_t, gate_t):
    shape = (row_id.shape[0], id_t.shape[1])
    sel = jnp.zeros(shape, F32)
    gsel = jnp.zeros(shape, F32)
    for k in range(TOP_K):
        hit = row_id == id_t[k:k + 1, :]
        sel = jnp.where(hit, 1.0, sel)
        gsel = jnp.where(hit, gate_t[k:k + 1, :], gsel)
    return sel.astype(BF16), jnp.sum(gsel, axis=1, keepdims=True)


MAIN_PENDING = 2


TAIL_ROWS = WIN + EXPERT_BLOCK


def _dispatch_kernel(ts_ref, tc_ref, ov_ref, re_ref, h_ref, col_t_ref, dest_t_ref, gate_t_ref,
                     buf_ref, gs_ref, wl_ref, pend_ref, main_x, main_g, stage_x, stage_g, main_sem, sem, *, junk_row):
    i = pl.program_id(0)
    base = i * N_EXPERTS

    @pl.when(i == 0)
    def _():
        pend_ref[0] = 0
        pend_ref[1] = 0
        pend_ref[MAIN_PENDING] = 0
        main_x[0:TAIL_ROWS, :] = jnp.zeros((TAIL_ROWS, D_MODEL), BF16)
        main_g[0:TAIL_ROWS, :] = jnp.zeros((TAIL_ROWS, LANES), F32)
        cps = []
        for e in range(N_EXPERTS):
            rows = pl.ds(pl.multiple_of(jnp.maximum(re_ref[e] - TAIL_ROWS, 0), SEG_ALIGN), TAIL_ROWS)
            cps.append(pltpu.make_async_copy(main_x.at[pl.ds(0, TAIL_ROWS)], buf_ref.at[rows], main_sem))
            cps.append(pltpu.make_async_copy(main_g.at[pl.ds(0, TAIL_ROWS)], gs_ref.at[rows], main_sem))
        for cp in cps:
            cp.start()
        for cp in cps:
            cp.wait()

        def zero_block_copies(blk):
            rows = pl.ds(pl.multiple_of(blk * EXPERT_BLOCK, EXPERT_BLOCK), EXPERT_BLOCK)
            return (pltpu.make_async_copy(main_x.at[pl.ds(0, EXPERT_BLOCK)], buf_ref.at[rows], main_sem),
                    pltpu.make_async_copy(main_g.at[pl.ds(0, EXPERT_BLOCK)], gs_ref.at[rows], main_sem))

        def start_zero(blk, c):
            for cp in zero_block_copies(blk):
                cp.start()
            return c

        def wait_zero(blk, c):
            for cp in zero_block_copies(blk):
                cp.wait()
            return c

        first_free, n_blocks = re_ref[N_EXPERTS - 1] // EXPERT_BLOCK, (junk_row + EXPERT_BLOCK) // EXPERT_BLOCK
        lax.fori_loop(first_free, n_blocks, start_zero, 0)
        lax.fori_loop(first_free, n_blocks, wait_zero, 0)

    gate_t = gate_t_ref[0]
    h = h_ref[...]

    def main_copies(first_row):
        cps = []
        for e in range(N_EXPERTS):
            rows = pl.ds(_seg_aligned(first_row(e)), WIN)
            cps.append(pltpu.make_async_copy(main_x.at[pl.ds(e * WIN, WIN)], buf_ref.at[rows], main_sem))
            cps.append(pltpu.make_async_copy(main_g.at[pl.ds(e * WIN, WIN)], gs_ref.at[rows], main_sem))
        return cps

    def drain_main():
        @pl.when(pend_ref[MAIN_PENDING] == 1)
        def _():
            for cp in main_copies(lambda e: 0):
                cp.wait()
            pend_ref[MAIN_PENDING] = 0

    col_t = col_t_ref[0]
    pieces = []
    for c in range(N_MAIN_ROWS // MAIN_CHUNK):
        row_id = lax.broadcasted_iota(jnp.int32, (MAIN_CHUNK, 1), 0) + c * MAIN_CHUNK
        sel, gate_row = _select_rows(row_id, col_t, gate_t)
        pieces.append((_dot(sel, h).astype(BF16), gate_row))

    def window_copies(slot, first_row):
        cps = []
        for wi in range(WIN_PER_STACK):
            rows = pl.ds(_seg_aligned(first_row(wi)), WIN)
            cps.append(pltpu.make_async_copy(stage_x.at[slot, pl.ds(wi * WIN, WIN)], buf_ref.at[rows], sem.at[slot]))
            cps.append(pltpu.make_async_copy(stage_g.at[slot, pl.ds(wi * WIN, WIN)], gs_ref.at[rows], sem.at[slot]))
        return cps

    def drain(slot):
        @pl.when(pend_ref[slot] == 1)
        def _():
            for cp in window_copies(slot, lambda wi: 0):
                cp.wait()
            pend_ref[slot] = 0

    drain_main()
    drain(0)
    drain(1)
    for c, (packed, gate_row) in enumerate(pieces):
        main_x[c * MAIN_CHUNK:(c + 1) * MAIN_CHUNK, :] = packed
        main_g[c * MAIN_CHUNK:(c + 1) * MAIN_CHUNK, :] = jnp.broadcast_to(gate_row, (MAIN_CHUNK, LANES))
    for n, cp in enumerate(main_copies(lambda e: ts_ref[base + e])):
        cp.start(priority=(n // 2) % 2)
    pend_ref[MAIN_PENDING] = 1

    @pl.when(ov_ref[i] > 0)
    def _():
        n_stacks = _build_worklist(ts_ref, tc_ref, wl_ref, junk_row)
        dest_t = dest_t_ref[0]

        def stack_body(s, c):
            slot = s & 1
            sel, gate_row = _select_rows(_stack_row_ids(wl_ref, s, (MOE_TILE, 1), 0), dest_t, gate_t)
            rows = _dot(sel, h)
            drain(slot)
            stage_x[slot] = rows.astype(BF16)
            stage_g[slot] = jnp.broadcast_to(gate_row, (MOE_TILE, LANES))
            for cp in window_copies(slot, lambda wi: wl_ref[s * WIN_PER_STACK + wi]):
                cp.start()
            pend_ref[slot] = 1
            return c

        lax.fori_loop(0, n_stacks, stack_body, 0)

    @pl.when(i == pl.num_programs(0) - 1)
    def _():
        drain_main()
        drain(0)
        drain(1)


def _moe_grid_spec(n_tiles, in_specs, out_specs, scratch_shapes):
    return pltpu.PrefetchScalarGridSpec(num_scalar_prefetch=4, grid=(n_tiles,), in_specs=in_specs,
                                        out_specs=out_specs, scratch_shapes=scratch_shapes)


def _dispatch(tile_start, tile_cnt, tile_ovf, region_ends, h2, col_t, dest_t, gate_t, n_rows):
    n_tiles = h2.shape[0] // MOE_TILE
    junk_row = n_rows - EXPERT_BLOCK
    any_spec = pl.BlockSpec(memory_space=pl.ANY)
    tok_t = pl.BlockSpec((1, 8, MOE_TILE), lambda i, *_: (i, 0, 0))
    return pl.pallas_call(
        functools.partial(_dispatch_kernel, junk_row=junk_row),
        out_shape=[jax.ShapeDtypeStruct((n_rows, D_MODEL), BF16), jax.ShapeDtypeStruct((n_rows, LANES), F32)],
        grid_spec=_moe_grid_spec(
            n_tiles,
            [pl.BlockSpec((MOE_TILE, D_MODEL), lambda i, *_: (i, 0)), tok_t, tok_t, tok_t],
            [any_spec, any_spec],
            [pltpu.SMEM((WORKLIST_LEN,), jnp.int32), pltpu.SMEM((3,), jnp.int32),
             pltpu.VMEM((N_MAIN_ROWS, D_MODEL), BF16), pltpu.VMEM((N_MAIN_ROWS, LANES), F32),
             pltpu.VMEM((2, MOE_TILE, D_MODEL), BF16), pltpu.VMEM((2, MOE_TILE, LANES), F32),
             pltpu.SemaphoreType.DMA(()), pltpu.SemaphoreType.DMA((2,))]),
        compiler_params=pltpu.CompilerParams(dimension_semantics=("arbitrary",),
                                             vmem_limit_bytes=56 * 1024 * 1024),
        name="moe_dispatch",
    )(tile_start, tile_cnt, tile_ovf, region_ends, h2, col_t, dest_t, gate_t)


def _select_cols(ids, col_id):
    sel = jnp.zeros((ids.shape[0], col_id.shape[1]), F32)
    for k in range(TOP_K):
        sel = jnp.where(ids[:, k:k + 1] == col_id, 1.0, sel)
    return sel.astype(BF16)


def _combine_kernel(ts_ref, tc_ref, ov_ref, re_ref, col_ref, dest_ref, x1_ref, mod_ref, y_ref, o_ref,
                    wl_ref, main_y, stage_y, acc_ref, main_sem, sem, *, junk_row):
    del re_ref
    i = pl.program_id(0)
    slot_i = i & 1

    def main_copies(slot, first_row):
        return [pltpu.make_async_copy(y_ref.at[pl.ds(_seg_aligned(first_row(e)), WIN)],
                                      main_y.at[slot, pl.ds(e * WIN, WIN)], main_sem.at[slot])
                for e in range(N_EXPERTS)]

    def fetch_main(tile, slot):
        for n, cp in enumerate(main_copies(slot, lambda e: ts_ref[tile * N_EXPERTS + e])):
            cp.start(priority=n % 2)

    @pl.when(i == 0)
    def _():
        fetch_main(0, 0)

    @pl.when(i + 1 < pl.num_programs(0))
    def _():
        fetch_main(i + 1, 1 - slot_i)

    for cp in main_copies(slot_i, lambda e: 0):
        cp.wait()

    col = col_ref[...]
    sel = jnp.concatenate(
        [_select_cols(col, lax.broadcasted_iota(jnp.int32, (1, MAIN_CHUNK), 1) + c * MAIN_CHUNK)
         for c in range(N_MAIN_ROWS // MAIN_CHUNK)], axis=1)
    acc_ref[...] = _dot(sel, main_y[slot_i])

    @pl.when(ov_ref[i] > 0)
    def _():
        n_stacks = _build_worklist(ts_ref, tc_ref, wl_ref, junk_row)
        dest = dest_ref[...]
        if WIN_PER_STACK * WIN < MOE_TILE:
            for slot in range(2):
                stage_y[slot, WIN_PER_STACK * WIN:, :] = jnp.zeros((MOE_TILE - WIN_PER_STACK * WIN, D_MODEL), BF16)

        def window_copies(slot, first_row):
            return [pltpu.make_async_copy(y_ref.at[pl.ds(_seg_aligned(first_row(wi)), WIN)],
                                          stage_y.at[slot, pl.ds(wi * WIN, WIN)], sem.at[slot])
                    for wi in range(WIN_PER_STACK)]

        def fetch(s, slot):
            for cp in window_copies(slot, lambda wi: wl_ref[s * WIN_PER_STACK + wi]):
                cp.start()

        @pl.when(n_stacks > 0)
        def _():
            fetch(0, 0)

        def stack_body(s, c):
            slot = s & 1

            @pl.when(s + 1 < n_stacks)
            def _():
                fetch(s + 1, 1 - slot)

            for cp in window_copies(slot, lambda wi: 0):
                cp.wait()
            sel_o = _select_cols(dest, _stack_row_ids(wl_ref, s, (1, MOE_TILE), 1))
            acc_ref[...] += _dot(sel_o, stage_y[slot])
            return c

        lax.fori_loop(0, n_stacks, stack_body, 0)

    g2 = mod_ref[0][5:6, :]
    o_ref[...] = x1_ref[...] + g2 * acc_ref[...]


def _combine(tile_start, tile_cnt, tile_ovf, region_ends, col, dest, x1, mod3, y, l):
    t = x1.shape[0]
    per_batch = l // MOE_TILE
    junk_row = y.shape[0] - EXPERT_BLOCK
    tok = lambda width: pl.BlockSpec((MOE_TILE, width), lambda i, *_: (i, 0))
    return pl.pallas_call(
        functools.partial(_combine_kernel, junk_row=junk_row),
        out_shape=jax.ShapeDtypeStruct((t, D_MODEL), F32),
        grid_spec=_moe_grid_spec(
            t // MOE_TILE,
            [tok(LANES), tok(LANES), tok(D_MODEL),
             pl.BlockSpec((1, 6, D_MODEL), lambda i, *_: (i // per_batch, 0, 0)),
             pl.BlockSpec(memory_space=pl.ANY)],
            tok(D_MODEL),
            [pltpu.SMEM((WORKLIST_LEN,), jnp.int32), pltpu.VMEM((2, N_MAIN_ROWS, D_MODEL), BF16),
             pltpu.VMEM((2, MOE_TILE, D_MODEL), BF16), pltpu.VMEM((MOE_TILE, D_MODEL), F32),
             pltpu.SemaphoreType.DMA((2,)), pltpu.SemaphoreType.DMA((2,))]),
        compiler_params=pltpu.CompilerParams(dimension_semantics=("arbitrary",),
                                             vmem_limit_bytes=56 * 1024 * 1024),
        name="moe_combine",
    )(tile_start, tile_cnt, tile_ovf, region_ends, col, dest, x1, mod3, y)


def _expert_kernel(be_ref, nused_ref, x_ref, g_ref, w1_ref, b1_ref, w2_ref, b2_ref, y_ref, w1b_ref, w2b_ref):
    i = pl.program_id(0)
    live = i < nused_ref[0]

    @pl.when(live & ((i == 0) | (be_ref[i] != be_ref[jnp.maximum(i - 1, 0)])))
    def _():
        w1b_ref[...] = w1_ref[0].astype(BF16)
        w2b_ref[...] = w2_ref[0].astype(BF16)

    @pl.when(live)
    def _():
        gu = _dot(x_ref[...], w1b_ref[...]) + b1_ref[0]
        glu = jnp.minimum(gu[:, :D_EXPERT], SWIGLU_LIMIT)
        lin = jnp.clip(gu[:, D_EXPERT:], -SWIGLU_LIMIT, SWIGLU_LIMIT)
        act = glu * (1.0 / (1.0 + jnp.exp(-SWIGLU_ALPHA * glu))) * (lin + 1.0)
        y = (_dot(act.astype(BF16), w2b_ref[...]) + b2_ref[0]) * g_ref[:, 0:1]
        y_ref[...] = y.astype(BF16)

    @pl.when(pl.program_id(0) >= nused_ref[0])
    def _():
        y_ref[...] = jnp.zeros_like(y_ref)


def _experts(block_e, nused, buf, gs, w1, b1, w2, b2):
    nb = buf.shape[0] // EXPERT_BLOCK
    row = lambda i, be, nu: (jnp.minimum(i, nu[0] - 1), 0)
    out_row = lambda i, be, nu: (i, 0)
    exp3 = lambda i, be, nu: (be[jnp.minimum(i, nu[0] - 1)], 0, 0)
    return pl.pallas_call(
        _expert_kernel,
        out_shape=jax.ShapeDtypeStruct((buf.shape[0], D_MODEL), BF16),
        grid_spec=pltpu.PrefetchScalarGridSpec(
            num_scalar_prefetch=2,
            grid=(nb,),
            in_specs=[pl.BlockSpec((EXPERT_BLOCK, D_MODEL), row),
                      pl.BlockSpec((EXPERT_BLOCK, LANES), row),
                      pl.BlockSpec((1, D_MODEL, 2 * D_EXPERT), exp3),
                      pl.BlockSpec((1, 1, 2 * D_EXPERT), exp3),
                      pl.BlockSpec((1, D_EXPERT, D_MODEL), exp3),
                      pl.BlockSpec((1, 1, D_MODEL), exp3)],
            out_specs=pl.BlockSpec((EXPERT_BLOCK, D_MODEL), out_row),
            scratch_shapes=[pltpu.VMEM((D_MODEL, 2 * D_EXPERT), BF16), pltpu.VMEM((D_EXPERT, D_MODEL), BF16)]),
        compiler_params=pltpu.CompilerParams(dimension_semantics=("arbitrary",),
                                             vmem_limit_bytes=56 * 1024 * 1024),
        name="moe_experts",
    )(block_e, nused, buf, gs, w1, b1, w2, b2)


def _rope_tables(seq_len):
    pos = jnp.arange(seq_len, dtype=jnp.int32)
    row = (pos // GRID_W).astype(F32)[:, None]
    col = (pos % GRID_W).astype(F32)[:, None]

    def table(d_axis, pad):
        inv_freq = ROPE_THETA ** (-jnp.arange(0, d_axis, 2, dtype=F32) / d_axis)
        ar, ac = row * inv_freq[None, :], col * inv_freq[None, :]
        cos = jnp.concatenate([jnp.cos(ar), jnp.cos(ar), jnp.cos(ac), jnp.cos(ac)], axis=-1)
        sin = jnp.concatenate([-jnp.sin(ar), jnp.sin(ar), -jnp.sin(ac), jnp.sin(ac)], axis=-1)
        if pad:
            cos = jnp.concatenate([cos, jnp.ones((seq_len, pad), F32)], axis=-1)
            sin = jnp.concatenate([sin, jnp.zeros((seq_len, pad), F32)], axis=-1)
        return cos, sin

    cg, sg = table(GQA_HEAD_DIM // 2, 0)
    cm, sm = table(MLA_ROPE_DIM // 2, LANES - MLA_ROPE_DIM)
    return cg, sg, cm, sm


def _pad_lanes(a, width):
    return jnp.pad(a, [(0, 0)] * (a.ndim - 1) + [(0, width - a.shape[-1])])


def _prep_weights(p):
    w_in = p["w_in"]
    w = {"norm_mix": p["norm_mix"].reshape(1, D_MODEL), "norm_ffn": p["norm_ffn"].reshape(1, D_MODEL)}
    w["wq"] = w_in[:, :OFF_GQA_K].astype(BF16)
    w["wk"] = w_in[:, OFF_GQA_K:OFF_GQA_V].astype(BF16)
    w["wv"] = w_in[:, OFF_GQA_V:OFF_MLA_QA].T.astype(BF16)
    w["wqa"] = w_in[:, OFF_MLA_QA:OFF_MLA_KVA].astype(BF16)
    w["wckv"] = w_in[:, OFF_MLA_KVA:OFF_MLA_KVA + MLA_KV_RANK].astype(BF16)
    w["wkr"] = _pad_lanes(w_in[:, OFF_MLA_KVA + MLA_KV_RANK:OFF_GATE], LANES).astype(BF16)
    w["wg"] = w_in[:, OFF_GATE:].astype(BF16)
    wqb = p["mla_w_qb"].reshape(MLA_Q_RANK, MLA_HEADS, MLA_QK_DIM)
    w["wqb"] = _pad_lanes(wqb, MLA_HEAD_PAD).reshape(MLA_Q_RANK, MLA_HEADS * MLA_HEAD_PAD).astype(BF16)
    wkvb = p["mla_w_kvb"].reshape(MLA_KV_RANK, MLA_HEADS, MLA_NOPE_DIM + MLA_V_DIM)
    w["wkb"] = wkvb[:, :, :MLA_NOPE_DIM].reshape(MLA_KV_RANK, MLA_HEADS * MLA_NOPE_DIM).astype(BF16)
    w["wvb"] = wkvb[:, :, MLA_NOPE_DIM:].reshape(MLA_KV_RANK, MLA_V_W).T.astype(BF16)
    w["gq"] = p["gqa_q_norm"].reshape(1, GQA_HEAD_DIM)
    w["gk"] = p["gqa_k_norm"].reshape(1, GQA_HEAD_DIM)
    w["gqa"] = p["mla_q_a_norm"].reshape(1, MLA_Q_RANK)
    w["gkva"] = p["mla_kv_a_norm"].reshape(1, MLA_KV_RANK)
    w["gmq"] = _pad_lanes(p["mla_q_norm"].reshape(1, MLA_QK_DIM), MLA_HEAD_PAD)
    w["gmk"] = _pad_lanes(p["mla_k_norm"].reshape(1, MLA_QK_DIM), MLA_HEAD_PAD)
    w["woa"] = p["w_o_gqa"].astype(BF16)
    w["wob"] = p["w_o_mla"].astype(BF16)
    w["wout"] = p["w_out"].astype(BF16)
    rw = _pad_lanes(p["router_w"], LANES)
    w["rw_hi"] = rw.astype(BF16)
    w["rw_lo"] = (rw - w["rw_hi"].astype(F32)).astype(BF16)
    w["rb"] = _pad_lanes(p["router_b"].reshape(1, N_EXPERTS), LANES)
    return w


def _layer(x, c, ctx, c_ctx, p):
    b, l, _ = x.shape
    lc = ctx.shape[1]
    t = b * l
    w = _prep_weights(p)

    n_mod_rows = -(-(b + 1) // 8) * 8
    cc = jnp.zeros((n_mod_rows, D_MODEL), F32).at[:b].set(c).at[b].set(c_ctx)
    mod3 = _ada_mod(cc, p["ada_w"], p["ada_b"]).reshape(n_mod_rows, 6, D_MODEL)

    tables = _rope_tables(l)
    ident = (jnp.ones((lc, LANES), F32), jnp.zeros((lc, LANES), F32)) * 2
    tm = min(512, l)
    q_a, k_a, v_a, q_m, k_m, v_m, gates = _proj(x, mod3, lambda bi: bi, tables, w, True, tm)
    kc_a, vc_a, kc_m, vc_m = _proj(ctx, mod3, lambda bi: b, ident, w, False, min(256, lc))

    tq, tk = min(2048, l), min(1024, l)
    o_a = _attention(q_a, k_a, v_a, kc_a, vc_a, GQA_HEADS, GQA_GROUP, GQA_HEAD_DIM, GQA_HEAD_DIM, tq, tk, "attn_gqa",
                     _score_bound(p["gqa_q_norm"], p["gqa_k_norm"], GQA_HEAD_DIM))
    o_m = _attention(q_m, k_m, v_m, kc_m, vc_m, MLA_HEADS, 1, MLA_HEAD_PAD, MLA_V_DIM, tq, tk, "attn_mla",
                     _score_bound(p["mla_q_norm"], p["mla_k_norm"], MLA_QK_DIM))

    x1, h2, eidx, egate = _merge(o_a, o_m, gates, x, mod3, w, tm)
    x1, h2 = x1.reshape(t, D_MODEL), h2.reshape(t, D_MODEL)
    eidx, egate = eidx.reshape(t, LANES), egate.reshape(t, LANES)

    assert t % MOE_TILE == 0
    n_tiles = t // MOE_TILE
    rank, before_raw, tcnt, cnt = _route(eidx, MOE_TILE)
    seg_rows = cnt[0, :N_EXPERTS].astype(jnp.int32)
    before = before_raw[:, 0, :N_EXPERTS].astype(jnp.int32)
    region = (seg_rows + WIN + EXPERT_BLOCK - 1) // EXPERT_BLOCK * EXPERT_BLOCK
    region_ends = jnp.cumsum(region)
    region_starts = region_ends - region
    max_rows = t * TOP_K + n_tiles * N_EXPERTS * (SEG_ALIGN - 1) + N_EXPERTS * (WIN + EXPERT_BLOCK - 1)
    n_blocks = max_rows // EXPERT_BLOCK + 1
    n_rows = n_blocks * EXPERT_BLOCK
    block_row = jnp.arange(n_blocks, dtype=jnp.int32) * EXPERT_BLOCK
    block_e = jnp.minimum(jnp.sum(region_ends[None, :] <= block_row[:, None], axis=1), N_EXPERTS - 1).astype(jnp.int32)
    nused = (region_ends[-1:] // EXPERT_BLOCK).astype(jnp.int32)
    tile_start = (region_starts[None, :] + before).reshape(-1)
    tile_cnt = tcnt[:, 0, :N_EXPERTS].astype(jnp.int32)
    tile_ovf = jnp.sum(jnp.maximum((tile_cnt + WIN - 1) // WIN - 1, 0), axis=1).astype(jnp.int32)
    tile_cnt = tile_cnt.reshape(-1)
    start = jnp.zeros((8, LANES), F32).at[0, :N_EXPERTS].set(region_starts.astype(F32))
    dest, col, dest_t, col_t, gate_t = _dest(eidx, rank, egate, start, before_raw, MOE_TILE)

    region_ends = region_ends.astype(jnp.int32)
    buf, gs = _dispatch(tile_start, tile_cnt, tile_ovf, region_ends, h2, col_t, dest_t, gate_t, n_rows)
    y = _experts(block_e, nused, buf, gs, p["expert_w1"], p["expert_b1"].reshape(N_EXPERTS, 1, -1),
                 p["expert_w2"], p["expert_b2"].reshape(N_EXPERTS, 1, -1))
    out = _combine(tile_start, tile_cnt, tile_ovf, region_ends, col, dest, x1, mod3, y, l)
    return out.reshape(b, l, D_MODEL)


def kernel(x, c, ctx, c_ctx, ada_w, ada_b, norm_mix, norm_ffn, w_in, gqa_q_norm, gqa_k_norm, mla_q_a_norm, mla_kv_a_norm, mla_w_qb, mla_w_kvb, mla_q_norm, mla_k_norm, w_o_gqa, w_o_mla, w_out, router_w, router_b, expert_w1, expert_b1, expert_w2, expert_b2):
    assert ada_w.shape[0] == 1, "single-layer problem: the context stream is never updated"
    p = {
        "ada_w": ada_w[0], "ada_b": ada_b[0], "norm_mix": norm_mix[0], "norm_ffn": norm_ffn[0],
        "w_in": w_in[0], "gqa_q_norm": gqa_q_norm[0], "gqa_k_norm": gqa_k_norm[0],
        "mla_q_a_norm": mla_q_a_norm[0], "mla_kv_a_norm": mla_kv_a_norm[0],
        "mla_w_qb": mla_w_qb[0], "mla_w_kvb": mla_w_kvb[0],
        "mla_q_norm": mla_q_norm[0], "mla_k_norm": mla_k_norm[0],
        "w_o_gqa": w_o_gqa[0], "w_o_mla": w_o_mla[0], "w_out": w_out[0],
        "router_w": router_w[0], "router_b": router_b[0],
        "expert_w1": expert_w1[0], "expert_b1": expert_b1[0],
        "expert_w2": expert_w2[0], "expert_b2": expert_b2[0],
    }
    return _layer(x, c, ctx, c_ctx, p)
```

```python
import functools
import math

import jax
import jax.numpy as jnp
from jax import lax
from jax.experimental import pallas as pl
from jax.experimental.pallas import tpu as pltpu

D_MODEL = 1024
GRID_W = 64
EPS = 1e-6
ROPE_THETA = 10000.0

GQA_HEADS = 8
GQA_KV_HEADS = 2
GQA_GROUP = GQA_HEADS // GQA_KV_HEADS
GQA_HEAD_DIM = 128
GQA_Q_W = GQA_HEADS * GQA_HEAD_DIM
GQA_KV_W = GQA_KV_HEADS * GQA_HEAD_DIM

MLA_HEADS = 8
MLA_Q_RANK = 256
MLA_KV_RANK = 128
MLA_NOPE_DIM = 128
MLA_ROPE_DIM = 64
MLA_V_DIM = 128
MLA_QK_DIM = MLA_NOPE_DIM + MLA_ROPE_DIM
MLA_HEAD_PAD = 256
MLA_V_W = MLA_HEADS * MLA_V_DIM

OFF_GQA_K = GQA_Q_W
OFF_GQA_V = OFF_GQA_K + GQA_KV_W
OFF_MLA_QA = OFF_GQA_V + GQA_KV_W
OFF_MLA_KVA = OFF_MLA_QA + MLA_Q_RANK
OFF_GATE = OFF_MLA_KVA + MLA_KV_RANK + MLA_ROPE_DIM

N_EXPERTS = 32
TOP_K = 4
D_EXPERT = 1024
SWIGLU_LIMIT = 7.0
SWIGLU_ALPHA = 1.702
EXPERT_BLOCK = 512

LANES = 128
NEG_INF = float("-inf")
LOG2_E = math.log2(math.e)

BF16 = jnp.bfloat16
F32 = jnp.float32


def _dot(a, b):
    return jnp.dot(a, b, preferred_element_type=F32)


def _dot_nt(a, b):
    return lax.dot_general(a, b, (((1,), (1,)), ((), ())), preferred_element_type=F32)


def _split_bf16(a):
    hi = a.astype(BF16)
    lo = (a - hi.astype(F32)).astype(BF16)
    return hi, lo


def _rms(x, g, n):
    ms = jnp.sum(x * x, axis=-1, keepdims=True) * (1.0 / n)
    return x * lax.rsqrt(ms + EPS) * g


def _swap_halves(x, k):
    lane = lax.broadcasted_iota(jnp.int32, x.shape, 1)
    return jnp.where((lane & k) != 0, pltpu.roll(x, k, 1), pltpu.roll(x, LANES - k, 1))


def _rope(x, cos, sin_signed, k):
    return x * cos + _swap_halves(x, k) * sin_signed


def _ada_kernel(c_ref, w_ref, b_ref, o_ref):
    c = c_ref[...]
    s = c * (1.0 / (1.0 + jnp.exp(-c)))
    s_hi, s_lo = _split_bf16(s)
    w_hi, w_lo = _split_bf16(w_ref[...])
    o_ref[...] = _dot(s_hi, w_hi) + (_dot(s_hi, w_lo) + _dot(s_lo, w_hi)) + b_ref[...]


def _ada_mod(cc, ada_w, ada_b):
    n = ada_w.shape[1]
    tn = 1024
    return pl.pallas_call(
        _ada_kernel,
        out_shape=jax.ShapeDtypeStruct((cc.shape[0], n), F32),
        grid=(n // tn,),
        in_specs=[
            pl.BlockSpec((cc.shape[0], D_MODEL), lambda j: (0, 0)),
            pl.BlockSpec((D_MODEL, tn), lambda j: (0, j)),
            pl.BlockSpec((1, tn), lambda j: (0, j)),
        ],
        out_specs=pl.BlockSpec((cc.shape[0], tn), lambda j: (0, j)),
        compiler_params=pltpu.CompilerParams(dimension_semantics=("parallel",)),
        name="ada_mod",
    )(cc, ada_w, ada_b.reshape(1, n))


def _proj_kernel(*refs, with_q):
    if with_q:
        (x_ref, mod_ref, nmix_ref, cg_ref, sg_ref, cm_ref, sm_ref,
         wq_ref, wk_ref, wv_ref, wqa_ref, wckv_ref, wkr_ref, wg_ref, wqb_ref, wkb_ref, wvb_ref,
         gq_ref, gk_ref, gqa_ref, gkva_ref, gmq_ref, gmk_ref,
         q_ref, k_ref, v_ref, qm_ref, km_ref, vm_ref, gate_ref) = refs
    else:
        (x_ref, mod_ref, nmix_ref, cg_ref, sg_ref, cm_ref, sm_ref,
         wk_ref, wv_ref, wckv_ref, wkr_ref, wkb_ref, wvb_ref,
         gk_ref, gkva_ref, gmk_ref,
         k_ref, v_ref, km_ref, vm_ref) = refs

    x = x_ref[0]
    mod = mod_ref[0]
    shift, scale = mod[0:1, :], mod[1:2, :]
    h = _rms(x, nmix_ref[...], D_MODEL) * (1.0 + scale) + shift
    hb = h.astype(BF16)
    cg, sg = cg_ref[...], sg_ref[...]
    cm, sm = cm_ref[...], sm_ref[...]

    kk = _dot(hb, wk_ref[...])
    for j in range(GQA_KV_HEADS):
        sl = slice(j * GQA_HEAD_DIM, (j + 1) * GQA_HEAD_DIM)
        kn = _rms(kk[:, sl], gk_ref[...], GQA_HEAD_DIM)
        k_ref[0, :, sl] = _rope(kn, cg, sg, 32).astype(BF16)
    v_ref[0] = _dot_nt(wv_ref[...], hb).astype(BF16)

    ckv = _rms(_dot(hb, wckv_ref[...]), gkva_ref[...], MLA_KV_RANK).astype(BF16)
    vm_ref[0] = _dot_nt(wvb_ref[...], ckv).astype(BF16)
    knope = _dot(ckv, wkb_ref[...])
    kr = _dot(hb, wkr_ref[...])
    gmk = gmk_ref[...]
    g_nope, g_rope = gmk[:, :MLA_NOPE_DIM], gmk[:, MLA_NOPE_DIM:]
    ssq_r = jnp.sum(kr * kr, axis=-1, keepdims=True)
    kr_roped = _rope(kr * g_rope, cm, sm, 16)
    for j in range(MLA_HEADS):
        kn = knope[:, j * MLA_NOPE_DIM:(j + 1) * MLA_NOPE_DIM]
        ms = (jnp.sum(kn * kn, axis=-1, keepdims=True) + ssq_r) * (1.0 / MLA_QK_DIM)
        r = lax.rsqrt(ms + EPS)
        base = j * MLA_HEAD_PAD
        km_ref[0, :, base:base + MLA_NOPE_DIM] = (kn * r * g_nope).astype(BF16)
        km_ref[0, :, base + MLA_NOPE_DIM:base + MLA_HEAD_PAD] = (kr_roped * r).astype(BF16)

    if not with_q:
        return

    qq = _dot(hb, wq_ref[...])
    q_scale = GQA_HEAD_DIM ** -0.5 * LOG2_E
    for j in range(GQA_HEADS):
        sl = slice(j * GQA_HEAD_DIM, (j + 1) * GQA_HEAD_DIM)
        qn = _rms(qq[:, sl], gq_ref[...], GQA_HEAD_DIM)
        q_ref[0, :, sl] = (_rope(qn, cg, sg, 32) * q_scale).astype(BF16)

    qa = _rms(_dot(hb, wqa_ref[...]), gqa_ref[...], MLA_Q_RANK).astype(BF16)
    q2 = _dot(qa, wqb_ref[...])
    gmq = gmq_ref[...]
    gq_nope, gq_rope = gmq[:, :MLA_NOPE_DIM], gmq[:, MLA_NOPE_DIM:]
    m_scale = MLA_QK_DIM ** -0.5 * LOG2_E
    for j in range(MLA_HEADS):
        base = j * MLA_HEAD_PAD
        qn = q2[:, base:base + MLA_NOPE_DIM]
        qr = q2[:, base + MLA_NOPE_DIM:base + MLA_HEAD_PAD]
        ms = (jnp.sum(qn * qn, axis=-1, keepdims=True) + jnp.sum(qr * qr, axis=-1, keepdims=True)) * (1.0 / MLA_QK_DIM)
        r = lax.rsqrt(ms + EPS)
        qm_ref[0, :, base:base + MLA_NOPE_DIM] = (qn * r * gq_nope * m_scale).astype(BF16)
        qm_ref[0, :, base + MLA_NOPE_DIM:base + MLA_HEAD_PAD] = (_rope(qr * r * gq_rope, cm, sm, 16) * m_scale).astype(BF16)

    gl = _dot(hb, wg_ref[...])
    gate_ref[0] = (1.0 / (1.0 + jnp.exp(-gl))).astype(BF16)


def _proj(x, mod3, mod_row_of_batch, tables, w, with_q, tm):
    b, l, _ = x.shape
    cg, sg, cm, sm = tables
    const = lambda shape: pl.BlockSpec(shape, lambda bi, i: (0,) * len(shape), pipeline_mode=pl.Buffered(1))
    tab = pl.BlockSpec((tm, LANES), lambda bi, i: (i, 0))
    in_specs = [
        pl.BlockSpec((1, tm, D_MODEL), lambda bi, i: (bi, i, 0)),
        pl.BlockSpec((1, 6, D_MODEL), lambda bi, i: (mod_row_of_batch(bi), 0, 0)),
        const((1, D_MODEL)), tab, tab, tab, tab,
    ]
    if with_q:
        weights = [w["wq"], w["wk"], w["wv"], w["wqa"], w["wckv"], w["wkr"], w["wg"], w["wqb"], w["wkb"], w["wvb"],
                   w["gq"], w["gk"], w["gqa"], w["gkva"], w["gmq"], w["gmk"]]
    else:
        weights = [w["wk"], w["wv"], w["wckv"], w["wkr"], w["wkb"], w["wvb"], w["gk"], w["gkva"], w["gmk"]]
    in_specs += [const(a.shape) for a in weights]

    def out(width):
        return jax.ShapeDtypeStruct((b, l, width), BF16), pl.BlockSpec((1, tm, width), lambda bi, i: (bi, i, 0))

    def out_t(width):
        return jax.ShapeDtypeStruct((b, width, l), BF16), pl.BlockSpec((1, width, tm), lambda bi, i: (bi, 0, i))

    outs = [out(GQA_KV_W), out_t(GQA_KV_W), out(MLA_HEADS * MLA_HEAD_PAD), out_t(MLA_V_W)]
    if with_q:
        outs = [out(GQA_Q_W)] + outs[:2] + [out(MLA_HEADS * MLA_HEAD_PAD)] + outs[2:] + [out(2 * D_MODEL)]
    return pl.pallas_call(
        functools.partial(_proj_kernel, with_q=with_q),
        out_shape=[o[0] for o in outs],
        grid=(b, l // tm),
        in_specs=in_specs,
        out_specs=[o[1] for o in outs],
        compiler_params=pltpu.CompilerParams(dimension_semantics=("parallel", "parallel"),
                                             vmem_limit_bytes=56 * 1024 * 1024),
        name="proj_latent" if with_q else "proj_ctx",
    )(x, mod3, w["norm_mix"], cg, sg, cm, sm, *weights)


def _attn_kernel(q_ref, kl_ref, vl_ref, kc_ref, vc_ref, o_ref, *, tk, n_lat, bounded):
    q = q_ref[0]
    tq = q.shape[0]
    dv = vl_ref.shape[1]
    n = n_lat // tk

    if bounded:
        l = jnp.zeros((1, tq), F32)
        acc = jnp.zeros((dv, tq), F32)
        for k, vt in [(kl_ref[0, j * tk:(j + 1) * tk, :], vl_ref[0, :, j * tk:(j + 1) * tk]) for j in range(n)] + [
                (kc_ref[0], vc_ref[0])]:
            p = jnp.exp2(_dot_nt(k, q))
            l = l + jnp.sum(p, axis=0, keepdims=True)
            acc = acc + _dot(vt, p.astype(BF16))
        o_ref[0] = (acc / l).T.astype(BF16)
        return

    def update(carry, s, vt):
        m, l, acc = carry
        m_new = jnp.maximum(m, jnp.max(s, axis=0, keepdims=True))
        alpha = jnp.exp2(m - m_new)
        p = jnp.exp2(s - m_new)
        l = alpha * l + jnp.sum(p, axis=0, keepdims=True)
        return m_new, l, alpha * acc + _dot(vt, p.astype(BF16))

    n = n_lat // tk
    keys = [kl_ref[0, j * tk:(j + 1) * tk, :] for j in range(n)] + [kc_ref[0]]
    vals = [vl_ref[0, :, j * tk:(j + 1) * tk] for j in range(n)] + [vc_ref[0]]
    carry = (jnp.full((1, tq), NEG_INF, F32), jnp.zeros((1, tq), F32), jnp.zeros((dv, tq), F32))
    s = _dot_nt(keys[0], q)
    for j in range(n + 1):
        s_next = _dot_nt(keys[j + 1], q) if j < n else None
        carry = update(carry, s, vals[j])
        s = s_next
    m, l, acc = carry
    o_ref[0] = (acc / l).T.astype(BF16)


SOFTMAX_SAFE_EXPONENT = 56.0


def _score_bound(gain_q, gain_k, dim):
    return dim * jnp.max(jnp.abs(gain_q)) * jnp.max(jnp.abs(gain_k)) * (dim ** -0.5 * LOG2_E) * 1.02


def _attention(q, k_lat, v_lat, k_ctx, v_ctx, n_heads, group, d_qk, dv, tq, tk, name, score_bound):
    run = functools.partial(_attention_call, q, k_lat, v_lat, k_ctx, v_ctx, n_heads, group, d_qk, dv, tq, tk, name)
    return lax.cond(score_bound <= SOFTMAX_SAFE_EXPONENT, lambda: run(True), lambda: run(False))


def _attention_call(q, k_lat, v_lat, k_ctx, v_ctx, n_heads, group, d_qk, dv, tq, tk, name, bounded):
    b, l, _ = q.shape
    lc = k_ctx.shape[1]
    name = name + ("_bounded" if bounded else "_online")
    return pl.pallas_call(
        functools.partial(_attn_kernel, tk=tk, n_lat=l, bounded=bounded),
        out_shape=jax.ShapeDtypeStruct((b, l, n_heads * dv), BF16),
        grid=(b, n_heads, l // tq),
        in_specs=[
            pl.BlockSpec((1, tq, d_qk), lambda bi, h, i: (bi, i, h)),
            pl.BlockSpec((1, l, d_qk), lambda bi, h, i: (bi, 0, h // group)),
            pl.BlockSpec((1, dv, l), lambda bi, h, i: (bi, h // group, 0)),
            pl.BlockSpec((1, lc, d_qk), lambda bi, h, i: (bi, 0, h // group)),
            pl.BlockSpec((1, dv, lc), lambda bi, h, i: (bi, h // group, 0)),
        ],
        out_specs=pl.BlockSpec((1, tq, dv), lambda bi, h, i: (bi, i, h)),
        compiler_params=pltpu.CompilerParams(dimension_semantics=("parallel", "parallel", "parallel"),
                                             vmem_limit_bytes=56 * 1024 * 1024),
        name=name,
    )(q, k_lat, v_lat, k_ctx, v_ctx)


def _merge_kernel(oa_ref, ob_ref, gate_ref, x_ref, mod_ref, woa_ref, wob_ref, wout_ref, nffn_ref,
                  rwh_ref, rwl_ref, rb_ref, x1_ref, h2_ref, eidx_ref, egate_ref):
    mod = mod_ref[0]
    g1, shift2, scale2 = mod[2:3, :], mod[3:4, :], mod[4:5, :]
    ya = _dot(oa_ref[0], woa_ref[...])
    yb = _dot(ob_ref[0], wob_ref[...])
    g = gate_ref[0].astype(F32)
    y = g[:, :D_MODEL] * ya + g[:, D_MODEL:] * yb
    z = _dot(y.astype(BF16), wout_ref[...])
    x1 = x_ref[0] + g1 * z
    x1_ref[0] = x1
    h2 = _rms(x1, nffn_ref[...], D_MODEL) * (1.0 + scale2) + shift2
    h2_ref[0] = h2.astype(BF16)

    h_hi, h_lo = _split_bf16(h2)
    logits = _dot(h_hi, rwh_ref[...]) + (_dot(h_hi, rwl_ref[...]) + _dot(h_lo, rwh_ref[...])) + rb_ref[...]
    lane = lax.broadcasted_iota(jnp.int32, logits.shape, 1).astype(F32)
    cur = jnp.where(lane < N_EXPERTS, logits, NEG_INF)
    vals, idxs = [], []
    for _ in range(TOP_K):
        mx = jnp.max(cur, axis=-1, keepdims=True)
        ix = jnp.min(jnp.where(cur == mx, lane, float(LANES)), axis=-1, keepdims=True)
        vals.append(mx)
        idxs.append(ix)
        cur = jnp.where(lane == ix, NEG_INF, cur)
    ex = [jnp.exp(v - vals[0]) for v in vals]
    den = ex[0] + ex[1] + ex[2] + ex[3]
    eidx = jnp.zeros(logits.shape, F32)
    egate = jnp.zeros(logits.shape, F32)
    for k in range(TOP_K):
        eidx = jnp.where(lane == k, idxs[k], eidx)
        egate = jnp.where(lane == k, ex[k] / den, egate)
    eidx_ref[0] = eidx.astype(jnp.int32)
    egate_ref[0] = egate


def _merge(o_a, o_b, gates, x, mod3, w, tm):
    b, l, _ = x.shape
    const = lambda shape: pl.BlockSpec(shape, lambda bi, i: (0,) * len(shape), pipeline_mode=pl.Buffered(1))
    tok = lambda width: pl.BlockSpec((1, tm, width), lambda bi, i: (bi, i, 0))
    return pl.pallas_call(
        _merge_kernel,
        out_shape=[jax.ShapeDtypeStruct((b, l, D_MODEL), F32), jax.ShapeDtypeStruct((b, l, D_MODEL), BF16),
                   jax.ShapeDtypeStruct((b, l, LANES), jnp.int32), jax.ShapeDtypeStruct((b, l, LANES), F32)],
        grid=(b, l // tm),
        in_specs=[tok(GQA_Q_W), tok(MLA_V_W), tok(2 * D_MODEL), tok(D_MODEL),
                  pl.BlockSpec((1, 6, D_MODEL), lambda bi, i: (bi, 0, 0)),
                  const((GQA_Q_W, D_MODEL)), const((MLA_V_W, D_MODEL)), const((D_MODEL, D_MODEL)), const((1, D_MODEL)),
                  const((D_MODEL, LANES)), const((D_MODEL, LANES)), const((1, LANES))],
        out_specs=[tok(D_MODEL), tok(D_MODEL), tok(LANES), tok(LANES)],
        compiler_params=pltpu.CompilerParams(dimension_semantics=("parallel", "parallel"),
                                             vmem_limit_bytes=56 * 1024 * 1024),
        name="merge_router",
    )(o_a, o_b, gates, x, mod3, w["woa"], w["wob"], w["wout"], w["norm_ffn"], w["rw_hi"], w["rw_lo"], w["rb"])


def _onehots(idx, lane):
    return [lane == idx[:, k:k + 1] for k in range(TOP_K)]


SEG_ALIGN = 16


def _route_kernel(eidx_ref, rank_ref, before_ref, tcnt_ref, cnt_ref, carry_ref):
    @pl.when(pl.program_id(0) == 0)
    def _():
        carry_ref[...] = jnp.zeros_like(carry_ref)

    before_ref[0] = carry_ref[...]
    idx = eidx_ref[...]
    tm = idx.shape[0]
    lane = lax.broadcasted_iota(jnp.int32, idx.shape, 1)
    oh = _onehots(idx, lane)
    total = jnp.zeros(idx.shape, F32)
    for k in range(TOP_K):
        total = total + jnp.where(oh[k], 1.0, 0.0)
    row = lax.broadcasted_iota(jnp.int32, (tm, tm), 0)
    col = lax.broadcasted_iota(jnp.int32, (tm, tm), 1)
    tri = jnp.where(row > col, 1.0, 0.0).astype(BF16)
    before = _dot(tri, total.astype(BF16)) + carry_ref[0:1, :]
    rank = jnp.where(lane < TOP_K, jnp.take_along_axis(before, idx, axis=1), 0.0)
    rank_ref[...] = rank.astype(jnp.int32)
    tile_cnt = jnp.sum(total, axis=0, keepdims=True)
    tcnt_ref[0] = jnp.broadcast_to(tile_cnt, carry_ref.shape)
    carry_ref[...] = carry_ref[...] + tile_cnt
    cnt_ref[...] = carry_ref[...]


def _route(eidx, tm):
    t = eidx.shape[0]
    return pl.pallas_call(
        _route_kernel,
        out_shape=[jax.ShapeDtypeStruct((t, LANES), jnp.int32), jax.ShapeDtypeStruct((t // tm, 8, LANES), F32),
                   jax.ShapeDtypeStruct((t // tm, 8, LANES), F32), jax.ShapeDtypeStruct((8, LANES), F32)],
        grid=(t // tm,),
        in_specs=[pl.BlockSpec((tm, LANES), lambda i: (i, 0))],
        out_specs=[pl.BlockSpec((tm, LANES), lambda i: (i, 0)), pl.BlockSpec((1, 8, LANES), lambda i: (i, 0, 0)),
                   pl.BlockSpec((1, 8, LANES), lambda i: (i, 0, 0)), pl.BlockSpec((8, LANES), lambda i: (0, 0))],
        scratch_shapes=[pltpu.VMEM((8, LANES), F32)],
        compiler_params=pltpu.CompilerParams(dimension_semantics=("arbitrary",)),
        name="route_rank",
    )(eidx)


def _dest_kernel(eidx_ref, rank_ref, egate_ref, start_ref, before_ref, dest_ref, col_ref, dest_t_ref, col_t_ref,
                 gate_t_ref):
    idx = eidx_ref[...]
    valid = lax.broadcasted_iota(jnp.int32, idx.shape, 1) < TOP_K
    region_start = jnp.take_along_axis(jnp.broadcast_to(start_ref[0:1, :], idx.shape), idx, axis=1)
    seg_start = jnp.take_along_axis(jnp.broadcast_to(before_ref[0][0:1, :], idx.shape), idx, axis=1)
    rank = rank_ref[...].astype(F32)
    window_start = jnp.floor(seg_start * (1.0 / SEG_ALIGN)) * SEG_ALIGN
    local = rank - window_start
    dest = jnp.where(valid, region_start + rank, 0.0)
    col = jnp.where(valid & (local < WIN), idx.astype(F32) * WIN + local, -1.0)
    dest_ref[...] = dest.astype(jnp.int32)
    col_ref[...] = col.astype(jnp.int32)
    dest_t_ref[0] = dest.T[0:8, :].astype(jnp.int32)
    col_t_ref[0] = col.T[0:8, :].astype(jnp.int32)
    gate_t_ref[0] = egate_ref[...].T[0:8, :]


def _dest(eidx, rank, egate, start, before, tm):
    t = eidx.shape[0]
    blk = pl.BlockSpec((tm, LANES), lambda i: (i, 0))
    blk_t = pl.BlockSpec((1, 8, tm), lambda i: (i, 0, 0))
    tok = jax.ShapeDtypeStruct((t, LANES), jnp.int32)
    tok_t = jax.ShapeDtypeStruct((t // tm, 8, tm), jnp.int32)
    return pl.pallas_call(
        _dest_kernel,
        out_shape=[tok, tok, tok_t, tok_t, jax.ShapeDtypeStruct((t // tm, 8, tm), F32)],
        grid=(t // tm,),
        in_specs=[blk, blk, blk, pl.BlockSpec((8, LANES), lambda i: (0, 0)),
                  pl.BlockSpec((1, 8, LANES), lambda i: (i, 0, 0))],
        out_specs=[blk, blk, blk_t, blk_t, blk_t],
        compiler_params=pltpu.CompilerParams(dimension_semantics=("parallel",)),
        name="route_dest",
    )(eidx, rank, egate, start, before)


MOE_TILE = 256
WIN = 64
N_MAIN_ROWS = N_EXPERTS * WIN
MAIN_CHUNK = 512
WIN_PER_STACK = MOE_TILE // WIN
MAX_WINDOWS = (MOE_TILE * TOP_K + N_EXPERTS * SEG_ALIGN) // WIN
CARRY_ROWS = N_EXPERTS * SEG_ALIGN
WORKLIST_LEN = -(-(MAX_WINDOWS + WIN_PER_STACK) // WIN_PER_STACK) * WIN_PER_STACK


def _build_worklist(ts_ref, tc_ref, wl_ref, junk_row):
    base = pl.program_id(0) * N_EXPERTS

    def per_expert(e, n):
        first = ts_ref[base + e]
        n_win = (tc_ref[base + e] + (WIN - 1)) // WIN

        def per_window(wi, n):
            wl_ref[n] = first + wi * WIN
            return n + 1

        return lax.fori_loop(1, n_win, per_window, n)

    n = lax.fori_loop(0, N_EXPERTS, per_expert, 0)
    n_stacks = (n + (WIN_PER_STACK - 1)) // WIN_PER_STACK

    def pad(j, c):
        wl_ref[j] = junk_row + (j % WIN_PER_STACK) * WIN
        return c

    lax.fori_loop(n, n_stacks * WIN_PER_STACK, pad, 0)
    return n_stacks


def _seg_aligned(row):
    return row if isinstance(row, int) else pl.multiple_of(row, SEG_ALIGN)


def _stack_row_ids(wl_ref, stack, shape, axis):
    pos = lax.broadcasted_iota(jnp.int32, shape, axis)
    row = jnp.full(shape, -1, jnp.int32)
    for wi in range(WIN_PER_STACK - 1, -1, -1):
        row = jnp.where(pos < (wi + 1) * WIN, wl_ref[stack * WIN_PER_STACK + wi] + (pos - wi * WIN), row)
    return row


def _select_rows(row_id, id_t, gate_t):
    shape = (row_id.shape[0], id_t.shape[1])
    sel = jnp.zeros(shape, F32)
    gsel = jnp.zeros(shape, F32)
    for k in range(TOP_K):
        hit = row_id == id_t[k:k + 1, :]
        sel = jnp.where(hit, 1.0, sel)
        gsel = jnp.where(hit, gate_t[k:k + 1, :], gsel)
    return sel.astype(BF16), jnp.sum(gsel, axis=1, keepdims=True)


MAIN_PENDING = 2


TAIL_ROWS = WIN + EXPERT_BLOCK


def _dispatch_kernel(ts_ref, tc_ref, ov_ref, re_ref, h_ref, col_t_ref, dest_t_ref, gate_t_ref,
                     buf_ref, gs_ref, wl_ref, pend_ref, main_x, main_g, carry_x, carry_g, stage_x, stage_g,
                     main_sem, sem, *, junk_row):
    i = pl.program_id(0)
    base = i * N_EXPERTS
    next_base = jnp.minimum(i + 1, pl.num_programs(0) - 1) * N_EXPERTS

    @pl.when(i == 0)
    def _():
        pend_ref[0] = 0
        pend_ref[1] = 0
        pend_ref[MAIN_PENDING] = 0
        carry_x[...] = jnp.zeros_like(carry_x)
        carry_g[...] = jnp.zeros_like(carry_g)
        main_x[0:TAIL_ROWS, :] = jnp.zeros((TAIL_ROWS, D_MODEL), BF16)
        main_g[0:TAIL_ROWS, :] = jnp.zeros((TAIL_ROWS, LANES), F32)
        cps = []
        for e in range(N_EXPERTS):
            rows = pl.ds(pl.multiple_of(jnp.maximum(re_ref[e] - TAIL_ROWS, 0), SEG_ALIGN), TAIL_ROWS)
            cps.append(pltpu.make_async_copy(main_x.at[pl.ds(0, TAIL_ROWS)], buf_ref.at[rows], main_sem))
            cps.append(pltpu.make_async_copy(main_g.at[pl.ds(0, TAIL_ROWS)], gs_ref.at[rows], main_sem))
        for cp in cps:
            cp.start()
        for cp in cps:
            cp.wait()

        def zero_block_copies(blk):
            rows = pl.ds(pl.multiple_of(blk * EXPERT_BLOCK, EXPERT_BLOCK), EXPERT_BLOCK)
            return (pltpu.make_async_copy(main_x.at[pl.ds(0, EXPERT_BLOCK)], buf_ref.at[rows], main_sem),
                    pltpu.make_async_copy(main_g.at[pl.ds(0, EXPERT_BLOCK)], gs_ref.at[rows], main_sem))

        def start_zero(blk, c):
            for cp in zero_block_copies(blk):
                cp.start()
            return c

        def wait_zero(blk, c):
            for cp in zero_block_copies(blk):
                cp.wait()
            return c

        first_free, n_blocks = re_ref[N_EXPERTS - 1] // EXPERT_BLOCK, (junk_row + EXPERT_BLOCK) // EXPERT_BLOCK
        lax.fori_loop(first_free, n_blocks, start_zero, 0)
        lax.fori_loop(first_free, n_blocks, wait_zero, 0)

    gate_t = gate_t_ref[0]
    h = h_ref[...]

    def main_copies(first_row):
        cps = []
        for e in range(N_EXPERTS):
            rows = pl.ds(_seg_aligned(first_row(e)), WIN)
            cps.append(pltpu.make_async_copy(main_x.at[pl.ds(e * WIN, WIN)], buf_ref.at[rows], main_sem))
            cps.append(pltpu.make_async_copy(main_g.at[pl.ds(e * WIN, WIN)], gs_ref.at[rows], main_sem))
        return cps

    def drain_main():
        @pl.when(pend_ref[MAIN_PENDING] == 1)
        def _():
            for cp in main_copies(lambda e: 0):
                cp.wait()
            pend_ref[MAIN_PENDING] = 0

    col_t = col_t_ref[0]
    dest_t = dest_t_ref[0]
    pieces = []
    for c in range(N_MAIN_ROWS // MAIN_CHUNK):
        row_id = lax.broadcasted_iota(jnp.int32, (MAIN_CHUNK, 1), 0) + c * MAIN_CHUNK
        sel, gate_row = _select_rows(row_id, col_t, gate_t)
        pieces.append((_dot(sel, h).astype(BF16), gate_row))
    group_pos = lax.broadcasted_iota(jnp.int32, (SEG_ALIGN, 1), 0)
    end_group = jnp.concatenate([ts_ref[next_base + e] + group_pos for e in range(N_EXPERTS)], axis=0)
    sel, end_gate = _select_rows(end_group, dest_t, gate_t)
    end_rows = _dot(sel, h).astype(BF16)

    def window_copies(slot, first_row):
        cps = []
        for wi in range(WIN_PER_STACK):
            rows = pl.ds(_seg_aligned(first_row(wi)), WIN)
            cps.append(pltpu.make_async_copy(stage_x.at[slot, pl.ds(wi * WIN, WIN)], buf_ref.at[rows], sem.at[slot]))
            cps.append(pltpu.make_async_copy(stage_g.at[slot, pl.ds(wi * WIN, WIN)], gs_ref.at[rows], sem.at[slot]))
        return cps

    def drain(slot):
        @pl.when(pend_ref[slot] == 1)
        def _():
            for cp in window_copies(slot, lambda wi: 0):
                cp.wait()
            pend_ref[slot] = 0

    drain_main()
    drain(0)
    drain(1)
    for c, (packed, gate_row) in enumerate(pieces):
        main_x[c * MAIN_CHUNK:(c + 1) * MAIN_CHUNK, :] = packed
        main_g[c * MAIN_CHUNK:(c + 1) * MAIN_CHUNK, :] = jnp.broadcast_to(gate_row, (MAIN_CHUNK, LANES))
    end_gate = jnp.broadcast_to(end_gate, (CARRY_ROWS, LANES))
    for e in range(N_EXPERTS):
        win, grp = pl.ds(e * WIN, SEG_ALIGN), pl.ds(e * SEG_ALIGN, SEG_ALIGN)
        main_x[win, :] = main_x[win, :] + carry_x[grp, :]
        main_g[win, :] = main_g[win, :] + carry_g[grp, :]
        same = ts_ref[next_base + e] == ts_ref[base + e]
        carry_x[grp, :] = end_rows[e * SEG_ALIGN:(e + 1) * SEG_ALIGN, :] + jnp.where(same, carry_x[grp, :], 0).astype(BF16)
        carry_g[grp, :] = end_gate[e * SEG_ALIGN:(e + 1) * SEG_ALIGN, :] + jnp.where(same, carry_g[grp, :], 0.0)
    for n, cp in enumerate(main_copies(lambda e: ts_ref[base + e])):
        cp.start(priority=(n // 2) % 2)
    pend_ref[MAIN_PENDING] = 1

    @pl.when(ov_ref[i] > 0)
    def _():
        n_stacks = _build_worklist(ts_ref, tc_ref, wl_ref, junk_row)

        def stack_body(s, c):
            slot = s & 1
            sel, gate_row = _select_rows(_stack_row_ids(wl_ref, s, (MOE_TILE, 1), 0), dest_t, gate_t)
            rows = _dot(sel, h)
            drain(slot)
            stage_x[slot] = rows.astype(BF16)
            stage_g[slot] = jnp.broadcast_to(gate_row, (MOE_TILE, LANES))
            for cp in window_copies(slot, lambda wi: wl_ref[s * WIN_PER_STACK + wi]):
                cp.start()
            pend_ref[slot] = 1
            return c

        lax.fori_loop(0, n_stacks, stack_body, 0)

    @pl.when(i == pl.num_programs(0) - 1)
    def _():
        drain_main()
        drain(0)
        drain(1)


def _moe_grid_spec(n_tiles, in_specs, out_specs, scratch_shapes):
    return pltpu.PrefetchScalarGridSpec(num_scalar_prefetch=4, grid=(n_tiles,), in_specs=in_specs,
                                        out_specs=out_specs, scratch_shapes=scratch_shapes)


def _dispatch(tile_start, tile_cnt, tile_ovf, region_ends, h2, col_t, dest_t, gate_t, n_rows):
    n_tiles = h2.shape[0] // MOE_TILE
    junk_row = n_rows - EXPERT_BLOCK
    any_spec = pl.BlockSpec(memory_space=pl.ANY)
    tok_t = pl.BlockSpec((1, 8, MOE_TILE), lambda i, *_: (i, 0, 0))
    return pl.pallas_call(
        functools.partial(_dispatch_kernel, junk_row=junk_row),
        out_shape=[jax.ShapeDtypeStruct((n_rows, D_MODEL), BF16), jax.ShapeDtypeStruct((n_rows, LANES), F32)],
        grid_spec=_moe_grid_spec(
            n_tiles,
            [pl.BlockSpec((MOE_TILE, D_MODEL), lambda i, *_: (i, 0)), tok_t, tok_t, tok_t],
            [any_spec, any_spec],
            [pltpu.SMEM((WORKLIST_LEN,), jnp.int32), pltpu.SMEM((3,), jnp.int32),
             pltpu.VMEM((N_MAIN_ROWS, D_MODEL), BF16), pltpu.VMEM((N_MAIN_ROWS, LANES), F32),
             pltpu.VMEM((CARRY_ROWS, D_MODEL), BF16), pltpu.VMEM((CARRY_ROWS, LANES), F32),
             pltpu.VMEM((2, MOE_TILE, D_MODEL), BF16), pltpu.VMEM((2, MOE_TILE, LANES), F32),
             pltpu.SemaphoreType.DMA(()), pltpu.SemaphoreType.DMA((2,))]),
        compiler_params=pltpu.CompilerParams(dimension_semantics=("arbitrary",),
                                             vmem_limit_bytes=56 * 1024 * 1024),
        name="moe_dispatch",
    )(tile_start, tile_cnt, tile_ovf, region_ends, h2, col_t, dest_t, gate_t)


def _select_cols(ids, col_id):
    sel = jnp.zeros((ids.shape[0], col_id.shape[1]), F32)
    for k in range(TOP_K):
        sel = jnp.where(ids[:, k:k + 1] == col_id, 1.0, sel)
    return sel.astype(BF16)


def _combine_kernel(ts_ref, tc_ref, ov_ref, re_ref, col_ref, dest_ref, x1_ref, mod_ref, y_ref, o_ref,
                    wl_ref, main_y, stage_y, acc_ref, main_sem, sem, *, junk_row):
    del re_ref
    i = pl.program_id(0)
    slot_i = i & 1

    def main_copies(slot, first_row):
        return [pltpu.make_async_copy(y_ref.at[pl.ds(_seg_aligned(first_row(e)), WIN)],
                                      main_y.at[slot, pl.ds(e * WIN, WIN)], main_sem.at[slot])
                for e in range(N_EXPERTS)]

    def fetch_main(tile, slot):
        for n, cp in enumerate(main_copies(slot, lambda e: ts_ref[tile * N_EXPERTS + e])):
            cp.start(priority=n % 2)

    @pl.when(i == 0)
    def _():
        fetch_main(0, 0)

    @pl.when(i + 1 < pl.num_programs(0))
    def _():
        fetch_main(i + 1, 1 - slot_i)

    for cp in main_copies(slot_i, lambda e: 0):
        cp.wait()

    col = col_ref[...]
    sel = jnp.concatenate(
        [_select_cols(col, lax.broadcasted_iota(jnp.int32, (1, MAIN_CHUNK), 1) + c * MAIN_CHUNK)
         for c in range(N_MAIN_ROWS // MAIN_CHUNK)], axis=1)
    acc_ref[...] = _dot(sel, main_y[slot_i])

    @pl.when(ov_ref[i] > 0)
    def _():
        n_stacks = _build_worklist(ts_ref, tc_ref, wl_ref, junk_row)
        dest = dest_ref[...]
        if WIN_PER_STACK * WIN < MOE_TILE:
            for slot in range(2):
                stage_y[slot, WIN_PER_STACK * WIN:, :] = jnp.zeros((MOE_TILE - WIN_PER_STACK * WIN, D_MODEL), BF16)

        def window_copies(slot, first_row):
            return [pltpu.make_async_copy(y_ref.at[pl.ds(_seg_aligned(first_row(wi)), WIN)],
                                          stage_y.at[slot, pl.ds(wi * WIN, WIN)], sem.at[slot])
                    for wi in range(WIN_PER_STACK)]

        def fetch(s, slot):
            for cp in window_copies(slot, lambda wi: wl_ref[s * WIN_PER_STACK + wi]):
                cp.start()

        @pl.when(n_stacks > 0)
        def _():
            fetch(0, 0)

        def stack_body(s, c):
            slot = s & 1

            @pl.when(s + 1 < n_stacks)
            def _():
                fetch(s + 1, 1 - slot)

            for cp in window_copies(slot, lambda wi: 0):
                cp.wait()
            sel_o = _select_cols(dest, _stack_row_ids(wl_ref, s, (1, MOE_TILE), 1))
            acc_ref[...] += _dot(sel_o, stage_y[slot])
            return c

        lax.fori_loop(0, n_stacks, stack_body, 0)

    g2 = mod_ref[0][5:6, :]
    o_ref[...] = x1_ref[...] + g2 * acc_ref[...]


def _combine(tile_start, tile_cnt, tile_ovf, region_ends, col, dest, x1, mod3, y, l):
    t = x1.shape[0]
    per_batch = l // MOE_TILE
    junk_row = y.shape[0] - EXPERT_BLOCK
    tok = lambda width: pl.BlockSpec((MOE_TILE, width), lambda i, *_: (i, 0))
    return pl.pallas_call(
        functools.partial(_combine_kernel, junk_row=junk_row),
        out_shape=jax.ShapeDtypeStruct((t, D_MODEL), F32),
        grid_spec=_moe_grid_spec(
            t // MOE_TILE,
            [tok(LANES), tok(LANES), tok(D_MODEL),
             pl.BlockSpec((1, 6, D_MODEL), lambda i, *_: (i // per_batch, 0, 0)),
             pl.BlockSpec(memory_space=pl.ANY)],
            tok(D_MODEL),
            [pltpu.SMEM((WORKLIST_LEN,), jnp.int32), pltpu.VMEM((2, N_MAIN_ROWS, D_MODEL), BF16),
             pltpu.VMEM((2, MOE_TILE, D_MODEL), BF16), pltpu.VMEM((MOE_TILE, D_MODEL), F32),
             pltpu.SemaphoreType.DMA((2,)), pltpu.SemaphoreType.DMA((2,))]),
        compiler_params=pltpu.CompilerParams(dimension_semantics=("arbitrary",),
                                             vmem_limit_bytes=56 * 1024 * 1024),
        name="moe_combine",
    )(tile_start, tile_cnt, tile_ovf, region_ends, col, dest, x1, mod3, y)


def _expert_kernel(be_ref, nused_ref, x_ref, g_ref, w1_ref, b1_ref, w2_ref, b2_ref, y_ref, w1b_ref, w2b_ref):
    i = pl.program_id(0)
    live = i < nused_ref[0]

    @pl.when(live & ((i == 0) | (be_ref[i] != be_ref[jnp.maximum(i - 1, 0)])))
    def _():
        w1b_ref[...] = w1_ref[0].astype(BF16)
        w2b_ref[...] = w2_ref[0].astype(BF16)

    @pl.when(live)
    def _():
        gu = _dot(x_ref[...], w1b_ref[...]) + b1_ref[0]
        glu = jnp.minimum(gu[:, :D_EXPERT], SWIGLU_LIMIT)
        lin = jnp.clip(gu[:, D_EXPERT:], -SWIGLU_LIMIT, SWIGLU_LIMIT)
        act = glu * (1.0 / (1.0 + jnp.exp(-SWIGLU_ALPHA * glu))) * (lin + 1.0)
        y = (_dot(act.astype(BF16), w2b_ref[...]) + b2_ref[0]) * g_ref[:, 0:1]
        y_ref[...] = y.astype(BF16)

    @pl.when(pl.program_id(0) >= nused_ref[0])
    def _():
        y_ref[...] = jnp.zeros_like(y_ref)


def _experts(block_e, nused, buf, gs, w1, b1, w2, b2):
    nb = buf.shape[0] // EXPERT_BLOCK
    row = lambda i, be, nu: (jnp.minimum(i, nu[0] - 1), 0)
    out_row = lambda i, be, nu: (i, 0)
    exp3 = lambda i, be, nu: (be[jnp.minimum(i, nu[0] - 1)], 0, 0)
    return pl.pallas_call(
        _expert_kernel,
        out_shape=jax.ShapeDtypeStruct((buf.shape[0], D_MODEL), BF16),
        grid_spec=pltpu.PrefetchScalarGridSpec(
            num_scalar_prefetch=2,
            grid=(nb,),
            in_specs=[pl.BlockSpec((EXPERT_BLOCK, D_MODEL), row),
                      pl.BlockSpec((EXPERT_BLOCK, LANES), row),
                      pl.BlockSpec((1, D_MODEL, 2 * D_EXPERT), exp3),
                      pl.BlockSpec((1, 1, 2 * D_EXPERT), exp3),
                      pl.BlockSpec((1, D_EXPERT, D_MODEL), exp3),
                      pl.BlockSpec((1, 1, D_MODEL), exp3)],
            out_specs=pl.BlockSpec((EXPERT_BLOCK, D_MODEL), out_row),
            scratch_shapes=[pltpu.VMEM((D_MODEL, 2 * D_EXPERT), BF16), pltpu.VMEM((D_EXPERT, D_MODEL), BF16)]),
        compiler_params=pltpu.CompilerParams(dimension_semantics=("arbitrary",),
                                             vmem_limit_bytes=56 * 1024 * 1024),
        name="moe_experts",
    )(block_e, nused, buf, gs, w1, b1, w2, b2)


def _rope_tables(seq_len):
    pos = jnp.arange(seq_len, dtype=jnp.int32)
    row = (pos // GRID_W).astype(F32)[:, None]
    col = (pos % GRID_W).astype(F32)[:, None]

    def table(d_axis, pad):
        inv_freq = ROPE_THETA ** (-jnp.arange(0, d_axis, 2, dtype=F32) / d_axis)
        ar, ac = row * inv_freq[None, :], col * inv_freq[None, :]
        cos = jnp.concatenate([jnp.cos(ar), jnp.cos(ar), jnp.cos(ac), jnp.cos(ac)], axis=-1)
        sin = jnp.concatenate([-jnp.sin(ar), jnp.sin(ar), -jnp.sin(ac), jnp.sin(ac)], axis=-1)
        if pad:
            cos = jnp.concatenate([cos, jnp.ones((seq_len, pad), F32)], axis=-1)
            sin = jnp.concatenate([sin, jnp.zeros((seq_len, pad), F32)], axis=-1)
        return cos, sin

    cg, sg = table(GQA_HEAD_DIM // 2, 0)
    cm, sm = table(MLA_ROPE_DIM // 2, LANES - MLA_ROPE_DIM)
    return cg, sg, cm, sm


def _pad_lanes(a, width):
    return jnp.pad(a, [(0, 0)] * (a.ndim - 1) + [(0, width - a.shape[-1])])


def _prep_weights(p):
    w_in = p["w_in"]
    w = {"norm_mix": p["norm_mix"].reshape(1, D_MODEL), "norm_ffn": p["norm_ffn"].reshape(1, D_MODEL)}
    w["wq"] = w_in[:, :OFF_GQA_K].astype(BF16)
    w["wk"] = w_in[:, OFF_GQA_K:OFF_GQA_V].astype(BF16)
    w["wv"] = w_in[:, OFF_GQA_V:OFF_MLA_QA].T.astype(BF16)
    w["wqa"] = w_in[:, OFF_MLA_QA:OFF_MLA_KVA].astype(BF16)
    w["wckv"] = w_in[:, OFF_MLA_KVA:OFF_MLA_KVA + MLA_KV_RANK].astype(BF16)
    w["wkr"] = _pad_lanes(w_in[:, OFF_MLA_KVA + MLA_KV_RANK:OFF_GATE], LANES).astype(BF16)
    w["wg"] = w_in[:, OFF_GATE:].astype(BF16)
    wqb = p["mla_w_qb"].reshape(MLA_Q_RANK, MLA_HEADS, MLA_QK_DIM)
    w["wqb"] = _pad_lanes(wqb, MLA_HEAD_PAD).reshape(MLA_Q_RANK, MLA_HEADS * MLA_HEAD_PAD).astype(BF16)
    wkvb = p["mla_w_kvb"].reshape(MLA_KV_RANK, MLA_HEADS, MLA_NOPE_DIM + MLA_V_DIM)
    w["wkb"] = wkvb[:, :, :MLA_NOPE_DIM].reshape(MLA_KV_RANK, MLA_HEADS * MLA_NOPE_DIM).astype(BF16)
    w["wvb"] = wkvb[:, :, MLA_NOPE_DIM:].reshape(MLA_KV_RANK, MLA_V_W).T.astype(BF16)
    w["gq"] = p["gqa_q_norm"].reshape(1, GQA_HEAD_DIM)
    w["gk"] = p["gqa_k_norm"].reshape(1, GQA_HEAD_DIM)
    w["gqa"] = p["mla_q_a_norm"].reshape(1, MLA_Q_RANK)
    w["gkva"] = p["mla_kv_a_norm"].reshape(1, MLA_KV_RANK)
    w["gmq"] = _pad_lanes(p["mla_q_norm"].reshape(1, MLA_QK_DIM), MLA_HEAD_PAD)
    w["gmk"] = _pad_lanes(p["mla_k_norm"].reshape(1, MLA_QK_DIM), MLA_HEAD_PAD)
    w["woa"] = p["w_o_gqa"].astype(BF16)
    w["wob"] = p["w_o_mla"].astype(BF16)
    w["wout"] = p["w_out"].astype(BF16)
    rw = _pad_lanes(p["router_w"], LANES)
    w["rw_hi"] = rw.astype(BF16)
    w["rw_lo"] = (rw - w["rw_hi"].astype(F32)).astype(BF16)
    w["rb"] = _pad_lanes(p["router_b"].reshape(1, N_EXPERTS), LANES)
    return w


def _layer(x, c, ctx, c_ctx, p):
    b, l, _ = x.shape
    lc = ctx.shape[1]
    t = b * l
    w = _prep_weights(p)

    n_mod_rows = -(-(b + 1) // 8) * 8
    cc = jnp.zeros((n_mod_rows, D_MODEL), F32).at[:b].set(c).at[b].set(c_ctx)
    mod3 = _ada_mod(cc, p["ada_w"], p["ada_b"]).reshape(n_mod_rows, 6, D_MODEL)

    tables = _rope_tables(l)
    ident = (jnp.ones((lc, LANES), F32), jnp.zeros((lc, LANES), F32)) * 2
    tm = min(512, l)
    q_a, k_a, v_a, q_m, k_m, v_m, gates = _proj(x, mod3, lambda bi: bi, tables, w, True, tm)
    kc_a, vc_a, kc_m, vc_m = _proj(ctx, mod3, lambda bi: b, ident, w, False, min(256, lc))

    tq, tk = min(2048, l), min(1024, l)
    o_a = _attention(q_a, k_a, v_a, kc_a, vc_a, GQA_HEADS, GQA_GROUP, GQA_HEAD_DIM, GQA_HEAD_DIM, tq, tk, "attn_gqa",
                     _score_bound(p["gqa_q_norm"], p["gqa_k_norm"], GQA_HEAD_DIM))
    o_m = _attention(q_m, k_m, v_m, kc_m, vc_m, MLA_HEADS, 1, MLA_HEAD_PAD, MLA_V_DIM, tq, tk, "attn_mla",
                     _score_bound(p["mla_q_norm"], p["mla_k_norm"], MLA_QK_DIM))

    x1, h2, eidx, egate = _merge(o_a, o_m, gates, x, mod3, w, tm)
    x1, h2 = x1.reshape(t, D_MODEL), h2.reshape(t, D_MODEL)
    eidx, egate = eidx.reshape(t, LANES), egate.reshape(t, LANES)

    assert t % MOE_TILE == 0
    n_tiles = t // MOE_TILE
    rank, before_raw, tcnt, cnt = _route(eidx, MOE_TILE)
    counts = cnt[0, :N_EXPERTS].astype(jnp.int32)
    before = before_raw[:, 0, :N_EXPERTS].astype(jnp.int32)
    region = (counts + WIN + EXPERT_BLOCK - 1) // EXPERT_BLOCK * EXPERT_BLOCK
    region_ends = jnp.cumsum(region)
    region_starts = region_ends - region
    max_rows = t * TOP_K + N_EXPERTS * (WIN + EXPERT_BLOCK - 1)
    n_blocks = max_rows // EXPERT_BLOCK + 1
    n_rows = n_blocks * EXPERT_BLOCK
    block_row = jnp.arange(n_blocks, dtype=jnp.int32) * EXPERT_BLOCK
    block_e = jnp.minimum(jnp.sum(region_ends[None, :] <= block_row[:, None], axis=1), N_EXPERTS - 1).astype(jnp.int32)
    nused = (region_ends[-1:] // EXPERT_BLOCK).astype(jnp.int32)
    tile_start = (region_starts[None, :] + before // SEG_ALIGN * SEG_ALIGN).reshape(-1)
    tile_cnt = before % SEG_ALIGN + tcnt[:, 0, :N_EXPERTS].astype(jnp.int32)
    tile_ovf = jnp.sum(jnp.maximum((tile_cnt + WIN - 1) // WIN - 1, 0), axis=1).astype(jnp.int32)
    tile_cnt = tile_cnt.reshape(-1)
    start = jnp.zeros((8, LANES), F32).at[0, :N_EXPERTS].set(region_starts.astype(F32))
    dest, col, dest_t, col_t, gate_t = _dest(eidx, rank, egate, start, before_raw, MOE_TILE)

    region_ends = region_ends.astype(jnp.int32)
    buf, gs = _dispatch(tile_start, tile_cnt, tile_ovf, region_ends, h2, col_t, dest_t, gate_t, n_rows)
    y = _experts(block_e, nused, buf, gs, p["expert_w1"], p["expert_b1"].reshape(N_EXPERTS, 1, -1),
                 p["expert_w2"], p["expert_b2"].reshape(N_EXPERTS, 1, -1))
    out = _combine(tile_start, tile_cnt, tile_ovf, region_ends, col, dest, x1, mod3, y, l)
    return out.reshape(b, l, D_MODEL)


def kernel(x, c, ctx, c_ctx, ada_w, ada_b, norm_mix, norm_ffn, w_in, gqa_q_norm, gqa_k_norm, mla_q_a_norm, mla_kv_a_norm, mla_w_qb, mla_w_kvb, mla_q_norm, mla_k_norm, w_o_gqa, w_o_mla, w_out, router_w, router_b, expert_w1, expert_b1, expert_w2, expert_b2):
    assert ada_w.shape[0] == 1, "single-layer problem: the context stream is never updated"
    p = {
        "ada_w": ada_w[0], "ada_b": ada_b[0], "norm_mix": norm_mix[0], "norm_ffn": norm_ffn[0],
        "w_in": w_in[0], "gqa_q_norm": gqa_q_norm[0], "gqa_k_norm": gqa_k_norm[0],
        "mla_q_a_norm": mla_q_a_norm[0], "mla_kv_a_norm": mla_kv_a_norm[0],
        "mla_w_qb": mla_w_qb[0], "mla_w_kvb": mla_w_kvb[0],
        "mla_q_norm": mla_q_norm[0], "mla_k_norm": mla_k_norm[0],
        "w_o_gqa": w_o_gqa[0], "w_o_mla": w_o_mla[0], "w_out": w_out[0],
        "router_w": router_w[0], "router_b": router_b[0],
        "expert_w1": expert_w1[0], "expert_b1": expert_b1[0],
        "expert_w2": expert_w2[0], "expert_b2": expert_b2[0],
    }
    return _layer(x, c, ctx, c_ctx, p)
```

```python
import functools
import math

import jax
import jax.numpy as jnp
from jax import lax
from jax.experimental import pallas as pl
from jax.experimental.pallas import tpu as pltpu

D_MODEL = 1024
GRID_W = 64
EPS = 1e-6
ROPE_THETA = 10000.0

GQA_HEADS = 8
GQA_KV_HEADS = 2
GQA_GROUP = GQA_HEADS // GQA_KV_HEADS
GQA_HEAD_DIM = 128
GQA_Q_W = GQA_HEADS * GQA_HEAD_DIM
GQA_KV_W = GQA_KV_HEADS * GQA_HEAD_DIM

MLA_HEADS = 8
MLA_Q_RANK = 256
MLA_KV_RANK = 128
MLA_NOPE_DIM = 128
MLA_ROPE_DIM = 64
MLA_V_DIM = 128
MLA_QK_DIM = MLA_NOPE_DIM + MLA_ROPE_DIM
MLA_HEAD_PAD = 256
MLA_V_W = MLA_HEADS * MLA_V_DIM

OFF_GQA_K = GQA_Q_W
OFF_GQA_V = OFF_GQA_K + GQA_KV_W
OFF_MLA_QA = OFF_GQA_V + GQA_KV_W
OFF_MLA_KVA = OFF_MLA_QA + MLA_Q_RANK
OFF_GATE = OFF_MLA_KVA + MLA_KV_RANK + MLA_ROPE_DIM

N_EXPERTS = 32
TOP_K = 4
D_EXPERT = 1024
SWIGLU_LIMIT = 7.0
SWIGLU_ALPHA = 1.702
EXPERT_BLOCK = 512

LANES = 128
NEG_INF = float("-inf")
LOG2_E = math.log2(math.e)

BF16 = jnp.bfloat16
F32 = jnp.float32


def _dot(a, b):
    return jnp.dot(a, b, preferred_element_type=F32)


def _dot_nt(a, b):
    return lax.dot_general(a, b, (((1,), (1,)), ((), ())), preferred_element_type=F32)


def _split_bf16(a):
    hi = a.astype(BF16)
    lo = (a - hi.astype(F32)).astype(BF16)
    return hi, lo


def _rms(x, g, n):
    ms = jnp.sum(x * x, axis=-1, keepdims=True) * (1.0 / n)
    return x * lax.rsqrt(ms + EPS) * g


def _swap_halves(x, k):
    lane = lax.broadcasted_iota(jnp.int32, x.shape, 1)
    return jnp.where((lane & k) != 0, pltpu.roll(x, k, 1), pltpu.roll(x, LANES - k, 1))


def _rope(x, cos, sin_signed, k):
    return x * cos + _swap_halves(x, k) * sin_signed


def _ada_kernel(c_ref, w_ref, b_ref, o_ref):
    c = c_ref[...]
    s = c * (1.0 / (1.0 + jnp.exp(-c)))
    s_hi, s_lo = _split_bf16(s)
    w_hi, w_lo = _split_bf16(w_ref[...])
    o_ref[...] = _dot(s_hi, w_hi) + (_dot(s_hi, w_lo) + _dot(s_lo, w_hi)) + b_ref[...]


def _ada_mod(cc, ada_w, ada_b):
    n = ada_w.shape[1]
    tn = 1024
    return pl.pallas_call(
        _ada_kernel,
        out_shape=jax.ShapeDtypeStruct((cc.shape[0], n), F32),
        grid=(n // tn,),
        in_specs=[
            pl.BlockSpec((cc.shape[0], D_MODEL), lambda j: (0, 0)),
            pl.BlockSpec((D_MODEL, tn), lambda j: (0, j)),
            pl.BlockSpec((1, tn), lambda j: (0, j)),
        ],
        out_specs=pl.BlockSpec((cc.shape[0], tn), lambda j: (0, j)),
        compiler_params=pltpu.CompilerParams(dimension_semantics=("parallel",)),
        name="ada_mod",
    )(cc, ada_w, ada_b.reshape(1, n))


def _proj_kernel(*refs, with_q):
    if with_q:
        (x_ref, mod_ref, nmix_ref, cg_ref, sg_ref, cm_ref, sm_ref,
         wq_ref, wk_ref, wv_ref, wqa_ref, wckv_ref, wkr_ref, wg_ref, wqb_ref, wkb_ref, wvb_ref,
         gq_ref, gk_ref, gqa_ref, gkva_ref, gmq_ref, gmk_ref,
         q_ref, k_ref, v_ref, qm_ref, km_ref, vm_ref, gate_ref) = refs
    else:
        (x_ref, mod_ref, nmix_ref, cg_ref, sg_ref, cm_ref, sm_ref,
         wk_ref, wv_ref, wckv_ref, wkr_ref, wkb_ref, wvb_ref,
         gk_ref, gkva_ref, gmk_ref,
         k_ref, v_ref, km_ref, vm_ref) = refs

    x = x_ref[0]
    mod = mod_ref[0]
    shift, scale = mod[0:1, :], mod[1:2, :]
    h = _rms(x, nmix_ref[...], D_MODEL) * (1.0 + scale) + shift
    hb = h.astype(BF16)
    cg, sg = cg_ref[...], sg_ref[...]
    cm, sm = cm_ref[...], sm_ref[...]

    kk = _dot(hb, wk_ref[...])
    for j in range(GQA_KV_HEADS):
        sl = slice(j * GQA_HEAD_DIM, (j + 1) * GQA_HEAD_DIM)
        kn = _rms(kk[:, sl], gk_ref[...], GQA_HEAD_DIM)
        k_ref[0, :, sl] = _rope(kn, cg, sg, 32).astype(BF16)
    v_ref[0] = _dot_nt(wv_ref[...], hb).astype(BF16)

    ckv = _rms(_dot(hb, wckv_ref[...]), gkva_ref[...], MLA_KV_RANK).astype(BF16)
    vm_ref[0] = _dot_nt(wvb_ref[...], ckv).astype(BF16)
    knope = _dot(ckv, wkb_ref[...])
    kr = _dot(hb, wkr_ref[...])
    gmk = gmk_ref[...]
    g_nope, g_rope = gmk[:, :MLA_NOPE_DIM], gmk[:, MLA_NOPE_DIM:]
    ssq_r = jnp.sum(kr * kr, axis=-1, keepdims=True)
    kr_roped = _rope(kr * g_rope, cm, sm, 16)
    for j in range(MLA_HEADS):
        kn = knope[:, j * MLA_NOPE_DIM:(j + 1) * MLA_NOPE_DIM]
        ms = (jnp.sum(kn * kn, axis=-1, keepdims=True) + ssq_r) * (1.0 / MLA_QK_DIM)
        r = lax.rsqrt(ms + EPS)
        base = j * MLA_HEAD_PAD
        km_ref[0, :, base:base + MLA_NOPE_DIM] = (kn * r * g_nope).astype(BF16)
        km_ref[0, :, base + MLA_NOPE_DIM:base + MLA_HEAD_PAD] = (kr_roped * r).astype(BF16)

    if not with_q:
        return

    qq = _dot(hb, wq_ref[...])
    q_scale = GQA_HEAD_DIM ** -0.5 * LOG2_E
    for j in range(GQA_HEADS):
        sl = slice(j * GQA_HEAD_DIM, (j + 1) * GQA_HEAD_DIM)
        qn = _rms(qq[:, sl], gq_ref[...], GQA_HEAD_DIM)
        q_ref[0, :, sl] = (_rope(qn, cg, sg, 32) * q_scale).astype(BF16)

    qa = _rms(_dot(hb, wqa_ref[...]), gqa_ref[...], MLA_Q_RANK).astype(BF16)
    q2 = _dot(qa, wqb_ref[...])
    gmq = gmq_ref[...]
    gq_nope, gq_rope = gmq[:, :MLA_NOPE_DIM], gmq[:, MLA_NOPE_DIM:]
    m_scale = MLA_QK_DIM ** -0.5 * LOG2_E
    for j in range(MLA_HEADS):
        base = j * MLA_HEAD_PAD
        qn = q2[:, base:base + MLA_NOPE_DIM]
        qr = q2[:, base + MLA_NOPE_DIM:base + MLA_HEAD_PAD]
        ms = (jnp.sum(qn * qn, axis=-1, keepdims=True) + jnp.sum(qr * qr, axis=-1, keepdims=True)) * (1.0 / MLA_QK_DIM)
        r = lax.rsqrt(ms + EPS)
        qm_ref[0, :, base:base + MLA_NOPE_DIM] = (qn * r * gq_nope * m_scale).astype(BF16)
        qm_ref[0, :, base + MLA_NOPE_DIM:base + MLA_HEAD_PAD] = (_rope(qr * r * gq_rope, cm, sm, 16) * m_scale).astype(BF16)

    gl = _dot(hb, wg_ref[...])
    gate_ref[0] = (1.0 / (1.0 + jnp.exp(-gl))).astype(BF16)


def _proj(x, mod3, mod_row_of_batch, tables, w, with_q, tm):
    b, l, _ = x.shape
    cg, sg, cm, sm = tables
    const = lambda shape: pl.BlockSpec(shape, lambda bi, i: (0,) * len(shape), pipeline_mode=pl.Buffered(1))
    tab = pl.BlockSpec((tm, LANES), lambda bi, i: (i, 0))
    in_specs = [
        pl.BlockSpec((1, tm, D_MODEL), lambda bi, i: (bi, i, 0)),
        pl.BlockSpec((1, 6, D_MODEL), lambda bi, i: (mod_row_of_batch(bi), 0, 0)),
        const((1, D_MODEL)), tab, tab, tab, tab,
    ]
    if with_q:
        weights = [w["wq"], w["wk"], w["wv"], w["wqa"], w["wckv"], w["wkr"], w["wg"], w["wqb"], w["wkb"], w["wvb"],
                   w["gq"], w["gk"], w["gqa"], w["gkva"], w["gmq"], w["gmk"]]
    else:
        weights = [w["wk"], w["wv"], w["wckv"], w["wkr"], w["wkb"], w["wvb"], w["gk"], w["gkva"], w["gmk"]]
    in_specs += [const(a.shape) for a in weights]

    def out(width):
        return jax.ShapeDtypeStruct((b, l, width), BF16), pl.BlockSpec((1, tm, width), lambda bi, i: (bi, i, 0))

    def out_t(width):
        return jax.ShapeDtypeStruct((b, width, l), BF16), pl.BlockSpec((1, width, tm), lambda bi, i: (bi, 0, i))

    outs = [out(GQA_KV_W), out_t(GQA_KV_W), out(MLA_HEADS * MLA_HEAD_PAD), out_t(MLA_V_W)]
    if with_q:
        outs = [out(GQA_Q_W)] + outs[:2] + [out(MLA_HEADS * MLA_HEAD_PAD)] + outs[2:] + [out(2 * D_MODEL)]
    return pl.pallas_call(
        functools.partial(_proj_kernel, with_q=with_q),
        out_shape=[o[0] for o in outs],
        grid=(b, l // tm),
        in_specs=in_specs,
        out_specs=[o[1] for o in outs],
        compiler_params=pltpu.CompilerParams(dimension_semantics=("parallel", "parallel"),
                                             vmem_limit_bytes=56 * 1024 * 1024),
        name="proj_latent" if with_q else "proj_ctx",
    )(x, mod3, w["norm_mix"], cg, sg, cm, sm, *weights)


def _attn_kernel(q_ref, kl_ref, vl_ref, kc_ref, vc_ref, o_ref, *, tk, n_lat, bounded):
    q = q_ref[0]
    tq = q.shape[0]
    dv = vl_ref.shape[1]
    n = n_lat // tk

    if bounded:
        l = jnp.zeros((1, tq), F32)
        acc = jnp.zeros((dv, tq), F32)
        for k, vt in [(kl_ref[0, j * tk:(j + 1) * tk, :], vl_ref[0, :, j * tk:(j + 1) * tk]) for j in range(n)] + [
                (kc_ref[0], vc_ref[0])]:
            p = jnp.exp2(_dot_nt(k, q))
            l = l + jnp.sum(p, axis=0, keepdims=True)
            acc = acc + _dot(vt, p.astype(BF16))
        o_ref[0] = (acc / l).T.astype(BF16)
        return

    def update(carry, s, vt):
        m, l, acc = carry
        m_new = jnp.maximum(m, jnp.max(s, axis=0, keepdims=True))
        alpha = jnp.exp2(m - m_new)
        p = jnp.exp2(s - m_new)
        l = alpha * l + jnp.sum(p, axis=0, keepdims=True)
        return m_new, l, alpha * acc + _dot(vt, p.astype(BF16))

    n = n_lat // tk
    keys = [kl_ref[0, j * tk:(j + 1) * tk, :] for j in range(n)] + [kc_ref[0]]
    vals = [vl_ref[0, :, j * tk:(j + 1) * tk] for j in range(n)] + [vc_ref[0]]
    carry = (jnp.full((1, tq), NEG_INF, F32), jnp.zeros((1, tq), F32), jnp.zeros((dv, tq), F32))
    s = _dot_nt(keys[0], q)
    for j in range(n + 1):
        s_next = _dot_nt(keys[j + 1], q) if j < n else None
        carry = update(carry, s, vals[j])
        s = s_next
    m, l, acc = carry
    o_ref[0] = (acc / l).T.astype(BF16)


SOFTMAX_SAFE_EXPONENT = 56.0


def _score_bound(gain_q, gain_k, dim):
    return dim * jnp.max(jnp.abs(gain_q)) * jnp.max(jnp.abs(gain_k)) * (dim ** -0.5 * LOG2_E) * 1.02


def _attention(q, k_lat, v_lat, k_ctx, v_ctx, n_heads, group, d_qk, dv, tq, tk, name, score_bound):
    run = functools.partial(_attention_call, q, k_lat, v_lat, k_ctx, v_ctx, n_heads, group, d_qk, dv, tq, tk, name)
    return lax.cond(score_bound <= SOFTMAX_SAFE_EXPONENT, lambda: run(True), lambda: run(False))


def _attention_call(q, k_lat, v_lat, k_ctx, v_ctx, n_heads, group, d_qk, dv, tq, tk, name, bounded):
    b, l, _ = q.shape
    lc = k_ctx.shape[1]
    name = name + ("_bounded" if bounded else "_online")
    return pl.pallas_call(
        functools.partial(_attn_kernel, tk=tk, n_lat=l, bounded=bounded),
        out_shape=jax.ShapeDtypeStruct((b, l, n_heads * dv), BF16),
        grid=(b, n_heads, l // tq),
        in_specs=[
            pl.BlockSpec((1, tq, d_qk), lambda bi, h, i: (bi, i, h)),
            pl.BlockSpec((1, l, d_qk), lambda bi, h, i: (bi, 0, h // group)),
            pl.BlockSpec((1, dv, l), lambda bi, h, i: (bi, h // group, 0)),
            pl.BlockSpec((1, lc, d_qk), lambda bi, h, i: (bi, 0, h // group)),
            pl.BlockSpec((1, dv, lc), lambda bi, h, i: (bi, h // group, 0)),
        ],
        out_specs=pl.BlockSpec((1, tq, dv), lambda bi, h, i: (bi, i, h)),
        compiler_params=pltpu.CompilerParams(dimension_semantics=("parallel", "parallel", "parallel"),
                                             vmem_limit_bytes=56 * 1024 * 1024),
        name=name,
    )(q, k_lat, v_lat, k_ctx, v_ctx)


def _merge_kernel(oa_ref, ob_ref, gate_ref, x_ref, mod_ref, woa_ref, wob_ref, wout_ref, nffn_ref,
                  rwh_ref, rwl_ref, rb_ref, x1_ref, h2_ref, eidx_ref, egate_ref):
    mod = mod_ref[0]
    g1, shift2, scale2 = mod[2:3, :], mod[3:4, :], mod[4:5, :]
    ya = _dot(oa_ref[0], woa_ref[...])
    yb = _dot(ob_ref[0], wob_ref[...])
    g = gate_ref[0].astype(F32)
    y = g[:, :D_MODEL] * ya + g[:, D_MODEL:] * yb
    z = _dot(y.astype(BF16), wout_ref[...])
    x1 = x_ref[0] + g1 * z
    x1_ref[0] = x1
    h2 = _rms(x1, nffn_ref[...], D_MODEL) * (1.0 + scale2) + shift2
    h2_ref[0] = h2.astype(BF16)

    h_hi, h_lo = _split_bf16(h2)
    logits = _dot(h_hi, rwh_ref[...]) + (_dot(h_hi, rwl_ref[...]) + _dot(h_lo, rwh_ref[...])) + rb_ref[...]
    lane = lax.broadcasted_iota(jnp.int32, logits.shape, 1).astype(F32)
    cur = jnp.where(lane < N_EXPERTS, logits, NEG_INF)
    vals, idxs = [], []
    for _ in range(TOP_K):
        mx = jnp.max(cur, axis=-1, keepdims=True)
        ix = jnp.min(jnp.where(cur == mx, lane, float(LANES)), axis=-1, keepdims=True)
        vals.append(mx)
        idxs.append(ix)
        cur = jnp.where(lane == ix, NEG_INF, cur)
    ex = [jnp.exp(v - vals[0]) for v in vals]
    den = ex[0] + ex[1] + ex[2] + ex[3]
    eidx = jnp.zeros(logits.shape, F32)
    egate = jnp.zeros(logits.shape, F32)
    for k in range(TOP_K):
        eidx = jnp.where(lane == k, idxs[k], eidx)
        egate = jnp.where(lane == k, ex[k] / den, egate)
    eidx_ref[0] = eidx.astype(jnp.int32)
    egate_ref[0] = egate


def _merge(o_a, o_b, gates, x, mod3, w, tm):
    b, l, _ = x.shape
    const = lambda shape: pl.BlockSpec(shape, lambda bi, i: (0,) * len(shape), pipeline_mode=pl.Buffered(1))
    tok = lambda width: pl.BlockSpec((1, tm, width), lambda bi, i: (bi, i, 0))
    return pl.pallas_call(
        _merge_kernel,
        out_shape=[jax.ShapeDtypeStruct((b, l, D_MODEL), F32), jax.ShapeDtypeStruct((b, l, D_MODEL), BF16),
                   jax.ShapeDtypeStruct((b, l, LANES), jnp.int32), jax.ShapeDtypeStruct((b, l, LANES), F32)],
        grid=(b, l // tm),
        in_specs=[tok(GQA_Q_W), tok(MLA_V_W), tok(2 * D_MODEL), tok(D_MODEL),
                  pl.BlockSpec((1, 6, D_MODEL), lambda bi, i: (bi, 0, 0)),
                  const((GQA_Q_W, D_MODEL)), const((MLA_V_W, D_MODEL)), const((D_MODEL, D_MODEL)), const((1, D_MODEL)),
                  const((D_MODEL, LANES)), const((D_MODEL, LANES)), const((1, LANES))],
        out_specs=[tok(D_MODEL), tok(D_MODEL), tok(LANES), tok(LANES)],
        compiler_params=pltpu.CompilerParams(dimension_semantics=("parallel", "parallel"),
                                             vmem_limit_bytes=56 * 1024 * 1024),
        name="merge_router",
    )(o_a, o_b, gates, x, mod3, w["woa"], w["wob"], w["wout"], w["norm_ffn"], w["rw_hi"], w["rw_lo"], w["rb"])


def _onehots(idx, lane):
    return [lane == idx[:, k:k + 1] for k in range(TOP_K)]


SEG_ALIGN = 16


def _route_kernel(eidx_ref, rank_ref, before_ref, tcnt_ref, cnt_ref, carry_ref):
    @pl.when(pl.program_id(0) == 0)
    def _():
        carry_ref[...] = jnp.zeros_like(carry_ref)

    before_ref[0] = carry_ref[...]
    idx = eidx_ref[...]
    tm = idx.shape[0]
    lane = lax.broadcasted_iota(jnp.int32, idx.shape, 1)
    oh = _onehots(idx, lane)
    total = jnp.zeros(idx.shape, F32)
    for k in range(TOP_K):
        total = total + jnp.where(oh[k], 1.0, 0.0)
    row = lax.broadcasted_iota(jnp.int32, (tm, tm), 0)
    col = lax.broadcasted_iota(jnp.int32, (tm, tm), 1)
    tri = jnp.where(row > col, 1.0, 0.0).astype(BF16)
    before = _dot(tri, total.astype(BF16)) + carry_ref[0:1, :]
    rank = jnp.where(lane < TOP_K, jnp.take_along_axis(before, idx, axis=1), 0.0)
    rank_ref[...] = rank.astype(jnp.int32)
    tile_cnt = jnp.sum(total, axis=0, keepdims=True)
    tcnt_ref[0] = jnp.broadcast_to(tile_cnt, carry_ref.shape)
    carry_ref[...] = carry_ref[...] + tile_cnt
    cnt_ref[...] = carry_ref[...]


def _route(eidx, tm):
    t = eidx.shape[0]
    return pl.pallas_call(
        _route_kernel,
        out_shape=[jax.ShapeDtypeStruct((t, LANES), jnp.int32), jax.ShapeDtypeStruct((t // tm, 8, LANES), F32),
                   jax.ShapeDtypeStruct((t // tm, 8, LANES), F32), jax.ShapeDtypeStruct((8, LANES), F32)],
        grid=(t // tm,),
        in_specs=[pl.BlockSpec((tm, LANES), lambda i: (i, 0))],
        out_specs=[pl.BlockSpec((tm, LANES), lambda i: (i, 0)), pl.BlockSpec((1, 8, LANES), lambda i: (i, 0, 0)),
                   pl.BlockSpec((1, 8, LANES), lambda i: (i, 0, 0)), pl.BlockSpec((8, LANES), lambda i: (0, 0))],
        scratch_shapes=[pltpu.VMEM((8, LANES), F32)],
        compiler_params=pltpu.CompilerParams(dimension_semantics=("arbitrary",)),
        name="route_rank",
    )(eidx)


def _dest_kernel(eidx_ref, rank_ref, egate_ref, start_ref, before_ref, dest_ref, col_ref, dest_t_ref, col_t_ref,
                 gate_t_ref):
    idx = eidx_ref[...]
    valid = lax.broadcasted_iota(jnp.int32, idx.shape, 1) < TOP_K
    region_start = jnp.take_along_axis(jnp.broadcast_to(start_ref[0:1, :], idx.shape), idx, axis=1)
    seg_start = jnp.take_along_axis(jnp.broadcast_to(before_ref[0][0:1, :], idx.shape), idx, axis=1)
    rank = rank_ref[...].astype(F32)
    window_start = jnp.floor(seg_start * (1.0 / SEG_ALIGN)) * SEG_ALIGN
    local = rank - window_start
    dest = jnp.where(valid, region_start + rank, 0.0)
    col = jnp.where(valid & (local < WIN), idx.astype(F32) * WIN + local, -1.0)
    dest_ref[...] = dest.astype(jnp.int32)
    col_ref[...] = col.astype(jnp.int32)
    dest_t_ref[0] = dest.T[0:8, :].astype(jnp.int32)
    col_t_ref[0] = col.T[0:8, :].astype(jnp.int32)
    gate_t_ref[0] = egate_ref[...].T[0:8, :]


def _dest(eidx, rank, egate, start, before, tm):
    t = eidx.shape[0]
    blk = pl.BlockSpec((tm, LANES), lambda i: (i, 0))
    blk_t = pl.BlockSpec((1, 8, tm), lambda i: (i, 0, 0))
    tok = jax.ShapeDtypeStruct((t, LANES), jnp.int32)
    tok_t = jax.ShapeDtypeStruct((t // tm, 8, tm), jnp.int32)
    return pl.pallas_call(
        _dest_kernel,
        out_shape=[tok, tok, tok_t, tok_t, jax.ShapeDtypeStruct((t // tm, 8, tm), F32)],
        grid=(t // tm,),
        in_specs=[blk, blk, blk, pl.BlockSpec((8, LANES), lambda i: (0, 0)),
                  pl.BlockSpec((1, 8, LANES), lambda i: (i, 0, 0))],
        out_specs=[blk, blk, blk_t, blk_t, blk_t],
        compiler_params=pltpu.CompilerParams(dimension_semantics=("parallel",)),
        name="route_dest",
    )(eidx, rank, egate, start, before)


MOE_TILE = 256
WIN = 64
N_MAIN_ROWS = N_EXPERTS * WIN
MAIN_CHUNK = 512
WIN_PER_STACK = MOE_TILE // WIN
MAX_WINDOWS = (MOE_TILE * TOP_K + N_EXPERTS * SEG_ALIGN) // WIN
CARRY_ROWS = N_EXPERTS * SEG_ALIGN
WORKLIST_LEN = -(-(MAX_WINDOWS + WIN_PER_STACK) // WIN_PER_STACK) * WIN_PER_STACK


def _build_worklist(ts_ref, tc_ref, wl_ref, junk_row):
    base = pl.program_id(0) * N_EXPERTS

    def per_expert(e, n):
        first = ts_ref[base + e]
        n_win = (tc_ref[base + e] + (WIN - 1)) // WIN

        def per_window(wi, n):
            wl_ref[n] = first + wi * WIN
            return n + 1

        return lax.fori_loop(1, n_win, per_window, n)

    n = lax.fori_loop(0, N_EXPERTS, per_expert, 0)
    n_stacks = (n + (WIN_PER_STACK - 1)) // WIN_PER_STACK

    def pad(j, c):
        wl_ref[j] = junk_row + (j % WIN_PER_STACK) * WIN
        return c

    lax.fori_loop(n, n_stacks * WIN_PER_STACK, pad, 0)
    return n_stacks


def _seg_aligned(row):
    return row if isinstance(row, int) else pl.multiple_of(row, SEG_ALIGN)


def _stack_row_ids(wl_ref, stack, shape, axis):
    pos = lax.broadcasted_iota(jnp.int32, shape, axis)
    row = jnp.full(shape, -1, jnp.int32)
    for wi in range(WIN_PER_STACK - 1, -1, -1):
        row = jnp.where(pos < (wi + 1) * WIN, wl_ref[stack * WIN_PER_STACK + wi] + (pos - wi * WIN), row)
    return row


def _select_rows(row_id, id_t, gate_t):
    shape = (row_id.shape[0], id_t.shape[1])
    sel = jnp.zeros(shape, F32)
    gsel = jnp.zeros(shape, F32)
    for k in range(TOP_K):
        hit = row_id == id_t[k:k + 1, :]
        sel = jnp.where(hit, 1.0, sel)
        gsel = jnp.where(hit, gate_t[k:k + 1, :], gsel)
    return sel.astype(BF16), jnp.sum(gsel, axis=1, keepdims=True)


MAIN_PENDING = 2


TAIL_ROWS = WIN + EXPERT_BLOCK


def _dispatch_kernel(ts_ref, tc_ref, ov_ref, re_ref, h_ref, col_t_ref, dest_t_ref, gate_t_ref,
                     buf_ref, gs_ref, wl_ref, pend_ref, main_x, main_g, carry_x, carry_g, stage_x, stage_g,
                     main_sem, sem, *, junk_row):
    i = pl.program_id(0)
    base = i * N_EXPERTS
    next_base = jnp.minimum(i + 1, pl.num_programs(0) - 1) * N_EXPERTS

    @pl.when(i == 0)
    def _():
        pend_ref[0] = 0
        pend_ref[1] = 0
        pend_ref[MAIN_PENDING] = 0
        carry_x[...] = jnp.zeros_like(carry_x)
        carry_g[...] = jnp.zeros_like(carry_g)
        main_x[0:TAIL_ROWS, :] = jnp.zeros((TAIL_ROWS, D_MODEL), BF16)
        main_g[0:TAIL_ROWS, :] = jnp.zeros((TAIL_ROWS, LANES), F32)
        cps = []
        for e in range(N_EXPERTS):
            rows = pl.ds(pl.multiple_of(jnp.maximum(re_ref[e] - TAIL_ROWS, 0), SEG_ALIGN), TAIL_ROWS)
            cps.append(pltpu.make_async_copy(main_x.at[pl.ds(0, TAIL_ROWS)], buf_ref.at[rows], main_sem))
            cps.append(pltpu.make_async_copy(main_g.at[pl.ds(0, TAIL_ROWS)], gs_ref.at[rows], main_sem))
        for cp in cps:
            cp.start()
        for cp in cps:
            cp.wait()

        def zero_block_copies(blk):
            rows = pl.ds(pl.multiple_of(blk * EXPERT_BLOCK, EXPERT_BLOCK), EXPERT_BLOCK)
            return (pltpu.make_async_copy(main_x.at[pl.ds(0, EXPERT_BLOCK)], buf_ref.at[rows], main_sem),
                    pltpu.make_async_copy(main_g.at[pl.ds(0, EXPERT_BLOCK)], gs_ref.at[rows], main_sem))

        def start_zero(blk, c):
            for cp in zero_block_copies(blk):
                cp.start()
            return c

        def wait_zero(blk, c):
            for cp in zero_block_copies(blk):
                cp.wait()
            return c

        first_free, n_blocks = re_ref[N_EXPERTS - 1] // EXPERT_BLOCK, (junk_row + EXPERT_BLOCK) // EXPERT_BLOCK
        lax.fori_loop(first_free, n_blocks, start_zero, 0)
        lax.fori_loop(first_free, n_blocks, wait_zero, 0)

    gate_t = gate_t_ref[0]
    h = h_ref[...]

    def main_copies(first_row):
        cps = []
        for e in range(N_EXPERTS):
            rows = pl.ds(_seg_aligned(first_row(e)), WIN)
            cps.append(pltpu.make_async_copy(main_x.at[pl.ds(e * WIN, WIN)], buf_ref.at[rows], main_sem))
            cps.append(pltpu.make_async_copy(main_g.at[pl.ds(e * WIN, WIN)], gs_ref.at[rows], main_sem))
        return cps

    def drain_main():
        @pl.when(pend_ref[MAIN_PENDING] == 1)
        def _():
            for cp in main_copies(lambda e: 0):
                cp.wait()
            pend_ref[MAIN_PENDING] = 0

    col_t = col_t_ref[0]
    dest_t = dest_t_ref[0]
    pieces = []
    for c in range(N_MAIN_ROWS // MAIN_CHUNK):
        row_id = lax.broadcasted_iota(jnp.int32, (MAIN_CHUNK, 1), 0) + c * MAIN_CHUNK
        sel, gate_row = _select_rows(row_id, col_t, gate_t)
        pieces.append((_dot(sel, h).astype(BF16), gate_row))

    def window_copies(slot, first_row):
        cps = []
        for wi in range(WIN_PER_STACK):
            rows = pl.ds(_seg_aligned(first_row(wi)), WIN)
            cps.append(pltpu.make_async_copy(stage_x.at[slot, pl.ds(wi * WIN, WIN)], buf_ref.at[rows], sem.at[slot]))
            cps.append(pltpu.make_async_copy(stage_g.at[slot, pl.ds(wi * WIN, WIN)], gs_ref.at[rows], sem.at[slot]))
        return cps

    def drain(slot):
        @pl.when(pend_ref[slot] == 1)
        def _():
            for cp in window_copies(slot, lambda wi: 0):
                cp.wait()
            pend_ref[slot] = 0

    drain_main()
    drain(0)
    drain(1)
    for c, (packed, gate_row) in enumerate(pieces):
        main_x[c * MAIN_CHUNK:(c + 1) * MAIN_CHUNK, :] = packed
        main_g[c * MAIN_CHUNK:(c + 1) * MAIN_CHUNK, :] = jnp.broadcast_to(gate_row, (MAIN_CHUNK, LANES))
    for e in range(N_EXPERTS):
        win, grp = pl.ds(e * WIN, SEG_ALIGN), pl.ds(e * SEG_ALIGN, SEG_ALIGN)
        main_x[win, :] = main_x[win, :] + carry_x[grp, :]
        main_g[win, :] = main_g[win, :] + carry_g[grp, :]
        ahead = ts_ref[next_base + e] - ts_ref[base + e]
        end = pl.ds(pl.multiple_of(e * WIN + jnp.minimum(ahead, WIN - SEG_ALIGN), SEG_ALIGN), SEG_ALIGN)
        carry_x[grp, :] = jnp.where(ahead < WIN, main_x[end, :], 0).astype(BF16)
        carry_g[grp, :] = jnp.where(ahead < WIN, main_g[end, :], 0.0)
    for n, cp in enumerate(main_copies(lambda e: ts_ref[base + e])):
        cp.start(priority=(n // 2) % 2)
    pend_ref[MAIN_PENDING] = 1

    @pl.when(ov_ref[i] > 0)
    def _():
        n_stacks = _build_worklist(ts_ref, tc_ref, wl_ref, junk_row)
        group_pos = lax.broadcasted_iota(jnp.int32, (SEG_ALIGN, 1), 0)
        end_group = jnp.concatenate([ts_ref[next_base + e] + group_pos for e in range(N_EXPERTS)], axis=0)
        sel, end_gate = _select_rows(end_group, dest_t, gate_t)
        end_rows = _dot(sel, h).astype(BF16)
        end_gate = jnp.broadcast_to(end_gate, (CARRY_ROWS, LANES))
        for e in range(N_EXPERTS):
            grp = pl.ds(e * SEG_ALIGN, SEG_ALIGN)
            beyond = ts_ref[next_base + e] - ts_ref[base + e] >= WIN
            carry_x[grp, :] = jnp.where(beyond, end_rows[e * SEG_ALIGN:(e + 1) * SEG_ALIGN, :], carry_x[grp, :])
            carry_g[grp, :] = jnp.where(beyond, end_gate[e * SEG_ALIGN:(e + 1) * SEG_ALIGN, :], carry_g[grp, :])

        def stack_body(s, c):
            slot = s & 1
            sel, gate_row = _select_rows(_stack_row_ids(wl_ref, s, (MOE_TILE, 1), 0), dest_t, gate_t)
            rows = _dot(sel, h)
            drain(slot)
            stage_x[slot] = rows.astype(BF16)
            stage_g[slot] = jnp.broadcast_to(gate_row, (MOE_TILE, LANES))
            for cp in window_copies(slot, lambda wi: wl_ref[s * WIN_PER_STACK + wi]):
                cp.start()
            pend_ref[slot] = 1
            return c

        lax.fori_loop(0, n_stacks, stack_body, 0)

    @pl.when(i == pl.num_programs(0) - 1)
    def _():
        drain_main()
        drain(0)
        drain(1)


def _moe_grid_spec(n_tiles, in_specs, out_specs, scratch_shapes):
    return pltpu.PrefetchScalarGridSpec(num_scalar_prefetch=4, grid=(n_tiles,), in_specs=in_specs,
                                        out_specs=out_specs, scratch_shapes=scratch_shapes)


def _dispatch(tile_start, tile_cnt, tile_ovf, region_ends, h2, col_t, dest_t, gate_t, n_rows):
    n_tiles = h2.shape[0] // MOE_TILE
    junk_row = n_rows - EXPERT_BLOCK
    any_spec = pl.BlockSpec(memory_space=pl.ANY)
    tok_t = pl.BlockSpec((1, 8, MOE_TILE), lambda i, *_: (i, 0, 0))
    return pl.pallas_call(
        functools.partial(_dispatch_kernel, junk_row=junk_row),
        out_shape=[jax.ShapeDtypeStruct((n_rows, D_MODEL), BF16), jax.ShapeDtypeStruct((n_rows, LANES), F32)],
        grid_spec=_moe_grid_spec(
            n_tiles,
            [pl.BlockSpec((MOE_TILE, D_MODEL), lambda i, *_: (i, 0)), tok_t, tok_t, tok_t],
            [any_spec, any_spec],
            [pltpu.SMEM((WORKLIST_LEN,), jnp.int32), pltpu.SMEM((3,), jnp.int32),
             pltpu.VMEM((N_MAIN_ROWS, D_MODEL), BF16), pltpu.VMEM((N_MAIN_ROWS, LANES), F32),
             pltpu.VMEM((CARRY_ROWS, D_MODEL), BF16), pltpu.VMEM((CARRY_ROWS, LANES), F32),
             pltpu.VMEM((2, MOE_TILE, D_MODEL), BF16), pltpu.VMEM((2, MOE_TILE, LANES), F32),
             pltpu.SemaphoreType.DMA(()), pltpu.SemaphoreType.DMA((2,))]),
        compiler_params=pltpu.CompilerParams(dimension_semantics=("arbitrary",),
                                             vmem_limit_bytes=56 * 1024 * 1024),
        name="moe_dispatch",
    )(tile_start, tile_cnt, tile_ovf, region_ends, h2, col_t, dest_t, gate_t)


def _select_cols(ids, col_id):
    sel = jnp.zeros((ids.shape[0], col_id.shape[1]), F32)
    for k in range(TOP_K):
        sel = jnp.where(ids[:, k:k + 1] == col_id, 1.0, sel)
    return sel.astype(BF16)


def _combine_kernel(ts_ref, tc_ref, ov_ref, re_ref, col_ref, dest_ref, x1_ref, mod_ref, y_ref, o_ref,
                    wl_ref, main_y, stage_y, acc_ref, main_sem, sem, *, junk_row):
    del re_ref
    i = pl.program_id(0)
    slot_i = i & 1

    def main_copies(slot, first_row):
        return [pltpu.make_async_copy(y_ref.at[pl.ds(_seg_aligned(first_row(e)), WIN)],
                                      main_y.at[slot, pl.ds(e * WIN, WIN)], main_sem.at[slot])
                for e in range(N_EXPERTS)]

    def fetch_main(tile, slot):
        for n, cp in enumerate(main_copies(slot, lambda e: ts_ref[tile * N_EXPERTS + e])):
            cp.start(priority=n % 2)

    @pl.when(i == 0)
    def _():
        fetch_main(0, 0)

    @pl.when(i + 1 < pl.num_programs(0))
    def _():
        fetch_main(i + 1, 1 - slot_i)

    for cp in main_copies(slot_i, lambda e: 0):
        cp.wait()

    col = col_ref[...]
    sel = jnp.concatenate(
        [_select_cols(col, lax.broadcasted_iota(jnp.int32, (1, MAIN_CHUNK), 1) + c * MAIN_CHUNK)
         for c in range(N_MAIN_ROWS // MAIN_CHUNK)], axis=1)
    acc_ref[...] = _dot(sel, main_y[slot_i])

    @pl.when(ov_ref[i] > 0)
    def _():
        n_stacks = _build_worklist(ts_ref, tc_ref, wl_ref, junk_row)
        dest = dest_ref[...]
        if WIN_PER_STACK * WIN < MOE_TILE:
            for slot in range(2):
                stage_y[slot, WIN_PER_STACK * WIN:, :] = jnp.zeros((MOE_TILE - WIN_PER_STACK * WIN, D_MODEL), BF16)

        def window_copies(slot, first_row):
            return [pltpu.make_async_copy(y_ref.at[pl.ds(_seg_aligned(first_row(wi)), WIN)],
                                          stage_y.at[slot, pl.ds(wi * WIN, WIN)], sem.at[slot])
                    for wi in range(WIN_PER_STACK)]

        def fetch(s, slot):
            for cp in window_copies(slot, lambda wi: wl_ref[s * WIN_PER_STACK + wi]):
                cp.start()

        @pl.when(n_stacks > 0)
        def _():
            fetch(0, 0)

        def stack_body(s, c):
            slot = s & 1

            @pl.when(s + 1 < n_stacks)
            def _():
                fetch(s + 1, 1 - slot)

            for cp in window_copies(slot, lambda wi: 0):
                cp.wait()
            sel_o = _select_cols(dest, _stack_row_ids(wl_ref, s, (1, MOE_TILE), 1))
            acc_ref[...] += _dot(sel_o, stage_y[slot])
            return c

        lax.fori_loop(0, n_stacks, stack_body, 0)

    g2 = mod_ref[0][5:6, :]
    o_ref[...] = x1_ref[...] + g2 * acc_ref[...]


def _combine(tile_start, tile_cnt, tile_ovf, region_ends, col, dest, x1, mod3, y, l):
    t = x1.shape[0]
    per_batch = l // MOE_TILE
    junk_row = y.shape[0] - EXPERT_BLOCK
    tok = lambda width: pl.BlockSpec((MOE_TILE, width), lambda i, *_: (i, 0))
    return pl.pallas_call(
        functools.partial(_combine_kernel, junk_row=junk_row),
        out_shape=jax.ShapeDtypeStruct((t, D_MODEL), F32),
        grid_spec=_moe_grid_spec(
            t // MOE_TILE,
            [tok(LANES), tok(LANES), tok(D_MODEL),
             pl.BlockSpec((1, 6, D_MODEL), lambda i, *_: (i // per_batch, 0, 0)),
             pl.BlockSpec(memory_space=pl.ANY)],
            tok(D_MODEL),
            [pltpu.SMEM((WORKLIST_LEN,), jnp.int32), pltpu.VMEM((2, N_MAIN_ROWS, D_MODEL), BF16),
             pltpu.VMEM((2, MOE_TILE, D_MODEL), BF16), pltpu.VMEM((MOE_TILE, D_MODEL), F32),
             pltpu.SemaphoreType.DMA((2,)), pltpu.SemaphoreType.DMA((2,))]),
        compiler_params=pltpu.CompilerParams(dimension_semantics=("arbitrary",),
                                             vmem_limit_bytes=56 * 1024 * 1024),
        name="moe_combine",
    )(tile_start, tile_cnt, tile_ovf, region_ends, col, dest, x1, mod3, y)


def _expert_kernel(be_ref, nused_ref, x_ref, g_ref, w1_ref, b1_ref, w2_ref, b2_ref, y_ref, w1b_ref, w2b_ref):
    i = pl.program_id(0)
    live = i < nused_ref[0]

    @pl.when(live & ((i == 0) | (be_ref[i] != be_ref[jnp.maximum(i - 1, 0)])))
    def _():
        w1b_ref[...] = w1_ref[0].astype(BF16)
        w2b_ref[...] = w2_ref[0].astype(BF16)

    @pl.when(live)
    def _():
        gu = _dot(x_ref[...], w1b_ref[...]) + b1_ref[0]
        glu = jnp.minimum(gu[:, :D_EXPERT], SWIGLU_LIMIT)
        lin = jnp.clip(gu[:, D_EXPERT:], -SWIGLU_LIMIT, SWIGLU_LIMIT)
        act = glu * (1.0 / (1.0 + jnp.exp(-SWIGLU_ALPHA * glu))) * (lin + 1.0)
        y = (_dot(act.astype(BF16), w2b_ref[...]) + b2_ref[0]) * g_ref[:, 0:1]
        y_ref[...] = y.astype(BF16)

    @pl.when(pl.program_id(0) >= nused_ref[0])
    def _():
        y_ref[...] = jnp.zeros_like(y_ref)


def _experts(block_e, nused, buf, gs, w1, b1, w2, b2):
    nb = buf.shape[0] // EXPERT_BLOCK
    row = lambda i, be, nu: (jnp.minimum(i, nu[0] - 1), 0)
    out_row = lambda i, be, nu: (i, 0)
    exp3 = lambda i, be, nu: (be[jnp.minimum(i, nu[0] - 1)], 0, 0)
    return pl.pallas_call(
        _expert_kernel,
        out_shape=jax.ShapeDtypeStruct((buf.shape[0], D_MODEL), BF16),
        grid_spec=pltpu.PrefetchScalarGridSpec(
            num_scalar_prefetch=2,
            grid=(nb,),
            in_specs=[pl.BlockSpec((EXPERT_BLOCK, D_MODEL), row),
                      pl.BlockSpec((EXPERT_BLOCK, LANES), row),
                      pl.BlockSpec((1, D_MODEL, 2 * D_EXPERT), exp3),
                      pl.BlockSpec((1, 1, 2 * D_EXPERT), exp3),
                      pl.BlockSpec((1, D_EXPERT, D_MODEL), exp3),
                      pl.BlockSpec((1, 1, D_MODEL), exp3)],
            out_specs=pl.BlockSpec((EXPERT_BLOCK, D_MODEL), out_row),
            scratch_shapes=[pltpu.VMEM((D_MODEL, 2 * D_EXPERT), BF16), pltpu.VMEM((D_EXPERT, D_MODEL), BF16)]),
        compiler_params=pltpu.CompilerParams(dimension_semantics=("arbitrary",),
                                             vmem_limit_bytes=56 * 1024 * 1024),
        name="moe_experts",
    )(block_e, nused, buf, gs, w1, b1, w2, b2)


def _rope_tables(seq_len):
    pos = jnp.arange(seq_len, dtype=jnp.int32)
    row = (pos // GRID_W).astype(F32)[:, None]
    col = (pos % GRID_W).astype(F32)[:, None]

    def table(d_axis, pad):
        inv_freq = ROPE_THETA ** (-jnp.arange(0, d_axis, 2, dtype=F32) / d_axis)
        ar, ac = row * inv_freq[None, :], col * inv_freq[None, :]
        cos = jnp.concatenate([jnp.cos(ar), jnp.cos(ar), jnp.cos(ac), jnp.cos(ac)], axis=-1)
        sin = jnp.concatenate([-jnp.sin(ar), jnp.sin(ar), -jnp.sin(ac), jnp.sin(ac)], axis=-1)
        if pad:
            cos = jnp.concatenate([cos, jnp.ones((seq_len, pad), F32)], axis=-1)
            sin = jnp.concatenate([sin, jnp.zeros((seq_len, pad), F32)], axis=-1)
        return cos, sin

    cg, sg = table(GQA_HEAD_DIM // 2, 0)
    cm, sm = table(MLA_ROPE_DIM // 2, LANES - MLA_ROPE_DIM)
    return cg, sg, cm, sm


def _pad_lanes(a, width):
    return jnp.pad(a, [(0, 0)] * (a.ndim - 1) + [(0, width - a.shape[-1])])


def _prep_weights(p):
    w_in = p["w_in"]
    w = {"norm_mix": p["norm_mix"].reshape(1, D_MODEL), "norm_ffn": p["norm_ffn"].reshape(1, D_MODEL)}
    w["wq"] = w_in[:, :OFF_GQA_K].astype(BF16)
    w["wk"] = w_in[:, OFF_GQA_K:OFF_GQA_V].astype(BF16)
    w["wv"] = w_in[:, OFF_GQA_V:OFF_MLA_QA].T.astype(BF16)
    w["wqa"] = w_in[:, OFF_MLA_QA:OFF_MLA_KVA].astype(BF16)
    w["wckv"] = w_in[:, OFF_MLA_KVA:OFF_MLA_KVA + MLA_KV_RANK].astype(BF16)
    w["wkr"] = _pad_lanes(w_in[:, OFF_MLA_KVA + MLA_KV_RANK:OFF_GATE], LANES).astype(BF16)
    w["wg"] = w_in[:, OFF_GATE:].astype(BF16)
    wqb = p["mla_w_qb"].reshape(MLA_Q_RANK, MLA_HEADS, MLA_QK_DIM)
    w["wqb"] = _pad_lanes(wqb, MLA_HEAD_PAD).reshape(MLA_Q_RANK, MLA_HEADS * MLA_HEAD_PAD).astype(BF16)
    wkvb = p["mla_w_kvb"].reshape(MLA_KV_RANK, MLA_HEADS, MLA_NOPE_DIM + MLA_V_DIM)
    w["wkb"] = wkvb[:, :, :MLA_NOPE_DIM].reshape(MLA_KV_RANK, MLA_HEADS * MLA_NOPE_DIM).astype(BF16)
    w["wvb"] = wkvb[:, :, MLA_NOPE_DIM:].reshape(MLA_KV_RANK, MLA_V_W).T.astype(BF16)
    w["gq"] = p["gqa_q_norm"].reshape(1, GQA_HEAD_DIM)
    w["gk"] = p["gqa_k_norm"].reshape(1, GQA_HEAD_DIM)
    w["gqa"] = p["mla_q_a_norm"].reshape(1, MLA_Q_RANK)
    w["gkva"] = p["mla_kv_a_norm"].reshape(1, MLA_KV_RANK)
    w["gmq"] = _pad_lanes(p["mla_q_norm"].reshape(1, MLA_QK_DIM), MLA_HEAD_PAD)
    w["gmk"] = _pad_lanes(p["mla_k_norm"].reshape(1, MLA_QK_DIM), MLA_HEAD_PAD)
    w["woa"] = p["w_o_gqa"].astype(BF16)
    w["wob"] = p["w_o_mla"].astype(BF16)
    w["wout"] = p["w_out"].astype(BF16)
    rw = _pad_lanes(p["router_w"], LANES)
    w["rw_hi"] = rw.astype(BF16)
    w["rw_lo"] = (rw - w["rw_hi"].astype(F32)).astype(BF16)
    w["rb"] = _pad_lanes(p["router_b"].reshape(1, N_EXPERTS), LANES)
    return w


def _layer(x, c, ctx, c_ctx, p):
    b, l, _ = x.shape
    lc = ctx.shape[1]
    t = b * l
    w = _prep_weights(p)

    n_mod_rows = -(-(b + 1) // 8) * 8
    cc = jnp.zeros((n_mod_rows, D_MODEL), F32).at[:b].set(c).at[b].set(c_ctx)
    mod3 = _ada_mod(cc, p["ada_w"], p["ada_b"]).reshape(n_mod_rows, 6, D_MODEL)

    tables = _rope_tables(l)
    ident = (jnp.ones((lc, LANES), F32), jnp.zeros((lc, LANES), F32)) * 2
    tm = min(512, l)
    q_a, k_a, v_a, q_m, k_m, v_m, gates = _proj(x, mod3, lambda bi: bi, tables, w, True, tm)
    kc_a, vc_a, kc_m, vc_m = _proj(ctx, mod3, lambda bi: b, ident, w, False, min(256, lc))

    tq, tk = min(2048, l), min(1024, l)
    o_a = _attention(q_a, k_a, v_a, kc_a, vc_a, GQA_HEADS, GQA_GROUP, GQA_HEAD_DIM, GQA_HEAD_DIM, tq, tk, "attn_gqa",
                     _score_bound(p["gqa_q_norm"], p["gqa_k_norm"], GQA_HEAD_DIM))
    o_m = _attention(q_m, k_m, v_m, kc_m, vc_m, MLA_HEADS, 1, MLA_HEAD_PAD, MLA_V_DIM, tq, tk, "attn_mla",
                     _score_bound(p["mla_q_norm"], p["mla_k_norm"], MLA_QK_DIM))

    x1, h2, eidx, egate = _merge(o_a, o_m, gates, x, mod3, w, tm)
    x1, h2 = x1.reshape(t, D_MODEL), h2.reshape(t, D_MODEL)
    eidx, egate = eidx.reshape(t, LANES), egate.reshape(t, LANES)

    assert t % MOE_TILE == 0
    n_tiles = t // MOE_TILE
    rank, before_raw, tcnt, cnt = _route(eidx, MOE_TILE)
    counts = cnt[0, :N_EXPERTS].astype(jnp.int32)
    before = before_raw[:, 0, :N_EXPERTS].astype(jnp.int32)
    region = (counts + WIN + EXPERT_BLOCK - 1) // EXPERT_BLOCK * EXPERT_BLOCK
    region_ends = jnp.cumsum(region)
    region_starts = region_ends - region
    max_rows = t * TOP_K + N_EXPERTS * (WIN + EXPERT_BLOCK - 1)
    n_blocks = max_rows // EXPERT_BLOCK + 1
    n_rows = n_blocks * EXPERT_BLOCK
    block_row = jnp.arange(n_blocks, dtype=jnp.int32) * EXPERT_BLOCK
    block_e = jnp.minimum(jnp.sum(region_ends[None, :] <= block_row[:, None], axis=1), N_EXPERTS - 1).astype(jnp.int32)
    nused = (region_ends[-1:] // EXPERT_BLOCK).astype(jnp.int32)
    tile_start = (region_starts[None, :] + before // SEG_ALIGN * SEG_ALIGN).reshape(-1)
    tile_cnt = before % SEG_ALIGN + tcnt[:, 0, :N_EXPERTS].astype(jnp.int32)
    tile_ovf = jnp.sum(jnp.maximum((tile_cnt + WIN - 1) // WIN - 1, 0), axis=1).astype(jnp.int32)
    tile_cnt = tile_cnt.reshape(-1)
    start = jnp.zeros((8, LANES), F32).at[0, :N_EXPERTS].set(region_starts.astype(F32))
    dest, col, dest_t, col_t, gate_t = _dest(eidx, rank, egate, start, before_raw, MOE_TILE)

    region_ends = region_ends.astype(jnp.int32)
    buf, gs = _dispatch(tile_start, tile_cnt, tile_ovf, region_ends, h2, col_t, dest_t, gate_t, n_rows)
    y = _experts(block_e, nused, buf, gs, p["expert_w1"], p["expert_b1"].reshape(N_EXPERTS, 1, -1),
                 p["expert_w2"], p["expert_b2"].reshape(N_EXPERTS, 1, -1))
    out = _combine(tile_start, tile_cnt, tile_ovf, region_ends, col, dest, x1, mod3, y, l)
    return out.reshape(b, l, D_MODEL)


def kernel(x, c, ctx, c_ctx, ada_w, ada_b, norm_mix, norm_ffn, w_in, gqa_q_norm, gqa_k_norm, mla_q_a_norm, mla_kv_a_norm, mla_w_qb, mla_w_kvb, mla_q_norm, mla_k_norm, w_o_gqa, w_o_mla, w_out, router_w, router_b, expert_w1, expert_b1, expert_w2, expert_b2):
    assert ada_w.shape[0] == 1, "single-layer problem: the context stream is never updated"
    p = {
        "ada_w": ada_w[0], "ada_b": ada_b[0], "norm_mix": norm_mix[0], "norm_ffn": norm_ffn[0],
        "w_in": w_in[0], "gqa_q_norm": gqa_q_norm[0], "gqa_k_norm": gqa_k_norm[0],
        "mla_q_a_norm": mla_q_a_norm[0], "mla_kv_a_norm": mla_kv_a_norm[0],
        "mla_w_qb": mla_w_qb[0], "mla_w_kvb": mla_w_kvb[0],
        "mla_q_norm": mla_q_norm[0], "mla_k_norm": mla_k_norm[0],
        "w_o_gqa": w_o_gqa[0], "w_o_mla": w_o_mla[0], "w_out": w_out[0],
        "router_w": router_w[0], "router_b": router_b[0],
        "expert_w1": expert_w1[0], "expert_b1": expert_b1[0],
        "expert_w2": expert_w2[0], "expert_b2": expert_b2[0],
    }
    return _layer(x, c, ctx, c_ctx, p)
```

```python
import functools
import math

import jax
import jax.numpy as jnp
from jax import lax
from jax.experimental import pallas as pl
from jax.experimental.pallas import tpu as pltpu

D_MODEL = 1024
GRID_W = 64
EPS = 1e-6
ROPE_THETA = 10000.0

GQA_HEADS = 8
GQA_KV_HEADS = 2
GQA_GROUP = GQA_HEADS // GQA_KV_HEADS
GQA_HEAD_DIM = 128
GQA_Q_W = GQA_HEADS * GQA_HEAD_DIM
GQA_KV_W = GQA_KV_HEADS * GQA_HEAD_DIM

MLA_HEADS = 8
MLA_Q_RANK = 256
MLA_KV_RANK = 128
MLA_NOPE_DIM = 128
MLA_ROPE_DIM = 64
MLA_V_DIM = 128
MLA_QK_DIM = MLA_NOPE_DIM + MLA_ROPE_DIM
MLA_HEAD_PAD = 256
MLA_V_W = MLA_HEADS * MLA_V_DIM

OFF_GQA_K = GQA_Q_W
OFF_GQA_V = OFF_GQA_K + GQA_KV_W
OFF_MLA_QA = OFF_GQA_V + GQA_KV_W
OFF_MLA_KVA = OFF_MLA_QA + MLA_Q_RANK
OFF_GATE = OFF_MLA_KVA + MLA_KV_RANK + MLA_ROPE_DIM

N_EXPERTS = 32
TOP_K = 4
D_EXPERT = 1024
SWIGLU_LIMIT = 7.0
SWIGLU_ALPHA = 1.702
EXPERT_BLOCK = 512

LANES = 128
NEG_INF = float("-inf")
LOG2_E = math.log2(math.e)

BF16 = jnp.bfloat16
F32 = jnp.float32


def _dot(a, b):
    return jnp.dot(a, b, preferred_element_type=F32)


def _dot_nt(a, b):
    return lax.dot_general(a, b, (((1,), (1,)), ((), ())), preferred_element_type=F32)


def _split_bf16(a):
    hi = a.astype(BF16)
    lo = (a - hi.astype(F32)).astype(BF16)
    return hi, lo


def _rms(x, g, n):
    ms = jnp.sum(x * x, axis=-1, keepdims=True) * (1.0 / n)
    return x * lax.rsqrt(ms + EPS) * g


def _swap_halves(x, k):
    lane = lax.broadcasted_iota(jnp.int32, x.shape, 1)
    return jnp.where((lane & k) != 0, pltpu.roll(x, k, 1), pltpu.roll(x, LANES - k, 1))


def _rope(x, cos, sin_signed, k):
    return x * cos + _swap_halves(x, k) * sin_signed


def _ada_kernel(c_ref, w_ref, b_ref, o_ref):
    c = c_ref[...]
    s = c * (1.0 / (1.0 + jnp.exp(-c)))
    s_hi, s_lo = _split_bf16(s)
    w_hi, w_lo = _split_bf16(w_ref[...])
    o_ref[...] = _dot(s_hi, w_hi) + (_dot(s_hi, w_lo) + _dot(s_lo, w_hi)) + b_ref[...]


def _ada_mod(cc, ada_w, ada_b):
    n = ada_w.shape[1]
    tn = 1024
    return pl.pallas_call(
        _ada_kernel,
        out_shape=jax.ShapeDtypeStruct((cc.shape[0], n), F32),
        grid=(n // tn,),
        in_specs=[
            pl.BlockSpec((cc.shape[0], D_MODEL), lambda j: (0, 0)),
            pl.BlockSpec((D_MODEL, tn), lambda j: (0, j)),
            pl.BlockSpec((1, tn), lambda j: (0, j)),
        ],
        out_specs=pl.BlockSpec((cc.shape[0], tn), lambda j: (0, j)),
        compiler_params=pltpu.CompilerParams(dimension_semantics=("parallel",)),
        name="ada_mod",
    )(cc, ada_w, ada_b.reshape(1, n))


def _proj_kernel(*refs, with_q):
    if with_q:
        (x_ref, mod_ref, nmix_ref, cg_ref, sg_ref, cm_ref, sm_ref,
         wq_ref, wk_ref, wv_ref, wqa_ref, wckv_ref, wkr_ref, wg_ref, wqb_ref, wkb_ref, wvb_ref,
         gq_ref, gk_ref, gqa_ref, gkva_ref, gmq_ref, gmk_ref,
         q_ref, k_ref, v_ref, qm_ref, km_ref, vm_ref, gate_ref) = refs
    else:
        (x_ref, mod_ref, nmix_ref, cg_ref, sg_ref, cm_ref, sm_ref,
         wk_ref, wv_ref, wckv_ref, wkr_ref, wkb_ref, wvb_ref,
         gk_ref, gkva_ref, gmk_ref,
         k_ref, v_ref, km_ref, vm_ref) = refs

    x = x_ref[0]
    mod = mod_ref[0]
    shift, scale = mod[0:1, :], mod[1:2, :]
    h = _rms(x, nmix_ref[...], D_MODEL) * (1.0 + scale) + shift
    hb = h.astype(BF16)
    cg, sg = cg_ref[...], sg_ref[...]
    cm, sm = cm_ref[...], sm_ref[...]

    kk = _dot(hb, wk_ref[...])
    for j in range(GQA_KV_HEADS):
        sl = slice(j * GQA_HEAD_DIM, (j + 1) * GQA_HEAD_DIM)
        kn = _rms(kk[:, sl], gk_ref[...], GQA_HEAD_DIM)
        k_ref[0, :, sl] = _rope(kn, cg, sg, 32).astype(BF16)
    v_ref[0] = _dot_nt(wv_ref[...], hb).astype(BF16)

    ckv = _rms(_dot(hb, wckv_ref[...]), gkva_ref[...], MLA_KV_RANK).astype(BF16)
    vm_ref[0] = _dot_nt(wvb_ref[...], ckv).astype(BF16)
    knope = _dot(ckv, wkb_ref[...])
    kr = _dot(hb, wkr_ref[...])
    gmk = gmk_ref[...]
    g_nope, g_rope = gmk[:, :MLA_NOPE_DIM], gmk[:, MLA_NOPE_DIM:]
    ssq_r = jnp.sum(kr * kr, axis=-1, keepdims=True)
    kr_roped = _rope(kr * g_rope, cm, sm, 16)
    for j in range(MLA_HEADS):
        kn = knope[:, j * MLA_NOPE_DIM:(j + 1) * MLA_NOPE_DIM]
        ms = (jnp.sum(kn * kn, axis=-1, keepdims=True) + ssq_r) * (1.0 / MLA_QK_DIM)
        r = lax.rsqrt(ms + EPS)
        base = j * MLA_HEAD_PAD
        km_ref[0, :, base:base + MLA_NOPE_DIM] = (kn * r * g_nope).astype(BF16)
        km_ref[0, :, base + MLA_NOPE_DIM:base + MLA_HEAD_PAD] = (kr_roped * r).astype(BF16)

    if not with_q:
        return

    qq = _dot(hb, wq_ref[...])
    q_scale = GQA_HEAD_DIM ** -0.5 * LOG2_E
    for j in range(GQA_HEADS):
        sl = slice(j * GQA_HEAD_DIM, (j + 1) * GQA_HEAD_DIM)
        qn = _rms(qq[:, sl], gq_ref[...], GQA_HEAD_DIM)
        q_ref[0, :, sl] = (_rope(qn, cg, sg, 32) * q_scale).astype(BF16)

    qa = _rms(_dot(hb, wqa_ref[...]), gqa_ref[...], MLA_Q_RANK).astype(BF16)
    q2 = _dot(qa, wqb_ref[...])
    gmq = gmq_ref[...]
    gq_nope, gq_rope = gmq[:, :MLA_NOPE_DIM], gmq[:, MLA_NOPE_DIM:]
    m_scale = MLA_QK_DIM ** -0.5 * LOG2_E
    for j in range(MLA_HEADS):
        base = j * MLA_HEAD_PAD
        qn = q2[:, base:base + MLA_NOPE_DIM]
        qr = q2[:, base + MLA_NOPE_DIM:base + MLA_HEAD_PAD]
        ms = (jnp.sum(qn * qn, axis=-1, keepdims=True) + jnp.sum(qr * qr, axis=-1, keepdims=True)) * (1.0 / MLA_QK_DIM)
        r = lax.rsqrt(ms + EPS)
        qm_ref[0, :, base:base + MLA_NOPE_DIM] = (qn * r * gq_nope * m_scale).astype(BF16)
        qm_ref[0, :, base + MLA_NOPE_DIM:base + MLA_HEAD_PAD] = (_rope(qr * r * gq_rope, cm, sm, 16) * m_scale).astype(BF16)

    gl = _dot(hb, wg_ref[...])
    gate_ref[0] = (1.0 / (1.0 + jnp.exp(-gl))).astype(BF16)


def _proj(x, mod3, mod_row_of_batch, tables, w, with_q, tm):
    b, l, _ = x.shape
    cg, sg, cm, sm = tables
    const = lambda shape: pl.BlockSpec(shape, lambda bi, i: (0,) * len(shape), pipeline_mode=pl.Buffered(1))
    tab = pl.BlockSpec((tm, LANES), lambda bi, i: (i, 0))
    in_specs = [
        pl.BlockSpec((1, tm, D_MODEL), lambda bi, i: (bi, i, 0)),
        pl.BlockSpec((1, 6, D_MODEL), lambda bi, i: (mod_row_of_batch(bi), 0, 0)),
        const((1, D_MODEL)), tab, tab, tab, tab,
    ]
    if with_q:
        weights = [w["wq"], w["wk"], w["wv"], w["wqa"], w["wckv"], w["wkr"], w["wg"], w["wqb"], w["wkb"], w["wvb"],
                   w["gq"], w["gk"], w["gqa"], w["gkva"], w["gmq"], w["gmk"]]
    else:
        weights = [w["wk"], w["wv"], w["wckv"], w["wkr"], w["wkb"], w["wvb"], w["gk"], w["gkva"], w["gmk"]]
    in_specs += [const(a.shape) for a in weights]

    def out(width):
        return jax.ShapeDtypeStruct((b, l, width), BF16), pl.BlockSpec((1, tm, width), lambda bi, i: (bi, i, 0))

    def out_t(width):
        return jax.ShapeDtypeStruct((b, width, l), BF16), pl.BlockSpec((1, width, tm), lambda bi, i: (bi, 0, i))

    outs = [out(GQA_KV_W), out_t(GQA_KV_W), out(MLA_HEADS * MLA_HEAD_PAD), out_t(MLA_V_W)]
    if with_q:
        outs = [out(GQA_Q_W)] + outs[:2] + [out(MLA_HEADS * MLA_HEAD_PAD)] + outs[2:] + [out(2 * D_MODEL)]
    return pl.pallas_call(
        functools.partial(_proj_kernel, with_q=with_q),
        out_shape=[o[0] for o in outs],
        grid=(b, l // tm),
        in_specs=in_specs,
        out_specs=[o[1] for o in outs],
        compiler_params=pltpu.CompilerParams(dimension_semantics=("parallel", "parallel"),
                                             vmem_limit_bytes=56 * 1024 * 1024),
        name="proj_latent" if with_q else "proj_ctx",
    )(x, mod3, w["norm_mix"], cg, sg, cm, sm, *weights)


def _attn_kernel(q_ref, kl_ref, vl_ref, kc_ref, vc_ref, o_ref, *, tk, n_lat, bounded):
    q = q_ref[0]
    tq = q.shape[0]
    dv = vl_ref.shape[1]
    n = n_lat // tk

    if bounded:
        l = jnp.zeros((1, tq), F32)
        acc = jnp.zeros((dv, tq), F32)
        for k, vt in [(kl_ref[0, j * tk:(j + 1) * tk, :], vl_ref[0, :, j * tk:(j + 1) * tk]) for j in range(n)] + [
                (kc_ref[0], vc_ref[0])]:
            p = jnp.exp2(_dot_nt(k, q))
            l = l + jnp.sum(p, axis=0, keepdims=True)
            acc = acc + _dot(vt, p.astype(BF16))
        o_ref[0] = (acc / l).T.astype(BF16)
        return

    def update(carry, s, vt):
        m, l, acc = carry
        m_new = jnp.maximum(m, jnp.max(s, axis=0, keepdims=True))
        alpha = jnp.exp2(m - m_new)
        p = jnp.exp2(s - m_new)
        l = alpha * l + jnp.sum(p, axis=0, keepdims=True)
        return m_new, l, alpha * acc + _dot(vt, p.astype(BF16))

    n = n_lat // tk
    keys = [kl_ref[0, j * tk:(j + 1) * tk, :] for j in range(n)] + [kc_ref[0]]
    vals = [vl_ref[0, :, j * tk:(j + 1) * tk] for j in range(n)] + [vc_ref[0]]
    carry = (jnp.full((1, tq), NEG_INF, F32), jnp.zeros((1, tq), F32), jnp.zeros((dv, tq), F32))
    s = _dot_nt(keys[0], q)
    for j in range(n + 1):
        s_next = _dot_nt(keys[j + 1], q) if j < n else None
        carry = update(carry, s, vals[j])
        s = s_next
    m, l, acc = carry
    o_ref[0] = (acc / l).T.astype(BF16)


SOFTMAX_SAFE_EXPONENT = 56.0


def _score_bound(gain_q, gain_k, dim):
    return dim * jnp.max(jnp.abs(gain_q)) * jnp.max(jnp.abs(gain_k)) * (dim ** -0.5 * LOG2_E) * 1.02


def _attention(q, k_lat, v_lat, k_ctx, v_ctx, n_heads, group, d_qk, dv, tq, tk, name, score_bound):
    run = functools.partial(_attention_call, q, k_lat, v_lat, k_ctx, v_ctx, n_heads, group, d_qk, dv, tq, tk, name)
    return lax.cond(score_bound <= SOFTMAX_SAFE_EXPONENT, lambda: run(True), lambda: run(False))


def _attention_call(q, k_lat, v_lat, k_ctx, v_ctx, n_heads, group, d_qk, dv, tq, tk, name, bounded):
    b, l, _ = q.shape
    lc = k_ctx.shape[1]
    name = name + ("_bounded" if bounded else "_online")
    return pl.pallas_call(
        functools.partial(_attn_kernel, tk=tk, n_lat=l, bounded=bounded),
        out_shape=jax.ShapeDtypeStruct((b, l, n_heads * dv), BF16),
        grid=(b, n_heads, l // tq),
        in_specs=[
            pl.BlockSpec((1, tq, d_qk), lambda bi, h, i: (bi, i, h)),
            pl.BlockSpec((1, l, d_qk), lambda bi, h, i: (bi, 0, h // group)),
            pl.BlockSpec((1, dv, l), lambda bi, h, i: (bi, h // group, 0)),
            pl.BlockSpec((1, lc, d_qk), lambda bi, h, i: (bi, 0, h // group)),
            pl.BlockSpec((1, dv, lc), lambda bi, h, i: (bi, h // group, 0)),
        ],
        out_specs=pl.BlockSpec((1, tq, dv), lambda bi, h, i: (bi, i, h)),
        compiler_params=pltpu.CompilerParams(dimension_semantics=("parallel", "parallel", "parallel"),
                                             vmem_limit_bytes=56 * 1024 * 1024),
        name=name,
    )(q, k_lat, v_lat, k_ctx, v_ctx)


def _merge_kernel(oa_ref, ob_ref, gate_ref, x_ref, mod_ref, woa_ref, wob_ref, wout_ref, nffn_ref,
                  rwh_ref, rwl_ref, rb_ref, x1_ref, h2_ref, eidx_ref, egate_ref):
    mod = mod_ref[0]
    g1, shift2, scale2 = mod[2:3, :], mod[3:4, :], mod[4:5, :]
    ya = _dot(oa_ref[0], woa_ref[...])
    yb = _dot(ob_ref[0], wob_ref[...])
    g = gate_ref[0].astype(F32)
    y = g[:, :D_MODEL] * ya + g[:, D_MODEL:] * yb
    z = _dot(y.astype(BF16), wout_ref[...])
    x1 = x_ref[0] + g1 * z
    x1_ref[0] = x1
    h2 = _rms(x1, nffn_ref[...], D_MODEL) * (1.0 + scale2) + shift2
    h2_ref[0] = h2.astype(BF16)

    h_hi, h_lo = _split_bf16(h2)
    logits = _dot(h_hi, rwh_ref[...]) + (_dot(h_hi, rwl_ref[...]) + _dot(h_lo, rwh_ref[...])) + rb_ref[...]
    lane = lax.broadcasted_iota(jnp.int32, logits.shape, 1).astype(F32)
    cur = jnp.where(lane < N_EXPERTS, logits, NEG_INF)
    vals, idxs = [], []
    for _ in range(TOP_K):
        mx = jnp.max(cur, axis=-1, keepdims=True)
        ix = jnp.min(jnp.where(cur == mx, lane, float(LANES)), axis=-1, keepdims=True)
        vals.append(mx)
        idxs.append(ix)
        cur = jnp.where(lane == ix, NEG_INF, cur)
    ex = [jnp.exp(v - vals[0]) for v in vals]
    den = ex[0] + ex[1] + ex[2] + ex[3]
    eidx = jnp.zeros(logits.shape, F32)
    egate = jnp.zeros(logits.shape, F32)
    for k in range(TOP_K):
        eidx = jnp.where(lane == k, idxs[k], eidx)
        egate = jnp.where(lane == k, ex[k] / den, egate)
    eidx_ref[0] = eidx.astype(jnp.int32)
    egate_ref[0] = egate


def _merge(o_a, o_b, gates, x, mod3, w, tm):
    b, l, _ = x.shape
    const = lambda shape: pl.BlockSpec(shape, lambda bi, i: (0,) * len(shape), pipeline_mode=pl.Buffered(1))
    tok = lambda width: pl.BlockSpec((1, tm, width), lambda bi, i: (bi, i, 0))
    return pl.pallas_call(
        _merge_kernel,
        out_shape=[jax.ShapeDtypeStruct((b, l, D_MODEL), F32), jax.ShapeDtypeStruct((b, l, D_MODEL), BF16),
                   jax.ShapeDtypeStruct((b, l, LANES), jnp.int32), jax.ShapeDtypeStruct((b, l, LANES), F32)],
        grid=(b, l // tm),
        in_specs=[tok(GQA_Q_W), tok(MLA_V_W), tok(2 * D_MODEL), tok(D_MODEL),
                  pl.BlockSpec((1, 6, D_MODEL), lambda bi, i: (bi, 0, 0)),
                  const((GQA_Q_W, D_MODEL)), const((MLA_V_W, D_MODEL)), const((D_MODEL, D_MODEL)), const((1, D_MODEL)),
                  const((D_MODEL, LANES)), const((D_MODEL, LANES)), const((1, LANES))],
        out_specs=[tok(D_MODEL), tok(D_MODEL), tok(LANES), tok(LANES)],
        compiler_params=pltpu.CompilerParams(dimension_semantics=("parallel", "parallel"),
                                             vmem_limit_bytes=56 * 1024 * 1024),
        name="merge_router",
    )(o_a, o_b, gates, x, mod3, w["woa"], w["wob"], w["wout"], w["norm_ffn"], w["rw_hi"], w["rw_lo"], w["rb"])


def _onehots(idx, lane):
    return [lane == idx[:, k:k + 1] for k in range(TOP_K)]


SEG_ALIGN = 16


def _route_kernel(eidx_ref, rank_ref, before_ref, tcnt_ref, cnt_ref, carry_ref):
    @pl.when(pl.program_id(0) == 0)
    def _():
        carry_ref[...] = jnp.zeros_like(carry_ref)

    before_ref[0] = carry_ref[...]
    idx = eidx_ref[...]
    tm = idx.shape[0]
    lane = lax.broadcasted_iota(jnp.int32, idx.shape, 1)
    oh = _onehots(idx, lane)
    total = jnp.zeros(idx.shape, F32)
    for k in range(TOP_K):
        total = total + jnp.where(oh[k], 1.0, 0.0)
    row = lax.broadcasted_iota(jnp.int32, (tm, tm), 0)
    col = lax.broadcasted_iota(jnp.int32, (tm, tm), 1)
    tri = jnp.where(row > col, 1.0, 0.0).astype(BF16)
    before = _dot(tri, total.astype(BF16)) + carry_ref[0:1, :]
    rank = jnp.where(lane < TOP_K, jnp.take_along_axis(before, idx, axis=1), 0.0)
    rank_ref[...] = rank.astype(jnp.int32)
    tile_cnt = jnp.sum(total, axis=0, keepdims=True)
    tcnt_ref[0] = jnp.broadcast_to(tile_cnt, carry_ref.shape)
    carry_ref[...] = carry_ref[...] + tile_cnt
    cnt_ref[...] = carry_ref[...]


def _route(eidx, tm):
    t = eidx.shape[0]
    return pl.pallas_call(
        _route_kernel,
        out_shape=[jax.ShapeDtypeStruct((t, LANES), jnp.int32), jax.ShapeDtypeStruct((t // tm, 8, LANES), F32),
                   jax.ShapeDtypeStruct((t // tm, 8, LANES), F32), jax.ShapeDtypeStruct((8, LANES), F32)],
        grid=(t // tm,),
        in_specs=[pl.BlockSpec((tm, LANES), lambda i: (i, 0))],
        out_specs=[pl.BlockSpec((tm, LANES), lambda i: (i, 0)), pl.BlockSpec((1, 8, LANES), lambda i: (i, 0, 0)),
                   pl.BlockSpec((1, 8, LANES), lambda i: (i, 0, 0)), pl.BlockSpec((8, LANES), lambda i: (0, 0))],
        scratch_shapes=[pltpu.VMEM((8, LANES), F32)],
        compiler_params=pltpu.CompilerParams(dimension_semantics=("arbitrary",)),
        name="route_rank",
    )(eidx)


def _dest_kernel(eidx_ref, rank_ref, start_ref, before_ref, dest_ref, col_ref, dest_t_ref, col_t_ref):
    idx = eidx_ref[...]
    valid = lax.broadcasted_iota(jnp.int32, idx.shape, 1) < TOP_K
    region_start = jnp.take_along_axis(jnp.broadcast_to(start_ref[0:1, :], idx.shape), idx, axis=1)
    seg_start = jnp.take_along_axis(jnp.broadcast_to(before_ref[0][0:1, :], idx.shape), idx, axis=1)
    rank = rank_ref[...].astype(F32)
    window_start = jnp.floor(seg_start * (1.0 / SEG_ALIGN)) * SEG_ALIGN
    local = rank - window_start
    dest = jnp.where(valid, region_start + rank, 0.0)
    col = jnp.where(valid & (local < WIN), idx.astype(F32) * WIN + local, -1.0)
    dest_ref[...] = dest.astype(jnp.int32)
    col_ref[...] = col.astype(jnp.int32)
    dest_t_ref[0] = dest.T[0:8, :].astype(jnp.int32)
    col_t_ref[0] = col.T[0:8, :].astype(jnp.int32)


def _dest(eidx, rank, start, before, tm):
    t = eidx.shape[0]
    blk = pl.BlockSpec((tm, LANES), lambda i: (i, 0))
    blk_t = pl.BlockSpec((1, 8, tm), lambda i: (i, 0, 0))
    tok = jax.ShapeDtypeStruct((t, LANES), jnp.int32)
    tok_t = jax.ShapeDtypeStruct((t // tm, 8, tm), jnp.int32)
    return pl.pallas_call(
        _dest_kernel,
        out_shape=[tok, tok, tok_t, tok_t],
        grid=(t // tm,),
        in_specs=[blk, blk, pl.BlockSpec((8, LANES), lambda i: (0, 0)),
                  pl.BlockSpec((1, 8, LANES), lambda i: (i, 0, 0))],
        out_specs=[blk, blk, blk_t, blk_t],
        compiler_params=pltpu.CompilerParams(dimension_semantics=("parallel",)),
        name="route_dest",
    )(eidx, rank, start, before)


MOE_TILE = 256
WIN = 64
N_MAIN_ROWS = N_EXPERTS * WIN
MAIN_CHUNK = 512
WIN_PER_STACK = MOE_TILE // WIN
MAX_WINDOWS = (MOE_TILE * TOP_K + N_EXPERTS * SEG_ALIGN) // WIN
CARRY_ROWS = N_EXPERTS * SEG_ALIGN
WORKLIST_LEN = -(-(MAX_WINDOWS + WIN_PER_STACK) // WIN_PER_STACK) * WIN_PER_STACK


def _build_worklist(ts_ref, tc_ref, wl_ref, junk_row):
    base = pl.program_id(0) * N_EXPERTS

    def per_expert(e, n):
        first = ts_ref[base + e]
        n_win = (tc_ref[base + e] + (WIN - 1)) // WIN

        def per_window(wi, n):
            wl_ref[n] = first + wi * WIN
            return n + 1

        return lax.fori_loop(1, n_win, per_window, n)

    n = lax.fori_loop(0, N_EXPERTS, per_expert, 0)
    n_stacks = (n + (WIN_PER_STACK - 1)) // WIN_PER_STACK

    def pad(j, c):
        wl_ref[j] = junk_row + (j % WIN_PER_STACK) * WIN
        return c

    lax.fori_loop(n, n_stacks * WIN_PER_STACK, pad, 0)
    return n_stacks


def _seg_aligned(row):
    return row if isinstance(row, int) else pl.multiple_of(row, SEG_ALIGN)


def _stack_row_ids(wl_ref, stack, shape, axis):
    pos = lax.broadcasted_iota(jnp.int32, shape, axis)
    row = jnp.full(shape, -1, jnp.int32)
    for wi in range(WIN_PER_STACK - 1, -1, -1):
        row = jnp.where(pos < (wi + 1) * WIN, wl_ref[stack * WIN_PER_STACK + wi] + (pos - wi * WIN), row)
    return row


def _select_rows(row_id, id_t):
    shape = (row_id.shape[0], id_t.shape[1])
    sel = jnp.zeros(shape, F32)
    for k in range(TOP_K):
        hit = row_id == id_t[k:k + 1, :]
        sel = jnp.where(hit, 1.0, sel)
    return sel.astype(BF16)


MAIN_PENDING = 2


TAIL_ROWS = WIN + EXPERT_BLOCK


def _dispatch_kernel(ts_ref, tc_ref, ov_ref, re_ref, h_ref, col_t_ref, dest_t_ref,
                     buf_ref, wl_ref, pend_ref, main_x, carry_x, stage_x, main_sem, sem, *, junk_row):
    i = pl.program_id(0)
    base = i * N_EXPERTS
    next_base = jnp.minimum(i + 1, pl.num_programs(0) - 1) * N_EXPERTS

    @pl.when(i == 0)
    def _():
        pend_ref[0] = 0
        pend_ref[1] = 0
        pend_ref[MAIN_PENDING] = 0
        carry_x[...] = jnp.zeros_like(carry_x)
        main_x[0:TAIL_ROWS, :] = jnp.zeros((TAIL_ROWS, D_MODEL), BF16)
        cps = []
        for e in range(N_EXPERTS):
            rows = pl.ds(pl.multiple_of(jnp.maximum(re_ref[e] - TAIL_ROWS, 0), SEG_ALIGN), TAIL_ROWS)
            cps.append(pltpu.make_async_copy(main_x.at[pl.ds(0, TAIL_ROWS)], buf_ref.at[rows], main_sem))
        for cp in cps:
            cp.start()
        for cp in cps:
            cp.wait()

        def zero_block_copies(blk):
            rows = pl.ds(pl.multiple_of(blk * EXPERT_BLOCK, EXPERT_BLOCK), EXPERT_BLOCK)
            return (pltpu.make_async_copy(main_x.at[pl.ds(0, EXPERT_BLOCK)], buf_ref.at[rows], main_sem),)

        def start_zero(blk, c):
            for cp in zero_block_copies(blk):
                cp.start()
            return c

        def wait_zero(blk, c):
            for cp in zero_block_copies(blk):
                cp.wait()
            return c

        first_free, n_blocks = re_ref[N_EXPERTS - 1] // EXPERT_BLOCK, (junk_row + EXPERT_BLOCK) // EXPERT_BLOCK
        lax.fori_loop(first_free, n_blocks, start_zero, 0)
        lax.fori_loop(first_free, n_blocks, wait_zero, 0)

    h = h_ref[...]

    def main_copies(first_row):
        cps = []
        for e in range(N_EXPERTS):
            rows = pl.ds(_seg_aligned(first_row(e)), WIN)
            cps.append(pltpu.make_async_copy(main_x.at[pl.ds(e * WIN, WIN)], buf_ref.at[rows], main_sem))
        return cps

    def drain_main():
        @pl.when(pend_ref[MAIN_PENDING] == 1)
        def _():
            for cp in main_copies(lambda e: 0):
                cp.wait()
            pend_ref[MAIN_PENDING] = 0

    col_t = col_t_ref[0]
    dest_t = dest_t_ref[0]
    pieces = []
    for c in range(N_MAIN_ROWS // MAIN_CHUNK):
        row_id = lax.broadcasted_iota(jnp.int32, (MAIN_CHUNK, 1), 0) + c * MAIN_CHUNK
        pieces.append(_dot(_select_rows(row_id, col_t), h).astype(BF16))
    group_pos = lax.broadcasted_iota(jnp.int32, (SEG_ALIGN, 1), 0)
    end_group = jnp.concatenate([ts_ref[next_base + e] + group_pos for e in range(N_EXPERTS)], axis=0)
    end_rows = _dot(_select_rows(end_group, dest_t), h).astype(BF16)

    def window_copies(slot, first_row):
        cps = []
        for wi in range(WIN_PER_STACK):
            rows = pl.ds(_seg_aligned(first_row(wi)), WIN)
            cps.append(pltpu.make_async_copy(stage_x.at[slot, pl.ds(wi * WIN, WIN)], buf_ref.at[rows], sem.at[slot]))
        return cps

    def drain(slot):
        @pl.when(pend_ref[slot] == 1)
        def _():
            for cp in window_copies(slot, lambda wi: 0):
                cp.wait()
            pend_ref[slot] = 0

    drain_main()
    drain(0)
    drain(1)
    for c, packed in enumerate(pieces):
        main_x[c * MAIN_CHUNK:(c + 1) * MAIN_CHUNK, :] = packed
    for e in range(N_EXPERTS):
        win, grp = pl.ds(e * WIN, SEG_ALIGN), pl.ds(e * SEG_ALIGN, SEG_ALIGN)
        main_x[win, :] = main_x[win, :] + carry_x[grp, :]
        same = ts_ref[next_base + e] == ts_ref[base + e]
        carry_x[grp, :] = end_rows[e * SEG_ALIGN:(e + 1) * SEG_ALIGN, :] + jnp.where(same, carry_x[grp, :], 0).astype(BF16)
    for n, cp in enumerate(main_copies(lambda e: ts_ref[base + e])):
        cp.start(priority=n % 2)
    pend_ref[MAIN_PENDING] = 1

    @pl.when(ov_ref[i] > 0)
    def _():
        n_stacks = _build_worklist(ts_ref, tc_ref, wl_ref, junk_row)

        def stack_body(s, c):
            slot = s & 1
            rows = _dot(_select_rows(_stack_row_ids(wl_ref, s, (MOE_TILE, 1), 0), dest_t), h)
            drain(slot)
            stage_x[slot] = rows.astype(BF16)
            for cp in window_copies(slot, lambda wi: wl_ref[s * WIN_PER_STACK + wi]):
                cp.start()
            pend_ref[slot] = 1
            return c

        lax.fori_loop(0, n_stacks, stack_body, 0)

    @pl.when(i == pl.num_programs(0) - 1)
    def _():
        drain_main()
        drain(0)
        drain(1)


def _moe_grid_spec(n_tiles, in_specs, out_specs, scratch_shapes):
    return pltpu.PrefetchScalarGridSpec(num_scalar_prefetch=4, grid=(n_tiles,), in_specs=in_specs,
                                        out_specs=out_specs, scratch_shapes=scratch_shapes)


def _dispatch(tile_start, tile_cnt, tile_ovf, region_ends, h2, col_t, dest_t, n_rows):
    n_tiles = h2.shape[0] // MOE_TILE
    junk_row = n_rows - EXPERT_BLOCK
    any_spec = pl.BlockSpec(memory_space=pl.ANY)
    tok_t = pl.BlockSpec((1, 8, MOE_TILE), lambda i, *_: (i, 0, 0))
    return pl.pallas_call(
        functools.partial(_dispatch_kernel, junk_row=junk_row),
        out_shape=jax.ShapeDtypeStruct((n_rows, D_MODEL), BF16),
        grid_spec=_moe_grid_spec(
            n_tiles,
            [pl.BlockSpec((MOE_TILE, D_MODEL), lambda i, *_: (i, 0)), tok_t, tok_t],
            any_spec,
            [pltpu.SMEM((WORKLIST_LEN,), jnp.int32), pltpu.SMEM((3,), jnp.int32),
             pltpu.VMEM((N_MAIN_ROWS, D_MODEL), BF16), pltpu.VMEM((CARRY_ROWS, D_MODEL), BF16),
             pltpu.VMEM((2, MOE_TILE, D_MODEL), BF16),
             pltpu.SemaphoreType.DMA(()), pltpu.SemaphoreType.DMA((2,))]),
        compiler_params=pltpu.CompilerParams(dimension_semantics=("arbitrary",),
                                             vmem_limit_bytes=56 * 1024 * 1024),
        name="moe_dispatch",
    )(tile_start, tile_cnt, tile_ovf, region_ends, h2, col_t, dest_t)


def _select_cols(ids, gates, col_id):
    sel = jnp.zeros((ids.shape[0], col_id.shape[1]), F32)
    for k in range(TOP_K):
        sel = jnp.where(ids[:, k:k + 1] == col_id, gates[:, k:k + 1], sel)
    return sel.astype(BF16)


def _combine_kernel(ts_ref, tc_ref, ov_ref, re_ref, col_ref, dest_ref, egate_ref, x1_ref, mod_ref, y_ref, o_ref,
                    wl_ref, main_y, stage_y, acc_ref, main_sem, sem, *, junk_row):
    del re_ref
    i = pl.program_id(0)
    slot_i = i & 1

    def main_copies(slot, first_row):
        return [pltpu.make_async_copy(y_ref.at[pl.ds(_seg_aligned(first_row(e)), WIN)],
                                      main_y.at[slot, pl.ds(e * WIN, WIN)], main_sem.at[slot])
                for e in range(N_EXPERTS)]

    def fetch_main(tile, slot):
        for n, cp in enumerate(main_copies(slot, lambda e: ts_ref[tile * N_EXPERTS + e])):
            cp.start(priority=n % 2)

    @pl.when(i == 0)
    def _():
        fetch_main(0, 0)

    @pl.when(i + 1 < pl.num_programs(0))
    def _():
        fetch_main(i + 1, 1 - slot_i)

    for cp in main_copies(slot_i, lambda e: 0):
        cp.wait()

    col = col_ref[...]
    egate = egate_ref[...]
    sel = jnp.concatenate(
        [_select_cols(col, egate, lax.broadcasted_iota(jnp.int32, (1, MAIN_CHUNK), 1) + c * MAIN_CHUNK)
         for c in range(N_MAIN_ROWS // MAIN_CHUNK)], axis=1)
    acc_ref[...] = _dot(sel, main_y[slot_i])

    @pl.when(ov_ref[i] > 0)
    def _():
        n_stacks = _build_worklist(ts_ref, tc_ref, wl_ref, junk_row)
        dest = dest_ref[...]
        if WIN_PER_STACK * WIN < MOE_TILE:
            for slot in range(2):
                stage_y[slot, WIN_PER_STACK * WIN:, :] = jnp.zeros((MOE_TILE - WIN_PER_STACK * WIN, D_MODEL), BF16)

        def window_copies(slot, first_row):
            return [pltpu.make_async_copy(y_ref.at[pl.ds(_seg_aligned(first_row(wi)), WIN)],
                                          stage_y.at[slot, pl.ds(wi * WIN, WIN)], sem.at[slot])
                    for wi in range(WIN_PER_STACK)]

        def fetch(s, slot):
            for cp in window_copies(slot, lambda wi: wl_ref[s * WIN_PER_STACK + wi]):
                cp.start()

        @pl.when(n_stacks > 0)
        def _():
            fetch(0, 0)

        def stack_body(s, c):
            slot = s & 1

            @pl.when(s + 1 < n_stacks)
            def _():
                fetch(s + 1, 1 - slot)

            for cp in window_copies(slot, lambda wi: 0):
                cp.wait()
            sel_o = _select_cols(dest, egate, _stack_row_ids(wl_ref, s, (1, MOE_TILE), 1))
            acc_ref[...] += _dot(sel_o, stage_y[slot])
            return c

        lax.fori_loop(0, n_stacks, stack_body, 0)

    g2 = mod_ref[0][5:6, :]
    o_ref[...] = x1_ref[...] + g2 * acc_ref[...]


def _combine(tile_start, tile_cnt, tile_ovf, region_ends, col, dest, egate, x1, mod3, y, l):
    t = x1.shape[0]
    per_batch = l // MOE_TILE
    junk_row = y.shape[0] - EXPERT_BLOCK
    tok = lambda width: pl.BlockSpec((MOE_TILE, width), lambda i, *_: (i, 0))
    return pl.pallas_call(
        functools.partial(_combine_kernel, junk_row=junk_row),
        out_shape=jax.ShapeDtypeStruct((t, D_MODEL), F32),
        grid_spec=_moe_grid_spec(
            t // MOE_TILE,
            [tok(LANES), tok(LANES), tok(LANES), tok(D_MODEL),
             pl.BlockSpec((1, 6, D_MODEL), lambda i, *_: (i // per_batch, 0, 0)),
             pl.BlockSpec(memory_space=pl.ANY)],
            tok(D_MODEL),
            [pltpu.SMEM((WORKLIST_LEN,), jnp.int32), pltpu.VMEM((2, N_MAIN_ROWS, D_MODEL), BF16),
             pltpu.VMEM((2, MOE_TILE, D_MODEL), BF16), pltpu.VMEM((MOE_TILE, D_MODEL), F32),
             pltpu.SemaphoreType.DMA((2,)), pltpu.SemaphoreType.DMA((2,))]),
        compiler_params=pltpu.CompilerParams(dimension_semantics=("arbitrary",),
                                             vmem_limit_bytes=56 * 1024 * 1024),
        name="moe_combine",
    )(tile_start, tile_cnt, tile_ovf, region_ends, col, dest, egate, x1, mod3, y)


def _expert_kernel(be_ref, nused_ref, x_ref, w1_ref, b1_ref, w2_ref, b2_ref, y_ref, w1b_ref, w2b_ref):
    i = pl.program_id(0)
    live = i < nused_ref[0]

    @pl.when(live & ((i == 0) | (be_ref[i] != be_ref[jnp.maximum(i - 1, 0)])))
    def _():
        w1b_ref[...] = w1_ref[0].astype(BF16)
        w2b_ref[...] = w2_ref[0].astype(BF16)

    @pl.when(live)
    def _():
        gu = _dot(x_ref[...], w1b_ref[...]) + b1_ref[0]
        glu = jnp.minimum(gu[:, :D_EXPERT], SWIGLU_LIMIT)
        lin = jnp.clip(gu[:, D_EXPERT:], -SWIGLU_LIMIT, SWIGLU_LIMIT)
        act = glu * (1.0 / (1.0 + jnp.exp(-SWIGLU_ALPHA * glu))) * (lin + 1.0)
        y_ref[...] = (_dot(act.astype(BF16), w2b_ref[...]) + b2_ref[0]).astype(BF16)

    @pl.when(pl.program_id(0) >= nused_ref[0])
    def _():
        y_ref[...] = jnp.zeros_like(y_ref)


def _experts(block_e, nused, buf, w1, b1, w2, b2):
    nb = buf.shape[0] // EXPERT_BLOCK
    row = lambda i, be, nu: (jnp.minimum(i, nu[0] - 1), 0)
    out_row = lambda i, be, nu: (i, 0)
    exp3 = lambda i, be, nu: (be[jnp.minimum(i, nu[0] - 1)], 0, 0)
    return pl.pallas_call(
        _expert_kernel,
        out_shape=jax.ShapeDtypeStruct((buf.shape[0], D_MODEL), BF16),
        grid_spec=pltpu.PrefetchScalarGridSpec(
            num_scalar_prefetch=2,
            grid=(nb,),
            in_specs=[pl.BlockSpec((EXPERT_BLOCK, D_MODEL), row),
                      pl.BlockSpec((1, D_MODEL, 2 * D_EXPERT), exp3),
                      pl.BlockSpec((1, 1, 2 * D_EXPERT), exp3),
                      pl.BlockSpec((1, D_EXPERT, D_MODEL), exp3),
                      pl.BlockSpec((1, 1, D_MODEL), exp3)],
            out_specs=pl.BlockSpec((EXPERT_BLOCK, D_MODEL), out_row),
            scratch_shapes=[pltpu.VMEM((D_MODEL, 2 * D_EXPERT), BF16), pltpu.VMEM((D_EXPERT, D_MODEL), BF16)]),
        compiler_params=pltpu.CompilerParams(dimension_semantics=("arbitrary",),
                                             vmem_limit_bytes=56 * 1024 * 1024),
        name="moe_experts",
    )(block_e, nused, buf, w1, b1, w2, b2)


def _rope_tables(seq_len):
    pos = jnp.arange(seq_len, dtype=jnp.int32)
    row = (pos // GRID_W).astype(F32)[:, None]
    col = (pos % GRID_W).astype(F32)[:, None]

    def table(d_axis, pad):
        inv_freq = ROPE_THETA ** (-jnp.arange(0, d_axis, 2, dtype=F32) / d_axis)
        ar, ac = row * inv_freq[None, :], col * inv_freq[None, :]
        cos = jnp.concatenate([jnp.cos(ar), jnp.cos(ar), jnp.cos(ac), jnp.cos(ac)], axis=-1)
        sin = jnp.concatenate([-jnp.sin(ar), jnp.sin(ar), -jnp.sin(ac), jnp.sin(ac)], axis=-1)
        if pad:
            cos = jnp.concatenate([cos, jnp.ones((seq_len, pad), F32)], axis=-1)
            sin = jnp.concatenate([sin, jnp.zeros((seq_len, pad), F32)], axis=-1)
        return cos, sin

    cg, sg = table(GQA_HEAD_DIM // 2, 0)
    cm, sm = table(MLA_ROPE_DIM // 2, LANES - MLA_ROPE_DIM)
    return cg, sg, cm, sm


def _pad_lanes(a, width):
    return jnp.pad(a, [(0, 0)] * (a.ndim - 1) + [(0, width - a.shape[-1])])


def _prep_weights(p):
    w_in = p["w_in"]
    w = {"norm_mix": p["norm_mix"].reshape(1, D_MODEL), "norm_ffn": p["norm_ffn"].reshape(1, D_MODEL)}
    w["wq"] = w_in[:, :OFF_GQA_K].astype(BF16)
    w["wk"] = w_in[:, OFF_GQA_K:OFF_GQA_V].astype(BF16)
    w["wv"] = w_in[:, OFF_GQA_V:OFF_MLA_QA].T.astype(BF16)
    w["wqa"] = w_in[:, OFF_MLA_QA:OFF_MLA_KVA].astype(BF16)
    w["wckv"] = w_in[:, OFF_MLA_KVA:OFF_MLA_KVA + MLA_KV_RANK].astype(BF16)
    w["wkr"] = _pad_lanes(w_in[:, OFF_MLA_KVA + MLA_KV_RANK:OFF_GATE], LANES).astype(BF16)
    w["wg"] = w_in[:, OFF_GATE:].astype(BF16)
    wqb = p["mla_w_qb"].reshape(MLA_Q_RANK, MLA_HEADS, MLA_QK_DIM)
    w["wqb"] = _pad_lanes(wqb, MLA_HEAD_PAD).reshape(MLA_Q_RANK, MLA_HEADS * MLA_HEAD_PAD).astype(BF16)
    wkvb = p["mla_w_kvb"].reshape(MLA_KV_RANK, MLA_HEADS, MLA_NOPE_DIM + MLA_V_DIM)
    w["wkb"] = wkvb[:, :, :MLA_NOPE_DIM].reshape(MLA_KV_RANK, MLA_HEADS * MLA_NOPE_DIM).astype(BF16)
    w["wvb"] = wkvb[:, :, MLA_NOPE_DIM:].reshape(MLA_KV_RANK, MLA_V_W).T.astype(BF16)
    w["gq"] = p["gqa_q_norm"].reshape(1, GQA_HEAD_DIM)
    w["gk"] = p["gqa_k_norm"].reshape(1, GQA_HEAD_DIM)
    w["gqa"] = p["mla_q_a_norm"].reshape(1, MLA_Q_RANK)
    w["gkva"] = p["mla_kv_a_norm"].reshape(1, MLA_KV_RANK)
    w["gmq"] = _pad_lanes(p["mla_q_norm"].reshape(1, MLA_QK_DIM), MLA_HEAD_PAD)
    w["gmk"] = _pad_lanes(p["mla_k_norm"].reshape(1, MLA_QK_DIM), MLA_HEAD_PAD)
    w["woa"] = p["w_o_gqa"].astype(BF16)
    w["wob"] = p["w_o_mla"].astype(BF16)
    w["wout"] = p["w_out"].astype(BF16)
    rw = _pad_lanes(p["router_w"], LANES)
    w["rw_hi"] = rw.astype(BF16)
    w["rw_lo"] = (rw - w["rw_hi"].astype(F32)).astype(BF16)
    w["rb"] = _pad_lanes(p["router_b"].reshape(1, N_EXPERTS), LANES)
    return w


def _layer(x, c, ctx, c_ctx, p):
    b, l, _ = x.shape
    lc = ctx.shape[1]
    t = b * l
    w = _prep_weights(p)

    n_mod_rows = -(-(b + 1) // 8) * 8
    cc = jnp.zeros((n_mod_rows, D_MODEL), F32).at[:b].set(c).at[b].set(c_ctx)
    mod3 = _ada_mod(cc, p["ada_w"], p["ada_b"]).reshape(n_mod_rows, 6, D_MODEL)

    tables = _rope_tables(l)
    ident = (jnp.ones((lc, LANES), F32), jnp.zeros((lc, LANES), F32)) * 2
    tm = min(512, l)
    q_a, k_a, v_a, q_m, k_m, v_m, gates = _proj(x, mod3, lambda bi: bi, tables, w, True, tm)
    kc_a, vc_a, kc_m, vc_m = _proj(ctx, mod3, lambda bi: b, ident, w, False, min(256, lc))

    tq, tk = min(2048, l), min(2048, l)
    o_a = _attention(q_a, k_a, v_a, kc_a, vc_a, GQA_HEADS, GQA_GROUP, GQA_HEAD_DIM, GQA_HEAD_DIM, tq, tk, "attn_gqa",
                     _score_bound(p["gqa_q_norm"], p["gqa_k_norm"], GQA_HEAD_DIM))
    o_m = _attention(q_m, k_m, v_m, kc_m, vc_m, MLA_HEADS, 1, MLA_HEAD_PAD, MLA_V_DIM, tq, tk, "attn_mla",
                     _score_bound(p["mla_q_norm"], p["mla_k_norm"], MLA_QK_DIM))

    x1, h2, eidx, egate = _merge(o_a, o_m, gates, x, mod3, w, tm)
    x1, h2 = x1.reshape(t, D_MODEL), h2.reshape(t, D_MODEL)
    eidx, egate = eidx.reshape(t, LANES), egate.reshape(t, LANES)

    assert t % MOE_TILE == 0
    n_tiles = t // MOE_TILE
    rank, before_raw, tcnt, cnt = _route(eidx, MOE_TILE)
    counts = cnt[0, :N_EXPERTS].astype(jnp.int32)
    before = before_raw[:, 0, :N_EXPERTS].astype(jnp.int32)
    region = (counts + WIN + EXPERT_BLOCK - 1) // EXPERT_BLOCK * EXPERT_BLOCK
    region_ends = jnp.cumsum(region)
    region_starts = region_ends - region
    max_rows = t * TOP_K + N_EXPERTS * (WIN + EXPERT_BLOCK - 1)
    n_blocks = max_rows // EXPERT_BLOCK + 1
    n_rows = n_blocks * EXPERT_BLOCK
    block_row = jnp.arange(n_blocks, dtype=jnp.int32) * EXPERT_BLOCK
    block_e = jnp.minimum(jnp.sum(region_ends[None, :] <= block_row[:, None], axis=1), N_EXPERTS - 1).astype(jnp.int32)
    nused = (region_ends[-1:] // EXPERT_BLOCK).astype(jnp.int32)
    tile_start = (region_starts[None, :] + before // SEG_ALIGN * SEG_ALIGN).reshape(-1)
    tile_cnt = before % SEG_ALIGN + tcnt[:, 0, :N_EXPERTS].astype(jnp.int32)
    tile_ovf = jnp.sum(jnp.maximum((tile_cnt + WIN - 1) // WIN - 1, 0), axis=1).astype(jnp.int32)
    tile_cnt = tile_cnt.reshape(-1)
    start = jnp.zeros((8, LANES), F32).at[0, :N_EXPERTS].set(region_starts.astype(F32))
    dest, col, dest_t, col_t = _dest(eidx, rank, start, before_raw, MOE_TILE)

    region_ends = region_ends.astype(jnp.int32)
    buf = _dispatch(tile_start, tile_cnt, tile_ovf, region_ends, h2, col_t, dest_t, n_rows)
    y = _experts(block_e, nused, buf, p["expert_w1"], p["expert_b1"].reshape(N_EXPERTS, 1, -1),
                 p["expert_w2"], p["expert_b2"].reshape(N_EXPERTS, 1, -1))
    out = _combine(tile_start, tile_cnt, tile_ovf, region_ends, col, dest, egate, x1, mod3, y, l)
    return out.reshape(b, l, D_MODEL)


def kernel(x, c, ctx, c_ctx, ada_w, ada_b, norm_mix, norm_ffn, w_in, gqa_q_norm, gqa_k_norm, mla_q_a_norm, mla_kv_a_norm, mla_w_qb, mla_w_kvb, mla_q_norm, mla_k_norm, w_o_gqa, w_o_mla, w_out, router_w, router_b, expert_w1, expert_b1, expert_w2, expert_b2):
    assert ada_w.shape[0] == 1, "single-layer problem: the context stream is never updated"
    p = {
        "ada_w": ada_w[0], "ada_b": ada_b[0], "norm_mix": norm_mix[0], "norm_ffn": norm_ffn[0],
        "w_in": w_in[0], "gqa_q_norm": gqa_q_norm[0], "gqa_k_norm": gqa_k_norm[0],
        "mla_q_a_norm": mla_q_a_norm[0], "mla_kv_a_norm": mla_kv_a_norm[0],
        "mla_w_qb": mla_w_qb[0], "mla_w_kvb": mla_w_kvb[0],
        "mla_q_norm": mla_q_norm[0], "mla_k_norm": mla_k_norm[0],
        "w_o_gqa": w_o_gqa[0], "w_o_mla": w_o_mla[0], "w_out": w_out[0],
        "router_w": router_w[0], "router_b": router_b[0],
        "expert_w1": expert_w1[0], "expert_b1": expert_b1[0],
        "expert_w2": expert_w2[0], "expert_b2": expert_b2[0],
    }
    return _layer(x, c, ctx, c_ctx, p)
```

```python
import functools
import math

import jax
import jax.numpy as jnp
from jax import lax
from jax.experimental import pallas as pl
from jax.experimental.pallas import tpu as pltpu

D_MODEL = 1024
GRID_W = 64
EPS = 1e-6
ROPE_THETA = 10000.0

GQA_HEADS = 8
GQA_KV_HEADS = 2
GQA_GROUP = GQA_HEADS // GQA_KV_HEADS
GQA_HEAD_DIM = 128
GQA_Q_W = GQA_HEADS * GQA_HEAD_DIM
GQA_KV_W = GQA_KV_HEADS * GQA_HEAD_DIM

MLA_HEADS = 8
MLA_Q_RANK = 256
MLA_KV_RANK = 128
MLA_NOPE_DIM = 128
MLA_ROPE_DIM = 64
MLA_V_DIM = 128
MLA_QK_DIM = MLA_NOPE_DIM + MLA_ROPE_DIM
MLA_HEAD_PAD = 256
MLA_V_W = MLA_HEADS * MLA_V_DIM

OFF_GQA_K = GQA_Q_W
OFF_GQA_V = OFF_GQA_K + GQA_KV_W
OFF_MLA_QA = OFF_GQA_V + GQA_KV_W
OFF_MLA_KVA = OFF_MLA_QA + MLA_Q_RANK
OFF_GATE = OFF_MLA_KVA + MLA_KV_RANK + MLA_ROPE_DIM

N_EXPERTS = 32
TOP_K = 4
D_EXPERT = 1024
SWIGLU_LIMIT = 7.0
SWIGLU_ALPHA = 1.702
EXPERT_BLOCK = 512

LANES = 128
NEG_INF = float("-inf")
LOG2_E = math.log2(math.e)

BF16 = jnp.bfloat16
F32 = jnp.float32


def _dot(a, b):
    return jnp.dot(a, b, preferred_element_type=F32)


def _dot_nt(a, b):
    return lax.dot_general(a, b, (((1,), (1,)), ((), ())), preferred_element_type=F32)


def _split_bf16(a):
    hi = a.astype(BF16)
    lo = (a - hi.astype(F32)).astype(BF16)
    return hi, lo


def _rms(x, g, n):
    ms = jnp.sum(x * x, axis=-1, keepdims=True) * (1.0 / n)
    return x * lax.rsqrt(ms + EPS) * g


def _swap_halves(x, k):
    lane = lax.broadcasted_iota(jnp.int32, x.shape, 1)
    return jnp.where((lane & k) != 0, pltpu.roll(x, k, 1), pltpu.roll(x, LANES - k, 1))


def _rope(x, cos, sin_signed, k):
    return x * cos + _swap_halves(x, k) * sin_signed


def _ada_kernel(c_ref, w_ref, b_ref, o_ref):
    c = c_ref[...]
    s = c * (1.0 / (1.0 + jnp.exp(-c)))
    s_hi, s_lo = _split_bf16(s)
    w_hi, w_lo = _split_bf16(w_ref[...])
    o_ref[...] = _dot(s_hi, w_hi) + (_dot(s_hi, w_lo) + _dot(s_lo, w_hi)) + b_ref[...]


def _ada_mod(cc, ada_w, ada_b):
    n = ada_w.shape[1]
    tn = 1024
    return pl.pallas_call(
        _ada_kernel,
        out_shape=jax.ShapeDtypeStruct((cc.shape[0], n), F32),
        grid=(n // tn,),
        in_specs=[
            pl.BlockSpec((cc.shape[0], D_MODEL), lambda j: (0, 0)),
            pl.BlockSpec((D_MODEL, tn), lambda j: (0, j)),
            pl.BlockSpec((1, tn), lambda j: (0, j)),
        ],
        out_specs=pl.BlockSpec((cc.shape[0], tn), lambda j: (0, j)),
        compiler_params=pltpu.CompilerParams(dimension_semantics=("parallel",)),
        name="ada_mod",
    )(cc, ada_w, ada_b.reshape(1, n))


def _proj_kernel(*refs, with_q):
    if with_q:
        (x_ref, mod_ref, nmix_ref, cg_ref, sg_ref, cm_ref, sm_ref,
         wq_ref, wk_ref, wv_ref, wqa_ref, wckv_ref, wkr_ref, wg_ref, wqb_ref, wkb_ref, wvb_ref,
         gq_ref, gk_ref, gqa_ref, gkva_ref, gmq_ref, gmk_ref,
         q_ref, k_ref, v_ref, qm_ref, km_ref, vm_ref, gate_ref) = refs
    else:
        (x_ref, mod_ref, nmix_ref, cg_ref, sg_ref, cm_ref, sm_ref,
         wk_ref, wv_ref, wckv_ref, wkr_ref, wkb_ref, wvb_ref,
         gk_ref, gkva_ref, gmk_ref,
         k_ref, v_ref, km_ref, vm_ref) = refs

    x = x_ref[0]
    mod = mod_ref[0]
    shift, scale = mod[0:1, :], mod[1:2, :]
    h = _rms(x, nmix_ref[...], D_MODEL) * (1.0 + scale) + shift
    hb = h.astype(BF16)
    cg, sg = cg_ref[...], sg_ref[...]
    cm, sm = cm_ref[...], sm_ref[...]

    kk = _dot(hb, wk_ref[...])
    for j in range(GQA_KV_HEADS):
        sl = slice(j * GQA_HEAD_DIM, (j + 1) * GQA_HEAD_DIM)
        kn = _rms(kk[:, sl], gk_ref[...], GQA_HEAD_DIM)
        k_ref[0, :, sl] = _rope(kn, cg, sg, 32).astype(BF16)
    v_ref[0] = _dot_nt(wv_ref[...], hb).astype(BF16)

    ckv = _rms(_dot(hb, wckv_ref[...]), gkva_ref[...], MLA_KV_RANK).astype(BF16)
    vm_ref[0] = _dot_nt(wvb_ref[...], ckv).astype(BF16)
    knope = _dot(ckv, wkb_ref[...])
    kr = _dot(hb, wkr_ref[...])
    gmk = gmk_ref[...]
    g_nope, g_rope = gmk[:, :MLA_NOPE_DIM], gmk[:, MLA_NOPE_DIM:]
    ssq_r = jnp.sum(kr * kr, axis=-1, keepdims=True)
    kr_roped = _rope(kr * g_rope, cm, sm, 16)
    for j in range(MLA_HEADS):
        kn = knope[:, j * MLA_NOPE_DIM:(j + 1) * MLA_NOPE_DIM]
        ms = (jnp.sum(kn * kn, axis=-1, keepdims=True) + ssq_r) * (1.0 / MLA_QK_DIM)
        r = lax.rsqrt(ms + EPS)
        base = j * MLA_HEAD_PAD
        km_ref[0, :, base:base + MLA_NOPE_DIM] = (kn * r * g_nope).astype(BF16)
        km_ref[0, :, base + MLA_NOPE_DIM:base + MLA_HEAD_PAD] = (kr_roped * r).astype(BF16)

    if not with_q:
        return

    qq = _dot(hb, wq_ref[...])
    q_scale = GQA_HEAD_DIM ** -0.5 * LOG2_E
    for j in range(GQA_HEADS):
        sl = slice(j * GQA_HEAD_DIM, (j + 1) * GQA_HEAD_DIM)
        qn = _rms(qq[:, sl], gq_ref[...], GQA_HEAD_DIM)
        q_ref[0, :, sl] = (_rope(qn, cg, sg, 32) * q_scale).astype(BF16)

    qa = _rms(_dot(hb, wqa_ref[...]), gqa_ref[...], MLA_Q_RANK).astype(BF16)
    q2 = _dot(qa, wqb_ref[...])
    gmq = gmq_ref[...]
    gq_nope, gq_rope = gmq[:, :MLA_NOPE_DIM], gmq[:, MLA_NOPE_DIM:]
    m_scale = MLA_QK_DIM ** -0.5 * LOG2_E
    for j in range(MLA_HEADS):
        base = j * MLA_HEAD_PAD
        qn = q2[:, base:base + MLA_NOPE_DIM]
        qr = q2[:, base + MLA_NOPE_DIM:base + MLA_HEAD_PAD]
        ms = (jnp.sum(qn * qn, axis=-1, keepdims=True) + jnp.sum(qr * qr, axis=-1, keepdims=True)) * (1.0 / MLA_QK_DIM)
        r = lax.rsqrt(ms + EPS)
        qm_ref[0, :, base:base + MLA_NOPE_DIM] = (qn * r * gq_nope * m_scale).astype(BF16)
        qm_ref[0, :, base + MLA_NOPE_DIM:base + MLA_HEAD_PAD] = (_rope(qr * r * gq_rope, cm, sm, 16) * m_scale).astype(BF16)

    gl = _dot(hb, wg_ref[...])
    gate_ref[0] = (1.0 / (1.0 + jnp.exp(-gl))).astype(BF16)


def _proj(x, mod3, mod_row_of_batch, tables, w, with_q, tm):
    b, l, _ = x.shape
    cg, sg, cm, sm = tables
    const = lambda shape: pl.BlockSpec(shape, lambda bi, i: (0,) * len(shape), pipeline_mode=pl.Buffered(1))
    tab = pl.BlockSpec((tm, LANES), lambda bi, i: (i, 0))
    in_specs = [
        pl.BlockSpec((1, tm, D_MODEL), lambda bi, i: (bi, i, 0)),
        pl.BlockSpec((1, 6, D_MODEL), lambda bi, i: (mod_row_of_batch(bi), 0, 0)),
        const((1, D_MODEL)), tab, tab, tab, tab,
    ]
    if with_q:
        weights = [w["wq"], w["wk"], w["wv"], w["wqa"], w["wckv"], w["wkr"], w["wg"], w["wqb"], w["wkb"], w["wvb"],
                   w["gq"], w["gk"], w["gqa"], w["gkva"], w["gmq"], w["gmk"]]
    else:
        weights = [w["wk"], w["wv"], w["wckv"], w["wkr"], w["wkb"], w["wvb"], w["gk"], w["gkva"], w["gmk"]]
    in_specs += [const(a.shape) for a in weights]

    def out(width):
        return jax.ShapeDtypeStruct((b, l, width), BF16), pl.BlockSpec((1, tm, width), lambda bi, i: (bi, i, 0))

    def out_t(width):
        return jax.ShapeDtypeStruct((b, width, l), BF16), pl.BlockSpec((1, width, tm), lambda bi, i: (bi, 0, i))

    outs = [out(GQA_KV_W), out_t(GQA_KV_W), out(MLA_HEADS * MLA_HEAD_PAD), out_t(MLA_V_W)]
    if with_q:
        outs = [out(GQA_Q_W)] + outs[:2] + [out(MLA_HEADS * MLA_HEAD_PAD)] + outs[2:] + [out(2 * D_MODEL)]
    return pl.pallas_call(
        functools.partial(_proj_kernel, with_q=with_q),
        out_shape=[o[0] for o in outs],
        grid=(b, l // tm),
        in_specs=in_specs,
        out_specs=[o[1] for o in outs],
        compiler_params=pltpu.CompilerParams(dimension_semantics=("parallel", "parallel"),
                                             vmem_limit_bytes=56 * 1024 * 1024),
        name="proj_latent" if with_q else "proj_ctx",
    )(x, mod3, w["norm_mix"], cg, sg, cm, sm, *weights)


def _attn_kernel(q_ref, kl_ref, vl_ref, kc_ref, vc_ref, o_ref, *, tk, n_lat, bounded):
    q = q_ref[0]
    tq = q.shape[0]
    dv = vl_ref.shape[1]
    n = n_lat // tk

    if bounded:
        l = jnp.zeros((1, tq), F32)
        acc = jnp.zeros((dv, tq), F32)
        for k, vt in [(kl_ref[0, j * tk:(j + 1) * tk, :], vl_ref[0, :, j * tk:(j + 1) * tk]) for j in range(n)] + [
                (kc_ref[0], vc_ref[0])]:
            p = jnp.exp2(_dot_nt(k, q))
            l = l + jnp.sum(p, axis=0, keepdims=True)
            acc = acc + _dot(vt, p.astype(BF16))
        o_ref[0] = (acc / l).T.astype(BF16)
        return

    def update(carry, s, vt):
        m, l, acc = carry
        m_new = jnp.maximum(m, jnp.max(s, axis=0, keepdims=True))
        alpha = jnp.exp2(m - m_new)
        p = jnp.exp2(s - m_new)
        l = alpha * l + jnp.sum(p, axis=0, keepdims=True)
        return m_new, l, alpha * acc + _dot(vt, p.astype(BF16))

    n = n_lat // tk
    keys = [kl_ref[0, j * tk:(j + 1) * tk, :] for j in range(n)] + [kc_ref[0]]
    vals = [vl_ref[0, :, j * tk:(j + 1) * tk] for j in range(n)] + [vc_ref[0]]
    carry = (jnp.full((1, tq), NEG_INF, F32), jnp.zeros((1, tq), F32), jnp.zeros((dv, tq), F32))
    s = _dot_nt(keys[0], q)
    for j in range(n + 1):
        s_next = _dot_nt(keys[j + 1], q) if j < n else None
        carry = update(carry, s, vals[j])
        s = s_next
    m, l, acc = carry
    o_ref[0] = (acc / l).T.astype(BF16)


SOFTMAX_SAFE_EXPONENT = 56.0


def _score_bound(gain_q, gain_k, dim):
    return dim * jnp.max(jnp.abs(gain_q)) * jnp.max(jnp.abs(gain_k)) * (dim ** -0.5 * LOG2_E) * 1.02


def _attention(q, k_lat, v_lat, k_ctx, v_ctx, n_heads, group, d_qk, dv, tq, tk, name, score_bound):
    run = functools.partial(_attention_call, q, k_lat, v_lat, k_ctx, v_ctx, n_heads, group, d_qk, dv, tq, tk, name)
    return lax.cond(score_bound <= SOFTMAX_SAFE_EXPONENT, lambda: run(True), lambda: run(False))


def _attention_call(q, k_lat, v_lat, k_ctx, v_ctx, n_heads, group, d_qk, dv, tq, tk, name, bounded):
    b, l, _ = q.shape
    lc = k_ctx.shape[1]
    name = name + ("_bounded" if bounded else "_online")
    return pl.pallas_call(
        functools.partial(_attn_kernel, tk=tk, n_lat=l, bounded=bounded),
        out_shape=jax.ShapeDtypeStruct((b, l, n_heads * dv), BF16),
        grid=(b, n_heads, l // tq),
        in_specs=[
            pl.BlockSpec((1, tq, d_qk), lambda bi, h, i: (bi, i, h)),
            pl.BlockSpec((1, l, d_qk), lambda bi, h, i: (bi, 0, h // group)),
            pl.BlockSpec((1, dv, l), lambda bi, h, i: (bi, h // group, 0)),
            pl.BlockSpec((1, lc, d_qk), lambda bi, h, i: (bi, 0, h // group)),
            pl.BlockSpec((1, dv, lc), lambda bi, h, i: (bi, h // group, 0)),
        ],
        out_specs=pl.BlockSpec((1, tq, dv), lambda bi, h, i: (bi, i, h)),
        compiler_params=pltpu.CompilerParams(dimension_semantics=("parallel", "parallel", "parallel"),
                                             vmem_limit_bytes=56 * 1024 * 1024),
        name=name,
    )(q, k_lat, v_lat, k_ctx, v_ctx)


def _merge_kernel(oa_ref, ob_ref, gate_ref, x_ref, mod_ref, woa_ref, wob_ref, wout_ref, nffn_ref,
                  rwh_ref, rwl_ref, rb_ref, x1_ref, h2_ref, eidx_ref, egate_ref, rank_ref, before_ref, tcnt_ref,
                  cnt_ref, carry_ref):
    @pl.when((pl.program_id(0) == 0) & (pl.program_id(1) == 0))
    def _():
        carry_ref[...] = jnp.zeros_like(carry_ref)

    mod = mod_ref[0]
    g1, shift2, scale2 = mod[2:3, :], mod[3:4, :], mod[4:5, :]
    ya = _dot(oa_ref[0], woa_ref[...])
    yb = _dot(ob_ref[0], wob_ref[...])
    g = gate_ref[0].astype(F32)
    y = g[:, :D_MODEL] * ya + g[:, D_MODEL:] * yb
    z = _dot(y.astype(BF16), wout_ref[...])
    x1 = x_ref[0] + g1 * z
    x1_ref[0] = x1
    h2 = _rms(x1, nffn_ref[...], D_MODEL) * (1.0 + scale2) + shift2
    h2_ref[0] = h2.astype(BF16)

    h_hi, h_lo = _split_bf16(h2)
    logits = _dot(h_hi, rwh_ref[...]) + (_dot(h_hi, rwl_ref[...]) + _dot(h_lo, rwh_ref[...])) + rb_ref[...]
    lane = lax.broadcasted_iota(jnp.int32, logits.shape, 1).astype(F32)
    cur = jnp.where(lane < N_EXPERTS, logits, NEG_INF)
    vals, idxs = [], []
    for _ in range(TOP_K):
        mx = jnp.max(cur, axis=-1, keepdims=True)
        ix = jnp.min(jnp.where(cur == mx, lane, float(LANES)), axis=-1, keepdims=True)
        vals.append(mx)
        idxs.append(ix)
        cur = jnp.where(lane == ix, NEG_INF, cur)
    ex = [jnp.exp(v - vals[0]) for v in vals]
    den = ex[0] + ex[1] + ex[2] + ex[3]
    eidx = jnp.zeros(logits.shape, F32)
    egate = jnp.zeros(logits.shape, F32)
    for k in range(TOP_K):
        eidx = jnp.where(lane == k, idxs[k], eidx)
        egate = jnp.where(lane == k, ex[k] / den, egate)
    eidx = eidx.astype(jnp.int32)
    eidx_ref[0] = eidx
    egate_ref[0] = egate

    for s in range(eidx.shape[0] // MOE_TILE):
        rows = slice(s * MOE_TILE, (s + 1) * MOE_TILE)
        before_ref[s] = carry_ref[...]
        rank, tile_cnt = _route_tile(eidx[rows], carry_ref)
        rank_ref[0, rows, :] = rank
        tcnt_ref[s] = jnp.broadcast_to(tile_cnt, carry_ref.shape)
    cnt_ref[...] = carry_ref[...]


def _merge(o_a, o_b, gates, x, mod3, w, tm):
    b, l, _ = x.shape
    const = lambda shape: pl.BlockSpec(shape, lambda bi, i: (0,) * len(shape), pipeline_mode=pl.Buffered(1))
    tok = lambda width: pl.BlockSpec((1, tm, width), lambda bi, i: (bi, i, 0))
    assert tm % MOE_TILE == 0
    sub = tm // MOE_TILE
    per_tile = jax.ShapeDtypeStruct((b * l // MOE_TILE, 8, LANES), F32)
    tile_spec = pl.BlockSpec((sub, 8, LANES), lambda bi, i: (bi * (l // tm) + i, 0, 0))
    return pl.pallas_call(
        _merge_kernel,
        out_shape=[jax.ShapeDtypeStruct((b, l, D_MODEL), F32), jax.ShapeDtypeStruct((b, l, D_MODEL), BF16),
                   jax.ShapeDtypeStruct((b, l, LANES), jnp.int32), jax.ShapeDtypeStruct((b, l, LANES), F32),
                   jax.ShapeDtypeStruct((b, l, LANES), jnp.int32), per_tile, per_tile,
                   jax.ShapeDtypeStruct((8, LANES), F32)],
        grid=(b, l // tm),
        in_specs=[tok(GQA_Q_W), tok(MLA_V_W), tok(2 * D_MODEL), tok(D_MODEL),
                  pl.BlockSpec((1, 6, D_MODEL), lambda bi, i: (bi, 0, 0)),
                  const((GQA_Q_W, D_MODEL)), const((MLA_V_W, D_MODEL)), const((D_MODEL, D_MODEL)), const((1, D_MODEL)),
                  const((D_MODEL, LANES)), const((D_MODEL, LANES)), const((1, LANES))],
        out_specs=[tok(D_MODEL), tok(D_MODEL), tok(LANES), tok(LANES), tok(LANES), tile_spec, tile_spec,
                   pl.BlockSpec((8, LANES), lambda bi, i: (0, 0))],
        scratch_shapes=[pltpu.VMEM((8, LANES), F32)],
        compiler_params=pltpu.CompilerParams(dimension_semantics=("arbitrary", "arbitrary"),
                                             vmem_limit_bytes=56 * 1024 * 1024),
        name="merge_router",
    )(o_a, o_b, gates, x, mod3, w["woa"], w["wob"], w["wout"], w["norm_ffn"], w["rw_hi"], w["rw_lo"], w["rb"])


def _onehots(idx, lane):
    return [lane == idx[:, k:k + 1] for k in range(TOP_K)]


SEG_ALIGN = 16


def _route_tile(idx, carry_ref):
    tm = idx.shape[0]
    lane = lax.broadcasted_iota(jnp.int32, idx.shape, 1)
    oh = _onehots(idx, lane)
    total = jnp.zeros(idx.shape, F32)
    for k in range(TOP_K):
        total = total + jnp.where(oh[k], 1.0, 0.0)
    row = lax.broadcasted_iota(jnp.int32, (tm, tm), 0)
    col = lax.broadcasted_iota(jnp.int32, (tm, tm), 1)
    tri = jnp.where(row > col, 1.0, 0.0).astype(BF16)
    before = _dot(tri, total.astype(BF16)) + carry_ref[0:1, :]
    rank = jnp.where(lane < TOP_K, jnp.take_along_axis(before, idx, axis=1), 0.0)
    tile_cnt = jnp.sum(total, axis=0, keepdims=True)
    carry_ref[...] = carry_ref[...] + tile_cnt
    return rank.astype(jnp.int32), tile_cnt


def _dest_kernel(eidx_ref, rank_ref, start_ref, before_ref, dest_ref, col_ref, dest_t_ref, col_t_ref):
    idx = eidx_ref[...]
    valid = lax.broadcasted_iota(jnp.int32, idx.shape, 1) < TOP_K
    region_start = jnp.take_along_axis(jnp.broadcast_to(start_ref[0:1, :], idx.shape), idx, axis=1)
    seg_start = jnp.take_along_axis(jnp.broadcast_to(before_ref[0][0:1, :], idx.shape), idx, axis=1)
    rank = rank_ref[...].astype(F32)
    window_start = jnp.floor(seg_start * (1.0 / SEG_ALIGN)) * SEG_ALIGN
    local = rank - window_start
    dest = jnp.where(valid, region_start + rank, 0.0)
    col = jnp.where(valid & (local < WIN), idx.astype(F32) * WIN + local, -1.0)
    dest_ref[...] = dest.astype(jnp.int32)
    col_ref[...] = col.astype(jnp.int32)
    dest_t_ref[0] = dest.T[0:8, :].astype(jnp.int32)
    col_t_ref[0] = col.T[0:8, :].astype(jnp.int32)


def _dest(eidx, rank, start, before, tm):
    t = eidx.shape[0]
    blk = pl.BlockSpec((tm, LANES), lambda i: (i, 0))
    blk_t = pl.BlockSpec((1, 8, tm), lambda i: (i, 0, 0))
    tok = jax.ShapeDtypeStruct((t, LANES), jnp.int32)
    tok_t = jax.ShapeDtypeStruct((t // tm, 8, tm), jnp.int32)
    return pl.pallas_call(
        _dest_kernel,
        out_shape=[tok, tok, tok_t, tok_t],
        grid=(t // tm,),
        in_specs=[blk, blk, pl.BlockSpec((8, LANES), lambda i: (0, 0)),
                  pl.BlockSpec((1, 8, LANES), lambda i: (i, 0, 0))],
        out_specs=[blk, blk, blk_t, blk_t],
        compiler_params=pltpu.CompilerParams(dimension_semantics=("parallel",)),
        name="route_dest",
    )(eidx, rank, start, before)


MOE_TILE = 256
WIN = 64
N_MAIN_ROWS = N_EXPERTS * WIN
MAIN_CHUNK = 512
WIN_PER_STACK = MOE_TILE // WIN
MAX_WINDOWS = (MOE_TILE * TOP_K + N_EXPERTS * SEG_ALIGN) // WIN
CARRY_ROWS = N_EXPERTS * SEG_ALIGN
WORKLIST_LEN = -(-(MAX_WINDOWS + WIN_PER_STACK) // WIN_PER_STACK) * WIN_PER_STACK


def _build_worklist(ts_ref, tc_ref, wl_ref, junk_row):
    base = pl.program_id(0) * N_EXPERTS

    def per_expert(e, n):
        first = ts_ref[base + e]
        n_win = (tc_ref[base + e] + (WIN - 1)) // WIN

        def per_window(wi, n):
            wl_ref[n] = first + wi * WIN
            return n + 1

        return lax.fori_loop(1, n_win, per_window, n)

    n = lax.fori_loop(0, N_EXPERTS, per_expert, 0)
    n_stacks = (n + (WIN_PER_STACK - 1)) // WIN_PER_STACK

    def pad(j, c):
        wl_ref[j] = junk_row + (j % WIN_PER_STACK) * WIN
        return c

    lax.fori_loop(n, n_stacks * WIN_PER_STACK, pad, 0)
    return n_stacks


def _seg_aligned(row):
    return row if isinstance(row, int) else pl.multiple_of(row, SEG_ALIGN)


def _stack_row_ids(wl_ref, stack, shape, axis):
    pos = lax.broadcasted_iota(jnp.int32, shape, axis)
    row = jnp.full(shape, -1, jnp.int32)
    for wi in range(WIN_PER_STACK - 1, -1, -1):
        row = jnp.where(pos < (wi + 1) * WIN, wl_ref[stack * WIN_PER_STACK + wi] + (pos - wi * WIN), row)
    return row


def _select_rows(row_id, id_t):
    shape = (row_id.shape[0], id_t.shape[1])
    sel = jnp.zeros(shape, F32)
    for k in range(TOP_K):
        hit = row_id == id_t[k:k + 1, :]
        sel = jnp.where(hit, 1.0, sel)
    return sel.astype(BF16)


MAIN_PENDING = 2


TAIL_ROWS = WIN + EXPERT_BLOCK


def _dispatch_kernel(ts_ref, tc_ref, ov_ref, re_ref, h_ref, col_t_ref, dest_t_ref,
                     buf_ref, wl_ref, pend_ref, main_x, carry_x, stage_x, main_sem, sem, *, junk_row):
    i = pl.program_id(0)
    base = i * N_EXPERTS
    next_base = jnp.minimum(i + 1, pl.num_programs(0) - 1) * N_EXPERTS

    @pl.when(i == 0)
    def _():
        pend_ref[0] = 0
        pend_ref[1] = 0
        pend_ref[MAIN_PENDING] = 0
        carry_x[...] = jnp.zeros_like(carry_x)
        main_x[0:TAIL_ROWS, :] = jnp.zeros((TAIL_ROWS, D_MODEL), BF16)
        cps = []
        for e in range(N_EXPERTS):
            rows = pl.ds(pl.multiple_of(jnp.maximum(re_ref[e] - TAIL_ROWS, 0), SEG_ALIGN), TAIL_ROWS)
            cps.append(pltpu.make_async_copy(main_x.at[pl.ds(0, TAIL_ROWS)], buf_ref.at[rows], main_sem))
        for cp in cps:
            cp.start()
        for cp in cps:
            cp.wait()

        def zero_block_copies(blk):
            rows = pl.ds(pl.multiple_of(blk * EXPERT_BLOCK, EXPERT_BLOCK), EXPERT_BLOCK)
            return (pltpu.make_async_copy(main_x.at[pl.ds(0, EXPERT_BLOCK)], buf_ref.at[rows], main_sem),)

        def start_zero(blk, c):
            for cp in zero_block_copies(blk):
                cp.start()
            return c

        def wait_zero(blk, c):
            for cp in zero_block_copies(blk):
                cp.wait()
            return c

        first_free, n_blocks = re_ref[N_EXPERTS - 1] // EXPERT_BLOCK, (junk_row + EXPERT_BLOCK) // EXPERT_BLOCK
        lax.fori_loop(first_free, n_blocks, start_zero, 0)
        lax.fori_loop(first_free, n_blocks, wait_zero, 0)

    h = h_ref[...]

    def main_copies(first_row):
        cps = []
        for e in range(N_EXPERTS):
            rows = pl.ds(_seg_aligned(first_row(e)), WIN)
            cps.append(pltpu.make_async_copy(main_x.at[pl.ds(e * WIN, WIN)], buf_ref.at[rows], main_sem))
        return cps

    def drain_main():
        @pl.when(pend_ref[MAIN_PENDING] == 1)
        def _():
            for cp in main_copies(lambda e: 0):
                cp.wait()
            pend_ref[MAIN_PENDING] = 0

    col_t = col_t_ref[0]
    dest_t = dest_t_ref[0]
    pieces = []
    for c in range(N_MAIN_ROWS // MAIN_CHUNK):
        row_id = lax.broadcasted_iota(jnp.int32, (MAIN_CHUNK, 1), 0) + c * MAIN_CHUNK
        pieces.append(_dot(_select_rows(row_id, col_t), h).astype(BF16))
    group_pos = lax.broadcasted_iota(jnp.int32, (SEG_ALIGN, 1), 0)
    end_group = jnp.concatenate([ts_ref[next_base + e] + group_pos for e in range(N_EXPERTS)], axis=0)
    end_rows = _dot(_select_rows(end_group, dest_t), h).astype(BF16)

    def window_copies(slot, first_row):
        cps = []
        for wi in range(WIN_PER_STACK):
            rows = pl.ds(_seg_aligned(first_row(wi)), WIN)
            cps.append(pltpu.make_async_copy(stage_x.at[slot, pl.ds(wi * WIN, WIN)], buf_ref.at[rows], sem.at[slot]))
        return cps

    def drain(slot):
        @pl.when(pend_ref[slot] == 1)
        def _():
            for cp in window_copies(slot, lambda wi: 0):
                cp.wait()
            pend_ref[slot] = 0

    drain_main()
    drain(0)
    drain(1)
    for c, packed in enumerate(pieces):
        main_x[c * MAIN_CHUNK:(c + 1) * MAIN_CHUNK, :] = packed
    for e in range(N_EXPERTS):
        win, grp = pl.ds(e * WIN, SEG_ALIGN), pl.ds(e * SEG_ALIGN, SEG_ALIGN)
        main_x[win, :] = main_x[win, :] + carry_x[grp, :]
        same = ts_ref[next_base + e] == ts_ref[base + e]
        carry_x[grp, :] = end_rows[e * SEG_ALIGN:(e + 1) * SEG_ALIGN, :] + jnp.where(same, carry_x[grp, :], 0).astype(BF16)
    for n, cp in enumerate(main_copies(lambda e: ts_ref[base + e])):
        cp.start(priority=n % 2)
    pend_ref[MAIN_PENDING] = 1

    @pl.when(ov_ref[i] > 0)
    def _():
        n_stacks = _build_worklist(ts_ref, tc_ref, wl_ref, junk_row)

        def stack_body(s, c):
            slot = s & 1
            rows = _dot(_select_rows(_stack_row_ids(wl_ref, s, (MOE_TILE, 1), 0), dest_t), h)
            drain(slot)
            stage_x[slot] = rows.astype(BF16)
            for cp in window_copies(slot, lambda wi: wl_ref[s * WIN_PER_STACK + wi]):
                cp.start()
            pend_ref[slot] = 1
            return c

        lax.fori_loop(0, n_stacks, stack_body, 0)

    @pl.when(i == pl.num_programs(0) - 1)
    def _():
        drain_main()
        drain(0)
        drain(1)


def _moe_grid_spec(n_tiles, in_specs, out_specs, scratch_shapes):
    return pltpu.PrefetchScalarGridSpec(num_scalar_prefetch=4, grid=(n_tiles,), in_specs=in_specs,
                                        out_specs=out_specs, scratch_shapes=scratch_shapes)


def _dispatch(tile_start, tile_cnt, tile_ovf, region_ends, h2, col_t, dest_t, n_rows):
    n_tiles = h2.shape[0] // MOE_TILE
    junk_row = n_rows - EXPERT_BLOCK
    any_spec = pl.BlockSpec(memory_space=pl.ANY)
    tok_t = pl.BlockSpec((1, 8, MOE_TILE), lambda i, *_: (i, 0, 0))
    return pl.pallas_call(
        functools.partial(_dispatch_kernel, junk_row=junk_row),
        out_shape=jax.ShapeDtypeStruct((n_rows, D_MODEL), BF16),
        grid_spec=_moe_grid_spec(
            n_tiles,
            [pl.BlockSpec((MOE_TILE, D_MODEL), lambda i, *_: (i, 0)), tok_t, tok_t],
            any_spec,
            [pltpu.SMEM((WORKLIST_LEN,), jnp.int32), pltpu.SMEM((3,), jnp.int32),
             pltpu.VMEM((N_MAIN_ROWS, D_MODEL), BF16), pltpu.VMEM((CARRY_ROWS, D_MODEL), BF16),
             pltpu.VMEM((2, MOE_TILE, D_MODEL), BF16),
             pltpu.SemaphoreType.DMA(()), pltpu.SemaphoreType.DMA((2,))]),
        compiler_params=pltpu.CompilerParams(dimension_semantics=("arbitrary",),
                                             vmem_limit_bytes=56 * 1024 * 1024),
        name="moe_dispatch",
    )(tile_start, tile_cnt, tile_ovf, region_ends, h2, col_t, dest_t)


def _select_cols(ids, gates, col_id):
    sel = jnp.zeros((ids.shape[0], col_id.shape[1]), F32)
    for k in range(TOP_K):
        sel = jnp.where(ids[:, k:k + 1] == col_id, gates[:, k:k + 1], sel)
    return sel.astype(BF16)


def _combine_kernel(ts_ref, tc_ref, ov_ref, re_ref, col_ref, dest_ref, egate_ref, x1_ref, mod_ref, y_ref, o_ref,
                    wl_ref, main_y, stage_y, acc_ref, main_sem, sem, *, junk_row):
    del re_ref
    i = pl.program_id(0)
    slot_i = i & 1

    def main_copies(slot, first_row):
        return [pltpu.make_async_copy(y_ref.at[pl.ds(_seg_aligned(first_row(e)), WIN)],
                                      main_y.at[slot, pl.ds(e * WIN, WIN)], main_sem.at[slot])
                for e in range(N_EXPERTS)]

    def fetch_main(tile, slot):
        for n, cp in enumerate(main_copies(slot, lambda e: ts_ref[tile * N_EXPERTS + e])):
            cp.start(priority=n % 2)

    @pl.when(i == 0)
    def _():
        fetch_main(0, 0)

    @pl.when(i + 1 < pl.num_programs(0))
    def _():
        fetch_main(i + 1, 1 - slot_i)

    for cp in main_copies(slot_i, lambda e: 0):
        cp.wait()

    col = col_ref[...]
    egate = egate_ref[...]
    sel = jnp.concatenate(
        [_select_cols(col, egate, lax.broadcasted_iota(jnp.int32, (1, MAIN_CHUNK), 1) + c * MAIN_CHUNK)
         for c in range(N_MAIN_ROWS // MAIN_CHUNK)], axis=1)
    acc_ref[...] = _dot(sel, main_y[slot_i])

    @pl.when(ov_ref[i] > 0)
    def _():
        n_stacks = _build_worklist(ts_ref, tc_ref, wl_ref, junk_row)
        dest = dest_ref[...]
        if WIN_PER_STACK * WIN < MOE_TILE:
            for slot in range(2):
                stage_y[slot, WIN_PER_STACK * WIN:, :] = jnp.zeros((MOE_TILE - WIN_PER_STACK * WIN, D_MODEL), BF16)

        def window_copies(slot, first_row):
            return [pltpu.make_async_copy(y_ref.at[pl.ds(_seg_aligned(first_row(wi)), WIN)],
                                          stage_y.at[slot, pl.ds(wi * WIN, WIN)], sem.at[slot])
                    for wi in range(WIN_PER_STACK)]

        def fetch(s, slot):
            for cp in window_copies(slot, lambda wi: wl_ref[s * WIN_PER_STACK + wi]):
                cp.start()

        @pl.when(n_stacks > 0)
        def _():
            fetch(0, 0)

        def stack_body(s, c):
            slot = s & 1

            @pl.when(s + 1 < n_stacks)
            def _():
                fetch(s + 1, 1 - slot)

            for cp in window_copies(slot, lambda wi: 0):
                cp.wait()
            sel_o = _select_cols(dest, egate, _stack_row_ids(wl_ref, s, (1, MOE_TILE), 1))
            acc_ref[...] += _dot(sel_o, stage_y[slot])
            return c

        lax.fori_loop(0, n_stacks, stack_body, 0)

    g2 = mod_ref[0][5:6, :]
    o_ref[...] = x1_ref[...] + g2 * acc_ref[...]


def _combine(tile_start, tile_cnt, tile_ovf, region_ends, col, dest, egate, x1, mod3, y, l):
    t = x1.shape[0]
    per_batch = l // MOE_TILE
    junk_row = y.shape[0] - EXPERT_BLOCK
    tok = lambda width: pl.BlockSpec((MOE_TILE, width), lambda i, *_: (i, 0))
    return pl.pallas_call(
        functools.partial(_combine_kernel, junk_row=junk_row),
        out_shape=jax.ShapeDtypeStruct((t, D_MODEL), F32),
        grid_spec=_moe_grid_spec(
            t // MOE_TILE,
            [tok(LANES), tok(LANES), tok(LANES), tok(D_MODEL),
             pl.BlockSpec((1, 6, D_MODEL), lambda i, *_: (i // per_batch, 0, 0)),
             pl.BlockSpec(memory_space=pl.ANY)],
            tok(D_MODEL),
            [pltpu.SMEM((WORKLIST_LEN,), jnp.int32), pltpu.VMEM((2, N_MAIN_ROWS, D_MODEL), BF16),
             pltpu.VMEM((2, MOE_TILE, D_MODEL), BF16), pltpu.VMEM((MOE_TILE, D_MODEL), F32),
             pltpu.SemaphoreType.DMA((2,)), pltpu.SemaphoreType.DMA((2,))]),
        compiler_params=pltpu.CompilerParams(dimension_semantics=("arbitrary",),
                                             vmem_limit_bytes=56 * 1024 * 1024),
        name="moe_combine",
    )(tile_start, tile_cnt, tile_ovf, region_ends, col, dest, egate, x1, mod3, y)


def _expert_kernel(be_ref, nused_ref, x_ref, w1_ref, b1_ref, w2_ref, b2_ref, y_ref, w1b_ref, w2b_ref):
    i = pl.program_id(0)
    live = i < nused_ref[0]

    @pl.when(live & ((i == 0) | (be_ref[i] != be_ref[jnp.maximum(i - 1, 0)])))
    def _():
        w1b_ref[...] = w1_ref[0].astype(BF16)
        w2b_ref[...] = w2_ref[0].astype(BF16)

    @pl.when(live)
    def _():
        gu = _dot(x_ref[...], w1b_ref[...]) + b1_ref[0]
        glu = jnp.minimum(gu[:, :D_EXPERT], SWIGLU_LIMIT)
        lin = jnp.clip(gu[:, D_EXPERT:], -SWIGLU_LIMIT, SWIGLU_LIMIT)
        act = glu * (1.0 / (1.0 + jnp.exp(-SWIGLU_ALPHA * glu))) * (lin + 1.0)
        y_ref[...] = (_dot(act.astype(BF16), w2b_ref[...]) + b2_ref[0]).astype(BF16)

    @pl.when(pl.program_id(0) >= nused_ref[0])
    def _():
        y_ref[...] = jnp.zeros_like(y_ref)


def _experts(block_e, nused, buf, w1, b1, w2, b2):
    nb = buf.shape[0] // EXPERT_BLOCK
    row = lambda i, be, nu: (jnp.minimum(i, nu[0] - 1), 0)
    out_row = lambda i, be, nu: (i, 0)
    exp3 = lambda i, be, nu: (be[jnp.minimum(i, nu[0] - 1)], 0, 0)
    return pl.pallas_call(
        _expert_kernel,
        out_shape=jax.ShapeDtypeStruct((buf.shape[0], D_MODEL), BF16),
        grid_spec=pltpu.PrefetchScalarGridSpec(
            num_scalar_prefetch=2,
            grid=(nb,),
            in_specs=[pl.BlockSpec((EXPERT_BLOCK, D_MODEL), row),
                      pl.BlockSpec((1, D_MODEL, 2 * D_EXPERT), exp3),
                      pl.BlockSpec((1, 1, 2 * D_EXPERT), exp3),
                      pl.BlockSpec((1, D_EXPERT, D_MODEL), exp3),
                      pl.BlockSpec((1, 1, D_MODEL), exp3)],
            out_specs=pl.BlockSpec((EXPERT_BLOCK, D_MODEL), out_row),
            scratch_shapes=[pltpu.VMEM((D_MODEL, 2 * D_EXPERT), BF16), pltpu.VMEM((D_EXPERT, D_MODEL), BF16)]),
        compiler_params=pltpu.CompilerParams(dimension_semantics=("arbitrary",),
                                             vmem_limit_bytes=56 * 1024 * 1024),
        name="moe_experts",
    )(block_e, nused, buf, w1, b1, w2, b2)


def _rope_tables(seq_len):
    pos = jnp.arange(seq_len, dtype=jnp.int32)
    row = (pos // GRID_W).astype(F32)[:, None]
    col = (pos % GRID_W).astype(F32)[:, None]

    def table(d_axis, pad):
        inv_freq = ROPE_THETA ** (-jnp.arange(0, d_axis, 2, dtype=F32) / d_axis)
        ar, ac = row * inv_freq[None, :], col * inv_freq[None, :]
        cos = jnp.concatenate([jnp.cos(ar), jnp.cos(ar), jnp.cos(ac), jnp.cos(ac)], axis=-1)
        sin = jnp.concatenate([-jnp.sin(ar), jnp.sin(ar), -jnp.sin(ac), jnp.sin(ac)], axis=-1)
        if pad:
            cos = jnp.concatenate([cos, jnp.ones((seq_len, pad), F32)], axis=-1)
            sin = jnp.concatenate([sin, jnp.zeros((seq_len, pad), F32)], axis=-1)
        return cos, sin

    cg, sg = table(GQA_HEAD_DIM // 2, 0)
    cm, sm = table(MLA_ROPE_DIM // 2, LANES - MLA_ROPE_DIM)
    return cg, sg, cm, sm


def _pad_lanes(a, width):
    return jnp.pad(a, [(0, 0)] * (a.ndim - 1) + [(0, width - a.shape[-1])])


def _prep_weights(p):
    w_in = p["w_in"]
    w = {"norm_mix": p["norm_mix"].reshape(1, D_MODEL), "norm_ffn": p["norm_ffn"].reshape(1, D_MODEL)}
    w["wq"] = w_in[:, :OFF_GQA_K].astype(BF16)
    w["wk"] = w_in[:, OFF_GQA_K:OFF_GQA_V].astype(BF16)
    w["wv"] = w_in[:, OFF_GQA_V:OFF_MLA_QA].T.astype(BF16)
    w["wqa"] = w_in[:, OFF_MLA_QA:OFF_MLA_KVA].astype(BF16)
    w["wckv"] = w_in[:, OFF_MLA_KVA:OFF_MLA_KVA + MLA_KV_RANK].astype(BF16)
    w["wkr"] = _pad_lanes(w_in[:, OFF_MLA_KVA + MLA_KV_RANK:OFF_GATE], LANES).astype(BF16)
    w["wg"] = w_in[:, OFF_GATE:].astype(BF16)
    wqb = p["mla_w_qb"].reshape(MLA_Q_RANK, MLA_HEADS, MLA_QK_DIM)
    w["wqb"] = _pad_lanes(wqb, MLA_HEAD_PAD).reshape(MLA_Q_RANK, MLA_HEADS * MLA_HEAD_PAD).astype(BF16)
    wkvb = p["mla_w_kvb"].reshape(MLA_KV_RANK, MLA_HEADS, MLA_NOPE_DIM + MLA_V_DIM)
    w["wkb"] = wkvb[:, :, :MLA_NOPE_DIM].reshape(MLA_KV_RANK, MLA_HEADS * MLA_NOPE_DIM).astype(BF16)
    w["wvb"] = wkvb[:, :, MLA_NOPE_DIM:].reshape(MLA_KV_RANK, MLA_V_W).T.astype(BF16)
    w["gq"] = p["gqa_q_norm"].reshape(1, GQA_HEAD_DIM)
    w["gk"] = p["gqa_k_norm"].reshape(1, GQA_HEAD_DIM)
    w["gqa"] = p["mla_q_a_norm"].reshape(1, MLA_Q_RANK)
    w["gkva"] = p["mla_kv_a_norm"].reshape(1, MLA_KV_RANK)
    w["gmq"] = _pad_lanes(p["mla_q_norm"].reshape(1, MLA_QK_DIM), MLA_HEAD_PAD)
    w["gmk"] = _pad_lanes(p["mla_k_norm"].reshape(1, MLA_QK_DIM), MLA_HEAD_PAD)
    w["woa"] = p["w_o_gqa"].astype(BF16)
    w["wob"] = p["w_o_mla"].astype(BF16)
    w["wout"] = p["w_out"].astype(BF16)
    rw = _pad_lanes(p["router_w"], LANES)
    w["rw_hi"] = rw.astype(BF16)
    w["rw_lo"] = (rw - w["rw_hi"].astype(F32)).astype(BF16)
    w["rb"] = _pad_lanes(p["router_b"].reshape(1, N_EXPERTS), LANES)
    return w


def _layer(x, c, ctx, c_ctx, p):
    b, l, _ = x.shape
    lc = ctx.shape[1]
    t = b * l
    w = _prep_weights(p)

    n_mod_rows = -(-(b + 1) // 8) * 8
    cc = jnp.zeros((n_mod_rows, D_MODEL), F32).at[:b].set(c).at[b].set(c_ctx)
    mod3 = _ada_mod(cc, p["ada_w"], p["ada_b"]).reshape(n_mod_rows, 6, D_MODEL)

    tables = _rope_tables(l)
    ident = (jnp.ones((lc, LANES), F32), jnp.zeros((lc, LANES), F32)) * 2
    tm = min(512, l)
    q_a, k_a, v_a, q_m, k_m, v_m, gates = _proj(x, mod3, lambda bi: bi, tables, w, True, tm)
    kc_a, vc_a, kc_m, vc_m = _proj(ctx, mod3, lambda bi: b, ident, w, False, min(256, lc))

    tq, tk = min(2048, l), min(2048, l)
    o_a = _attention(q_a, k_a, v_a, kc_a, vc_a, GQA_HEADS, GQA_GROUP, GQA_HEAD_DIM, GQA_HEAD_DIM, tq, tk, "attn_gqa",
                     _score_bound(p["gqa_q_norm"], p["gqa_k_norm"], GQA_HEAD_DIM))
    o_m = _attention(q_m, k_m, v_m, kc_m, vc_m, MLA_HEADS, 1, MLA_HEAD_PAD, MLA_V_DIM, tq, tk, "attn_mla",
                     _score_bound(p["mla_q_norm"], p["mla_k_norm"], MLA_QK_DIM))

    x1, h2, eidx, egate, rank, before_raw, tcnt, cnt = _merge(o_a, o_m, gates, x, mod3, w, tm)
    x1, h2 = x1.reshape(t, D_MODEL), h2.reshape(t, D_MODEL)
    eidx, egate, rank = eidx.reshape(t, LANES), egate.reshape(t, LANES), rank.reshape(t, LANES)

    assert t % MOE_TILE == 0
    counts = cnt[0, :N_EXPERTS].astype(jnp.int32)
    before = before_raw[:, 0, :N_EXPERTS].astype(jnp.int32)
    region = (counts + WIN + EXPERT_BLOCK - 1) // EXPERT_BLOCK * EXPERT_BLOCK
    region_ends = jnp.cumsum(region)
    region_starts = region_ends - region
    max_rows = t * TOP_K + N_EXPERTS * (WIN + EXPERT_BLOCK - 1)
    n_blocks = max_rows // EXPERT_BLOCK + 1
    n_rows = n_blocks * EXPERT_BLOCK
    block_row = jnp.arange(n_blocks, dtype=jnp.int32) * EXPERT_BLOCK
    block_e = jnp.minimum(jnp.sum(region_ends[None, :] <= block_row[:, None], axis=1), N_EXPERTS - 1).astype(jnp.int32)
    nused = (region_ends[-1:] // EXPERT_BLOCK).astype(jnp.int32)
    tile_start = (region_starts[None, :] + before // SEG_ALIGN * SEG_ALIGN).reshape(-1)
    tile_cnt = before % SEG_ALIGN + tcnt[:, 0, :N_EXPERTS].astype(jnp.int32)
    tile_ovf = jnp.sum(jnp.maximum((tile_cnt + WIN - 1) // WIN - 1, 0), axis=1).astype(jnp.int32)
    tile_cnt = tile_cnt.reshape(-1)
    start = jnp.zeros((8, LANES), F32).at[0, :N_EXPERTS].set(region_starts.astype(F32))
    dest, col, dest_t, col_t = _dest(eidx, rank, start, before_raw, MOE_TILE)

    region_ends = region_ends.astype(jnp.int32)
    buf = _dispatch(tile_start, tile_cnt, tile_ovf, region_ends, h2, col_t, dest_t, n_rows)
    y = _experts(block_e, nused, buf, p["expert_w1"], p["expert_b1"].reshape(N_EXPERTS, 1, -1),
                 p["expert_w2"], p["expert_b2"].reshape(N_EXPERTS, 1, -1))
    out = _combine(tile_start, tile_cnt, tile_ovf, region_ends, col, dest, egate, x1, mod3, y, l)
    return out.reshape(b, l, D_MODEL)


def kernel(x, c, ctx, c_ctx, ada_w, ada_b, norm_mix, norm_ffn, w_in, gqa_q_norm, gqa_k_norm, mla_q_a_norm, mla_kv_a_norm, mla_w_qb, mla_w_kvb, mla_q_norm, mla_k_norm, w_o_gqa, w_o_mla, w_out, router_w, router_b, expert_w1, expert_b1, expert_w2, expert_b2):
    assert ada_w.shape[0] == 1, "single-layer problem: the context stream is never updated"
    p = {
        "ada_w": ada_w[0], "ada_b": ada_b[0], "norm_mix": norm_mix[0], "norm_ffn": norm_ffn[0],
        "w_in": w_in[0], "gqa_q_norm": gqa_q_norm[0], "gqa_k_norm": gqa_k_norm[0],
        "mla_q_a_norm": mla_q_a_norm[0], "mla_kv_a_norm": mla_kv_a_norm[0],
        "mla_w_qb": mla_w_qb[0], "mla_w_kvb": mla_w_kvb[0],
        "mla_q_norm": mla_q_norm[0], "mla_k_norm": mla_k_norm[0],
        "w_o_gqa": w_o_gqa[0], "w_o_mla": w_o_mla[0], "w_out": w_out[0],
        "router_w": router_w[0], "router_b": router_b[0],
        "expert_w1": expert_w1[0], "expert_b1": expert_b1[0],
        "expert_w2": expert_w2[0], "expert_b2": expert_b2[0],
    }
    return _layer(x, c, ctx, c_ctx, p)
```

```python
import functools
import math

import jax
import jax.numpy as jnp
from jax import lax
from jax.experimental import pallas as pl
from jax.experimental.pallas import tpu as pltpu

D_MODEL = 1024
GRID_W = 64
EPS = 1e-6
ROPE_THETA = 10000.0

GQA_HEADS = 8
GQA_KV_HEADS = 2
GQA_GROUP = GQA_HEADS // GQA_KV_HEADS
GQA_HEAD_DIM = 128
GQA_Q_W = GQA_HEADS * GQA_HEAD_DIM
GQA_KV_W = GQA_KV_HEADS * GQA_HEAD_DIM

MLA_HEADS = 8
MLA_Q_RANK = 256
MLA_KV_RANK = 128
MLA_NOPE_DIM = 128
MLA_ROPE_DIM = 64
MLA_V_DIM = 128
MLA_QK_DIM = MLA_NOPE_DIM + MLA_ROPE_DIM
MLA_HEAD_PAD = 256
MLA_V_W = MLA_HEADS * MLA_V_DIM

OFF_GQA_K = GQA_Q_W
OFF_GQA_V = OFF_GQA_K + GQA_KV_W
OFF_MLA_QA = OFF_GQA_V + GQA_KV_W
OFF_MLA_KVA = OFF_MLA_QA + MLA_Q_RANK
OFF_GATE = OFF_MLA_KVA + MLA_KV_RANK + MLA_ROPE_DIM

N_EXPERTS = 32
TOP_K = 4
D_EXPERT = 1024
SWIGLU_LIMIT = 7.0
SWIGLU_ALPHA = 1.702
EXPERT_BLOCK = 512

LANES = 128
NEG_INF = float("-inf")
LOG2_E = math.log2(math.e)

BF16 = jnp.bfloat16
F32 = jnp.float32


def _dot(a, b):
    return jnp.dot(a, b, preferred_element_type=F32)


def _dot_nt(a, b):
    return lax.dot_general(a, b, (((1,), (1,)), ((), ())), preferred_element_type=F32)


def _split_bf16(a):
    hi = a.astype(BF16)
    lo = (a - hi.astype(F32)).astype(BF16)
    return hi, lo


def _rms(x, g, n):
    ms = jnp.sum(x * x, axis=-1, keepdims=True) * (1.0 / n)
    return x * lax.rsqrt(ms + EPS) * g


def _swap_halves(x, k):
    lane = lax.broadcasted_iota(jnp.int32, x.shape, 1)
    return jnp.where((lane & k) != 0, pltpu.roll(x, k, 1), pltpu.roll(x, LANES - k, 1))


def _rope(x, cos, sin_signed, k):
    return x * cos + _swap_halves(x, k) * sin_signed


def _ada_kernel(c_ref, w_ref, b_ref, o_ref):
    c = c_ref[...]
    s = c * (1.0 / (1.0 + jnp.exp(-c)))
    s_hi, s_lo = _split_bf16(s)
    w_hi, w_lo = _split_bf16(w_ref[...])
    o_ref[...] = _dot(s_hi, w_hi) + (_dot(s_hi, w_lo) + _dot(s_lo, w_hi)) + b_ref[...]


def _ada_mod(cc, ada_w, ada_b):
    n = ada_w.shape[1]
    tn = 1024
    return pl.pallas_call(
        _ada_kernel,
        out_shape=jax.ShapeDtypeStruct((cc.shape[0], n), F32),
        grid=(n // tn,),
        in_specs=[
            pl.BlockSpec((cc.shape[0], D_MODEL), lambda j: (0, 0)),
            pl.BlockSpec((D_MODEL, tn), lambda j: (0, j)),
            pl.BlockSpec((1, tn), lambda j: (0, j)),
        ],
        out_specs=pl.BlockSpec((cc.shape[0], tn), lambda j: (0, j)),
        compiler_params=pltpu.CompilerParams(dimension_semantics=("parallel",)),
        name="ada_mod",
    )(cc, ada_w, ada_b.reshape(1, n))


def _proj_kernel(*refs, with_q):
    if with_q:
        (x_ref, mod_ref, nmix_ref, cg_ref, sg_ref, cm_ref, sm_ref,
         wq_ref, wk_ref, wv_ref, wqa_ref, wckv_ref, wkr_ref, wg_ref, wqb_ref, wkb_ref, wvb_ref,
         gq_ref, gk_ref, gqa_ref, gkva_ref, gmq_ref, gmk_ref,
         q_ref, k_ref, v_ref, qm_ref, km_ref, vm_ref, gate_ref) = refs
    else:
        (x_ref, mod_ref, nmix_ref, cg_ref, sg_ref, cm_ref, sm_ref,
         wk_ref, wv_ref, wckv_ref, wkr_ref, wkb_ref, wvb_ref,
         gk_ref, gkva_ref, gmk_ref,
         k_ref, v_ref, km_ref, vm_ref) = refs

    x = x_ref[0]
    mod = mod_ref[0]
    shift, scale = mod[0:1, :], mod[1:2, :]
    h = _rms(x, nmix_ref[...], D_MODEL) * (1.0 + scale) + shift
    hb = h.astype(BF16)
    cg, sg = cg_ref[...], sg_ref[...]
    cm, sm = cm_ref[...], sm_ref[...]

    kk = _dot(hb, wk_ref[...])
    for j in range(GQA_KV_HEADS):
        sl = slice(j * GQA_HEAD_DIM, (j + 1) * GQA_HEAD_DIM)
        kn = _rms(kk[:, sl], gk_ref[...], GQA_HEAD_DIM)
        k_ref[0, :, sl] = _rope(kn, cg, sg, 32).astype(BF16)
    v_ref[0] = _dot_nt(wv_ref[...], hb).astype(BF16)

    ckv = _rms(_dot(hb, wckv_ref[...]), gkva_ref[...], MLA_KV_RANK).astype(BF16)
    vm_ref[0] = _dot_nt(wvb_ref[...], ckv).astype(BF16)
    knope = _dot(ckv, wkb_ref[...])
    kr = _dot(hb, wkr_ref[...])
    gmk = gmk_ref[...]
    g_nope, g_rope = gmk[:, :MLA_NOPE_DIM], gmk[:, MLA_NOPE_DIM:]
    ssq_r = jnp.sum(kr * kr, axis=-1, keepdims=True)
    kr_roped = _rope(kr * g_rope, cm, sm, 16)
    for j in range(MLA_HEADS):
        kn = knope[:, j * MLA_NOPE_DIM:(j + 1) * MLA_NOPE_DIM]
        ms = (jnp.sum(kn * kn, axis=-1, keepdims=True) + ssq_r) * (1.0 / MLA_QK_DIM)
        r = lax.rsqrt(ms + EPS)
        base = j * MLA_HEAD_PAD
        km_ref[0, :, base:base + MLA_NOPE_DIM] = (kn * r * g_nope).astype(BF16)
        km_ref[0, :, base + MLA_NOPE_DIM:base + MLA_HEAD_PAD] = (kr_roped * r).astype(BF16)

    if not with_q:
        return

    qq = _dot(hb, wq_ref[...])
    q_scale = GQA_HEAD_DIM ** -0.5 * LOG2_E
    for j in range(GQA_HEADS):
        sl = slice(j * GQA_HEAD_DIM, (j + 1) * GQA_HEAD_DIM)
        qn = _rms(qq[:, sl], gq_ref[...], GQA_HEAD_DIM)
        q_ref[0, :, sl] = (_rope(qn, cg, sg, 32) * q_scale).astype(BF16)

    qa = _rms(_dot(hb, wqa_ref[...]), gqa_ref[...], MLA_Q_RANK).astype(BF16)
    q2 = _dot(qa, wqb_ref[...])
    gmq = gmq_ref[...]
    gq_nope, gq_rope = gmq[:, :MLA_NOPE_DIM], gmq[:, MLA_NOPE_DIM:]
    m_scale = MLA_QK_DIM ** -0.5 * LOG2_E
    for j in range(MLA_HEADS):
        base = j * MLA_HEAD_PAD
        qn = q2[:, base:base + MLA_NOPE_DIM]
        qr = q2[:, base + MLA_NOPE_DIM:base + MLA_HEAD_PAD]
        ms = (jnp.sum(qn * qn, axis=-1, keepdims=True) + jnp.sum(qr * qr, axis=-1, keepdims=True)) * (1.0 / MLA_QK_DIM)
        r = lax.rsqrt(ms + EPS)
        qm_ref[0, :, base:base + MLA_NOPE_DIM] = (qn * r * gq_nope * m_scale).astype(BF16)
        qm_ref[0, :, base + MLA_NOPE_DIM:base + MLA_HEAD_PAD] = (_rope(qr * r * gq_rope, cm, sm, 16) * m_scale).astype(BF16)

    gl = _dot(hb, wg_ref[...])
    gate_ref[0] = (1.0 / (1.0 + jnp.exp(-gl))).astype(BF16)


def _proj(x, mod3, mod_row_of_batch, tables, w, with_q, tm):
    b, l, _ = x.shape
    cg, sg, cm, sm = tables
    const = lambda shape: pl.BlockSpec(shape, lambda bi, i: (0,) * len(shape), pipeline_mode=pl.Buffered(1))
    tab = pl.BlockSpec((tm, LANES), lambda bi, i: (i, 0))
    in_specs = [
        pl.BlockSpec((1, tm, D_MODEL), lambda bi, i: (bi, i, 0)),
        pl.BlockSpec((1, 6, D_MODEL), lambda bi, i: (mod_row_of_batch(bi), 0, 0)),
        const((1, D_MODEL)), tab, tab, tab, tab,
    ]
    if with_q:
        weights = [w["wq"], w["wk"], w["wv"], w["wqa"], w["wckv"], w["wkr"], w["wg"], w["wqb"], w["wkb"], w["wvb"],
                   w["gq"], w["gk"], w["gqa"], w["gkva"], w["gmq"], w["gmk"]]
    else:
        weights = [w["wk"], w["wv"], w["wckv"], w["wkr"], w["wkb"], w["wvb"], w["gk"], w["gkva"], w["gmk"]]
    in_specs += [const(a.shape) for a in weights]

    def out(width):
        return jax.ShapeDtypeStruct((b, l, width), BF16), pl.BlockSpec((1, tm, width), lambda bi, i: (bi, i, 0))

    def out_t(width):
        return jax.ShapeDtypeStruct((b, width, l), BF16), pl.BlockSpec((1, width, tm), lambda bi, i: (bi, 0, i))

    outs = [out(GQA_KV_W), out_t(GQA_KV_W), out(MLA_HEADS * MLA_HEAD_PAD), out_t(MLA_V_W)]
    if with_q:
        outs = [out(GQA_Q_W)] + outs[:2] + [out(MLA_HEADS * MLA_HEAD_PAD)] + outs[2:] + [out(2 * D_MODEL)]
    return pl.pallas_call(
        functools.partial(_proj_kernel, with_q=with_q),
        out_shape=[o[0] for o in outs],
        grid=(b, l // tm),
        in_specs=in_specs,
        out_specs=[o[1] for o in outs],
        compiler_params=pltpu.CompilerParams(dimension_semantics=("parallel", "parallel"),
                                             vmem_limit_bytes=56 * 1024 * 1024),
        name="proj_latent" if with_q else "proj_ctx",
    )(x, mod3, w["norm_mix"], cg, sg, cm, sm, *weights)


def _attn_kernel(q_ref, kl_ref, vl_ref, kc_ref, vc_ref, o_ref, *, tk, n_lat, bounded):
    q = q_ref[0]
    tq = q.shape[0]
    dv = vl_ref.shape[1]
    n = n_lat // tk

    if bounded:
        l = jnp.zeros((1, tq), F32)
        acc = jnp.zeros((dv, tq), F32)
        for k, vt in [(kl_ref[0, j * tk:(j + 1) * tk, :], vl_ref[0, :, j * tk:(j + 1) * tk]) for j in range(n)] + [
                (kc_ref[0], vc_ref[0])]:
            p = jnp.exp2(_dot_nt(k, q))
            l = l + jnp.sum(p, axis=0, keepdims=True)
            acc = acc + _dot(vt, p.astype(BF16))
        o_ref[0] = (acc / l).T.astype(BF16)
        return

    def update(carry, s, vt):
        m, l, acc = carry
        m_new = jnp.maximum(m, jnp.max(s, axis=0, keepdims=True))
        alpha = jnp.exp2(m - m_new)
        p = jnp.exp2(s - m_new)
        l = alpha * l + jnp.sum(p, axis=0, keepdims=True)
        return m_new, l, alpha * acc + _dot(vt, p.astype(BF16))

    n = n_lat // tk
    keys = [kl_ref[0, j * tk:(j + 1) * tk, :] for j in range(n)] + [kc_ref[0]]
    vals = [vl_ref[0, :, j * tk:(j + 1) * tk] for j in range(n)] + [vc_ref[0]]
    carry = (jnp.full((1, tq), NEG_INF, F32), jnp.zeros((1, tq), F32), jnp.zeros((dv, tq), F32))
    s = _dot_nt(keys[0], q)
    for j in range(n + 1):
        s_next = _dot_nt(keys[j + 1], q) if j < n else None
        carry = update(carry, s, vals[j])
        s = s_next
    m, l, acc = carry
    o_ref[0] = (acc / l).T.astype(BF16)


SOFTMAX_SAFE_EXPONENT = 56.0


def _score_bound(gain_q, gain_k, dim):
    return dim * jnp.max(jnp.abs(gain_q)) * jnp.max(jnp.abs(gain_k)) * (dim ** -0.5 * LOG2_E) * 1.02


def _attention(q, k_lat, v_lat, k_ctx, v_ctx, n_heads, group, d_qk, dv, tq, tk, name, score_bound):
    run = functools.partial(_attention_call, q, k_lat, v_lat, k_ctx, v_ctx, n_heads, group, d_qk, dv, tq, tk, name)
    return lax.cond(score_bound <= SOFTMAX_SAFE_EXPONENT, lambda: run(True), lambda: run(False))


def _attention_call(q, k_lat, v_lat, k_ctx, v_ctx, n_heads, group, d_qk, dv, tq, tk, name, bounded):
    b, l, _ = q.shape
    lc = k_ctx.shape[1]
    name = name + ("_bounded" if bounded else "_online")
    return pl.pallas_call(
        functools.partial(_attn_kernel, tk=tk, n_lat=l, bounded=bounded),
        out_shape=jax.ShapeDtypeStruct((b, l, n_heads * dv), BF16),
        grid=(b, n_heads, l // tq),
        in_specs=[
            pl.BlockSpec((1, tq, d_qk), lambda bi, h, i: (bi, i, h)),
            pl.BlockSpec((1, l, d_qk), lambda bi, h, i: (bi, 0, h // group)),
            pl.BlockSpec((1, dv, l), lambda bi, h, i: (bi, h // group, 0)),
            pl.BlockSpec((1, lc, d_qk), lambda bi, h, i: (bi, 0, h // group)),
            pl.BlockSpec((1, dv, lc), lambda bi, h, i: (bi, h // group, 0)),
        ],
        out_specs=pl.BlockSpec((1, tq, dv), lambda bi, h, i: (bi, i, h)),
        compiler_params=pltpu.CompilerParams(dimension_semantics=("parallel", "parallel", "parallel"),
                                             vmem_limit_bytes=56 * 1024 * 1024),
        name=name,
    )(q, k_lat, v_lat, k_ctx, v_ctx)


def _merge_kernel(oa_ref, ob_ref, gate_ref, x_ref, mod_ref, woa_ref, wob_ref, wout_ref, nffn_ref,
                  rwh_ref, rwl_ref, rb_ref, x1_ref, h2_ref, eidx_ref, egate_ref, rank_ref, before_ref, tcnt_ref,
                  cnt_ref, carry_ref):
    @pl.when((pl.program_id(0) == 0) & (pl.program_id(1) == 0))
    def _():
        carry_ref[...] = jnp.zeros_like(carry_ref)

    mod = mod_ref[0]
    g1, shift2, scale2 = mod[2:3, :], mod[3:4, :], mod[4:5, :]
    ya = _dot(oa_ref[0], woa_ref[...])
    yb = _dot(ob_ref[0], wob_ref[...])
    g = gate_ref[0].astype(F32)
    y = g[:, :D_MODEL] * ya + g[:, D_MODEL:] * yb
    z = _dot(y.astype(BF16), wout_ref[...])
    x1 = x_ref[0] + g1 * z
    x1_ref[0] = x1
    h2 = _rms(x1, nffn_ref[...], D_MODEL) * (1.0 + scale2) + shift2
    h2_ref[0] = h2.astype(BF16)

    h_hi, h_lo = _split_bf16(h2)
    logits = _dot(h_hi, rwh_ref[...]) + (_dot(h_hi, rwl_ref[...]) + _dot(h_lo, rwh_ref[...])) + rb_ref[...]
    lane = lax.broadcasted_iota(jnp.int32, logits.shape, 1).astype(F32)
    cur = jnp.where(lane < N_EXPERTS, logits, NEG_INF)
    vals, idxs = [], []
    for _ in range(TOP_K):
        mx = jnp.max(cur, axis=-1, keepdims=True)
        ix = jnp.min(jnp.where(cur == mx, lane, float(LANES)), axis=-1, keepdims=True)
        vals.append(mx)
        idxs.append(ix)
        cur = jnp.where(lane == ix, NEG_INF, cur)
    ex = [jnp.exp(v - vals[0]) for v in vals]
    den = ex[0] + ex[1] + ex[2] + ex[3]
    eidx = jnp.zeros(logits.shape, F32)
    egate = jnp.zeros(logits.shape, F32)
    for k in range(TOP_K):
        eidx = jnp.where(lane == k, idxs[k], eidx)
        egate = jnp.where(lane == k, ex[k] / den, egate)
    eidx = eidx.astype(jnp.int32)
    eidx_ref[0] = eidx
    egate_ref[0] = egate

    for s in range(eidx.shape[0] // MOE_TILE):
        rows = slice(s * MOE_TILE, (s + 1) * MOE_TILE)
        before_ref[s] = carry_ref[...]
        rank, tile_cnt = _route_tile(eidx[rows], carry_ref)
        rank_ref[0, rows, :] = rank
        tcnt_ref[s] = jnp.broadcast_to(tile_cnt, carry_ref.shape)
    cnt_ref[...] = carry_ref[...]


def _merge(o_a, o_b, gates, x, mod3, w, tm):
    b, l, _ = x.shape
    const = lambda shape: pl.BlockSpec(shape, lambda bi, i: (0,) * len(shape), pipeline_mode=pl.Buffered(1))
    tok = lambda width: pl.BlockSpec((1, tm, width), lambda bi, i: (bi, i, 0))
    assert tm % MOE_TILE == 0
    sub = tm // MOE_TILE
    per_tile = jax.ShapeDtypeStruct((b * l // MOE_TILE, 8, LANES), F32)
    tile_spec = pl.BlockSpec((sub, 8, LANES), lambda bi, i: (bi * (l // tm) + i, 0, 0))
    return pl.pallas_call(
        _merge_kernel,
        out_shape=[jax.ShapeDtypeStruct((b, l, D_MODEL), F32), jax.ShapeDtypeStruct((b, l, D_MODEL), BF16),
                   jax.ShapeDtypeStruct((b, l, LANES), jnp.int32), jax.ShapeDtypeStruct((b, l, LANES), F32),
                   jax.ShapeDtypeStruct((b, l, LANES), jnp.int32), per_tile, per_tile,
                   jax.ShapeDtypeStruct((8, LANES), F32)],
        grid=(b, l // tm),
        in_specs=[tok(GQA_Q_W), tok(MLA_V_W), tok(2 * D_MODEL), tok(D_MODEL),
                  pl.BlockSpec((1, 6, D_MODEL), lambda bi, i: (bi, 0, 0)),
                  const((GQA_Q_W, D_MODEL)), const((MLA_V_W, D_MODEL)), const((D_MODEL, D_MODEL)), const((1, D_MODEL)),
                  const((D_MODEL, LANES)), const((D_MODEL, LANES)), const((1, LANES))],
        out_specs=[tok(D_MODEL), tok(D_MODEL), tok(LANES), tok(LANES), tok(LANES), tile_spec, tile_spec,
                   pl.BlockSpec((8, LANES), lambda bi, i: (0, 0))],
        scratch_shapes=[pltpu.VMEM((8, LANES), F32)],
        compiler_params=pltpu.CompilerParams(dimension_semantics=("arbitrary", "arbitrary"),
                                             vmem_limit_bytes=56 * 1024 * 1024),
        name="merge_router",
    )(o_a, o_b, gates, x, mod3, w["woa"], w["wob"], w["wout"], w["norm_ffn"], w["rw_hi"], w["rw_lo"], w["rb"])


def _onehots(idx, lane):
    return [lane == idx[:, k:k + 1] for k in range(TOP_K)]


SEG_ALIGN = 16


def _route_tile(idx, carry_ref):
    tm = idx.shape[0]
    lane = lax.broadcasted_iota(jnp.int32, idx.shape, 1)
    oh = _onehots(idx, lane)
    total = jnp.zeros(idx.shape, F32)
    for k in range(TOP_K):
        total = total + jnp.where(oh[k], 1.0, 0.0)
    row = lax.broadcasted_iota(jnp.int32, (tm, tm), 0)
    col = lax.broadcasted_iota(jnp.int32, (tm, tm), 1)
    tri = jnp.where(row > col, 1.0, 0.0).astype(BF16)
    before = _dot(tri, total.astype(BF16)) + carry_ref[0:1, :]
    rank = jnp.where(lane < TOP_K, jnp.take_along_axis(before, idx, axis=1), 0.0)
    tile_cnt = jnp.sum(total, axis=0, keepdims=True)
    carry_ref[...] = carry_ref[...] + tile_cnt
    return rank.astype(jnp.int32), tile_cnt


def _dest_tables(idx, rank, region_starts, seg_starts):
    valid = lax.broadcasted_iota(jnp.int32, idx.shape, 1) < TOP_K
    region_start = jnp.take_along_axis(jnp.broadcast_to(region_starts, idx.shape), idx, axis=1)
    seg_start = jnp.take_along_axis(jnp.broadcast_to(seg_starts, idx.shape), idx, axis=1)
    rank = rank.astype(F32)
    window_start = jnp.floor(seg_start * (1.0 / SEG_ALIGN)) * SEG_ALIGN
    local = rank - window_start
    dest = jnp.where(valid, region_start + rank, 0.0)
    col = jnp.where(valid & (local < WIN), idx.astype(F32) * WIN + local, -1.0)
    return dest, col


MOE_TILE = 256
WIN = 64
N_MAIN_ROWS = N_EXPERTS * WIN
MAIN_CHUNK = 512
WIN_PER_STACK = MOE_TILE // WIN
MAX_WINDOWS = (MOE_TILE * TOP_K + N_EXPERTS * SEG_ALIGN) // WIN
CARRY_ROWS = N_EXPERTS * SEG_ALIGN
WORKLIST_LEN = -(-(MAX_WINDOWS + WIN_PER_STACK) // WIN_PER_STACK) * WIN_PER_STACK


def _build_worklist(ts_ref, tc_ref, wl_ref, junk_row):
    base = pl.program_id(0) * N_EXPERTS

    def per_expert(e, n):
        first = ts_ref[base + e]
        n_win = (tc_ref[base + e] + (WIN - 1)) // WIN

        def per_window(wi, n):
            wl_ref[n] = first + wi * WIN
            return n + 1

        return lax.fori_loop(1, n_win, per_window, n)

    n = lax.fori_loop(0, N_EXPERTS, per_expert, 0)
    n_stacks = (n + (WIN_PER_STACK - 1)) // WIN_PER_STACK

    def pad(j, c):
        wl_ref[j] = junk_row + (j % WIN_PER_STACK) * WIN
        return c

    lax.fori_loop(n, n_stacks * WIN_PER_STACK, pad, 0)
    return n_stacks


def _seg_aligned(row):
    return row if isinstance(row, int) else pl.multiple_of(row, SEG_ALIGN)


def _stack_row_ids(wl_ref, stack, shape, axis):
    pos = lax.broadcasted_iota(jnp.int32, shape, axis)
    row = jnp.full(shape, -1, jnp.int32)
    for wi in range(WIN_PER_STACK - 1, -1, -1):
        row = jnp.where(pos < (wi + 1) * WIN, wl_ref[stack * WIN_PER_STACK + wi] + (pos - wi * WIN), row)
    return row


def _select_rows(row_id, id_t):
    shape = (row_id.shape[0], id_t.shape[1])
    sel = jnp.zeros(shape, F32)
    for k in range(TOP_K):
        hit = row_id == id_t[k:k + 1, :]
        sel = jnp.where(hit, 1.0, sel)
    return sel.astype(BF16)


MAIN_PENDING = 2


TAIL_ROWS = WIN + EXPERT_BLOCK


def _dispatch_kernel(ts_ref, tc_ref, ov_ref, re_ref, h_ref, eidx_ref, rank_ref, start_ref, before_ref,
                     buf_ref, dest_ref, col_ref, wl_ref, pend_ref, main_x, carry_x, stage_x, main_sem, sem, *,
                     junk_row):
    i = pl.program_id(0)
    base = i * N_EXPERTS
    next_base = jnp.minimum(i + 1, pl.num_programs(0) - 1) * N_EXPERTS

    @pl.when(i == 0)
    def _():
        pend_ref[0] = 0
        pend_ref[1] = 0
        pend_ref[MAIN_PENDING] = 0
        carry_x[...] = jnp.zeros_like(carry_x)
        main_x[0:TAIL_ROWS, :] = jnp.zeros((TAIL_ROWS, D_MODEL), BF16)
        cps = []
        for e in range(N_EXPERTS):
            rows = pl.ds(pl.multiple_of(jnp.maximum(re_ref[e] - TAIL_ROWS, 0), SEG_ALIGN), TAIL_ROWS)
            cps.append(pltpu.make_async_copy(main_x.at[pl.ds(0, TAIL_ROWS)], buf_ref.at[rows], main_sem))
        for cp in cps:
            cp.start()
        for cp in cps:
            cp.wait()

        def zero_block_copies(blk):
            rows = pl.ds(pl.multiple_of(blk * EXPERT_BLOCK, EXPERT_BLOCK), EXPERT_BLOCK)
            return (pltpu.make_async_copy(main_x.at[pl.ds(0, EXPERT_BLOCK)], buf_ref.at[rows], main_sem),)

        def start_zero(blk, c):
            for cp in zero_block_copies(blk):
                cp.start()
            return c

        def wait_zero(blk, c):
            for cp in zero_block_copies(blk):
                cp.wait()
            return c

        first_free, n_blocks = re_ref[N_EXPERTS - 1] // EXPERT_BLOCK, (junk_row + EXPERT_BLOCK) // EXPERT_BLOCK
        lax.fori_loop(first_free, n_blocks, start_zero, 0)
        lax.fori_loop(first_free, n_blocks, wait_zero, 0)

    h = h_ref[...]

    def main_copies(first_row):
        cps = []
        for e in range(N_EXPERTS):
            rows = pl.ds(_seg_aligned(first_row(e)), WIN)
            cps.append(pltpu.make_async_copy(main_x.at[pl.ds(e * WIN, WIN)], buf_ref.at[rows], main_sem))
        return cps

    def drain_main():
        @pl.when(pend_ref[MAIN_PENDING] == 1)
        def _():
            for cp in main_copies(lambda e: 0):
                cp.wait()
            pend_ref[MAIN_PENDING] = 0

    dest, col = _dest_tables(eidx_ref[...], rank_ref[...], start_ref[0:1, :], before_ref[0][0:1, :])
    dest_ref[...] = dest.astype(jnp.int32)
    col_ref[...] = col.astype(jnp.int32)
    dest_t = dest.T[0:8, :].astype(jnp.int32)
    col_t = col.T[0:8, :].astype(jnp.int32)
    pieces = []
    for c in range(N_MAIN_ROWS // MAIN_CHUNK):
        row_id = lax.broadcasted_iota(jnp.int32, (MAIN_CHUNK, 1), 0) + c * MAIN_CHUNK
        pieces.append(_dot(_select_rows(row_id, col_t), h).astype(BF16))
    group_pos = lax.broadcasted_iota(jnp.int32, (SEG_ALIGN, 1), 0)
    end_group = jnp.concatenate([ts_ref[next_base + e] + group_pos for e in range(N_EXPERTS)], axis=0)
    end_rows = _dot(_select_rows(end_group, dest_t), h).astype(BF16)

    def window_copies(slot, first_row):
        cps = []
        for wi in range(WIN_PER_STACK):
            rows = pl.ds(_seg_aligned(first_row(wi)), WIN)
            cps.append(pltpu.make_async_copy(stage_x.at[slot, pl.ds(wi * WIN, WIN)], buf_ref.at[rows], sem.at[slot]))
        return cps

    def drain(slot):
        @pl.when(pend_ref[slot] == 1)
        def _():
            for cp in window_copies(slot, lambda wi: 0):
                cp.wait()
            pend_ref[slot] = 0

    drain_main()
    drain(0)
    drain(1)
    for c, packed in enumerate(pieces):
        main_x[c * MAIN_CHUNK:(c + 1) * MAIN_CHUNK, :] = packed
    for e in range(N_EXPERTS):
        win, grp = pl.ds(e * WIN, SEG_ALIGN), pl.ds(e * SEG_ALIGN, SEG_ALIGN)
        main_x[win, :] = main_x[win, :] + carry_x[grp, :]
        same = ts_ref[next_base + e] == ts_ref[base + e]
        carry_x[grp, :] = end_rows[e * SEG_ALIGN:(e + 1) * SEG_ALIGN, :] + jnp.where(same, carry_x[grp, :], 0).astype(BF16)
    for n, cp in enumerate(main_copies(lambda e: ts_ref[base + e])):
        cp.start(priority=n % 2)
    pend_ref[MAIN_PENDING] = 1

    @pl.when(ov_ref[i] > 0)
    def _():
        n_stacks = _build_worklist(ts_ref, tc_ref, wl_ref, junk_row)

        def stack_body(s, c):
            slot = s & 1
            rows = _dot(_select_rows(_stack_row_ids(wl_ref, s, (MOE_TILE, 1), 0), dest_t), h)
            drain(slot)
            stage_x[slot] = rows.astype(BF16)
            for cp in window_copies(slot, lambda wi: wl_ref[s * WIN_PER_STACK + wi]):
                cp.start()
            pend_ref[slot] = 1
            return c

        lax.fori_loop(0, n_stacks, stack_body, 0)

    @pl.when(i == pl.num_programs(0) - 1)
    def _():
        drain_main()
        drain(0)
        drain(1)


def _moe_grid_spec(n_tiles, in_specs, out_specs, scratch_shapes):
    return pltpu.PrefetchScalarGridSpec(num_scalar_prefetch=4, grid=(n_tiles,), in_specs=in_specs,
                                        out_specs=out_specs, scratch_shapes=scratch_shapes)


def _dispatch(tile_start, tile_cnt, tile_ovf, region_ends, h2, eidx, rank, start, before, n_rows):
    t = h2.shape[0]
    n_tiles = t // MOE_TILE
    junk_row = n_rows - EXPERT_BLOCK
    tok = lambda width: pl.BlockSpec((MOE_TILE, width), lambda i, *_: (i, 0))
    table = jax.ShapeDtypeStruct((t, LANES), jnp.int32)
    return pl.pallas_call(
        functools.partial(_dispatch_kernel, junk_row=junk_row),
        out_shape=[jax.ShapeDtypeStruct((n_rows, D_MODEL), BF16), table, table],
        grid_spec=_moe_grid_spec(
            n_tiles,
            [tok(D_MODEL), tok(LANES), tok(LANES), pl.BlockSpec((8, LANES), lambda i, *_: (0, 0)),
             pl.BlockSpec((1, 8, LANES), lambda i, *_: (i, 0, 0))],
            [pl.BlockSpec(memory_space=pl.ANY), tok(LANES), tok(LANES)],
            [pltpu.SMEM((WORKLIST_LEN,), jnp.int32), pltpu.SMEM((3,), jnp.int32),
             pltpu.VMEM((N_MAIN_ROWS, D_MODEL), BF16), pltpu.VMEM((CARRY_ROWS, D_MODEL), BF16),
             pltpu.VMEM((2, MOE_TILE, D_MODEL), BF16),
             pltpu.SemaphoreType.DMA(()), pltpu.SemaphoreType.DMA((2,))]),
        compiler_params=pltpu.CompilerParams(dimension_semantics=("arbitrary",),
                                             vmem_limit_bytes=56 * 1024 * 1024),
        name="moe_dispatch",
    )(tile_start, tile_cnt, tile_ovf, region_ends, h2, eidx, rank, start, before)


def _select_cols(ids, gates, col_id):
    sel = jnp.zeros((ids.shape[0], col_id.shape[1]), F32)
    for k in range(TOP_K):
        sel = jnp.where(ids[:, k:k + 1] == col_id, gates[:, k:k + 1], sel)
    return sel.astype(BF16)


def _combine_kernel(ts_ref, tc_ref, ov_ref, re_ref, col_ref, dest_ref, egate_ref, x1_ref, mod_ref, y_ref, o_ref,
                    wl_ref, main_y, stage_y, acc_ref, main_sem, sem, *, junk_row):
    del re_ref
    i = pl.program_id(0)
    slot_i = i & 1

    def main_copies(slot, first_row):
        return [pltpu.make_async_copy(y_ref.at[pl.ds(_seg_aligned(first_row(e)), WIN)],
                                      main_y.at[slot, pl.ds(e * WIN, WIN)], main_sem.at[slot])
                for e in range(N_EXPERTS)]

    def fetch_main(tile, slot):
        for n, cp in enumerate(main_copies(slot, lambda e: ts_ref[tile * N_EXPERTS + e])):
            cp.start(priority=n % 2)

    @pl.when(i == 0)
    def _():
        fetch_main(0, 0)

    @pl.when(i + 1 < pl.num_programs(0))
    def _():
        fetch_main(i + 1, 1 - slot_i)

    for cp in main_copies(slot_i, lambda e: 0):
        cp.wait()

    col = col_ref[...]
    egate = egate_ref[...]
    sel = jnp.concatenate(
        [_select_cols(col, egate, lax.broadcasted_iota(jnp.int32, (1, MAIN_CHUNK), 1) + c * MAIN_CHUNK)
         for c in range(N_MAIN_ROWS // MAIN_CHUNK)], axis=1)
    acc_ref[...] = _dot(sel, main_y[slot_i])

    @pl.when(ov_ref[i] > 0)
    def _():
        n_stacks = _build_worklist(ts_ref, tc_ref, wl_ref, junk_row)
        dest = dest_ref[...]
        if WIN_PER_STACK * WIN < MOE_TILE:
            for slot in range(2):
                stage_y[slot, WIN_PER_STACK * WIN:, :] = jnp.zeros((MOE_TILE - WIN_PER_STACK * WIN, D_MODEL), BF16)

        def window_copies(slot, first_row):
            return [pltpu.make_async_copy(y_ref.at[pl.ds(_seg_aligned(first_row(wi)), WIN)],
                                          stage_y.at[slot, pl.ds(wi * WIN, WIN)], sem.at[slot])
                    for wi in range(WIN_PER_STACK)]

        def fetch(s, slot):
            for cp in window_copies(slot, lambda wi: wl_ref[s * WIN_PER_STACK + wi]):
                cp.start()

        @pl.when(n_stacks > 0)
        def _():
            fetch(0, 0)

        def stack_body(s, c):
            slot = s & 1

            @pl.when(s + 1 < n_stacks)
            def _():
                fetch(s + 1, 1 - slot)

            for cp in window_copies(slot, lambda wi: 0):
                cp.wait()
            sel_o = _select_cols(dest, egate, _stack_row_ids(wl_ref, s, (1, MOE_TILE), 1))
            acc_ref[...] += _dot(sel_o, stage_y[slot])
            return c

        lax.fori_loop(0, n_stacks, stack_body, 0)

    g2 = mod_ref[0][5:6, :]
    o_ref[...] = x1_ref[...] + g2 * acc_ref[...]


def _combine(tile_start, tile_cnt, tile_ovf, region_ends, col, dest, egate, x1, mod3, y, l):
    t = x1.shape[0]
    per_batch = l // MOE_TILE
    junk_row = y.shape[0] - EXPERT_BLOCK
    tok = lambda width: pl.BlockSpec((MOE_TILE, width), lambda i, *_: (i, 0))
    return pl.pallas_call(
        functools.partial(_combine_kernel, junk_row=junk_row),
        out_shape=jax.ShapeDtypeStruct((t, D_MODEL), F32),
        grid_spec=_moe_grid_spec(
            t // MOE_TILE,
            [tok(LANES), tok(LANES), tok(LANES), tok(D_MODEL),
             pl.BlockSpec((1, 6, D_MODEL), lambda i, *_: (i // per_batch, 0, 0)),
             pl.BlockSpec(memory_space=pl.ANY)],
            tok(D_MODEL),
            [pltpu.SMEM((WORKLIST_LEN,), jnp.int32), pltpu.VMEM((2, N_MAIN_ROWS, D_MODEL), BF16),
             pltpu.VMEM((2, MOE_TILE, D_MODEL), BF16), pltpu.VMEM((MOE_TILE, D_MODEL), F32),
             pltpu.SemaphoreType.DMA((2,)), pltpu.SemaphoreType.DMA((2,))]),
        compiler_params=pltpu.CompilerParams(dimension_semantics=("arbitrary",),
                                             vmem_limit_bytes=56 * 1024 * 1024),
        name="moe_combine",
    )(tile_start, tile_cnt, tile_ovf, region_ends, col, dest, egate, x1, mod3, y)


def _expert_kernel(be_ref, nused_ref, x_ref, w1_ref, b1_ref, w2_ref, b2_ref, y_ref, w1b_ref, w2b_ref):
    i = pl.program_id(0)
    live = i < nused_ref[0]

    @pl.when(live & ((i == 0) | (be_ref[i] != be_ref[jnp.maximum(i - 1, 0)])))
    def _():
        w1b_ref[...] = w1_ref[0].astype(BF16)
        w2b_ref[...] = w2_ref[0].astype(BF16)

    @pl.when(live)
    def _():
        gu = _dot(x_ref[...], w1b_ref[...]) + b1_ref[0]
        glu = jnp.minimum(gu[:, :D_EXPERT], SWIGLU_LIMIT)
        lin = jnp.clip(gu[:, D_EXPERT:], -SWIGLU_LIMIT, SWIGLU_LIMIT)
        act = glu * (1.0 / (1.0 + jnp.exp(-SWIGLU_ALPHA * glu))) * (lin + 1.0)
        y_ref[...] = (_dot(act.astype(BF16), w2b_ref[...]) + b2_ref[0]).astype(BF16)

    @pl.when(pl.program_id(0) >= nused_ref[0])
    def _():
        y_ref[...] = jnp.zeros_like(y_ref)


def _experts(block_e, nused, buf, w1, b1, w2, b2):
    nb = buf.shape[0] // EXPERT_BLOCK
    row = lambda i, be, nu: (jnp.minimum(i, nu[0] - 1), 0)
    out_row = lambda i, be, nu: (i, 0)
    exp3 = lambda i, be, nu: (be[jnp.minimum(i, nu[0] - 1)], 0, 0)
    return pl.pallas_call(
        _expert_kernel,
        out_shape=jax.ShapeDtypeStruct((buf.shape[0], D_MODEL), BF16),
        grid_spec=pltpu.PrefetchScalarGridSpec(
            num_scalar_prefetch=2,
            grid=(nb,),
            in_specs=[pl.BlockSpec((EXPERT_BLOCK, D_MODEL), row),
                      pl.BlockSpec((1, D_MODEL, 2 * D_EXPERT), exp3),
                      pl.BlockSpec((1, 1, 2 * D_EXPERT), exp3),
                      pl.BlockSpec((1, D_EXPERT, D_MODEL), exp3),
                      pl.BlockSpec((1, 1, D_MODEL), exp3)],
            out_specs=pl.BlockSpec((EXPERT_BLOCK, D_MODEL), out_row),
            scratch_shapes=[pltpu.VMEM((D_MODEL, 2 * D_EXPERT), BF16), pltpu.VMEM((D_EXPERT, D_MODEL), BF16)]),
        compiler_params=pltpu.CompilerParams(dimension_semantics=("arbitrary",),
                                             vmem_limit_bytes=56 * 1024 * 1024),
        name="moe_experts",
    )(block_e, nused, buf, w1, b1, w2, b2)


def _rope_tables(seq_len):
    pos = jnp.arange(seq_len, dtype=jnp.int32)
    row = (pos // GRID_W).astype(F32)[:, None]
    col = (pos % GRID_W).astype(F32)[:, None]

    def table(d_axis, pad):
        inv_freq = ROPE_THETA ** (-jnp.arange(0, d_axis, 2, dtype=F32) / d_axis)
        ar, ac = row * inv_freq[None, :], col * inv_freq[None, :]
        cos = jnp.concatenate([jnp.cos(ar), jnp.cos(ar), jnp.cos(ac), jnp.cos(ac)], axis=-1)
        sin = jnp.concatenate([-jnp.sin(ar), jnp.sin(ar), -jnp.sin(ac), jnp.sin(ac)], axis=-1)
        if pad:
            cos = jnp.concatenate([cos, jnp.ones((seq_len, pad), F32)], axis=-1)
            sin = jnp.concatenate([sin, jnp.zeros((seq_len, pad), F32)], axis=-1)
        return cos, sin

    cg, sg = table(GQA_HEAD_DIM // 2, 0)
    cm, sm = table(MLA_ROPE_DIM // 2, LANES - MLA_ROPE_DIM)
    return cg, sg, cm, sm


def _pad_lanes(a, width):
    return jnp.pad(a, [(0, 0)] * (a.ndim - 1) + [(0, width - a.shape[-1])])


def _prep_weights(p):
    w_in = p["w_in"]
    w = {"norm_mix": p["norm_mix"].reshape(1, D_MODEL), "norm_ffn": p["norm_ffn"].reshape(1, D_MODEL)}
    w["wq"] = w_in[:, :OFF_GQA_K].astype(BF16)
    w["wk"] = w_in[:, OFF_GQA_K:OFF_GQA_V].astype(BF16)
    w["wv"] = w_in[:, OFF_GQA_V:OFF_MLA_QA].T.astype(BF16)
    w["wqa"] = w_in[:, OFF_MLA_QA:OFF_MLA_KVA].astype(BF16)
    w["wckv"] = w_in[:, OFF_MLA_KVA:OFF_MLA_KVA + MLA_KV_RANK].astype(BF16)
    w["wkr"] = _pad_lanes(w_in[:, OFF_MLA_KVA + MLA_KV_RANK:OFF_GATE], LANES).astype(BF16)
    w["wg"] = w_in[:, OFF_GATE:].astype(BF16)
    wqb = p["mla_w_qb"].reshape(MLA_Q_RANK, MLA_HEADS, MLA_QK_DIM)
    w["wqb"] = _pad_lanes(wqb, MLA_HEAD_PAD).reshape(MLA_Q_RANK, MLA_HEADS * MLA_HEAD_PAD).astype(BF16)
    wkvb = p["mla_w_kvb"].reshape(MLA_KV_RANK, MLA_HEADS, MLA_NOPE_DIM + MLA_V_DIM)
    w["wkb"] = wkvb[:, :, :MLA_NOPE_DIM].reshape(MLA_KV_RANK, MLA_HEADS * MLA_NOPE_DIM).astype(BF16)
    w["wvb"] = wkvb[:, :, MLA_NOPE_DIM:].reshape(MLA_KV_RANK, MLA_V_W).T.astype(BF16)
    w["gq"] = p["gqa_q_norm"].reshape(1, GQA_HEAD_DIM)
    w["gk"] = p["gqa_k_norm"].reshape(1, GQA_HEAD_DIM)
    w["gqa"] = p["mla_q_a_norm"].reshape(1, MLA_Q_RANK)
    w["gkva"] = p["mla_kv_a_norm"].reshape(1, MLA_KV_RANK)
    w["gmq"] = _pad_lanes(p["mla_q_norm"].reshape(1, MLA_QK_DIM), MLA_HEAD_PAD)
    w["gmk"] = _pad_lanes(p["mla_k_norm"].reshape(1, MLA_QK_DIM), MLA_HEAD_PAD)
    w["woa"] = p["w_o_gqa"].astype(BF16)
    w["wob"] = p["w_o_mla"].astype(BF16)
    w["wout"] = p["w_out"].astype(BF16)
    rw = _pad_lanes(p["router_w"], LANES)
    w["rw_hi"] = rw.astype(BF16)
    w["rw_lo"] = (rw - w["rw_hi"].astype(F32)).astype(BF16)
    w["rb"] = _pad_lanes(p["router_b"].reshape(1, N_EXPERTS), LANES)
    return w


def _layer(x, c, ctx, c_ctx, p):
    b, l, _ = x.shape
    lc = ctx.shape[1]
    t = b * l
    w = _prep_weights(p)

    n_mod_rows = -(-(b + 1) // 8) * 8
    cc = jnp.zeros((n_mod_rows, D_MODEL), F32).at[:b].set(c).at[b].set(c_ctx)
    mod3 = _ada_mod(cc, p["ada_w"], p["ada_b"]).reshape(n_mod_rows, 6, D_MODEL)

    tables = _rope_tables(l)
    ident = (jnp.ones((lc, LANES), F32), jnp.zeros((lc, LANES), F32)) * 2
    tm = min(512, l)
    q_a, k_a, v_a, q_m, k_m, v_m, gates = _proj(x, mod3, lambda bi: bi, tables, w, True, tm)
    kc_a, vc_a, kc_m, vc_m = _proj(ctx, mod3, lambda bi: b, ident, w, False, min(256, lc))

    tq, tk = min(2048, l), min(2048, l)
    o_a = _attention(q_a, k_a, v_a, kc_a, vc_a, GQA_HEADS, GQA_GROUP, GQA_HEAD_DIM, GQA_HEAD_DIM, tq, tk, "attn_gqa",
                     _score_bound(p["gqa_q_norm"], p["gqa_k_norm"], GQA_HEAD_DIM))
    o_m = _attention(q_m, k_m, v_m, kc_m, vc_m, MLA_HEADS, 1, MLA_HEAD_PAD, MLA_V_DIM, tq, tk, "attn_mla",
                     _score_bound(p["mla_q_norm"], p["mla_k_norm"], MLA_QK_DIM))

    x1, h2, eidx, egate, rank, before_raw, tcnt, cnt = _merge(o_a, o_m, gates, x, mod3, w, tm)
    x1, h2 = x1.reshape(t, D_MODEL), h2.reshape(t, D_MODEL)
    eidx, egate, rank = eidx.reshape(t, LANES), egate.reshape(t, LANES), rank.reshape(t, LANES)

    assert t % MOE_TILE == 0
    counts = cnt[0, :N_EXPERTS].astype(jnp.int32)
    before = before_raw[:, 0, :N_EXPERTS].astype(jnp.int32)
    region = (counts + WIN + EXPERT_BLOCK - 1) // EXPERT_BLOCK * EXPERT_BLOCK
    region_ends = jnp.cumsum(region)
    region_starts = region_ends - region
    max_rows = t * TOP_K + N_EXPERTS * (WIN + EXPERT_BLOCK - 1)
    n_blocks = max_rows // EXPERT_BLOCK + 1
    n_rows = n_blocks * EXPERT_BLOCK
    block_row = jnp.arange(n_blocks, dtype=jnp.int32) * EXPERT_BLOCK
    block_e = jnp.minimum(jnp.sum(region_ends[None, :] <= block_row[:, None], axis=1), N_EXPERTS - 1).astype(jnp.int32)
    nused = (region_ends[-1:] // EXPERT_BLOCK).astype(jnp.int32)
    tile_start = (region_starts[None, :] + before // SEG_ALIGN * SEG_ALIGN).reshape(-1)
    tile_cnt = before % SEG_ALIGN + tcnt[:, 0, :N_EXPERTS].astype(jnp.int32)
    tile_ovf = jnp.sum(jnp.maximum((tile_cnt + WIN - 1) // WIN - 1, 0), axis=1).astype(jnp.int32)
    tile_cnt = tile_cnt.reshape(-1)
    start = jnp.zeros((8, LANES), F32).at[0, :N_EXPERTS].set(region_starts.astype(F32))
    region_ends = region_ends.astype(jnp.int32)
    buf, dest, col = _dispatch(tile_start, tile_cnt, tile_ovf, region_ends, h2, eidx, rank, start, before_raw, n_rows)
    y = _experts(block_e, nused, buf, p["expert_w1"], p["expert_b1"].reshape(N_EXPERTS, 1, -1),
                 p["expert_w2"], p["expert_b2"].reshape(N_EXPERTS, 1, -1))
    out = _combine(tile_start, tile_cnt, tile_ovf, region_ends, col, dest, egate, x1, mod3, y, l)
    return out.reshape(b, l, D_MODEL)


def kernel(x, c, ctx, c_ctx, ada_w, ada_b, norm_mix, norm_ffn, w_in, gqa_q_norm, gqa_k_norm, mla_q_a_norm, mla_kv_a_norm, mla_w_qb, mla_w_kvb, mla_q_norm, mla_k_norm, w_o_gqa, w_o_mla, w_out, router_w, router_b, expert_w1, expert_b1, expert_w2, expert_b2):
    assert ada_w.shape[0] == 1, "single-layer problem: the context stream is never updated"
    p = {
        "ada_w": ada_w[0], "ada_b": ada_b[0], "norm_mix": norm_mix[0], "norm_ffn": norm_ffn[0],
        "w_in": w_in[0], "gqa_q_norm": gqa_q_norm[0], "gqa_k_norm": gqa_k_norm[0],
        "mla_q_a_norm": mla_q_a_norm[0], "mla_kv_a_norm": mla_kv_a_norm[0],
        "mla_w_qb": mla_w_qb[0], "mla_w_kvb": mla_w_kvb[0],
        "mla_q_norm": mla_q_norm[0], "mla_k_norm": mla_k_norm[0],
        "w_o_gqa": w_o_gqa[0], "w_o_mla": w_o_mla[0], "w_out": w_out[0],
        "router_w": router_w[0], "router_b": router_b[0],
        "expert_w1": expert_w1[0], "expert_b1": expert_b1[0],
        "expert_w2": expert_w2[0], "expert_b2": expert_b2[0],
    }
    return _layer(x, c, ctx, c_ctx, p)
```

```python
import functools
import math

import jax
import jax.numpy as jnp
import numpy as np
from jax import lax
from jax.experimental import pallas as pl
from jax.experimental.pallas import tpu as pltpu

D_MODEL = 1024
GRID_W = 64
EPS = 1e-6
ROPE_THETA = 10000.0

GQA_HEADS = 8
GQA_KV_HEADS = 2
GQA_GROUP = GQA_HEADS // GQA_KV_HEADS
GQA_HEAD_DIM = 128
GQA_Q_W = GQA_HEADS * GQA_HEAD_DIM
GQA_KV_W = GQA_KV_HEADS * GQA_HEAD_DIM

MLA_HEADS = 8
MLA_Q_RANK = 256
MLA_KV_RANK = 128
MLA_NOPE_DIM = 128
MLA_ROPE_DIM = 64
MLA_V_DIM = 128
MLA_QK_DIM = MLA_NOPE_DIM + MLA_ROPE_DIM
MLA_HEAD_PAD = 256
MLA_V_W = MLA_HEADS * MLA_V_DIM

OFF_GQA_K = GQA_Q_W
OFF_GQA_V = OFF_GQA_K + GQA_KV_W
OFF_MLA_QA = OFF_GQA_V + GQA_KV_W
OFF_MLA_KVA = OFF_MLA_QA + MLA_Q_RANK
OFF_GATE = OFF_MLA_KVA + MLA_KV_RANK + MLA_ROPE_DIM

N_EXPERTS = 32
TOP_K = 4
D_EXPERT = 1024
SWIGLU_LIMIT = 7.0
SWIGLU_ALPHA = 1.702
EXPERT_BLOCK = 512

LANES = 128
NEG_INF = float("-inf")
LOG2_E = math.log2(math.e)

BF16 = jnp.bfloat16
F32 = jnp.float32


def _dot(a, b):
    return jnp.dot(a, b, preferred_element_type=F32)


def _dot_nt(a, b):
    return lax.dot_general(a, b, (((1,), (1,)), ((), ())), preferred_element_type=F32)


def _split_bf16(a):
    hi = a.astype(BF16)
    lo = (a - hi.astype(F32)).astype(BF16)
    return hi, lo


def _rms(x, g, n):
    ms = jnp.sum(x * x, axis=-1, keepdims=True) * (1.0 / n)
    return x * lax.rsqrt(ms + EPS) * g


def _swap_halves(x, k):
    lane = lax.broadcasted_iota(jnp.int32, x.shape, 1)
    return jnp.where((lane & k) != 0, pltpu.roll(x, k, 1), pltpu.roll(x, LANES - k, 1))


def _rope(x, cos, sin_signed, k):
    return x * cos + _swap_halves(x, k) * sin_signed


def _ada_kernel(c_ref, w_ref, b_ref, o_ref):
    c = c_ref[...]
    s = c * (1.0 / (1.0 + jnp.exp(-c)))
    s_hi, s_lo = _split_bf16(s)
    w_hi, w_lo = _split_bf16(w_ref[...])
    o_ref[...] = _dot(s_hi, w_hi) + (_dot(s_hi, w_lo) + _dot(s_lo, w_hi)) + b_ref[...]


def _ada_mod(cc, ada_w, ada_b):
    n = ada_w.shape[1]
    tn = 1024
    return pl.pallas_call(
        _ada_kernel,
        out_shape=jax.ShapeDtypeStruct((cc.shape[0], n), F32),
        grid=(n // tn,),
        in_specs=[
            pl.BlockSpec((cc.shape[0], D_MODEL), lambda j: (0, 0)),
            pl.BlockSpec((D_MODEL, tn), lambda j: (0, j)),
            pl.BlockSpec((1, tn), lambda j: (0, j)),
        ],
        out_specs=pl.BlockSpec((cc.shape[0], tn), lambda j: (0, j)),
        compiler_params=pltpu.CompilerParams(dimension_semantics=("parallel",)),
        name="ada_mod",
    )(cc, ada_w, ada_b.reshape(1, n))


def _proj_kernel(*refs, with_q):
    if with_q:
        (x_ref, mod_ref, nmix_ref, cg_ref, sg_ref, cm_ref, sm_ref,
         wq_ref, wk_ref, wv_ref, wqa_ref, wckv_ref, wkr_ref, wg_ref, wqb_ref, wkb_ref, wvb_ref,
         gq_ref, gk_ref, gqa_ref, gkva_ref, gmq_ref, gmk_ref,
         q_ref, k_ref, v_ref, qm_ref, km_ref, vm_ref, gate_ref) = refs
    else:
        (x_ref, mod_ref, nmix_ref, cg_ref, sg_ref, cm_ref, sm_ref,
         wk_ref, wv_ref, wckv_ref, wkr_ref, wkb_ref, wvb_ref,
         gk_ref, gkva_ref, gmk_ref,
         k_ref, v_ref, km_ref, vm_ref) = refs

    x = x_ref[0]
    mod = mod_ref[0]
    shift, scale = mod[0:1, :], mod[1:2, :]
    h = _rms(x, nmix_ref[...], D_MODEL) * (1.0 + scale) + shift
    hb = h.astype(BF16)
    cg, sg = cg_ref[...], sg_ref[...]
    cm, sm = cm_ref[...], sm_ref[...]

    kk = _dot(hb, wk_ref[...])
    for j in range(GQA_KV_HEADS):
        sl = slice(j * GQA_HEAD_DIM, (j + 1) * GQA_HEAD_DIM)
        kn = _rms(kk[:, sl], gk_ref[...], GQA_HEAD_DIM)
        k_ref[0, :, sl] = _rope(kn, cg, sg, 32).astype(BF16)
    v_ref[0] = _dot_nt(wv_ref[...], hb).astype(BF16)

    ckv = _rms(_dot(hb, wckv_ref[...]), gkva_ref[...], MLA_KV_RANK).astype(BF16)
    vm_ref[0] = _dot_nt(wvb_ref[...], ckv).astype(BF16)
    knope = _dot(ckv, wkb_ref[...])
    kr = _dot(hb, wkr_ref[...])
    gmk = gmk_ref[...]
    g_nope, g_rope = gmk[:, :MLA_NOPE_DIM], gmk[:, MLA_NOPE_DIM:]
    ssq_r = jnp.sum(kr * kr, axis=-1, keepdims=True)
    kr_roped = _rope(kr * g_rope, cm, sm, 16)
    for j in range(MLA_HEADS):
        kn = knope[:, j * MLA_NOPE_DIM:(j + 1) * MLA_NOPE_DIM]
        ms = (jnp.sum(kn * kn, axis=-1, keepdims=True) + ssq_r) * (1.0 / MLA_QK_DIM)
        r = lax.rsqrt(ms + EPS)
        base = j * MLA_HEAD_PAD
        km_ref[0, :, base:base + MLA_NOPE_DIM] = (kn * r * g_nope).astype(BF16)
        km_ref[0, :, base + MLA_NOPE_DIM:base + MLA_HEAD_PAD] = (kr_roped * r).astype(BF16)

    if not with_q:
        return

    qq = _dot(hb, wq_ref[...])
    q_scale = GQA_HEAD_DIM ** -0.5 * LOG2_E
    for j in range(GQA_HEADS):
        sl = slice(j * GQA_HEAD_DIM, (j + 1) * GQA_HEAD_DIM)
        qn = _rms(qq[:, sl], gq_ref[...], GQA_HEAD_DIM)
        q_ref[0, :, sl] = (_rope(qn, cg, sg, 32) * q_scale).astype(BF16)

    qa = _rms(_dot(hb, wqa_ref[...]), gqa_ref[...], MLA_Q_RANK).astype(BF16)
    q2 = _dot(qa, wqb_ref[...])
    gmq = gmq_ref[...]
    gq_nope, gq_rope = gmq[:, :MLA_NOPE_DIM], gmq[:, MLA_NOPE_DIM:]
    m_scale = MLA_QK_DIM ** -0.5 * LOG2_E
    for j in range(MLA_HEADS):
        base = j * MLA_HEAD_PAD
        qn = q2[:, base:base + MLA_NOPE_DIM]
        qr = q2[:, base + MLA_NOPE_DIM:base + MLA_HEAD_PAD]
        ms = (jnp.sum(qn * qn, axis=-1, keepdims=True) + jnp.sum(qr * qr, axis=-1, keepdims=True)) * (1.0 / MLA_QK_DIM)
        r = lax.rsqrt(ms + EPS)
        qm_ref[0, :, base:base + MLA_NOPE_DIM] = (qn * r * gq_nope * m_scale).astype(BF16)
        qm_ref[0, :, base + MLA_NOPE_DIM:base + MLA_HEAD_PAD] = (_rope(qr * r * gq_rope, cm, sm, 16) * m_scale).astype(BF16)

    gl = _dot(hb, wg_ref[...])
    gate_ref[0] = (1.0 / (1.0 + jnp.exp(-gl))).astype(BF16)


def _proj(x, mod3, mod_row_of_batch, tables, w, with_q, tm):
    b, l, _ = x.shape
    cg, sg, cm, sm = tables
    const = lambda shape: pl.BlockSpec(shape, lambda bi, i: (0,) * len(shape), pipeline_mode=pl.Buffered(1))
    tab = pl.BlockSpec((tm, LANES), lambda bi, i: (i, 0))
    in_specs = [
        pl.BlockSpec((1, tm, D_MODEL), lambda bi, i: (bi, i, 0)),
        pl.BlockSpec((1, 6, D_MODEL), lambda bi, i: (mod_row_of_batch(bi), 0, 0)),
        const((1, D_MODEL)), tab, tab, tab, tab,
    ]
    if with_q:
        weights = [w["wq"], w["wk"], w["wv"], w["wqa"], w["wckv"], w["wkr"], w["wg"], w["wqb"], w["wkb"], w["wvb"],
                   w["gq"], w["gk"], w["gqa"], w["gkva"], w["gmq"], w["gmk"]]
    else:
        weights = [w["wk"], w["wv"], w["wckv"], w["wkr"], w["wkb"], w["wvb"], w["gk"], w["gkva"], w["gmk"]]
    in_specs += [const(a.shape) for a in weights]

    def out(width):
        return jax.ShapeDtypeStruct((b, l, width), BF16), pl.BlockSpec((1, tm, width), lambda bi, i: (bi, i, 0))

    def out_t(width):
        return jax.ShapeDtypeStruct((b, width, l), BF16), pl.BlockSpec((1, width, tm), lambda bi, i: (bi, 0, i))

    outs = [out(GQA_KV_W), out_t(GQA_KV_W), out(MLA_HEADS * MLA_HEAD_PAD), out_t(MLA_V_W)]
    if with_q:
        outs = [out(GQA_Q_W)] + outs[:2] + [out(MLA_HEADS * MLA_HEAD_PAD)] + outs[2:] + [out(2 * D_MODEL)]
    return pl.pallas_call(
        functools.partial(_proj_kernel, with_q=with_q),
        out_shape=[o[0] for o in outs],
        grid=(b, l // tm),
        in_specs=in_specs,
        out_specs=[o[1] for o in outs],
        compiler_params=pltpu.CompilerParams(dimension_semantics=("parallel", "parallel"),
                                             vmem_limit_bytes=56 * 1024 * 1024),
        name="proj_latent" if with_q else "proj_ctx",
    )(x, mod3, w["norm_mix"], cg, sg, cm, sm, *weights)


def _attn_kernel(q_ref, kl_ref, vl_ref, kc_ref, vc_ref, o_ref, *, tk, n_lat, bounded):
    q = q_ref[0]
    tq = q.shape[0]
    dv = vl_ref.shape[1]
    n = n_lat // tk

    if bounded:
        l = jnp.zeros((1, tq), F32)
        acc = jnp.zeros((dv, tq), F32)
        for k, vt in [(kl_ref[0, j * tk:(j + 1) * tk, :], vl_ref[0, :, j * tk:(j + 1) * tk]) for j in range(n)] + [
                (kc_ref[0], vc_ref[0])]:
            p = jnp.exp2(_dot_nt(k, q))
            l = l + jnp.sum(p, axis=0, keepdims=True)
            acc = acc + _dot(vt, p.astype(BF16))
        o_ref[0] = (acc / l).T.astype(BF16)
        return

    def update(carry, s, vt):
        m, l, acc = carry
        m_new = jnp.maximum(m, jnp.max(s, axis=0, keepdims=True))
        alpha = jnp.exp2(m - m_new)
        p = jnp.exp2(s - m_new)
        l = alpha * l + jnp.sum(p, axis=0, keepdims=True)
        return m_new, l, alpha * acc + _dot(vt, p.astype(BF16))

    n = n_lat // tk
    keys = [kl_ref[0, j * tk:(j + 1) * tk, :] for j in range(n)] + [kc_ref[0]]
    vals = [vl_ref[0, :, j * tk:(j + 1) * tk] for j in range(n)] + [vc_ref[0]]
    carry = (jnp.full((1, tq), NEG_INF, F32), jnp.zeros((1, tq), F32), jnp.zeros((dv, tq), F32))
    s = _dot_nt(keys[0], q)
    for j in range(n + 1):
        s_next = _dot_nt(keys[j + 1], q) if j < n else None
        carry = update(carry, s, vals[j])
        s = s_next
    m, l, acc = carry
    o_ref[0] = (acc / l).T.astype(BF16)


SOFTMAX_SAFE_EXPONENT = 56.0


def _score_bound(gain_q, gain_k, dim):
    return dim * jnp.max(jnp.abs(gain_q)) * jnp.max(jnp.abs(gain_k)) * (dim ** -0.5 * LOG2_E) * 1.02


def _attention(q, k_lat, v_lat, k_ctx, v_ctx, n_heads, group, d_qk, dv, tq, tk, name, score_bound):
    run = functools.partial(_attention_call, q, k_lat, v_lat, k_ctx, v_ctx, n_heads, group, d_qk, dv, tq, tk, name)
    return lax.cond(score_bound <= SOFTMAX_SAFE_EXPONENT, lambda: run(True), lambda: run(False))


def _attention_call(q, k_lat, v_lat, k_ctx, v_ctx, n_heads, group, d_qk, dv, tq, tk, name, bounded):
    b, l, _ = q.shape
    lc = k_ctx.shape[1]
    name = name + ("_bounded" if bounded else "_online")
    return pl.pallas_call(
        functools.partial(_attn_kernel, tk=tk, n_lat=l, bounded=bounded),
        out_shape=jax.ShapeDtypeStruct((b, l, n_heads * dv), BF16),
        grid=(b, n_heads, l // tq),
        in_specs=[
            pl.BlockSpec((1, tq, d_qk), lambda bi, h, i: (bi, i, h)),
            pl.BlockSpec((1, l, d_qk), lambda bi, h, i: (bi, 0, h // group)),
            pl.BlockSpec((1, dv, l), lambda bi, h, i: (bi, h // group, 0)),
            pl.BlockSpec((1, lc, d_qk), lambda bi, h, i: (bi, 0, h // group)),
            pl.BlockSpec((1, dv, lc), lambda bi, h, i: (bi, h // group, 0)),
        ],
        out_specs=pl.BlockSpec((1, tq, dv), lambda bi, h, i: (bi, i, h)),
        compiler_params=pltpu.CompilerParams(dimension_semantics=("parallel", "parallel", "parallel"),
                                             vmem_limit_bytes=56 * 1024 * 1024),
        name=name,
    )(q, k_lat, v_lat, k_ctx, v_ctx)


def _merge_kernel(oa_ref, ob_ref, gate_ref, x_ref, mod_ref, woa_ref, wob_ref, wout_ref, nffn_ref,
                  rwh_ref, rwl_ref, rb_ref, x1_ref, h2_ref, eidx_ref, egate_ref, rank_ref, before_ref, tcnt_ref,
                  cnt_ref, carry_ref):
    @pl.when((pl.program_id(0) == 0) & (pl.program_id(1) == 0))
    def _():
        carry_ref[...] = jnp.zeros_like(carry_ref)

    mod = mod_ref[0]
    g1, shift2, scale2 = mod[2:3, :], mod[3:4, :], mod[4:5, :]
    ya = _dot(oa_ref[0], woa_ref[...])
    yb = _dot(ob_ref[0], wob_ref[...])
    g = gate_ref[0].astype(F32)
    y = g[:, :D_MODEL] * ya + g[:, D_MODEL:] * yb
    z = _dot(y.astype(BF16), wout_ref[...])
    x1 = x_ref[0] + g1 * z
    x1_ref[0] = x1
    h2 = _rms(x1, nffn_ref[...], D_MODEL) * (1.0 + scale2) + shift2
    h2_ref[0] = h2.astype(BF16)

    h_hi, h_lo = _split_bf16(h2)
    logits = _dot(h_hi, rwh_ref[...]) + (_dot(h_hi, rwl_ref[...]) + _dot(h_lo, rwh_ref[...])) + rb_ref[...]
    lane = lax.broadcasted_iota(jnp.int32, logits.shape, 1).astype(F32)
    cur = jnp.where(lane < N_EXPERTS, logits, NEG_INF)
    vals, idxs = [], []
    chosen = jnp.zeros(logits.shape, F32)
    for _ in range(TOP_K):
        mx = jnp.max(cur, axis=-1, keepdims=True)
        ix = jnp.min(jnp.where(cur == mx, lane, float(LANES)), axis=-1, keepdims=True)
        vals.append(mx)
        idxs.append(ix)
        sel = lane == ix
        chosen = chosen + jnp.where(sel, 1.0, 0.0)
        cur = jnp.where(sel, NEG_INF, cur)
    ex = [jnp.exp(v - vals[0]) for v in vals]
    den = ex[0] + ex[1] + ex[2] + ex[3]
    eidx = jnp.zeros(logits.shape, F32)
    egate = jnp.zeros(logits.shape, F32)
    for k in range(TOP_K):
        eidx = jnp.where(lane == k, idxs[k], eidx)
        egate = jnp.where(lane == k, ex[k] / den, egate)
    eidx = eidx.astype(jnp.int32)
    eidx_ref[0] = eidx
    egate_ref[0] = egate

    for s in range(eidx.shape[0] // MOE_TILE):
        rows = slice(s * MOE_TILE, (s + 1) * MOE_TILE)
        before_ref[s] = carry_ref[...]
        rank, tile_cnt = _route_tile(eidx[rows], chosen[rows], carry_ref)
        rank_ref[0, rows, :] = rank
        tcnt_ref[s] = jnp.broadcast_to(tile_cnt, carry_ref.shape)
    cnt_ref[...] = carry_ref[...]


def _merge(o_a, o_b, gates, x, mod3, w, tm):
    b, l, _ = x.shape
    const = lambda shape: pl.BlockSpec(shape, lambda bi, i: (0,) * len(shape), pipeline_mode=pl.Buffered(1))
    tok = lambda width: pl.BlockSpec((1, tm, width), lambda bi, i: (bi, i, 0))
    assert tm % MOE_TILE == 0
    sub = tm // MOE_TILE
    per_tile = jax.ShapeDtypeStruct((b * l // MOE_TILE, 8, LANES), F32)
    tile_spec = pl.BlockSpec((sub, 8, LANES), lambda bi, i: (bi * (l // tm) + i, 0, 0))
    return pl.pallas_call(
        _merge_kernel,
        out_shape=[jax.ShapeDtypeStruct((b, l, D_MODEL), F32), jax.ShapeDtypeStruct((b, l, D_MODEL), BF16),
                   jax.ShapeDtypeStruct((b, l, LANES), jnp.int32), jax.ShapeDtypeStruct((b, l, LANES), F32),
                   jax.ShapeDtypeStruct((b, l, LANES), jnp.int32), per_tile, per_tile,
                   jax.ShapeDtypeStruct((8, LANES), F32)],
        grid=(b, l // tm),
        in_specs=[tok(GQA_Q_W), tok(MLA_V_W), tok(2 * D_MODEL), tok(D_MODEL),
                  pl.BlockSpec((1, 6, D_MODEL), lambda bi, i: (bi, 0, 0)),
                  const((GQA_Q_W, D_MODEL)), const((MLA_V_W, D_MODEL)), const((D_MODEL, D_MODEL)), const((1, D_MODEL)),
                  const((D_MODEL, LANES)), const((D_MODEL, LANES)), const((1, LANES))],
        out_specs=[tok(D_MODEL), tok(D_MODEL), tok(LANES), tok(LANES), tok(LANES), tile_spec, tile_spec,
                   pl.BlockSpec((8, LANES), lambda bi, i: (0, 0))],
        scratch_shapes=[pltpu.VMEM((8, LANES), F32)],
        compiler_params=pltpu.CompilerParams(dimension_semantics=("arbitrary", "arbitrary"),
                                             vmem_limit_bytes=56 * 1024 * 1024),
        name="merge_router",
    )(o_a, o_b, gates, x, mod3, w["woa"], w["wob"], w["wout"], w["norm_ffn"], w["rw_hi"], w["rw_lo"], w["rb"])


SEG_ALIGN = 16


def _route_tile(idx, total, carry_ref):
    tm = idx.shape[0]
    lane = lax.broadcasted_iota(jnp.int32, idx.shape, 1)
    row = lax.broadcasted_iota(jnp.int32, (tm, tm), 0)
    col = lax.broadcasted_iota(jnp.int32, (tm, tm), 1)
    tri = jnp.where(row > col, 1.0, 0.0).astype(BF16)
    before = _dot(tri, total.astype(BF16)) + carry_ref[0:1, :]
    rank = jnp.where(lane < TOP_K, jnp.take_along_axis(before, idx, axis=1), 0.0)
    tile_cnt = jnp.sum(total, axis=0, keepdims=True)
    carry_ref[...] = carry_ref[...] + tile_cnt
    return rank.astype(jnp.int32), tile_cnt


def _dest_tables(idx, rank, region_starts, seg_starts):
    valid = lax.broadcasted_iota(jnp.int32, idx.shape, 1) < TOP_K
    region_start = jnp.take_along_axis(jnp.broadcast_to(region_starts, idx.shape), idx, axis=1)
    seg_start = jnp.take_along_axis(jnp.broadcast_to(seg_starts, idx.shape), idx, axis=1)
    rank = rank.astype(F32)
    window_start = jnp.floor(seg_start * (1.0 / SEG_ALIGN)) * SEG_ALIGN
    local = rank - window_start
    dest = jnp.where(valid, region_start + rank, 0.0)
    col = jnp.where(valid & (local < WIN), idx.astype(F32) * WIN + local, -1.0)
    return dest, col


MOE_TILE = 256
WIN = 64
N_MAIN_ROWS = N_EXPERTS * WIN
MAIN_CHUNK = 512
WIN_PER_STACK = MOE_TILE // WIN
MAX_WINDOWS = (MOE_TILE * TOP_K + N_EXPERTS * SEG_ALIGN) // WIN
CARRY_ROWS = N_EXPERTS * SEG_ALIGN
WORKLIST_LEN = -(-(MAX_WINDOWS + WIN_PER_STACK) // WIN_PER_STACK) * WIN_PER_STACK


def _build_worklist(ts_ref, tc_ref, wl_ref, junk_row):
    base = pl.program_id(0) * N_EXPERTS

    def per_expert(e, n):
        first = ts_ref[base + e]
        n_win = (tc_ref[base + e] + (WIN - 1)) // WIN

        def per_window(wi, n):
            wl_ref[n] = first + wi * WIN
            return n + 1

        return lax.fori_loop(1, n_win, per_window, n)

    n = lax.fori_loop(0, N_EXPERTS, per_expert, 0)
    n_stacks = (n + (WIN_PER_STACK - 1)) // WIN_PER_STACK

    def pad(j, c):
        wl_ref[j] = junk_row + (j % WIN_PER_STACK) * WIN
        return c

    lax.fori_loop(n, n_stacks * WIN_PER_STACK, pad, 0)
    return n_stacks


def _seg_aligned(row):
    return row if isinstance(row, int) else pl.multiple_of(row, SEG_ALIGN)


def _stack_row_ids(wl_ref, stack, shape, axis):
    pos = lax.broadcasted_iota(jnp.int32, shape, axis)
    row = jnp.full(shape, -1, jnp.int32)
    for wi in range(WIN_PER_STACK - 1, -1, -1):
        row = jnp.where(pos < (wi + 1) * WIN, wl_ref[stack * WIN_PER_STACK + wi] + (pos - wi * WIN), row)
    return row


def _select_rows(row_id, id_t):
    shape = (row_id.shape[0], id_t.shape[1])
    sel = jnp.zeros(shape, F32)
    for k in range(TOP_K):
        hit = row_id == id_t[k:k + 1, :]
        sel = jnp.where(hit, 1.0, sel)
    return sel.astype(BF16)


MAIN_PENDING = 2


TAIL_ROWS = WIN + EXPERT_BLOCK


def _dispatch_kernel(ts_ref, tc_ref, ov_ref, re_ref, h_ref, eidx_ref, rank_ref, start_ref, before_ref,
                     buf_ref, dest_ref, col_ref, wl_ref, pend_ref, main_x, carry_x, stage_x, main_sem, sem, *,
                     junk_row):
    i = pl.program_id(0)
    base = i * N_EXPERTS
    next_base = jnp.minimum(i + 1, pl.num_programs(0) - 1) * N_EXPERTS

    @pl.when(i == 0)
    def _():
        pend_ref[0] = 0
        pend_ref[1] = 0
        pend_ref[MAIN_PENDING] = 0
        carry_x[...] = jnp.zeros_like(carry_x)
        main_x[0:TAIL_ROWS, :] = jnp.zeros((TAIL_ROWS, D_MODEL), BF16)
        cps = []
        for e in range(N_EXPERTS):
            rows = pl.ds(pl.multiple_of(jnp.maximum(re_ref[e] - TAIL_ROWS, 0), SEG_ALIGN), TAIL_ROWS)
            cps.append(pltpu.make_async_copy(main_x.at[pl.ds(0, TAIL_ROWS)], buf_ref.at[rows], main_sem))
        for cp in cps:
            cp.start()
        for cp in cps:
            cp.wait()

        def zero_block_copies(blk):
            rows = pl.ds(pl.multiple_of(blk * EXPERT_BLOCK, EXPERT_BLOCK), EXPERT_BLOCK)
            return (pltpu.make_async_copy(main_x.at[pl.ds(0, EXPERT_BLOCK)], buf_ref.at[rows], main_sem),)

        def start_zero(blk, c):
            for cp in zero_block_copies(blk):
                cp.start()
            return c

        def wait_zero(blk, c):
            for cp in zero_block_copies(blk):
                cp.wait()
            return c

        first_free, n_blocks = re_ref[N_EXPERTS - 1] // EXPERT_BLOCK, (junk_row + EXPERT_BLOCK) // EXPERT_BLOCK
        lax.fori_loop(first_free, n_blocks, start_zero, 0)
        lax.fori_loop(first_free, n_blocks, wait_zero, 0)

    h = h_ref[...]

    def main_copies(first_row):
        cps = []
        for e in range(N_EXPERTS):
            rows = pl.ds(_seg_aligned(first_row(e)), WIN)
            cps.append(pltpu.make_async_copy(main_x.at[pl.ds(e * WIN, WIN)], buf_ref.at[rows], main_sem))
        return cps

    def drain_main():
        @pl.when(pend_ref[MAIN_PENDING] == 1)
        def _():
            for cp in main_copies(lambda e: 0):
                cp.wait()
            pend_ref[MAIN_PENDING] = 0

    dest, col = _dest_tables(eidx_ref[...], rank_ref[...], start_ref[0:1, :], before_ref[0][0:1, :])
    dest_ref[...] = dest.astype(jnp.int32)
    col_ref[...] = col.astype(jnp.int32)
    dest_t = dest.T[0:8, :].astype(jnp.int32)
    col_t = col.T[0:8, :].astype(jnp.int32)
    pieces = []
    for c in range(N_MAIN_ROWS // MAIN_CHUNK):
        row_id = lax.broadcasted_iota(jnp.int32, (MAIN_CHUNK, 1), 0) + c * MAIN_CHUNK
        pieces.append(_dot(_select_rows(row_id, col_t), h).astype(BF16))
    group_pos = lax.broadcasted_iota(jnp.int32, (SEG_ALIGN, 1), 0)
    end_group = jnp.concatenate([ts_ref[next_base + e] + group_pos for e in range(N_EXPERTS)], axis=0)
    end_rows = _dot(_select_rows(end_group, dest_t), h).astype(BF16)

    def window_copies(slot, first_row):
        cps = []
        for wi in range(WIN_PER_STACK):
            rows = pl.ds(_seg_aligned(first_row(wi)), WIN)
            cps.append(pltpu.make_async_copy(stage_x.at[slot, pl.ds(wi * WIN, WIN)], buf_ref.at[rows], sem.at[slot]))
        return cps

    def drain(slot):
        @pl.when(pend_ref[slot] == 1)
        def _():
            for cp in window_copies(slot, lambda wi: 0):
                cp.wait()
            pend_ref[slot] = 0

    drain_main()
    drain(0)
    drain(1)
    for c, packed in enumerate(pieces):
        main_x[c * MAIN_CHUNK:(c + 1) * MAIN_CHUNK, :] = packed
    for e in range(N_EXPERTS):
        win, grp = pl.ds(e * WIN, SEG_ALIGN), pl.ds(e * SEG_ALIGN, SEG_ALIGN)
        main_x[win, :] = main_x[win, :] + carry_x[grp, :]
        same = ts_ref[next_base + e] == ts_ref[base + e]
        carry_x[grp, :] = end_rows[e * SEG_ALIGN:(e + 1) * SEG_ALIGN, :] + jnp.where(same, carry_x[grp, :], 0).astype(BF16)
    for n, cp in enumerate(main_copies(lambda e: ts_ref[base + e])):
        cp.start(priority=n % 2)
    pend_ref[MAIN_PENDING] = 1

    @pl.when(ov_ref[i] > 0)
    def _():
        n_stacks = _build_worklist(ts_ref, tc_ref, wl_ref, junk_row)

        def stack_body(s, c):
            slot = s & 1
            rows = _dot(_select_rows(_stack_row_ids(wl_ref, s, (MOE_TILE, 1), 0), dest_t), h)
            drain(slot)
            stage_x[slot] = rows.astype(BF16)
            for cp in window_copies(slot, lambda wi: wl_ref[s * WIN_PER_STACK + wi]):
                cp.start()
            pend_ref[slot] = 1
            return c

        lax.fori_loop(0, n_stacks, stack_body, 0)

    @pl.when(i == pl.num_programs(0) - 1)
    def _():
        drain_main()
        drain(0)
        drain(1)


def _moe_grid_spec(n_tiles, in_specs, out_specs, scratch_shapes):
    return pltpu.PrefetchScalarGridSpec(num_scalar_prefetch=4, grid=(n_tiles,), in_specs=in_specs,
                                        out_specs=out_specs, scratch_shapes=scratch_shapes)


def _dispatch(tile_start, tile_cnt, tile_ovf, region_ends, h2, eidx, rank, start, before, n_rows):
    t = h2.shape[0]
    n_tiles = t // MOE_TILE
    junk_row = n_rows - EXPERT_BLOCK
    tok = lambda width: pl.BlockSpec((MOE_TILE, width), lambda i, *_: (i, 0))
    table = jax.ShapeDtypeStruct((t, LANES), jnp.int32)
    return pl.pallas_call(
        functools.partial(_dispatch_kernel, junk_row=junk_row),
        out_shape=[jax.ShapeDtypeStruct((n_rows, D_MODEL), BF16), table, table],
        grid_spec=_moe_grid_spec(
            n_tiles,
            [tok(D_MODEL), tok(LANES), tok(LANES), pl.BlockSpec((8, LANES), lambda i, *_: (0, 0)),
             pl.BlockSpec((1, 8, LANES), lambda i, *_: (i, 0, 0))],
            [pl.BlockSpec(memory_space=pl.ANY), tok(LANES), tok(LANES)],
            [pltpu.SMEM((WORKLIST_LEN,), jnp.int32), pltpu.SMEM((3,), jnp.int32),
             pltpu.VMEM((N_MAIN_ROWS, D_MODEL), BF16), pltpu.VMEM((CARRY_ROWS, D_MODEL), BF16),
             pltpu.VMEM((2, MOE_TILE, D_MODEL), BF16),
             pltpu.SemaphoreType.DMA(()), pltpu.SemaphoreType.DMA((2,))]),
        compiler_params=pltpu.CompilerParams(dimension_semantics=("arbitrary",),
                                             vmem_limit_bytes=56 * 1024 * 1024),
        name="moe_dispatch",
    )(tile_start, tile_cnt, tile_ovf, region_ends, h2, eidx, rank, start, before)


def _select_cols(ids, gates, col_id):
    sel = jnp.zeros((ids.shape[0], col_id.shape[1]), F32)
    for k in range(TOP_K):
        sel = jnp.where(ids[:, k:k + 1] == col_id, gates[:, k:k + 1], sel)
    return sel.astype(BF16)


def _combine_kernel(ts_ref, tc_ref, ov_ref, re_ref, col_ref, dest_ref, egate_ref, x1_ref, mod_ref, y_ref, o_ref,
                    wl_ref, main_y, stage_y, acc_ref, main_sem, sem, *, junk_row):
    del re_ref
    i = pl.program_id(0)
    slot_i = i & 1

    def main_copies(slot, first_row):
        return [pltpu.make_async_copy(y_ref.at[pl.ds(_seg_aligned(first_row(e)), WIN)],
                                      main_y.at[slot, pl.ds(e * WIN, WIN)], main_sem.at[slot])
                for e in range(N_EXPERTS)]

    def fetch_main(tile, slot):
        for n, cp in enumerate(main_copies(slot, lambda e: ts_ref[tile * N_EXPERTS + e])):
            cp.start(priority=n % 2)

    @pl.when(i == 0)
    def _():
        fetch_main(0, 0)

    @pl.when(i + 1 < pl.num_programs(0))
    def _():
        fetch_main(i + 1, 1 - slot_i)

    for cp in main_copies(slot_i, lambda e: 0):
        cp.wait()

    col = col_ref[...]
    egate = egate_ref[...]
    sel = jnp.concatenate(
        [_select_cols(col, egate, lax.broadcasted_iota(jnp.int32, (1, MAIN_CHUNK), 1) + c * MAIN_CHUNK)
         for c in range(N_MAIN_ROWS // MAIN_CHUNK)], axis=1)
    acc_ref[...] = _dot(sel, main_y[slot_i])

    @pl.when(ov_ref[i] > 0)
    def _():
        n_stacks = _build_worklist(ts_ref, tc_ref, wl_ref, junk_row)
        dest = dest_ref[...]
        if WIN_PER_STACK * WIN < MOE_TILE:
            for slot in range(2):
                stage_y[slot, WIN_PER_STACK * WIN:, :] = jnp.zeros((MOE_TILE - WIN_PER_STACK * WIN, D_MODEL), BF16)

        def window_copies(slot, first_row):
            return [pltpu.make_async_copy(y_ref.at[pl.ds(_seg_aligned(first_row(wi)), WIN)],
                                          stage_y.at[slot, pl.ds(wi * WIN, WIN)], sem.at[slot])
                    for wi in range(WIN_PER_STACK)]

        def fetch(s, slot):
            for cp in window_copies(slot, lambda wi: wl_ref[s * WIN_PER_STACK + wi]):
                cp.start()

        @pl.when(n_stacks > 0)
        def _():
            fetch(0, 0)

        def stack_body(s, c):
            slot = s & 1

            @pl.when(s + 1 < n_stacks)
            def _():
                fetch(s + 1, 1 - slot)

            for cp in window_copies(slot, lambda wi: 0):
                cp.wait()
            sel_o = _select_cols(dest, egate, _stack_row_ids(wl_ref, s, (1, MOE_TILE), 1))
            acc_ref[...] += _dot(sel_o, stage_y[slot])
            return c

        lax.fori_loop(0, n_stacks, stack_body, 0)

    g2 = mod_ref[0][5:6, :]
    o_ref[...] = x1_ref[...] + g2 * acc_ref[...]


def _combine(tile_start, tile_cnt, tile_ovf, region_ends, col, dest, egate, x1, mod3, y, l):
    t = x1.shape[0]
    per_batch = l // MOE_TILE
    junk_row = y.shape[0] - EXPERT_BLOCK
    tok = lambda width: pl.BlockSpec((MOE_TILE, width), lambda i, *_: (i, 0))
    return pl.pallas_call(
        functools.partial(_combine_kernel, junk_row=junk_row),
        out_shape=jax.ShapeDtypeStruct((t, D_MODEL), F32),
        grid_spec=_moe_grid_spec(
            t // MOE_TILE,
            [tok(LANES), tok(LANES), tok(LANES), tok(D_MODEL),
             pl.BlockSpec((1, 6, D_MODEL), lambda i, *_: (i // per_batch, 0, 0)),
             pl.BlockSpec(memory_space=pl.ANY)],
            tok(D_MODEL),
            [pltpu.SMEM((WORKLIST_LEN,), jnp.int32), pltpu.VMEM((2, N_MAIN_ROWS, D_MODEL), BF16),
             pltpu.VMEM((2, MOE_TILE, D_MODEL), BF16), pltpu.VMEM((MOE_TILE, D_MODEL), F32),
             pltpu.SemaphoreType.DMA((2,)), pltpu.SemaphoreType.DMA((2,))]),
        compiler_params=pltpu.CompilerParams(dimension_semantics=("arbitrary",),
                                             vmem_limit_bytes=56 * 1024 * 1024),
        name="moe_combine",
    )(tile_start, tile_cnt, tile_ovf, region_ends, col, dest, egate, x1, mod3, y)


def _expert_kernel(be_ref, nused_ref, x_ref, w1_ref, b1_ref, w2_ref, b2_ref, y_ref, w1b_ref, w2b_ref):
    i = pl.program_id(0)
    live = i < nused_ref[0]

    @pl.when(live & ((i == 0) | (be_ref[i] != be_ref[jnp.maximum(i - 1, 0)])))
    def _():
        w1b_ref[...] = w1_ref[0].astype(BF16)
        w2b_ref[...] = w2_ref[0].astype(BF16)

    @pl.when(live)
    def _():
        gu = _dot(x_ref[...], w1b_ref[...]) + b1_ref[0]
        glu = jnp.minimum(gu[:, :D_EXPERT], SWIGLU_LIMIT)
        lin = jnp.clip(gu[:, D_EXPERT:], -SWIGLU_LIMIT, SWIGLU_LIMIT)
        act = glu * (1.0 / (1.0 + jnp.exp(-SWIGLU_ALPHA * glu))) * (lin + 1.0)
        y_ref[...] = (_dot(act.astype(BF16), w2b_ref[...]) + b2_ref[0]).astype(BF16)

    @pl.when(pl.program_id(0) >= nused_ref[0])
    def _():
        y_ref[...] = jnp.zeros_like(y_ref)


def _experts(block_e, nused, buf, w1, b1, w2, b2):
    nb = buf.shape[0] // EXPERT_BLOCK
    row = lambda i, be, nu: (jnp.minimum(i, nu[0] - 1), 0)
    out_row = lambda i, be, nu: (i, 0)
    exp3 = lambda i, be, nu: (be[jnp.minimum(i, nu[0] - 1)], 0, 0)
    return pl.pallas_call(
        _expert_kernel,
        out_shape=jax.ShapeDtypeStruct((buf.shape[0], D_MODEL), BF16),
        grid_spec=pltpu.PrefetchScalarGridSpec(
            num_scalar_prefetch=2,
            grid=(nb,),
            in_specs=[pl.BlockSpec((EXPERT_BLOCK, D_MODEL), row),
                      pl.BlockSpec((1, D_MODEL, 2 * D_EXPERT), exp3),
                      pl.BlockSpec((1, 1, 2 * D_EXPERT), exp3),
                      pl.BlockSpec((1, D_EXPERT, D_MODEL), exp3),
                      pl.BlockSpec((1, 1, D_MODEL), exp3)],
            out_specs=pl.BlockSpec((EXPERT_BLOCK, D_MODEL), out_row),
            scratch_shapes=[pltpu.VMEM((D_MODEL, 2 * D_EXPERT), BF16), pltpu.VMEM((D_EXPERT, D_MODEL), BF16)]),
        compiler_params=pltpu.CompilerParams(dimension_semantics=("arbitrary",),
                                             vmem_limit_bytes=56 * 1024 * 1024),
        name="moe_experts",
    )(block_e, nused, buf, w1, b1, w2, b2)


def _rope_tables(seq_len):
    pos = np.arange(seq_len, dtype=np.int32)
    row = (pos // GRID_W).astype(np.float32)[:, None]
    col = (pos % GRID_W).astype(np.float32)[:, None]

    def table(d_axis, pad):
        inv_freq = (np.float32(ROPE_THETA) ** (-np.arange(0, d_axis, 2, dtype=np.float32) / np.float32(d_axis)))
        inv_freq = inv_freq.astype(np.float32)
        ar, ac = row * inv_freq[None, :], col * inv_freq[None, :]
        cos = np.concatenate([np.cos(ar), np.cos(ar), np.cos(ac), np.cos(ac)], axis=-1)
        sin = np.concatenate([-np.sin(ar), np.sin(ar), -np.sin(ac), np.sin(ac)], axis=-1)
        if pad:
            cos = np.concatenate([cos, np.ones((seq_len, pad), np.float32)], axis=-1)
            sin = np.concatenate([sin, np.zeros((seq_len, pad), np.float32)], axis=-1)
        return jnp.asarray(cos, F32), jnp.asarray(sin, F32)

    cg, sg = table(GQA_HEAD_DIM // 2, 0)
    cm, sm = table(MLA_ROPE_DIM // 2, LANES - MLA_ROPE_DIM)
    return cg, sg, cm, sm


def _pad_lanes(a, width):
    return jnp.pad(a, [(0, 0)] * (a.ndim - 1) + [(0, width - a.shape[-1])])


def _prep_weights(p):
    w_in = p["w_in"]
    w = {"norm_mix": p["norm_mix"].reshape(1, D_MODEL), "norm_ffn": p["norm_ffn"].reshape(1, D_MODEL)}
    w["wq"] = w_in[:, :OFF_GQA_K].astype(BF16)
    w["wk"] = w_in[:, OFF_GQA_K:OFF_GQA_V].astype(BF16)
    w["wv"] = w_in[:, OFF_GQA_V:OFF_MLA_QA].T.astype(BF16)
    w["wqa"] = w_in[:, OFF_MLA_QA:OFF_MLA_KVA].astype(BF16)
    w["wckv"] = w_in[:, OFF_MLA_KVA:OFF_MLA_KVA + MLA_KV_RANK].astype(BF16)
    w["wkr"] = _pad_lanes(w_in[:, OFF_MLA_KVA + MLA_KV_RANK:OFF_GATE], LANES).astype(BF16)
    w["wg"] = w_in[:, OFF_GATE:].astype(BF16)
    wqb = p["mla_w_qb"].reshape(MLA_Q_RANK, MLA_HEADS, MLA_QK_DIM)
    w["wqb"] = _pad_lanes(wqb, MLA_HEAD_PAD).reshape(MLA_Q_RANK, MLA_HEADS * MLA_HEAD_PAD).astype(BF16)
    wkvb = p["mla_w_kvb"].reshape(MLA_KV_RANK, MLA_HEADS, MLA_NOPE_DIM + MLA_V_DIM)
    w["wkb"] = wkvb[:, :, :MLA_NOPE_DIM].reshape(MLA_KV_RANK, MLA_HEADS * MLA_NOPE_DIM).astype(BF16)
    w["wvb"] = wkvb[:, :, MLA_NOPE_DIM:].reshape(MLA_KV_RANK, MLA_V_W).T.astype(BF16)
    w["gq"] = p["gqa_q_norm"].reshape(1, GQA_HEAD_DIM)
    w["gk"] = p["gqa_k_norm"].reshape(1, GQA_HEAD_DIM)
    w["gqa"] = p["mla_q_a_norm"].reshape(1, MLA_Q_RANK)
    w["gkva"] = p["mla_kv_a_norm"].reshape(1, MLA_KV_RANK)
    w["gmq"] = _pad_lanes(p["mla_q_norm"].reshape(1, MLA_QK_DIM), MLA_HEAD_PAD)
    w["gmk"] = _pad_lanes(p["mla_k_norm"].reshape(1, MLA_QK_DIM), MLA_HEAD_PAD)
    w["woa"] = p["w_o_gqa"].astype(BF16)
    w["wob"] = p["w_o_mla"].astype(BF16)
    w["wout"] = p["w_out"].astype(BF16)
    rw = _pad_lanes(p["router_w"], LANES)
    w["rw_hi"] = rw.astype(BF16)
    w["rw_lo"] = (rw - w["rw_hi"].astype(F32)).astype(BF16)
    w["rb"] = _pad_lanes(p["router_b"].reshape(1, N_EXPERTS), LANES)
    return w


def _layer(x, c, ctx, c_ctx, p):
    b, l, _ = x.shape
    lc = ctx.shape[1]
    t = b * l
    w = _prep_weights(p)

    n_mod_rows = -(-(b + 1) // 8) * 8
    cc = jnp.zeros((n_mod_rows, D_MODEL), F32).at[:b].set(c).at[b].set(c_ctx)
    mod3 = _ada_mod(cc, p["ada_w"], p["ada_b"]).reshape(n_mod_rows, 6, D_MODEL)

    tables = _rope_tables(l)
    ident = (jnp.ones((lc, LANES), F32), jnp.zeros((lc, LANES), F32)) * 2
    tm = min(512, l)
    q_a, k_a, v_a, q_m, k_m, v_m, gates = _proj(x, mod3, lambda bi: bi, tables, w, True, tm)
    kc_a, vc_a, kc_m, vc_m = _proj(ctx, mod3, lambda bi: b, ident, w, False, min(256, lc))

    tq, tk = min(2048, l), min(2048, l)
    o_a = _attention(q_a, k_a, v_a, kc_a, vc_a, GQA_HEADS, GQA_GROUP, GQA_HEAD_DIM, GQA_HEAD_DIM, tq, tk, "attn_gqa",
                     _score_bound(p["gqa_q_norm"], p["gqa_k_norm"], GQA_HEAD_DIM))
    o_m = _attention(q_m, k_m, v_m, kc_m, vc_m, MLA_HEADS, 1, MLA_HEAD_PAD, MLA_V_DIM, tq, tk, "attn_mla",
                     _score_bound(p["mla_q_norm"], p["mla_k_norm"], MLA_QK_DIM))

    x1, h2, eidx, egate, rank, before_raw, tcnt, cnt = _merge(o_a, o_m, gates, x, mod3, w, tm)
    x1, h2 = x1.reshape(t, D_MODEL), h2.reshape(t, D_MODEL)
    eidx, egate, rank = eidx.reshape(t, LANES), egate.reshape(t, LANES), rank.reshape(t, LANES)

    assert t % MOE_TILE == 0
    counts = cnt[0, :N_EXPERTS].astype(jnp.int32)
    before = before_raw[:, 0, :N_EXPERTS].astype(jnp.int32)
    region = (counts + WIN + EXPERT_BLOCK - 1) // EXPERT_BLOCK * EXPERT_BLOCK
    region_ends = jnp.cumsum(region)
    region_starts = region_ends - region
    max_rows = t * TOP_K + N_EXPERTS * (WIN + EXPERT_BLOCK - 1)
    n_blocks = max_rows // EXPERT_BLOCK + 1
    n_rows = n_blocks * EXPERT_BLOCK
    block_row = jnp.arange(n_blocks, dtype=jnp.int32) * EXPERT_BLOCK
    block_e = jnp.minimum(jnp.sum(region_ends[None, :] <= block_row[:, None], axis=1), N_EXPERTS - 1).astype(jnp.int32)
    nused = (region_ends[-1:] // EXPERT_BLOCK).astype(jnp.int32)
    tile_start = (region_starts[None, :] + before // SEG_ALIGN * SEG_ALIGN).reshape(-1)
    tile_cnt = before % SEG_ALIGN + tcnt[:, 0, :N_EXPERTS].astype(jnp.int32)
    tile_ovf = jnp.sum(jnp.maximum((tile_cnt + WIN - 1) // WIN - 1, 0), axis=1).astype(jnp.int32)
    tile_cnt = tile_cnt.reshape(-1)
    start = jnp.zeros((8, LANES), F32).at[0, :N_EXPERTS].set(region_starts.astype(F32))
    region_ends = region_ends.astype(jnp.int32)
    buf, dest, col = _dispatch(tile_start, tile_cnt, tile_ovf, region_ends, h2, eidx, rank, start, before_raw, n_rows)
    y = _experts(block_e, nused, buf, p["expert_w1"], p["expert_b1"].reshape(N_EXPERTS, 1, -1),
                 p["expert_w2"], p["expert_b2"].reshape(N_EXPERTS, 1, -1))
    out = _combine(tile_start, tile_cnt, tile_ovf, region_ends, col, dest, egate, x1, mod3, y, l)
    return out.reshape(b, l, D_MODEL)


def kernel(x, c, ctx, c_ctx, ada_w, ada_b, norm_mix, norm_ffn, w_in, gqa_q_norm, gqa_k_norm, mla_q_a_norm, mla_kv_a_norm, mla_w_qb, mla_w_kvb, mla_q_norm, mla_k_norm, w_o_gqa, w_o_mla, w_out, router_w, router_b, expert_w1, expert_b1, expert_w2, expert_b2):
    assert ada_w.shape[0] == 1, "single-layer problem: the context stream is never updated"
    p = {
        "ada_w": ada_w[0], "ada_b": ada_b[0], "norm_mix": norm_mix[0], "norm_ffn": norm_ffn[0],
        "w_in": w_in[0], "gqa_q_norm": gqa_q_norm[0], "gqa_k_norm": gqa_k_norm[0],
        "mla_q_a_norm": mla_q_a_norm[0], "mla_kv_a_norm": mla_kv_a_norm[0],
        "mla_w_qb": mla_w_qb[0], "mla_w_kvb": mla_w_kvb[0],
        "mla_q_norm": mla_q_norm[0], "mla_k_norm": mla_k_norm[0],
        "w_o_gqa": w_o_gqa[0], "w_o_mla": w_o_mla[0], "w_out": w_out[0],
        "router_w": router_w[0], "router_b": router_b[0],
        "expert_w1": expert_w1[0], "expert_b1": expert_b1[0],
        "expert_w2": expert_w2[0], "expert_b2": expert_b2[0],
    }
    return _layer(x, c, ctx, c_ctx, p)
```

```python
import functools
import math

import jax
import jax.numpy as jnp
import numpy as np
from jax import lax
from jax.experimental import pallas as pl
from jax.experimental.pallas import tpu as pltpu

D_MODEL = 1024
GRID_W = 64
EPS = 1e-6
ROPE_THETA = 10000.0

GQA_HEADS = 8
GQA_KV_HEADS = 2
GQA_GROUP = GQA_HEADS // GQA_KV_HEADS
GQA_HEAD_DIM = 128
GQA_Q_W = GQA_HEADS * GQA_HEAD_DIM
GQA_KV_W = GQA_KV_HEADS * GQA_HEAD_DIM

MLA_HEADS = 8
MLA_Q_RANK = 256
MLA_KV_RANK = 128
MLA_NOPE_DIM = 128
MLA_ROPE_DIM = 64
MLA_V_DIM = 128
MLA_QK_DIM = MLA_NOPE_DIM + MLA_ROPE_DIM
MLA_HEAD_PAD = 256
MLA_V_W = MLA_HEADS * MLA_V_DIM

OFF_GQA_K = GQA_Q_W
OFF_GQA_V = OFF_GQA_K + GQA_KV_W
OFF_MLA_QA = OFF_GQA_V + GQA_KV_W
OFF_MLA_KVA = OFF_MLA_QA + MLA_Q_RANK
OFF_GATE = OFF_MLA_KVA + MLA_KV_RANK + MLA_ROPE_DIM

N_EXPERTS = 32
TOP_K = 4
D_EXPERT = 1024
SWIGLU_LIMIT = 7.0
SWIGLU_ALPHA = 1.702
EXPERT_BLOCK = 512

LANES = 128
NEG_INF = float("-inf")
LOG2_E = math.log2(math.e)

BF16 = jnp.bfloat16
F32 = jnp.float32


def _dot(a, b):
    return jnp.dot(a, b, preferred_element_type=F32)


def _dot_nt(a, b):
    return lax.dot_general(a, b, (((1,), (1,)), ((), ())), preferred_element_type=F32)


def _split_bf16(a):
    hi = a.astype(BF16)
    lo = (a - hi.astype(F32)).astype(BF16)
    return hi, lo


def _rms(x, g, n):
    ms = jnp.sum(x * x, axis=-1, keepdims=True) * (1.0 / n)
    return x * lax.rsqrt(ms + EPS) * g


def _swap_halves(x, k):
    lane = lax.broadcasted_iota(jnp.int32, x.shape, 1)
    return jnp.where((lane & k) != 0, pltpu.roll(x, k, 1), pltpu.roll(x, LANES - k, 1))


def _rope(x, cos, sin_signed, k):
    return x * cos + _swap_halves(x, k) * sin_signed


def _ada_kernel(c_ref, w_ref, b_ref, o_ref):
    c = c_ref[...]
    s = c * (1.0 / (1.0 + jnp.exp(-c)))
    s_hi, s_lo = _split_bf16(s)
    w_hi, w_lo = _split_bf16(w_ref[...])
    o_ref[...] = _dot(s_hi, w_hi) + (_dot(s_hi, w_lo) + _dot(s_lo, w_hi)) + b_ref[...]


def _ada_mod(cc, ada_w, ada_b):
    n = ada_w.shape[1]
    tn = 1024
    return pl.pallas_call(
        _ada_kernel,
        out_shape=jax.ShapeDtypeStruct((cc.shape[0], n), F32),
        grid=(n // tn,),
        in_specs=[
            pl.BlockSpec((cc.shape[0], D_MODEL), lambda j: (0, 0)),
            pl.BlockSpec((D_MODEL, tn), lambda j: (0, j)),
            pl.BlockSpec((1, tn), lambda j: (0, j)),
        ],
        out_specs=pl.BlockSpec((cc.shape[0], tn), lambda j: (0, j)),
        compiler_params=pltpu.CompilerParams(dimension_semantics=("parallel",)),
        name="ada_mod",
    )(cc, ada_w, ada_b.reshape(1, n))


def _proj_kernel(*refs, with_q):
    if with_q:
        (x_ref, mod_ref, nmix_ref, cg_ref, sg_ref, cm_ref, sm_ref,
         wq_ref, wk_ref, wv_ref, wqa_ref, wckv_ref, wkr_ref, wg_ref, wqb_ref, wkb_ref, wvb_ref,
         gq_ref, gk_ref, gqa_ref, gkva_ref, gmq_ref, gmk_ref,
         q_ref, k_ref, v_ref, qm_ref, km_ref, vm_ref, gate_ref) = refs
    else:
        (x_ref, mod_ref, nmix_ref, cg_ref, sg_ref, cm_ref, sm_ref,
         wk_ref, wv_ref, wckv_ref, wkr_ref, wkb_ref, wvb_ref,
         gk_ref, gkva_ref, gmk_ref,
         k_ref, v_ref, km_ref, vm_ref) = refs

    x = x_ref[0]
    mod = mod_ref[0]
    shift, scale = mod[0:1, :], mod[1:2, :]
    h = _rms(x, nmix_ref[...], D_MODEL) * (1.0 + scale) + shift
    hb = h.astype(BF16)
    cg, sg = cg_ref[...], sg_ref[...]
    cm, sm = cm_ref[...], sm_ref[...]

    kk = _dot(hb, wk_ref[...])
    for j in range(GQA_KV_HEADS):
        sl = slice(j * GQA_HEAD_DIM, (j + 1) * GQA_HEAD_DIM)
        kn = _rms(kk[:, sl], gk_ref[...], GQA_HEAD_DIM)
        k_ref[0, :, sl] = _rope(kn, cg, sg, 32).astype(BF16)
    v_ref[0] = _dot_nt(wv_ref[...], hb).astype(BF16)

    ckv = _rms(_dot(hb, wckv_ref[...]), gkva_ref[...], MLA_KV_RANK).astype(BF16)
    vm_ref[0] = _dot_nt(wvb_ref[...], ckv).astype(BF16)
    knope = _dot(ckv, wkb_ref[...])
    kr = _dot(hb, wkr_ref[...])
    gmk = gmk_ref[...]
    g_nope, g_rope = gmk[:, :MLA_NOPE_DIM], gmk[:, MLA_NOPE_DIM:]
    ssq_r = jnp.sum(kr * kr, axis=-1, keepdims=True)
    kr_roped = _rope(kr * g_rope, cm, sm, 16)
    for j in range(MLA_HEADS):
        kn = knope[:, j * MLA_NOPE_DIM:(j + 1) * MLA_NOPE_DIM]
        ms = (jnp.sum(kn * kn, axis=-1, keepdims=True) + ssq_r) * (1.0 / MLA_QK_DIM)
        r = lax.rsqrt(ms + EPS)
        base = j * MLA_HEAD_PAD
        km_ref[0, :, base:base + MLA_NOPE_DIM] = (kn * r * g_nope).astype(BF16)
        km_ref[0, :, base + MLA_NOPE_DIM:base + MLA_HEAD_PAD] = (kr_roped * r).astype(BF16)

    if not with_q:
        return

    qq = _dot(hb, wq_ref[...])
    q_scale = GQA_HEAD_DIM ** -0.5 * LOG2_E
    for j in range(GQA_HEADS):
        sl = slice(j * GQA_HEAD_DIM, (j + 1) * GQA_HEAD_DIM)
        qn = _rms(qq[:, sl], gq_ref[...], GQA_HEAD_DIM)
        q_ref[0, :, sl] = (_rope(qn, cg, sg, 32) * q_scale).astype(BF16)

    qa = _rms(_dot(hb, wqa_ref[...]), gqa_ref[...], MLA_Q_RANK).astype(BF16)
    q2 = _dot(qa, wqb_ref[...])
    gmq = gmq_ref[...]
    gq_nope, gq_rope = gmq[:, :MLA_NOPE_DIM], gmq[:, MLA_NOPE_DIM:]
    m_scale = MLA_QK_DIM ** -0.5 * LOG2_E
    for j in range(MLA_HEADS):
        base = j * MLA_HEAD_PAD
        qn = q2[:, base:base + MLA_NOPE_DIM]
        qr = q2[:, base + MLA_NOPE_DIM:base + MLA_HEAD_PAD]
        ms = (jnp.sum(qn * qn, axis=-1, keepdims=True) + jnp.sum(qr * qr, axis=-1, keepdims=True)) * (1.0 / MLA_QK_DIM)
        r = lax.rsqrt(ms + EPS)
        qm_ref[0, :, base:base + MLA_NOPE_DIM] = (qn * r * gq_nope * m_scale).astype(BF16)
        qm_ref[0, :, base + MLA_NOPE_DIM:base + MLA_HEAD_PAD] = (_rope(qr * r * gq_rope, cm, sm, 16) * m_scale).astype(BF16)

    gl = _dot(hb, wg_ref[...])
    gate_ref[0] = (0.5 + 0.5 * jnp.tanh(0.5 * gl)).astype(BF16)


def _proj(x, mod3, mod_row_of_batch, tables, w, with_q, tm):
    b, l, _ = x.shape
    cg, sg, cm, sm = tables
    const = lambda shape: pl.BlockSpec(shape, lambda bi, i: (0,) * len(shape), pipeline_mode=pl.Buffered(1))
    tab = pl.BlockSpec((tm, LANES), lambda bi, i: (i, 0))
    in_specs = [
        pl.BlockSpec((1, tm, D_MODEL), lambda bi, i: (bi, i, 0)),
        pl.BlockSpec((1, 6, D_MODEL), lambda bi, i: (mod_row_of_batch(bi), 0, 0)),
        const((1, D_MODEL)), tab, tab, tab, tab,
    ]
    if with_q:
        weights = [w["wq"], w["wk"], w["wv"], w["wqa"], w["wckv"], w["wkr"], w["wg"], w["wqb"], w["wkb"], w["wvb"],
                   w["gq"], w["gk"], w["gqa"], w["gkva"], w["gmq"], w["gmk"]]
    else:
        weights = [w["wk"], w["wv"], w["wckv"], w["wkr"], w["wkb"], w["wvb"], w["gk"], w["gkva"], w["gmk"]]
    in_specs += [const(a.shape) for a in weights]

    def out(width):
        return jax.ShapeDtypeStruct((b, l, width), BF16), pl.BlockSpec((1, tm, width), lambda bi, i: (bi, i, 0))

    def out_t(width):
        return jax.ShapeDtypeStruct((b, width, l), BF16), pl.BlockSpec((1, width, tm), lambda bi, i: (bi, 0, i))

    outs = [out(GQA_KV_W), out_t(GQA_KV_W), out(MLA_HEADS * MLA_HEAD_PAD), out_t(MLA_V_W)]
    if with_q:
        outs = [out(GQA_Q_W)] + outs[:2] + [out(MLA_HEADS * MLA_HEAD_PAD)] + outs[2:] + [out(2 * D_MODEL)]
    return pl.pallas_call(
        functools.partial(_proj_kernel, with_q=with_q),
        out_shape=[o[0] for o in outs],
        grid=(b, l // tm),
        in_specs=in_specs,
        out_specs=[o[1] for o in outs],
        compiler_params=pltpu.CompilerParams(dimension_semantics=("parallel", "parallel"),
                                             vmem_limit_bytes=56 * 1024 * 1024),
        name="proj_latent" if with_q else "proj_ctx",
    )(x, mod3, w["norm_mix"], cg, sg, cm, sm, *weights)


def _attn_kernel(q_ref, kl_ref, vl_ref, kc_ref, vc_ref, o_ref, *, tk, n_lat, bounded):
    q = q_ref[0]
    tq = q.shape[0]
    dv = vl_ref.shape[1]
    n = n_lat // tk

    if bounded:
        l = jnp.zeros((1, tq), F32)
        acc = jnp.zeros((dv, tq), F32)
        for k, vt in [(kl_ref[0, j * tk:(j + 1) * tk, :], vl_ref[0, :, j * tk:(j + 1) * tk]) for j in range(n)] + [
                (kc_ref[0], vc_ref[0])]:
            p = jnp.exp2(_dot_nt(k, q))
            l = l + jnp.sum(p, axis=0, keepdims=True)
            acc = acc + _dot(vt, p.astype(BF16))
        o_ref[0] = (acc / l).T.astype(BF16)
        return

    def update(carry, s, vt):
        m, l, acc = carry
        m_new = jnp.maximum(m, jnp.max(s, axis=0, keepdims=True))
        alpha = jnp.exp2(m - m_new)
        p = jnp.exp2(s - m_new)
        l = alpha * l + jnp.sum(p, axis=0, keepdims=True)
        return m_new, l, alpha * acc + _dot(vt, p.astype(BF16))

    n = n_lat // tk
    keys = [kl_ref[0, j * tk:(j + 1) * tk, :] for j in range(n)] + [kc_ref[0]]
    vals = [vl_ref[0, :, j * tk:(j + 1) * tk] for j in range(n)] + [vc_ref[0]]
    carry = (jnp.full((1, tq), NEG_INF, F32), jnp.zeros((1, tq), F32), jnp.zeros((dv, tq), F32))
    s = _dot_nt(keys[0], q)
    for j in range(n + 1):
        s_next = _dot_nt(keys[j + 1], q) if j < n else None
        carry = update(carry, s, vals[j])
        s = s_next
    m, l, acc = carry
    o_ref[0] = (acc / l).T.astype(BF16)


SOFTMAX_SAFE_EXPONENT = 56.0


def _score_bound(gain_q, gain_k, dim):
    return dim * jnp.max(jnp.abs(gain_q)) * jnp.max(jnp.abs(gain_k)) * (dim ** -0.5 * LOG2_E) * 1.02


def _attention(q, k_lat, v_lat, k_ctx, v_ctx, n_heads, group, d_qk, dv, tq, tk, name, score_bound):
    run = functools.partial(_attention_call, q, k_lat, v_lat, k_ctx, v_ctx, n_heads, group, d_qk, dv, tq, tk, name)
    return lax.cond(score_bound <= SOFTMAX_SAFE_EXPONENT, lambda: run(True), lambda: run(False))


def _attention_call(q, k_lat, v_lat, k_ctx, v_ctx, n_heads, group, d_qk, dv, tq, tk, name, bounded):
    b, l, _ = q.shape
    lc = k_ctx.shape[1]
    name = name + ("_bounded" if bounded else "_online")
    return pl.pallas_call(
        functools.partial(_attn_kernel, tk=tk, n_lat=l, bounded=bounded),
        out_shape=jax.ShapeDtypeStruct((b, l, n_heads * dv), BF16),
        grid=(b, n_heads, l // tq),
        in_specs=[
            pl.BlockSpec((1, tq, d_qk), lambda bi, h, i: (bi, i, h)),
            pl.BlockSpec((1, l, d_qk), lambda bi, h, i: (bi, 0, h // group)),
            pl.BlockSpec((1, dv, l), lambda bi, h, i: (bi, h // group, 0)),
            pl.BlockSpec((1, lc, d_qk), lambda bi, h, i: (bi, 0, h // group)),
            pl.BlockSpec((1, dv, lc), lambda bi, h, i: (bi, h // group, 0)),
        ],
        out_specs=pl.BlockSpec((1, tq, dv), lambda bi, h, i: (bi, i, h)),
        compiler_params=pltpu.CompilerParams(dimension_semantics=("parallel", "parallel", "parallel"),
                                             vmem_limit_bytes=56 * 1024 * 1024),
        name=name,
    )(q, k_lat, v_lat, k_ctx, v_ctx)


def _merge_kernel(oa_ref, ob_ref, gate_ref, x_ref, mod_ref, woa_ref, wob_ref, wout_ref, nffn_ref,
                  rwh_ref, rwl_ref, rb_ref, x1_ref, h2_ref, eidx_ref, egate_ref, rank_ref, before_ref, tcnt_ref,
                  cnt_ref, carry_ref):
    @pl.when((pl.program_id(0) == 0) & (pl.program_id(1) == 0))
    def _():
        carry_ref[...] = jnp.zeros_like(carry_ref)

    mod = mod_ref[0]
    g1, shift2, scale2 = mod[2:3, :], mod[3:4, :], mod[4:5, :]
    ya = _dot(oa_ref[0], woa_ref[...])
    yb = _dot(ob_ref[0], wob_ref[...])
    g = gate_ref[0].astype(F32)
    y = g[:, :D_MODEL] * ya + g[:, D_MODEL:] * yb
    z = _dot(y.astype(BF16), wout_ref[...])
    x1 = x_ref[0] + g1 * z
    x1_ref[0] = x1
    h2 = _rms(x1, nffn_ref[...], D_MODEL) * (1.0 + scale2) + shift2
    h2_ref[0] = h2.astype(BF16)

    h_hi, h_lo = _split_bf16(h2)
    logits = _dot(h_hi, rwh_ref[...]) + (_dot(h_hi, rwl_ref[...]) + _dot(h_lo, rwh_ref[...])) + rb_ref[...]
    lane = lax.broadcasted_iota(jnp.int32, logits.shape, 1).astype(F32)
    cur = jnp.where(lane < N_EXPERTS, logits, NEG_INF)
    vals, idxs = [], []
    chosen = jnp.zeros(logits.shape, F32)
    for _ in range(TOP_K):
        mx = jnp.max(cur, axis=-1, keepdims=True)
        ix = jnp.min(jnp.where(cur == mx, lane, float(LANES)), axis=-1, keepdims=True)
        vals.append(mx)
        idxs.append(ix)
        sel = lane == ix
        chosen = chosen + jnp.where(sel, 1.0, 0.0)
        cur = jnp.where(sel, NEG_INF, cur)
    ex = [jnp.exp(v - vals[0]) for v in vals]
    den = ex[0] + ex[1] + ex[2] + ex[3]
    eidx = jnp.zeros(logits.shape, F32)
    egate = jnp.zeros(logits.shape, F32)
    for k in range(TOP_K):
        eidx = jnp.where(lane == k, idxs[k], eidx)
        egate = jnp.where(lane == k, ex[k] / den, egate)
    eidx = eidx.astype(jnp.int32)
    eidx_ref[0] = eidx
    egate_ref[0] = egate

    for s in range(eidx.shape[0] // MOE_TILE):
        rows = slice(s * MOE_TILE, (s + 1) * MOE_TILE)
        before_ref[s] = carry_ref[...]
        rank, tile_cnt = _route_tile(eidx[rows], chosen[rows], carry_ref)
        rank_ref[0, rows, :] = rank
        tcnt_ref[s] = jnp.broadcast_to(tile_cnt, carry_ref.shape)
    cnt_ref[...] = carry_ref[...]


def _merge(o_a, o_b, gates, x, mod3, w, tm):
    b, l, _ = x.shape
    const = lambda shape: pl.BlockSpec(shape, lambda bi, i: (0,) * len(shape), pipeline_mode=pl.Buffered(1))
    tok = lambda width: pl.BlockSpec((1, tm, width), lambda bi, i: (bi, i, 0))
    assert tm % MOE_TILE == 0
    sub = tm // MOE_TILE
    per_tile = jax.ShapeDtypeStruct((b * l // MOE_TILE, 8, LANES), F32)
    tile_spec = pl.BlockSpec((sub, 8, LANES), lambda bi, i: (bi * (l // tm) + i, 0, 0))
    return pl.pallas_call(
        _merge_kernel,
        out_shape=[jax.ShapeDtypeStruct((b, l, D_MODEL), F32), jax.ShapeDtypeStruct((b, l, D_MODEL), BF16),
                   jax.ShapeDtypeStruct((b, l, LANES), jnp.int32), jax.ShapeDtypeStruct((b, l, LANES), F32),
                   jax.ShapeDtypeStruct((b, l, LANES), jnp.int32), per_tile, per_tile,
                   jax.ShapeDtypeStruct((8, LANES), F32)],
        grid=(b, l // tm),
        in_specs=[tok(GQA_Q_W), tok(MLA_V_W), tok(2 * D_MODEL), tok(D_MODEL),
                  pl.BlockSpec((1, 6, D_MODEL), lambda bi, i: (bi, 0, 0)),
                  const((GQA_Q_W, D_MODEL)), const((MLA_V_W, D_MODEL)), const((D_MODEL, D_MODEL)), const((1, D_MODEL)),
                  const((D_MODEL, LANES)), const((D_MODEL, LANES)), const((1, LANES))],
        out_specs=[tok(D_MODEL), tok(D_MODEL), tok(LANES), tok(LANES), tok(LANES), tile_spec, tile_spec,
                   pl.BlockSpec((8, LANES), lambda bi, i: (0, 0))],
        scratch_shapes=[pltpu.VMEM((8, LANES), F32)],
        compiler_params=pltpu.CompilerParams(dimension_semantics=("arbitrary", "arbitrary"),
                                             vmem_limit_bytes=56 * 1024 * 1024),
        name="merge_router",
    )(o_a, o_b, gates, x, mod3, w["woa"], w["wob"], w["wout"], w["norm_ffn"], w["rw_hi"], w["rw_lo"], w["rb"])


SEG_ALIGN = 16


def _route_tile(idx, total, carry_ref):
    tm = idx.shape[0]
    lane = lax.broadcasted_iota(jnp.int32, idx.shape, 1)
    row = lax.broadcasted_iota(jnp.int32, (tm, tm), 0)
    col = lax.broadcasted_iota(jnp.int32, (tm, tm), 1)
    tri = jnp.where(row > col, 1.0, 0.0).astype(BF16)
    before = _dot(tri, total.astype(BF16)) + carry_ref[0:1, :]
    rank = jnp.where(lane < TOP_K, jnp.take_along_axis(before, idx, axis=1), 0.0)
    tile_cnt = jnp.sum(total, axis=0, keepdims=True)
    carry_ref[...] = carry_ref[...] + tile_cnt
    return rank.astype(jnp.int32), tile_cnt


def _dest_tables(idx, rank, region_starts, seg_starts):
    valid = lax.broadcasted_iota(jnp.int32, idx.shape, 1) < TOP_K
    region_start = jnp.take_along_axis(jnp.broadcast_to(region_starts, idx.shape), idx, axis=1)
    seg_start = jnp.take_along_axis(jnp.broadcast_to(seg_starts, idx.shape), idx, axis=1)
    rank = rank.astype(F32)
    window_start = jnp.floor(seg_start * (1.0 / SEG_ALIGN)) * SEG_ALIGN
    local = rank - window_start
    dest = jnp.where(valid, region_start + rank, 0.0)
    col = jnp.where(valid & (local < WIN), idx.astype(F32) * WIN + local, -1.0)
    return dest, col


MOE_TILE = 256
WIN = 64
N_MAIN_ROWS = N_EXPERTS * WIN
MAIN_CHUNK = 512
WIN_PER_STACK = MOE_TILE // WIN
MAX_WINDOWS = (MOE_TILE * TOP_K + N_EXPERTS * SEG_ALIGN) // WIN
CARRY_ROWS = N_EXPERTS * SEG_ALIGN
WORKLIST_LEN = -(-(MAX_WINDOWS + WIN_PER_STACK) // WIN_PER_STACK) * WIN_PER_STACK


def _build_worklist(ts_ref, tc_ref, wl_ref, junk_row):
    base = pl.program_id(0) * N_EXPERTS

    def per_expert(e, n):
        first = ts_ref[base + e]
        n_win = (tc_ref[base + e] + (WIN - 1)) // WIN

        def per_window(wi, n):
            wl_ref[n] = first + wi * WIN
            return n + 1

        return lax.fori_loop(1, n_win, per_window, n)

    n = lax.fori_loop(0, N_EXPERTS, per_expert, 0)
    n_stacks = (n + (WIN_PER_STACK - 1)) // WIN_PER_STACK

    def pad(j, c):
        wl_ref[j] = junk_row + (j % WIN_PER_STACK) * WIN
        return c

    lax.fori_loop(n, n_stacks * WIN_PER_STACK, pad, 0)
    return n_stacks


def _seg_aligned(row):
    return row if isinstance(row, int) else pl.multiple_of(row, SEG_ALIGN)


def _stack_row_ids(wl_ref, stack, shape, axis):
    pos = lax.broadcasted_iota(jnp.int32, shape, axis)
    row = jnp.full(shape, -1, jnp.int32)
    for wi in range(WIN_PER_STACK - 1, -1, -1):
        row = jnp.where(pos < (wi + 1) * WIN, wl_ref[stack * WIN_PER_STACK + wi] + (pos - wi * WIN), row)
    return row


def _select_rows(row_id, id_t):
    shape = (row_id.shape[0], id_t.shape[1])
    sel = jnp.zeros(shape, F32)
    for k in range(TOP_K):
        hit = row_id == id_t[k:k + 1, :]
        sel = jnp.where(hit, 1.0, sel)
    return sel.astype(BF16)


MAIN_PENDING = 2


TAIL_ROWS = WIN + EXPERT_BLOCK


def _dispatch_kernel(ts_ref, tc_ref, ov_ref, re_ref, h_ref, eidx_ref, rank_ref, start_ref, before_ref,
                     buf_ref, dest_ref, col_ref, wl_ref, pend_ref, main_x, carry_x, stage_x, main_sem, sem, *,
                     junk_row):
    i = pl.program_id(0)
    base = i * N_EXPERTS
    next_base = jnp.minimum(i + 1, pl.num_programs(0) - 1) * N_EXPERTS

    @pl.when(i == 0)
    def _():
        pend_ref[0] = 0
        pend_ref[1] = 0
        pend_ref[MAIN_PENDING] = 0
        carry_x[...] = jnp.zeros_like(carry_x)
        main_x[0:TAIL_ROWS, :] = jnp.zeros((TAIL_ROWS, D_MODEL), BF16)
        cps = []
        for e in range(N_EXPERTS):
            rows = pl.ds(pl.multiple_of(jnp.maximum(re_ref[e] - TAIL_ROWS, 0), SEG_ALIGN), TAIL_ROWS)
            cps.append(pltpu.make_async_copy(main_x.at[pl.ds(0, TAIL_ROWS)], buf_ref.at[rows], main_sem))
        for cp in cps:
            cp.start()
        for cp in cps:
            cp.wait()

        def zero_block_copies(blk):
            rows = pl.ds(pl.multiple_of(blk * EXPERT_BLOCK, EXPERT_BLOCK), EXPERT_BLOCK)
            return (pltpu.make_async_copy(main_x.at[pl.ds(0, EXPERT_BLOCK)], buf_ref.at[rows], main_sem),)

        def start_zero(blk, c):
            for cp in zero_block_copies(blk):
                cp.start()
            return c

        def wait_zero(blk, c):
            for cp in zero_block_copies(blk):
                cp.wait()
            return c

        first_free, n_blocks = re_ref[N_EXPERTS - 1] // EXPERT_BLOCK, (junk_row + EXPERT_BLOCK) // EXPERT_BLOCK
        lax.fori_loop(first_free, n_blocks, start_zero, 0)
        lax.fori_loop(first_free, n_blocks, wait_zero, 0)

    h = h_ref[...]

    def main_copies(first_row):
        cps = []
        for e in range(N_EXPERTS):
            rows = pl.ds(_seg_aligned(first_row(e)), WIN)
            cps.append(pltpu.make_async_copy(main_x.at[pl.ds(e * WIN, WIN)], buf_ref.at[rows], main_sem))
        return cps

    def drain_main():
        @pl.when(pend_ref[MAIN_PENDING] == 1)
        def _():
            for cp in main_copies(lambda e: 0):
                cp.wait()
            pend_ref[MAIN_PENDING] = 0

    dest, col = _dest_tables(eidx_ref[...], rank_ref[...], start_ref[0:1, :], before_ref[0][0:1, :])
    dest_ref[...] = dest.astype(jnp.int32)
    col_ref[...] = col.astype(jnp.int32)
    dest_t = dest.T[0:8, :].astype(jnp.int32)
    col_t = col.T[0:8, :].astype(jnp.int32)
    pieces = []
    for c in range(N_MAIN_ROWS // MAIN_CHUNK):
        row_id = lax.broadcasted_iota(jnp.int32, (MAIN_CHUNK, 1), 0) + c * MAIN_CHUNK
        pieces.append(_dot(_select_rows(row_id, col_t), h).astype(BF16))
    group_pos = lax.broadcasted_iota(jnp.int32, (SEG_ALIGN, 1), 0)
    end_group = jnp.concatenate([ts_ref[next_base + e] + group_pos for e in range(N_EXPERTS)], axis=0)
    end_rows = _dot(_select_rows(end_group, dest_t), h).astype(BF16)

    def window_copies(slot, first_row):
        cps = []
        for wi in range(WIN_PER_STACK):
            rows = pl.ds(_seg_aligned(first_row(wi)), WIN)
            cps.append(pltpu.make_async_copy(stage_x.at[slot, pl.ds(wi * WIN, WIN)], buf_ref.at[rows], sem.at[slot]))
        return cps

    def drain(slot):
        @pl.when(pend_ref[slot] == 1)
        def _():
            for cp in window_copies(slot, lambda wi: 0):
                cp.wait()
            pend_ref[slot] = 0

    drain_main()
    drain(0)
    drain(1)
    for c, packed in enumerate(pieces):
        main_x[c * MAIN_CHUNK:(c + 1) * MAIN_CHUNK, :] = packed
    for e in range(N_EXPERTS):
        win, grp = pl.ds(e * WIN, SEG_ALIGN), pl.ds(e * SEG_ALIGN, SEG_ALIGN)
        main_x[win, :] = main_x[win, :] + carry_x[grp, :]
        same = ts_ref[next_base + e] == ts_ref[base + e]
        carry_x[grp, :] = end_rows[e * SEG_ALIGN:(e + 1) * SEG_ALIGN, :] + jnp.where(same, carry_x[grp, :], 0).astype(BF16)
    for n, cp in enumerate(main_copies(lambda e: ts_ref[base + e])):
        cp.start(priority=n % 2)
    pend_ref[MAIN_PENDING] = 1

    @pl.when(ov_ref[i] > 0)
    def _():
        n_stacks = _build_worklist(ts_ref, tc_ref, wl_ref, junk_row)

        def stack_body(s, c):
            slot = s & 1
            rows = _dot(_select_rows(_stack_row_ids(wl_ref, s, (MOE_TILE, 1), 0), dest_t), h)
            drain(slot)
            stage_x[slot] = rows.astype(BF16)
            for cp in window_copies(slot, lambda wi: wl_ref[s * WIN_PER_STACK + wi]):
                cp.start()
            pend_ref[slot] = 1
            return c

        lax.fori_loop(0, n_stacks, stack_body, 0)

    @pl.when(i == pl.num_programs(0) - 1)
    def _():
        drain_main()
        drain(0)
        drain(1)


def _moe_grid_spec(n_tiles, in_specs, out_specs, scratch_shapes):
    return pltpu.PrefetchScalarGridSpec(num_scalar_prefetch=4, grid=(n_tiles,), in_specs=in_specs,
                                        out_specs=out_specs, scratch_shapes=scratch_shapes)


def _dispatch(tile_start, tile_cnt, tile_ovf, region_ends, h2, eidx, rank, start, before, n_rows):
    t = h2.shape[0]
    n_tiles = t // MOE_TILE
    junk_row = n_rows - EXPERT_BLOCK
    tok = lambda width: pl.BlockSpec((MOE_TILE, width), lambda i, *_: (i, 0))
    table = jax.ShapeDtypeStruct((t, LANES), jnp.int32)
    return pl.pallas_call(
        functools.partial(_dispatch_kernel, junk_row=junk_row),
        out_shape=[jax.ShapeDtypeStruct((n_rows, D_MODEL), BF16), table, table],
        grid_spec=_moe_grid_spec(
            n_tiles,
            [tok(D_MODEL), tok(LANES), tok(LANES), pl.BlockSpec((8, LANES), lambda i, *_: (0, 0)),
             pl.BlockSpec((1, 8, LANES), lambda i, *_: (i, 0, 0))],
            [pl.BlockSpec(memory_space=pl.ANY), tok(LANES), tok(LANES)],
            [pltpu.SMEM((WORKLIST_LEN,), jnp.int32), pltpu.SMEM((3,), jnp.int32),
             pltpu.VMEM((N_MAIN_ROWS, D_MODEL), BF16), pltpu.VMEM((CARRY_ROWS, D_MODEL), BF16),
             pltpu.VMEM((2, MOE_TILE, D_MODEL), BF16),
             pltpu.SemaphoreType.DMA(()), pltpu.SemaphoreType.DMA((2,))]),
        compiler_params=pltpu.CompilerParams(dimension_semantics=("arbitrary",),
                                             vmem_limit_bytes=56 * 1024 * 1024),
        name="moe_dispatch",
    )(tile_start, tile_cnt, tile_ovf, region_ends, h2, eidx, rank, start, before)


def _select_cols(ids, gates, col_id):
    sel = jnp.zeros((ids.shape[0], col_id.shape[1]), F32)
    for k in range(TOP_K):
        sel = jnp.where(ids[:, k:k + 1] == col_id, gates[:, k:k + 1], sel)
    return sel.astype(BF16)


def _combine_kernel(ts_ref, tc_ref, ov_ref, re_ref, col_ref, dest_ref, egate_ref, x1_ref, mod_ref, y_ref, o_ref,
                    wl_ref, main_y, stage_y, acc_ref, main_sem, sem, *, junk_row):
    del re_ref
    i = pl.program_id(0)
    slot_i = i & 1

    def main_copies(slot, first_row):
        return [pltpu.make_async_copy(y_ref.at[pl.ds(_seg_aligned(first_row(e)), WIN)],
                                      main_y.at[slot, pl.ds(e * WIN, WIN)], main_sem.at[slot])
                for e in range(N_EXPERTS)]

    def fetch_main(tile, slot):
        for n, cp in enumerate(main_copies(slot, lambda e: ts_ref[tile * N_EXPERTS + e])):
            cp.start(priority=n % 2)

    @pl.when(i == 0)
    def _():
        fetch_main(0, 0)

    @pl.when(i + 1 < pl.num_programs(0))
    def _():
        fetch_main(i + 1, 1 - slot_i)

    for cp in main_copies(slot_i, lambda e: 0):
        cp.wait()

    col = col_ref[...]
    egate = egate_ref[...]
    sel = jnp.concatenate(
        [_select_cols(col, egate, lax.broadcasted_iota(jnp.int32, (1, MAIN_CHUNK), 1) + c * MAIN_CHUNK)
         for c in range(N_MAIN_ROWS // MAIN_CHUNK)], axis=1)
    acc_ref[...] = _dot(sel, main_y[slot_i])

    @pl.when(ov_ref[i] > 0)
    def _():
        n_stacks = _build_worklist(ts_ref, tc_ref, wl_ref, junk_row)
        dest = dest_ref[...]
        if WIN_PER_STACK * WIN < MOE_TILE:
            for slot in range(2):
                stage_y[slot, WIN_PER_STACK * WIN:, :] = jnp.zeros((MOE_TILE - WIN_PER_STACK * WIN, D_MODEL), BF16)

        def window_copies(slot, first_row):
            return [pltpu.make_async_copy(y_ref.at[pl.ds(_seg_aligned(first_row(wi)), WIN)],
                                          stage_y.at[slot, pl.ds(wi * WIN, WIN)], sem.at[slot])
                    for wi in range(WIN_PER_STACK)]

        def fetch(s, slot):
            for cp in window_copies(slot, lambda wi: wl_ref[s * WIN_PER_STACK + wi]):
                cp.start()

        @pl.when(n_stacks > 0)
        def _():
            fetch(0, 0)

        def stack_body(s, c):
            slot = s & 1

            @pl.when(s + 1 < n_stacks)
            def _():
                fetch(s + 1, 1 - slot)

            for cp in window_copies(slot, lambda wi: 0):
                cp.wait()
            sel_o = _select_cols(dest, egate, _stack_row_ids(wl_ref, s, (1, MOE_TILE), 1))
            acc_ref[...] += _dot(sel_o, stage_y[slot])
            return c

        lax.fori_loop(0, n_stacks, stack_body, 0)

    g2 = mod_ref[0][5:6, :]
    o_ref[...] = x1_ref[...] + g2 * acc_ref[...]


def _combine(tile_start, tile_cnt, tile_ovf, region_ends, col, dest, egate, x1, mod3, y, l):
    t = x1.shape[0]
    per_batch = l // MOE_TILE
    junk_row = y.shape[0] - EXPERT_BLOCK
    tok = lambda width: pl.BlockSpec((MOE_TILE, width), lambda i, *_: (i, 0))
    return pl.pallas_call(
        functools.partial(_combine_kernel, junk_row=junk_row),
        out_shape=jax.ShapeDtypeStruct((t, D_MODEL), F32),
        grid_spec=_moe_grid_spec(
            t // MOE_TILE,
            [tok(LANES), tok(LANES), tok(LANES), tok(D_MODEL),
             pl.BlockSpec((1, 6, D_MODEL), lambda i, *_: (i // per_batch, 0, 0)),
             pl.BlockSpec(memory_space=pl.ANY)],
            tok(D_MODEL),
            [pltpu.SMEM((WORKLIST_LEN,), jnp.int32), pltpu.VMEM((2, N_MAIN_ROWS, D_MODEL), BF16),
             pltpu.VMEM((2, MOE_TILE, D_MODEL), BF16), pltpu.VMEM((MOE_TILE, D_MODEL), F32),
             pltpu.SemaphoreType.DMA((2,)), pltpu.SemaphoreType.DMA((2,))]),
        compiler_params=pltpu.CompilerParams(dimension_semantics=("arbitrary",),
                                             vmem_limit_bytes=56 * 1024 * 1024),
        name="moe_combine",
    )(tile_start, tile_cnt, tile_ovf, region_ends, col, dest, egate, x1, mod3, y)


def _expert_kernel(be_ref, nused_ref, x_ref, w1_ref, b1_ref, w2_ref, b2_ref, y_ref, w1b_ref, w2b_ref):
    i = pl.program_id(0)
    live = i < nused_ref[0]

    @pl.when(live & ((i == 0) | (be_ref[i] != be_ref[jnp.maximum(i - 1, 0)])))
    def _():
        w1b_ref[...] = w1_ref[0].astype(BF16)
        w2b_ref[...] = w2_ref[0].astype(BF16)

    @pl.when(live)
    def _():
        gu = _dot(x_ref[...], w1b_ref[...]) + b1_ref[0]
        glu = jnp.minimum(gu[:, :D_EXPERT], SWIGLU_LIMIT)
        lin = jnp.clip(gu[:, D_EXPERT:], -SWIGLU_LIMIT, SWIGLU_LIMIT)
        act = (0.5 * glu) * (1.0 + jnp.tanh((0.5 * SWIGLU_ALPHA) * glu)) * (lin + 1.0)
        y_ref[...] = (_dot(act.astype(BF16), w2b_ref[...]) + b2_ref[0]).astype(BF16)

    @pl.when(pl.program_id(0) >= nused_ref[0])
    def _():
        y_ref[...] = jnp.zeros_like(y_ref)


def _experts(block_e, nused, buf, w1, b1, w2, b2):
    nb = buf.shape[0] // EXPERT_BLOCK
    row = lambda i, be, nu: (jnp.minimum(i, nu[0] - 1), 0)
    out_row = lambda i, be, nu: (i, 0)
    exp3 = lambda i, be, nu: (be[jnp.minimum(i, nu[0] - 1)], 0, 0)
    return pl.pallas_call(
        _expert_kernel,
        out_shape=jax.ShapeDtypeStruct((buf.shape[0], D_MODEL), BF16),
        grid_spec=pltpu.PrefetchScalarGridSpec(
            num_scalar_prefetch=2,
            grid=(nb,),
            in_specs=[pl.BlockSpec((EXPERT_BLOCK, D_MODEL), row),
                      pl.BlockSpec((1, D_MODEL, 2 * D_EXPERT), exp3),
                      pl.BlockSpec((1, 1, 2 * D_EXPERT), exp3),
                      pl.BlockSpec((1, D_EXPERT, D_MODEL), exp3),
                      pl.BlockSpec((1, 1, D_MODEL), exp3)],
            out_specs=pl.BlockSpec((EXPERT_BLOCK, D_MODEL), out_row),
            scratch_shapes=[pltpu.VMEM((D_MODEL, 2 * D_EXPERT), BF16), pltpu.VMEM((D_EXPERT, D_MODEL), BF16)]),
        compiler_params=pltpu.CompilerParams(dimension_semantics=("arbitrary",),
                                             vmem_limit_bytes=56 * 1024 * 1024),
        name="moe_experts",
    )(block_e, nused, buf, w1, b1, w2, b2)


def _rope_tables(seq_len):
    pos = np.arange(seq_len, dtype=np.int32)
    row = (pos // GRID_W).astype(np.float32)[:, None]
    col = (pos % GRID_W).astype(np.float32)[:, None]

    def table(d_axis, pad):
        inv_freq = (np.float32(ROPE_THETA) ** (-np.arange(0, d_axis, 2, dtype=np.float32) / np.float32(d_axis)))
        inv_freq = inv_freq.astype(np.float32)
        ar, ac = row * inv_freq[None, :], col * inv_freq[None, :]
        cos = np.concatenate([np.cos(ar), np.cos(ar), np.cos(ac), np.cos(ac)], axis=-1)
        sin = np.concatenate([-np.sin(ar), np.sin(ar), -np.sin(ac), np.sin(ac)], axis=-1)
        if pad:
            cos = np.concatenate([cos, np.ones((seq_len, pad), np.float32)], axis=-1)
            sin = np.concatenate([sin, np.zeros((seq_len, pad), np.float32)], axis=-1)
        return jnp.asarray(cos, F32), jnp.asarray(sin, F32)

    cg, sg = table(GQA_HEAD_DIM // 2, 0)
    cm, sm = table(MLA_ROPE_DIM // 2, LANES - MLA_ROPE_DIM)
    return cg, sg, cm, sm


def _pad_lanes(a, width):
    return jnp.pad(a, [(0, 0)] * (a.ndim - 1) + [(0, width - a.shape[-1])])


def _prep_weights(p):
    w_in = p["w_in"]
    w = {"norm_mix": p["norm_mix"].reshape(1, D_MODEL), "norm_ffn": p["norm_ffn"].reshape(1, D_MODEL)}
    w["wq"] = w_in[:, :OFF_GQA_K].astype(BF16)
    w["wk"] = w_in[:, OFF_GQA_K:OFF_GQA_V].astype(BF16)
    w["wv"] = w_in[:, OFF_GQA_V:OFF_MLA_QA].T.astype(BF16)
    w["wqa"] = w_in[:, OFF_MLA_QA:OFF_MLA_KVA].astype(BF16)
    w["wckv"] = w_in[:, OFF_MLA_KVA:OFF_MLA_KVA + MLA_KV_RANK].astype(BF16)
    w["wkr"] = _pad_lanes(w_in[:, OFF_MLA_KVA + MLA_KV_RANK:OFF_GATE], LANES).astype(BF16)
    w["wg"] = w_in[:, OFF_GATE:].astype(BF16)
    wqb = p["mla_w_qb"].reshape(MLA_Q_RANK, MLA_HEADS, MLA_QK_DIM)
    w["wqb"] = _pad_lanes(wqb, MLA_HEAD_PAD).reshape(MLA_Q_RANK, MLA_HEADS * MLA_HEAD_PAD).astype(BF16)
    wkvb = p["mla_w_kvb"].reshape(MLA_KV_RANK, MLA_HEADS, MLA_NOPE_DIM + MLA_V_DIM)
    w["wkb"] = wkvb[:, :, :MLA_NOPE_DIM].reshape(MLA_KV_RANK, MLA_HEADS * MLA_NOPE_DIM).astype(BF16)
    w["wvb"] = wkvb[:, :, MLA_NOPE_DIM:].reshape(MLA_KV_RANK, MLA_V_W).T.astype(BF16)
    w["gq"] = p["gqa_q_norm"].reshape(1, GQA_HEAD_DIM)
    w["gk"] = p["gqa_k_norm"].reshape(1, GQA_HEAD_DIM)
    w["gqa"] = p["mla_q_a_norm"].reshape(1, MLA_Q_RANK)
    w["gkva"] = p["mla_kv_a_norm"].reshape(1, MLA_KV_RANK)
    w["gmq"] = _pad_lanes(p["mla_q_norm"].reshape(1, MLA_QK_DIM), MLA_HEAD_PAD)
    w["gmk"] = _pad_lanes(p["mla_k_norm"].reshape(1, MLA_QK_DIM), MLA_HEAD_PAD)
    w["woa"] = p["w_o_gqa"].astype(BF16)
    w["wob"] = p["w_o_mla"].astype(BF16)
    w["wout"] = p["w_out"].astype(BF16)
    rw = _pad_lanes(p["router_w"], LANES)
    w["rw_hi"] = rw.astype(BF16)
    w["rw_lo"] = (rw - w["rw_hi"].astype(F32)).astype(BF16)
    w["rb"] = _pad_lanes(p["router_b"].reshape(1, N_EXPERTS), LANES)
    return w


def _layer(x, c, ctx, c_ctx, p):
    b, l, _ = x.shape
    lc = ctx.shape[1]
    t = b * l
    w = _prep_weights(p)

    n_mod_rows = -(-(b + 1) // 8) * 8
    cc = jnp.zeros((n_mod_rows, D_MODEL), F32).at[:b].set(c).at[b].set(c_ctx)
    mod3 = _ada_mod(cc, p["ada_w"], p["ada_b"]).reshape(n_mod_rows, 6, D_MODEL)

    tables = _rope_tables(l)
    ident = (jnp.ones((lc, LANES), F32), jnp.zeros((lc, LANES), F32)) * 2
    tm = min(512, l)
    q_a, k_a, v_a, q_m, k_m, v_m, gates = _proj(x, mod3, lambda bi: bi, tables, w, True, tm)
    kc_a, vc_a, kc_m, vc_m = _proj(ctx, mod3, lambda bi: b, ident, w, False, min(256, lc))

    tq, tk = min(2048, l), min(2048, l)
    o_a = _attention(q_a, k_a, v_a, kc_a, vc_a, GQA_HEADS, GQA_GROUP, GQA_HEAD_DIM, GQA_HEAD_DIM, tq, tk, "attn_gqa",
                     _score_bound(p["gqa_q_norm"], p["gqa_k_norm"], GQA_HEAD_DIM))
    o_m = _attention(q_m, k_m, v_m, kc_m, vc_m, MLA_HEADS, 1, MLA_HEAD_PAD, MLA_V_DIM, tq, tk, "attn_mla",
                     _score_bound(p["mla_q_norm"], p["mla_k_norm"], MLA_QK_DIM))

    x1, h2, eidx, egate, rank, before_raw, tcnt, cnt = _merge(o_a, o_m, gates, x, mod3, w, tm)
    x1, h2 = x1.reshape(t, D_MODEL), h2.reshape(t, D_MODEL)
    eidx, egate, rank = eidx.reshape(t, LANES), egate.reshape(t, LANES), rank.reshape(t, LANES)

    assert t % MOE_TILE == 0
    counts = cnt[0, :N_EXPERTS].astype(jnp.int32)
    before = before_raw[:, 0, :N_EXPERTS].astype(jnp.int32)
    region = (counts + WIN + EXPERT_BLOCK - 1) // EXPERT_BLOCK * EXPERT_BLOCK
    region_ends = jnp.cumsum(region)
    region_starts = region_ends - region
    max_rows = t * TOP_K + N_EXPERTS * (WIN + EXPERT_BLOCK - 1)
    n_blocks = max_rows // EXPERT_BLOCK + 1
    n_rows = n_blocks * EXPERT_BLOCK
    block_row = jnp.arange(n_blocks, dtype=jnp.int32) * EXPERT_BLOCK
    block_e = jnp.minimum(jnp.sum(region_ends[None, :] <= block_row[:, None], axis=1), N_EXPERTS - 1).astype(jnp.int32)
    nused = (region_ends[-1:] // EXPERT_BLOCK).astype(jnp.int32)
    tile_start = (region_starts[None, :] + before // SEG_ALIGN * SEG_ALIGN).reshape(-1)
    tile_cnt = before % SEG_ALIGN + tcnt[:, 0, :N_EXPERTS].astype(jnp.int32)
    tile_ovf = jnp.sum(jnp.maximum((tile_cnt + WIN - 1) // WIN - 1, 0), axis=1).astype(jnp.int32)
    tile_cnt = tile_cnt.reshape(-1)
    start = jnp.zeros((8, LANES), F32).at[0, :N_EXPERTS].set(region_starts.astype(F32))
    region_ends = region_ends.astype(jnp.int32)
    buf, dest, col = _dispatch(tile_start, tile_cnt, tile_ovf, region_ends, h2, eidx, rank, start, before_raw, n_rows)
    y = _experts(block_e, nused, buf, p["expert_w1"], p["expert_b1"].reshape(N_EXPERTS, 1, -1),
                 p["expert_w2"], p["expert_b2"].reshape(N_EXPERTS, 1, -1))
    out = _combine(tile_start, tile_cnt, tile_ovf, region_ends, col, dest, egate, x1, mod3, y, l)
    return out.reshape(b, l, D_MODEL)


def kernel(x, c, ctx, c_ctx, ada_w, ada_b, norm_mix, norm_ffn, w_in, gqa_q_norm, gqa_k_norm, mla_q_a_norm, mla_kv_a_norm, mla_w_qb, mla_w_kvb, mla_q_norm, mla_k_norm, w_o_gqa, w_o_mla, w_out, router_w, router_b, expert_w1, expert_b1, expert_w2, expert_b2):
    assert ada_w.shape[0] == 1, "single-layer problem: the context stream is never updated"
    p = {
        "ada_w": ada_w[0], "ada_b": ada_b[0], "norm_mix": norm_mix[0], "norm_ffn": norm_ffn[0],
        "w_in": w_in[0], "gqa_q_norm": gqa_q_norm[0], "gqa_k_norm": gqa_k_norm[0],
        "mla_q_a_norm": mla_q_a_norm[0], "mla_kv_a_norm": mla_kv_a_norm[0],
        "mla_w_qb": mla_w_qb[0], "mla_w_kvb": mla_w_kvb[0],
        "mla_q_norm": mla_q_norm[0], "mla_k_norm": mla_k_norm[0],
        "w_o_gqa": w_o_gqa[0], "w_o_mla": w_o_mla[0], "w_out": w_out[0],
        "router_w": router_w[0], "router_b": router_b[0],
        "expert_w1": expert_w1[0], "expert_b1": expert_b1[0],
        "expert_w2": expert_w2[0], "expert_b2": expert_b2[0],
    }
    return _layer(x, c, ctx, c_ctx, p)
```

```python
import functools
import math

import jax
import jax.numpy as jnp
import numpy as np
from jax import lax
from jax.experimental import pallas as pl
from jax.experimental.pallas import tpu as pltpu

D_MODEL = 1024
GRID_W = 64
EPS = 1e-6
ROPE_THETA = 10000.0

GQA_HEADS = 8
GQA_KV_HEADS = 2
GQA_GROUP = GQA_HEADS // GQA_KV_HEADS
GQA_HEAD_DIM = 128
GQA_Q_W = GQA_HEADS * GQA_HEAD_DIM
GQA_KV_W = GQA_KV_HEADS * GQA_HEAD_DIM

MLA_HEADS = 8
MLA_Q_RANK = 256
MLA_KV_RANK = 128
MLA_NOPE_DIM = 128
MLA_ROPE_DIM = 64
MLA_V_DIM = 128
MLA_QK_DIM = MLA_NOPE_DIM + MLA_ROPE_DIM
MLA_HEAD_PAD = 256
MLA_V_W = MLA_HEADS * MLA_V_DIM

OFF_GQA_K = GQA_Q_W
OFF_GQA_V = OFF_GQA_K + GQA_KV_W
OFF_MLA_QA = OFF_GQA_V + GQA_KV_W
OFF_MLA_KVA = OFF_MLA_QA + MLA_Q_RANK
OFF_GATE = OFF_MLA_KVA + MLA_KV_RANK + MLA_ROPE_DIM

N_EXPERTS = 32
TOP_K = 4
D_EXPERT = 1024
SWIGLU_LIMIT = 7.0
SWIGLU_ALPHA = 1.702
EXPERT_BLOCK = 512

LANES = 128
NEG_INF = float("-inf")
LOG2_E = math.log2(math.e)

BF16 = jnp.bfloat16
F32 = jnp.float32


def _dot(a, b):
    return jnp.dot(a, b, preferred_element_type=F32)


def _dot_nt(a, b):
    return lax.dot_general(a, b, (((1,), (1,)), ((), ())), preferred_element_type=F32)


def _split_bf16(a):
    hi = a.astype(BF16)
    lo = (a - hi.astype(F32)).astype(BF16)
    return hi, lo


def _rms(x, g, n):
    ms = jnp.sum(x * x, axis=-1, keepdims=True) * (1.0 / n)
    return x * lax.rsqrt(ms + EPS) * g


def _swap_halves(x, k):
    lane = lax.broadcasted_iota(jnp.int32, x.shape, 1)
    return jnp.where((lane & k) != 0, pltpu.roll(x, k, 1), pltpu.roll(x, LANES - k, 1))


def _rope(x, cos, sin_signed, k):
    return x * cos + _swap_halves(x, k) * sin_signed


def _ada_kernel(c_ref, w_ref, b_ref, o_ref):
    c = c_ref[...]
    s = c * (1.0 / (1.0 + jnp.exp(-c)))
    s_hi, s_lo = _split_bf16(s)
    w_hi, w_lo = _split_bf16(w_ref[...])
    o_ref[...] = _dot(s_hi, w_hi) + (_dot(s_hi, w_lo) + _dot(s_lo, w_hi)) + b_ref[...]


def _ada_mod(cc, ada_w, ada_b):
    n = ada_w.shape[1]
    tn = 1024
    return pl.pallas_call(
        _ada_kernel,
        out_shape=jax.ShapeDtypeStruct((cc.shape[0], n), F32),
        grid=(n // tn,),
        in_specs=[
            pl.BlockSpec((cc.shape[0], D_MODEL), lambda j: (0, 0)),
            pl.BlockSpec((D_MODEL, tn), lambda j: (0, j)),
            pl.BlockSpec((1, tn), lambda j: (0, j)),
        ],
        out_specs=pl.BlockSpec((cc.shape[0], tn), lambda j: (0, j)),
        compiler_params=pltpu.CompilerParams(dimension_semantics=("parallel",)),
        name="ada_mod",
    )(cc, ada_w, ada_b.reshape(1, n))


def _proj_kernel(*refs, with_q):
    if with_q:
        (x_ref, mod_ref, nmix_ref, cg_ref, sg_ref, cm_ref, sm_ref,
         wq_ref, wk_ref, wv_ref, wqa_ref, wckv_ref, wkr_ref, wg_ref, wqb_ref, wkb_ref, wvb_ref,
         gq_ref, gk_ref, gqa_ref, gkva_ref, gmq_ref, gmk_ref,
         q_ref, k_ref, v_ref, qm_ref, km_ref, vm_ref, gate_ref) = refs
    else:
        (x_ref, mod_ref, nmix_ref, cg_ref, sg_ref, cm_ref, sm_ref,
         wk_ref, wv_ref, wckv_ref, wkr_ref, wkb_ref, wvb_ref,
         gk_ref, gkva_ref, gmk_ref,
         k_ref, v_ref, km_ref, vm_ref) = refs

    x = x_ref[0]
    mod = mod_ref[0]
    shift, scale = mod[0:1, :], mod[1:2, :]
    h = _rms(x, nmix_ref[...], D_MODEL) * (1.0 + scale) + shift
    hb = h.astype(BF16)
    cg, sg = cg_ref[...], sg_ref[...]
    cm, sm = cm_ref[...], sm_ref[...]

    kk = _dot(hb, wk_ref[...])
    for j in range(GQA_KV_HEADS):
        sl = slice(j * GQA_HEAD_DIM, (j + 1) * GQA_HEAD_DIM)
        kn = _rms(kk[:, sl], gk_ref[...], GQA_HEAD_DIM)
        k_ref[0, :, sl] = _rope(kn, cg, sg, 32).astype(BF16)
    v_ref[0] = _dot_nt(wv_ref[...], hb).astype(BF16)

    ckv = _rms(_dot(hb, wckv_ref[...]), gkva_ref[...], MLA_KV_RANK).astype(BF16)
    vm_ref[0] = _dot_nt(wvb_ref[...], ckv).astype(BF16)
    knope = _dot(ckv, wkb_ref[...])
    kr = _dot(hb, wkr_ref[...])
    gmk = gmk_ref[...]
    g_nope, g_rope = gmk[:, :MLA_NOPE_DIM], gmk[:, MLA_NOPE_DIM:]
    ssq_r = jnp.sum(kr * kr, axis=-1, keepdims=True)
    kr_roped = _rope(kr * g_rope, cm, sm, 16)
    for j in range(MLA_HEADS):
        kn = knope[:, j * MLA_NOPE_DIM:(j + 1) * MLA_NOPE_DIM]
        ms = (jnp.sum(kn * kn, axis=-1, keepdims=True) + ssq_r) * (1.0 / MLA_QK_DIM)
        r = lax.rsqrt(ms + EPS)
        base = j * MLA_HEAD_PAD
        km_ref[0, :, base:base + MLA_NOPE_DIM] = (kn * r * g_nope).astype(BF16)
        km_ref[0, :, base + MLA_NOPE_DIM:base + MLA_HEAD_PAD] = (kr_roped * r).astype(BF16)

    if not with_q:
        return

    qq = _dot(hb, wq_ref[...])
    q_scale = GQA_HEAD_DIM ** -0.5 * LOG2_E
    for j in range(GQA_HEADS):
        sl = slice(j * GQA_HEAD_DIM, (j + 1) * GQA_HEAD_DIM)
        qn = _rms(qq[:, sl], gq_ref[...], GQA_HEAD_DIM)
        q_ref[0, :, sl] = (_rope(qn, cg, sg, 32) * q_scale).astype(BF16)

    qa = _rms(_dot(hb, wqa_ref[...]), gqa_ref[...], MLA_Q_RANK).astype(BF16)
    q2 = _dot(qa, wqb_ref[...])
    gmq = gmq_ref[...]
    gq_nope, gq_rope = gmq[:, :MLA_NOPE_DIM], gmq[:, MLA_NOPE_DIM:]
    m_scale = MLA_QK_DIM ** -0.5 * LOG2_E
    for j in range(MLA_HEADS):
        base = j * MLA_HEAD_PAD
        qn = q2[:, base:base + MLA_NOPE_DIM]
        qr = q2[:, base + MLA_NOPE_DIM:base + MLA_HEAD_PAD]
        ms = (jnp.sum(qn * qn, axis=-1, keepdims=True) + jnp.sum(qr * qr, axis=-1, keepdims=True)) * (1.0 / MLA_QK_DIM)
        r = lax.rsqrt(ms + EPS)
        qm_ref[0, :, base:base + MLA_NOPE_DIM] = (qn * r * gq_nope * m_scale).astype(BF16)
        qm_ref[0, :, base + MLA_NOPE_DIM:base + MLA_HEAD_PAD] = (_rope(qr * r * gq_rope, cm, sm, 16) * m_scale).astype(BF16)

    gl = _dot(hb, wg_ref[...])
    gate_ref[0] = (0.5 + 0.5 * jnp.tanh(0.5 * gl)).astype(BF16)


def _proj(x, mod3, mod_row_of_batch, tables, w, with_q, tm):
    b, l, _ = x.shape
    cg, sg, cm, sm = tables
    const = lambda shape: pl.BlockSpec(shape, lambda bi, i: (0,) * len(shape), pipeline_mode=pl.Buffered(1))
    tab = pl.BlockSpec((tm, LANES), lambda bi, i: (i, 0))
    in_specs = [
        pl.BlockSpec((1, tm, D_MODEL), lambda bi, i: (bi, i, 0)),
        pl.BlockSpec((1, 6, D_MODEL), lambda bi, i: (mod_row_of_batch(bi), 0, 0)),
        const((1, D_MODEL)), tab, tab, tab, tab,
    ]
    if with_q:
        weights = [w["wq"], w["wk"], w["wv"], w["wqa"], w["wckv"], w["wkr"], w["wg"], w["wqb"], w["wkb"], w["wvb"],
                   w["gq"], w["gk"], w["gqa"], w["gkva"], w["gmq"], w["gmk"]]
    else:
        weights = [w["wk"], w["wv"], w["wckv"], w["wkr"], w["wkb"], w["wvb"], w["gk"], w["gkva"], w["gmk"]]
    in_specs += [const(a.shape) for a in weights]

    def out(width):
        return jax.ShapeDtypeStruct((b, l, width), BF16), pl.BlockSpec((1, tm, width), lambda bi, i: (bi, i, 0))

    def out_t(width):
        return jax.ShapeDtypeStruct((b, width, l), BF16), pl.BlockSpec((1, width, tm), lambda bi, i: (bi, 0, i))

    outs = [out(GQA_KV_W), out_t(GQA_KV_W), out(MLA_HEADS * MLA_HEAD_PAD), out_t(MLA_V_W)]
    if with_q:
        outs = [out(GQA_Q_W)] + outs[:2] + [out(MLA_HEADS * MLA_HEAD_PAD)] + outs[2:] + [out(2 * D_MODEL)]
    return pl.pallas_call(
        functools.partial(_proj_kernel, with_q=with_q),
        out_shape=[o[0] for o in outs],
        grid=(b, l // tm),
        in_specs=in_specs,
        out_specs=[o[1] for o in outs],
        compiler_params=pltpu.CompilerParams(dimension_semantics=("parallel", "parallel"),
                                             vmem_limit_bytes=56 * 1024 * 1024),
        name="proj_latent" if with_q else "proj_ctx",
    )(x, mod3, w["norm_mix"], cg, sg, cm, sm, *weights)


def _attn_kernel(q_ref, kl_ref, vl_ref, kc_ref, vc_ref, o_ref, *, tk, n_lat, bounded):
    q = q_ref[0]
    tq = q.shape[0]
    dv = vl_ref.shape[1]
    n = n_lat // tk

    if bounded:
        l = jnp.zeros((1, tq), F32)
        acc = jnp.zeros((dv, tq), F32)
        for k, vt in [(kl_ref[0, j * tk:(j + 1) * tk, :], vl_ref[0, :, j * tk:(j + 1) * tk]) for j in range(n)] + [
                (kc_ref[0], vc_ref[0])]:
            p = jnp.exp2(_dot_nt(k, q))
            l = l + jnp.sum(p, axis=0, keepdims=True)
            acc = acc + _dot(vt, p.astype(BF16))
        o_ref[0] = (acc / l).T.astype(BF16)
        return

    def update(carry, s, vt):
        m, l, acc = carry
        m_new = jnp.maximum(m, jnp.max(s, axis=0, keepdims=True))
        alpha = jnp.exp2(m - m_new)
        p = jnp.exp2(s - m_new)
        l = alpha * l + jnp.sum(p, axis=0, keepdims=True)
        return m_new, l, alpha * acc + _dot(vt, p.astype(BF16))

    n = n_lat // tk
    keys = [kl_ref[0, j * tk:(j + 1) * tk, :] for j in range(n)] + [kc_ref[0]]
    vals = [vl_ref[0, :, j * tk:(j + 1) * tk] for j in range(n)] + [vc_ref[0]]
    carry = (jnp.full((1, tq), NEG_INF, F32), jnp.zeros((1, tq), F32), jnp.zeros((dv, tq), F32))
    s = _dot_nt(keys[0], q)
    for j in range(n + 1):
        s_next = _dot_nt(keys[j + 1], q) if j < n else None
        carry = update(carry, s, vals[j])
        s = s_next
    m, l, acc = carry
    o_ref[0] = (acc / l).T.astype(BF16)


SOFTMAX_SAFE_EXPONENT = 56.0


def _score_bound(gain_q, gain_k, dim):
    return dim * jnp.max(jnp.abs(gain_q)) * jnp.max(jnp.abs(gain_k)) * (dim ** -0.5 * LOG2_E) * 1.02


def _attention(q, k_lat, v_lat, k_ctx, v_ctx, n_heads, group, d_qk, dv, tq, tk, name, score_bound):
    run = functools.partial(_attention_call, q, k_lat, v_lat, k_ctx, v_ctx, n_heads, group, d_qk, dv, tq, tk, name)
    return lax.cond(score_bound <= SOFTMAX_SAFE_EXPONENT, lambda: run(True), lambda: run(False))


def _attention_call(q, k_lat, v_lat, k_ctx, v_ctx, n_heads, group, d_qk, dv, tq, tk, name, bounded):
    b, l, _ = q.shape
    lc = k_ctx.shape[1]
    name = name + ("_bounded" if bounded else "_online")
    return pl.pallas_call(
        functools.partial(_attn_kernel, tk=tk, n_lat=l, bounded=bounded),
        out_shape=jax.ShapeDtypeStruct((b, l, n_heads * dv), BF16),
        grid=(b, n_heads, l // tq),
        in_specs=[
            pl.BlockSpec((1, tq, d_qk), lambda bi, h, i: (bi, i, h)),
            pl.BlockSpec((1, l, d_qk), lambda bi, h, i: (bi, 0, h // group)),
            pl.BlockSpec((1, dv, l), lambda bi, h, i: (bi, h // group, 0)),
            pl.BlockSpec((1, lc, d_qk), lambda bi, h, i: (bi, 0, h // group)),
            pl.BlockSpec((1, dv, lc), lambda bi, h, i: (bi, h // group, 0)),
        ],
        out_specs=pl.BlockSpec((1, tq, dv), lambda bi, h, i: (bi, i, h)),
        compiler_params=pltpu.CompilerParams(dimension_semantics=("parallel", "parallel", "parallel"),
                                             vmem_limit_bytes=56 * 1024 * 1024),
        name=name,
    )(q, k_lat, v_lat, k_ctx, v_ctx)


def _merge_kernel(oa_ref, ob_ref, gate_ref, x_ref, mod_ref, woa_ref, wob_ref, wout_ref, nffn_ref,
                  rwh_ref, rwl_ref, rb_ref, x1_ref, h2_ref, eidx_ref, egate_ref, rank_ref, before_ref, tcnt_ref,
                  cnt_ref, carry_ref):
    @pl.when((pl.program_id(0) == 0) & (pl.program_id(1) == 0))
    def _():
        carry_ref[...] = jnp.zeros_like(carry_ref)

    mod = mod_ref[0]
    g1, shift2, scale2 = mod[2:3, :], mod[3:4, :], mod[4:5, :]
    ya = _dot(oa_ref[0], woa_ref[...])
    yb = _dot(ob_ref[0], wob_ref[...])
    g = gate_ref[0].astype(F32)
    y = g[:, :D_MODEL] * ya + g[:, D_MODEL:] * yb
    z = _dot(y.astype(BF16), wout_ref[...])
    x1 = x_ref[0] + g1 * z
    x1_ref[0] = x1
    h2 = _rms(x1, nffn_ref[...], D_MODEL) * (1.0 + scale2) + shift2
    h2_ref[0] = h2.astype(BF16)

    h_hi, h_lo = _split_bf16(h2)
    logits = _dot(h_hi, rwh_ref[...]) + (_dot(h_hi, rwl_ref[...]) + _dot(h_lo, rwh_ref[...])) + rb_ref[...]
    lane = lax.broadcasted_iota(jnp.int32, logits.shape, 1).astype(F32)
    cur = jnp.where(lane < N_EXPERTS, logits, NEG_INF)
    vals, idxs = [], []
    chosen = jnp.zeros(logits.shape, F32)
    for _ in range(TOP_K):
        mx = jnp.max(cur, axis=-1, keepdims=True)
        ix = jnp.min(jnp.where(cur == mx, lane, float(LANES)), axis=-1, keepdims=True)
        vals.append(mx)
        idxs.append(ix)
        sel = lane == ix
        chosen = chosen + jnp.where(sel, 1.0, 0.0)
        cur = jnp.where(sel, NEG_INF, cur)
    ex = [jnp.exp(v - vals[0]) for v in vals]
    den = ex[0] + ex[1] + ex[2] + ex[3]
    eidx = jnp.zeros(logits.shape, F32)
    egate = jnp.zeros(logits.shape, F32)
    for k in range(TOP_K):
        eidx = jnp.where(lane == k, idxs[k], eidx)
        egate = jnp.where(lane == k, ex[k] / den, egate)
    eidx = eidx.astype(jnp.int32)
    eidx_ref[0] = eidx
    egate_ref[0] = egate

    for s in range(eidx.shape[0] // MOE_TILE):
        rows = slice(s * MOE_TILE, (s + 1) * MOE_TILE)
        before_ref[s] = carry_ref[...]
        rank, tile_cnt = _route_tile(eidx[rows], chosen[rows], carry_ref)
        rank_ref[0, rows, :] = rank
        tcnt_ref[s] = jnp.broadcast_to(tile_cnt, carry_ref.shape)
    cnt_ref[...] = carry_ref[...]


def _merge(o_a, o_b, gates, x, mod3, w, tm):
    b, l, _ = x.shape
    const = lambda shape: pl.BlockSpec(shape, lambda bi, i: (0,) * len(shape), pipeline_mode=pl.Buffered(1))
    tok = lambda width: pl.BlockSpec((1, tm, width), lambda bi, i: (bi, i, 0))
    assert tm % MOE_TILE == 0
    sub = tm // MOE_TILE
    per_tile = jax.ShapeDtypeStruct((b * l // MOE_TILE, 8, LANES), F32)
    tile_spec = pl.BlockSpec((sub, 8, LANES), lambda bi, i: (bi * (l // tm) + i, 0, 0))
    return pl.pallas_call(
        _merge_kernel,
        out_shape=[jax.ShapeDtypeStruct((b, l, D_MODEL), F32), jax.ShapeDtypeStruct((b, l, D_MODEL), BF16),
                   jax.ShapeDtypeStruct((b, l, LANES), jnp.int32), jax.ShapeDtypeStruct((b, l, LANES), F32),
                   jax.ShapeDtypeStruct((b, l, LANES), jnp.int32), per_tile, per_tile,
                   jax.ShapeDtypeStruct((8, LANES), F32)],
        grid=(b, l // tm),
        in_specs=[tok(GQA_Q_W), tok(MLA_V_W), tok(2 * D_MODEL), tok(D_MODEL),
                  pl.BlockSpec((1, 6, D_MODEL), lambda bi, i: (bi, 0, 0)),
                  const((GQA_Q_W, D_MODEL)), const((MLA_V_W, D_MODEL)), const((D_MODEL, D_MODEL)), const((1, D_MODEL)),
                  const((D_MODEL, LANES)), const((D_MODEL, LANES)), const((1, LANES))],
        out_specs=[tok(D_MODEL), tok(D_MODEL), tok(LANES), tok(LANES), tok(LANES), tile_spec, tile_spec,
                   pl.BlockSpec((8, LANES), lambda bi, i: (0, 0))],
        scratch_shapes=[pltpu.VMEM((8, LANES), F32)],
        compiler_params=pltpu.CompilerParams(dimension_semantics=("arbitrary", "arbitrary"),
                                             vmem_limit_bytes=56 * 1024 * 1024),
        name="merge_router",
    )(o_a, o_b, gates, x, mod3, w["woa"], w["wob"], w["wout"], w["norm_ffn"], w["rw_hi"], w["rw_lo"], w["rb"])


SEG_ALIGN = 16


def _route_tile(idx, total, carry_ref):
    tm = idx.shape[0]
    lane = lax.broadcasted_iota(jnp.int32, idx.shape, 1)
    row = lax.broadcasted_iota(jnp.int32, (tm, tm), 0)
    col = lax.broadcasted_iota(jnp.int32, (tm, tm), 1)
    tri = jnp.where(row > col, 1.0, 0.0).astype(BF16)
    before = _dot(tri, total.astype(BF16)) + carry_ref[0:1, :]
    rank = jnp.where(lane < TOP_K, jnp.take_along_axis(before, idx, axis=1), 0.0)
    tile_cnt = jnp.sum(total, axis=0, keepdims=True)
    carry_ref[...] = carry_ref[...] + tile_cnt
    return rank.astype(jnp.int32), tile_cnt


def _dest_tables(idx, rank, region_starts, seg_starts):
    valid = lax.broadcasted_iota(jnp.int32, idx.shape, 1) < TOP_K
    region_start = jnp.take_along_axis(jnp.broadcast_to(region_starts, idx.shape), idx, axis=1)
    seg_start = jnp.take_along_axis(jnp.broadcast_to(seg_starts, idx.shape), idx, axis=1)
    rank = rank.astype(F32)
    window_start = jnp.floor(seg_start * (1.0 / SEG_ALIGN)) * SEG_ALIGN
    local = rank - window_start
    dest = jnp.where(valid, region_start + rank, 0.0)
    col = jnp.where(valid & (local < WIN), idx.astype(F32) * WIN + local, -1.0)
    return dest, col


MOE_TILE = 256
WIN = 64
N_MAIN_ROWS = N_EXPERTS * WIN
MAIN_CHUNK = 512
WIN_PER_STACK = MOE_TILE // WIN
MAX_WINDOWS = (MOE_TILE * TOP_K + N_EXPERTS * SEG_ALIGN) // WIN
CARRY_ROWS = N_EXPERTS * SEG_ALIGN
WORKLIST_LEN = -(-(MAX_WINDOWS + WIN_PER_STACK) // WIN_PER_STACK) * WIN_PER_STACK


def _build_worklist(ts_ref, tc_ref, wl_ref, junk_row):
    base = pl.program_id(0) * N_EXPERTS

    def per_expert(e, n):
        first = ts_ref[base + e]
        n_win = (tc_ref[base + e] + (WIN - 1)) // WIN

        def per_window(wi, n):
            wl_ref[n] = first + wi * WIN
            return n + 1

        return lax.fori_loop(1, n_win, per_window, n)

    n = lax.fori_loop(0, N_EXPERTS, per_expert, 0)
    n_stacks = (n + (WIN_PER_STACK - 1)) // WIN_PER_STACK

    def pad(j, c):
        wl_ref[j] = junk_row + (j % WIN_PER_STACK) * WIN
        return c

    lax.fori_loop(n, n_stacks * WIN_PER_STACK, pad, 0)
    return n_stacks


def _seg_aligned(row):
    return row if isinstance(row, int) else pl.multiple_of(row, SEG_ALIGN)


def _stack_row_ids(wl_ref, stack, shape, axis):
    pos = lax.broadcasted_iota(jnp.int32, shape, axis)
    row = jnp.full(shape, -1, jnp.int32)
    for wi in range(WIN_PER_STACK - 1, -1, -1):
        row = jnp.where(pos < (wi + 1) * WIN, wl_ref[stack * WIN_PER_STACK + wi] + (pos - wi * WIN), row)
    return row


def _select_rows(row_id, id_t):
    shape = (row_id.shape[0], id_t.shape[1])
    sel = jnp.zeros(shape, F32)
    for k in range(TOP_K):
        hit = row_id == id_t[k:k + 1, :]
        sel = jnp.where(hit, 1.0, sel)
    return sel.astype(BF16)


MAIN_PENDING = 2


TAIL_ROWS = WIN + EXPERT_BLOCK


def _dispatch_kernel(ts_ref, tc_ref, ov_ref, re_ref, h_ref, eidx_ref, rank_ref, start_ref, before_ref,
                     buf_ref, dest_ref, col_ref, wl_ref, pend_ref, main_x, carry_x, stage_x, main_sem, sem, *,
                     junk_row):
    i = pl.program_id(0)
    base = i * N_EXPERTS
    next_base = jnp.minimum(i + 1, pl.num_programs(0) - 1) * N_EXPERTS

    @pl.when(i == 0)
    def _():
        pend_ref[0] = 0
        pend_ref[1] = 0
        pend_ref[MAIN_PENDING] = 0
        carry_x[...] = jnp.zeros_like(carry_x)
        main_x[0:TAIL_ROWS, :] = jnp.zeros((TAIL_ROWS, D_MODEL), BF16)
        cps = []
        for e in range(N_EXPERTS):
            rows = pl.ds(pl.multiple_of(jnp.maximum(re_ref[e] - TAIL_ROWS, 0), SEG_ALIGN), TAIL_ROWS)
            cps.append(pltpu.make_async_copy(main_x.at[pl.ds(0, TAIL_ROWS)], buf_ref.at[rows], main_sem))
        for cp in cps:
            cp.start()
        for cp in cps:
            cp.wait()

        def zero_block_copies(blk):
            rows = pl.ds(pl.multiple_of(blk * EXPERT_BLOCK, EXPERT_BLOCK), EXPERT_BLOCK)
            return (pltpu.make_async_copy(main_x.at[pl.ds(0, EXPERT_BLOCK)], buf_ref.at[rows], main_sem),)

        def start_zero(blk, c):
            for cp in zero_block_copies(blk):
                cp.start()
            return c

        def wait_zero(blk, c):
            for cp in zero_block_copies(blk):
                cp.wait()
            return c

        first_free, n_blocks = re_ref[N_EXPERTS - 1] // EXPERT_BLOCK, (junk_row + EXPERT_BLOCK) // EXPERT_BLOCK
        lax.fori_loop(first_free, n_blocks, start_zero, 0)
        lax.fori_loop(first_free, n_blocks, wait_zero, 0)

    h = h_ref[...]

    def main_copies(first_row):
        cps = []
        for e in range(N_EXPERTS):
            rows = pl.ds(_seg_aligned(first_row(e)), WIN)
            cps.append(pltpu.make_async_copy(main_x.at[pl.ds(e * WIN, WIN)], buf_ref.at[rows], main_sem))
        return cps

    def drain_main():
        @pl.when(pend_ref[MAIN_PENDING] == 1)
        def _():
            for cp in main_copies(lambda e: 0):
                cp.wait()
            pend_ref[MAIN_PENDING] = 0

    dest, col = _dest_tables(eidx_ref[...], rank_ref[...], start_ref[0:1, :], before_ref[0][0:1, :])
    dest_ref[...] = dest.astype(jnp.int32)
    col_ref[...] = col.astype(jnp.int32)
    dest_t = dest.T[0:8, :].astype(jnp.int32)
    col_t = col.T[0:8, :].astype(jnp.int32)
    pieces = []
    for c in range(N_MAIN_ROWS // MAIN_CHUNK):
        row_id = lax.broadcasted_iota(jnp.int32, (MAIN_CHUNK, 1), 0) + c * MAIN_CHUNK
        pieces.append(_dot(_select_rows(row_id, col_t), h).astype(BF16))
    group_pos = lax.broadcasted_iota(jnp.int32, (SEG_ALIGN, 1), 0)
    end_group = jnp.concatenate([ts_ref[next_base + e] + group_pos for e in range(N_EXPERTS)], axis=0)
    end_rows = _dot(_select_rows(end_group, dest_t), h).astype(BF16)

    def window_copies(slot, first_row):
        cps = []
        for wi in range(WIN_PER_STACK):
            rows = pl.ds(_seg_aligned(first_row(wi)), WIN)
            cps.append(pltpu.make_async_copy(stage_x.at[slot, pl.ds(wi * WIN, WIN)], buf_ref.at[rows], sem.at[slot]))
        return cps

    def drain(slot):
        @pl.when(pend_ref[slot] == 1)
        def _():
            for cp in window_copies(slot, lambda wi: 0):
                cp.wait()
            pend_ref[slot] = 0

    drain_main()
    drain(0)
    drain(1)
    for c, packed in enumerate(pieces):
        main_x[c * MAIN_CHUNK:(c + 1) * MAIN_CHUNK, :] = packed
    for e in range(N_EXPERTS):
        win, grp = pl.ds(e * WIN, SEG_ALIGN), pl.ds(e * SEG_ALIGN, SEG_ALIGN)
        main_x[win, :] = main_x[win, :] + carry_x[grp, :]
        same = ts_ref[next_base + e] == ts_ref[base + e]
        carry_x[grp, :] = end_rows[e * SEG_ALIGN:(e + 1) * SEG_ALIGN, :] + jnp.where(same, carry_x[grp, :], 0).astype(BF16)
    for n, cp in enumerate(main_copies(lambda e: ts_ref[base + e])):
        cp.start(priority=n % 2)
    pend_ref[MAIN_PENDING] = 1

    @pl.when(ov_ref[i] > 0)
    def _():
        n_stacks = _build_worklist(ts_ref, tc_ref, wl_ref, junk_row)

        def stack_body(s, c):
            slot = s & 1
            rows = _dot(_select_rows(_stack_row_ids(wl_ref, s, (MOE_TILE, 1), 0), dest_t), h)
            drain(slot)
            stage_x[slot] = rows.astype(BF16)
            for cp in window_copies(slot, lambda wi: wl_ref[s * WIN_PER_STACK + wi]):
                cp.start()
            pend_ref[slot] = 1
            return c

        lax.fori_loop(0, n_stacks, stack_body, 0)

    @pl.when(i == pl.num_programs(0) - 1)
    def _():
        drain_main()
        drain(0)
        drain(1)


def _moe_grid_spec(n_tiles, in_specs, out_specs, scratch_shapes):
    return pltpu.PrefetchScalarGridSpec(num_scalar_prefetch=4, grid=(n_tiles,), in_specs=in_specs,
                                        out_specs=out_specs, scratch_shapes=scratch_shapes)


def _dispatch(tile_start, tile_cnt, tile_ovf, region_ends, h2, eidx, rank, start, before, n_rows):
    t = h2.shape[0]
    n_tiles = t // MOE_TILE
    junk_row = n_rows - EXPERT_BLOCK
    tok = lambda width: pl.BlockSpec((MOE_TILE, width), lambda i, *_: (i, 0))
    table = jax.ShapeDtypeStruct((t, LANES), jnp.int32)
    return pl.pallas_call(
        functools.partial(_dispatch_kernel, junk_row=junk_row),
        out_shape=[jax.ShapeDtypeStruct((n_rows, D_MODEL), BF16), table, table],
        grid_spec=_moe_grid_spec(
            n_tiles,
            [tok(D_MODEL), tok(LANES), tok(LANES), pl.BlockSpec((8, LANES), lambda i, *_: (0, 0)),
             pl.BlockSpec((1, 8, LANES), lambda i, *_: (i, 0, 0))],
            [pl.BlockSpec(memory_space=pl.ANY), tok(LANES), tok(LANES)],
            [pltpu.SMEM((WORKLIST_LEN,), jnp.int32), pltpu.SMEM((3,), jnp.int32),
             pltpu.VMEM((N_MAIN_ROWS, D_MODEL), BF16), pltpu.VMEM((CARRY_ROWS, D_MODEL), BF16),
             pltpu.VMEM((2, MOE_TILE, D_MODEL), BF16),
             pltpu.SemaphoreType.DMA(()), pltpu.SemaphoreType.DMA((2,))]),
        compiler_params=pltpu.CompilerParams(dimension_semantics=("arbitrary",),
                                             vmem_limit_bytes=56 * 1024 * 1024),
        name="moe_dispatch",
    )(tile_start, tile_cnt, tile_ovf, region_ends, h2, eidx, rank, start, before)


def _select_cols(ids, gates, col_id):
    sel = jnp.zeros((ids.shape[0], col_id.shape[1]), F32)
    for k in range(TOP_K):
        sel = jnp.where(ids[:, k:k + 1] == col_id, gates[:, k:k + 1], sel)
    return sel.astype(BF16)


def _combine_kernel(ts_ref, tc_ref, ov_ref, re_ref, col_ref, dest_ref, egate_ref, x1_ref, mod_ref, y_ref, o_ref,
                    wl_ref, main_y, stage_y, acc_ref, main_sem, sem, *, junk_row):
    del re_ref
    i = pl.program_id(0)
    slot_i = i & 1

    def main_copies(slot, first_row):
        return [pltpu.make_async_copy(y_ref.at[pl.ds(_seg_aligned(first_row(e)), WIN)],
                                      main_y.at[slot, pl.ds(e * WIN, WIN)], main_sem.at[slot])
                for e in range(N_EXPERTS)]

    def fetch_main(tile, slot):
        for n, cp in enumerate(main_copies(slot, lambda e: ts_ref[tile * N_EXPERTS + e])):
            cp.start(priority=n % 2)

    @pl.when(i == 0)
    def _():
        fetch_main(0, 0)

    @pl.when(i + 1 < pl.num_programs(0))
    def _():
        fetch_main(i + 1, 1 - slot_i)

    for cp in main_copies(slot_i, lambda e: 0):
        cp.wait()

    col = col_ref[...]
    egate = egate_ref[...]
    sel = jnp.concatenate(
        [_select_cols(col, egate, lax.broadcasted_iota(jnp.int32, (1, MAIN_CHUNK), 1) + c * MAIN_CHUNK)
         for c in range(N_MAIN_ROWS // MAIN_CHUNK)], axis=1)
    acc_ref[...] = _dot(sel, main_y[slot_i])

    @pl.when(ov_ref[i] > 0)
    def _():
        n_stacks = _build_worklist(ts_ref, tc_ref, wl_ref, junk_row)
        dest = dest_ref[...]
        if WIN_PER_STACK * WIN < MOE_TILE:
            for slot in range(2):
                stage_y[slot, WIN_PER_STACK * WIN:, :] = jnp.zeros((MOE_TILE - WIN_PER_STACK * WIN, D_MODEL), BF16)

        def window_copies(slot, first_row):
            return [pltpu.make_async_copy(y_ref.at[pl.ds(_seg_aligned(first_row(wi)), WIN)],
                                          stage_y.at[slot, pl.ds(wi * WIN, WIN)], sem.at[slot])
                    for wi in range(WIN_PER_STACK)]

        def fetch(s, slot):
            for cp in window_copies(slot, lambda wi: wl_ref[s * WIN_PER_STACK + wi]):
                cp.start()

        @pl.when(n_stacks > 0)
        def _():
            fetch(0, 0)

        def stack_body(s, c):
            slot = s & 1

            @pl.when(s + 1 < n_stacks)
            def _():
                fetch(s + 1, 1 - slot)

            for cp in window_copies(slot, lambda wi: 0):
                cp.wait()
            sel_o = _select_cols(dest, egate, _stack_row_ids(wl_ref, s, (1, MOE_TILE), 1))
            acc_ref[...] += _dot(sel_o, stage_y[slot])
            return c

        lax.fori_loop(0, n_stacks, stack_body, 0)

    g2 = mod_ref[0][5:6, :]
    o_ref[...] = x1_ref[...] + g2 * acc_ref[...]


def _combine(tile_start, tile_cnt, tile_ovf, region_ends, col, dest, egate, x1, mod3, y, l):
    t = x1.shape[0]
    per_batch = l // MOE_TILE
    junk_row = y.shape[0] - EXPERT_BLOCK
    tok = lambda width: pl.BlockSpec((MOE_TILE, width), lambda i, *_: (i, 0))
    return pl.pallas_call(
        functools.partial(_combine_kernel, junk_row=junk_row),
        out_shape=jax.ShapeDtypeStruct((t, D_MODEL), F32),
        grid_spec=_moe_grid_spec(
            t // MOE_TILE,
            [tok(LANES), tok(LANES), tok(LANES), tok(D_MODEL),
             pl.BlockSpec((1, 6, D_MODEL), lambda i, *_: (i // per_batch, 0, 0)),
             pl.BlockSpec(memory_space=pl.ANY)],
            tok(D_MODEL),
            [pltpu.SMEM((WORKLIST_LEN,), jnp.int32), pltpu.VMEM((2, N_MAIN_ROWS, D_MODEL), BF16),
             pltpu.VMEM((2, MOE_TILE, D_MODEL), BF16), pltpu.VMEM((MOE_TILE, D_MODEL), F32),
             pltpu.SemaphoreType.DMA((2,)), pltpu.SemaphoreType.DMA((2,))]),
        compiler_params=pltpu.CompilerParams(dimension_semantics=("arbitrary",),
                                             vmem_limit_bytes=56 * 1024 * 1024),
        name="moe_combine",
    )(tile_start, tile_cnt, tile_ovf, region_ends, col, dest, egate, x1, mod3, y)


def _expert_kernel(be_ref, nused_ref, x_ref, w1_hbm, b1_ref, w2_hbm, b2_ref, y_ref,
                   w1f_ref, w2f_ref, w1b_ref, w2b_ref, wsem):
    i = pl.program_id(0)
    live = i < nused_ref[0]
    e = be_ref[i]

    def weight_copies(expert, slot):
        return (pltpu.make_async_copy(w1_hbm.at[expert], w1f_ref.at[slot], wsem.at[0, slot]),
                pltpu.make_async_copy(w2_hbm.at[expert], w2f_ref.at[slot], wsem.at[1, slot]))

    @pl.when(live & ((i == 0) | (e != be_ref[jnp.maximum(i - 1, 0)])))
    def _():
        slot = e & 1

        @pl.when(i == 0)
        def _():
            for cp in weight_copies(e, slot):
                cp.start()

        for cp in weight_copies(e, slot):
            cp.wait()

        @pl.when(e + 1 < N_EXPERTS)
        def _():
            for cp in weight_copies(e + 1, 1 - slot):
                cp.start()

        w1b_ref[...] = w1f_ref[slot].astype(BF16)
        w2b_ref[...] = w2f_ref[slot].astype(BF16)

    @pl.when(live)
    def _():
        gu = _dot(x_ref[...], w1b_ref[...]) + b1_ref[0]
        glu = jnp.minimum(gu[:, :D_EXPERT], SWIGLU_LIMIT)
        lin = jnp.clip(gu[:, D_EXPERT:], -SWIGLU_LIMIT, SWIGLU_LIMIT)
        act = (0.5 * glu) * (1.0 + jnp.tanh((0.5 * SWIGLU_ALPHA) * glu)) * (lin + 1.0)
        y_ref[...] = (_dot(act.astype(BF16), w2b_ref[...]) + b2_ref[0]).astype(BF16)

    @pl.when(pl.program_id(0) >= nused_ref[0])
    def _():
        y_ref[...] = jnp.zeros_like(y_ref)


def _experts(block_e, nused, buf, w1, b1, w2, b2):
    nb = buf.shape[0] // EXPERT_BLOCK
    row = lambda i, be, nu: (jnp.minimum(i, nu[0] - 1), 0)
    out_row = lambda i, be, nu: (i, 0)
    exp3 = lambda i, be, nu: (be[jnp.minimum(i, nu[0] - 1)], 0, 0)
    return pl.pallas_call(
        _expert_kernel,
        out_shape=jax.ShapeDtypeStruct((buf.shape[0], D_MODEL), BF16),
        grid_spec=pltpu.PrefetchScalarGridSpec(
            num_scalar_prefetch=2,
            grid=(nb,),
            in_specs=[pl.BlockSpec((EXPERT_BLOCK, D_MODEL), row),
                      pl.BlockSpec(memory_space=pl.ANY),
                      pl.BlockSpec((1, 1, 2 * D_EXPERT), exp3),
                      pl.BlockSpec(memory_space=pl.ANY),
                      pl.BlockSpec((1, 1, D_MODEL), exp3)],
            out_specs=pl.BlockSpec((EXPERT_BLOCK, D_MODEL), out_row),
            scratch_shapes=[pltpu.VMEM((2, D_MODEL, 2 * D_EXPERT), F32), pltpu.VMEM((2, D_EXPERT, D_MODEL), F32),
                            pltpu.VMEM((D_MODEL, 2 * D_EXPERT), BF16), pltpu.VMEM((D_EXPERT, D_MODEL), BF16),
                            pltpu.SemaphoreType.DMA((2, 2))]),
        compiler_params=pltpu.CompilerParams(dimension_semantics=("arbitrary",),
                                             vmem_limit_bytes=56 * 1024 * 1024),
        name="moe_experts",
    )(block_e, nused, buf, w1, b1, w2, b2)


def _rope_tables(seq_len):
    pos = np.arange(seq_len, dtype=np.int32)
    row = (pos // GRID_W).astype(np.float32)[:, None]
    col = (pos % GRID_W).astype(np.float32)[:, None]

    def table(d_axis, pad):
        inv_freq = (np.float32(ROPE_THETA) ** (-np.arange(0, d_axis, 2, dtype=np.float32) / np.float32(d_axis)))
        inv_freq = inv_freq.astype(np.float32)
        ar, ac = row * inv_freq[None, :], col * inv_freq[None, :]
        cos = np.concatenate([np.cos(ar), np.cos(ar), np.cos(ac), np.cos(ac)], axis=-1)
        sin = np.concatenate([-np.sin(ar), np.sin(ar), -np.sin(ac), np.sin(ac)], axis=-1)
        if pad:
            cos = np.concatenate([cos, np.ones((seq_len, pad), np.float32)], axis=-1)
            sin = np.concatenate([sin, np.zeros((seq_len, pad), np.float32)], axis=-1)
        return jnp.asarray(cos, F32), jnp.asarray(sin, F32)

    cg, sg = table(GQA_HEAD_DIM // 2, 0)
    cm, sm = table(MLA_ROPE_DIM // 2, LANES - MLA_ROPE_DIM)
    return cg, sg, cm, sm


def _pad_lanes(a, width):
    return jnp.pad(a, [(0, 0)] * (a.ndim - 1) + [(0, width - a.shape[-1])])


def _prep_weights(p):
    w_in = p["w_in"]
    w = {"norm_mix": p["norm_mix"].reshape(1, D_MODEL), "norm_ffn": p["norm_ffn"].reshape(1, D_MODEL)}
    w["wq"] = w_in[:, :OFF_GQA_K].astype(BF16)
    w["wk"] = w_in[:, OFF_GQA_K:OFF_GQA_V].astype(BF16)
    w["wv"] = w_in[:, OFF_GQA_V:OFF_MLA_QA].T.astype(BF16)
    w["wqa"] = w_in[:, OFF_MLA_QA:OFF_MLA_KVA].astype(BF16)
    w["wckv"] = w_in[:, OFF_MLA_KVA:OFF_MLA_KVA + MLA_KV_RANK].astype(BF16)
    w["wkr"] = _pad_lanes(w_in[:, OFF_MLA_KVA + MLA_KV_RANK:OFF_GATE], LANES).astype(BF16)
    w["wg"] = w_in[:, OFF_GATE:].astype(BF16)
    wqb = p["mla_w_qb"].reshape(MLA_Q_RANK, MLA_HEADS, MLA_QK_DIM)
    w["wqb"] = _pad_lanes(wqb, MLA_HEAD_PAD).reshape(MLA_Q_RANK, MLA_HEADS * MLA_HEAD_PAD).astype(BF16)
    wkvb = p["mla_w_kvb"].reshape(MLA_KV_RANK, MLA_HEADS, MLA_NOPE_DIM + MLA_V_DIM)
    w["wkb"] = wkvb[:, :, :MLA_NOPE_DIM].reshape(MLA_KV_RANK, MLA_HEADS * MLA_NOPE_DIM).astype(BF16)
    w["wvb"] = wkvb[:, :, MLA_NOPE_DIM:].reshape(MLA_KV_RANK, MLA_V_W).T.astype(BF16)
    w["gq"] = p["gqa_q_norm"].reshape(1, GQA_HEAD_DIM)
    w["gk"] = p["gqa_k_norm"].reshape(1, GQA_HEAD_DIM)
    w["gqa"] = p["mla_q_a_norm"].reshape(1, MLA_Q_RANK)
    w["gkva"] = p["mla_kv_a_norm"].reshape(1, MLA_KV_RANK)
    w["gmq"] = _pad_lanes(p["mla_q_norm"].reshape(1, MLA_QK_DIM), MLA_HEAD_PAD)
    w["gmk"] = _pad_lanes(p["mla_k_norm"].reshape(1, MLA_QK_DIM), MLA_HEAD_PAD)
    w["woa"] = p["w_o_gqa"].astype(BF16)
    w["wob"] = p["w_o_mla"].astype(BF16)
    w["wout"] = p["w_out"].astype(BF16)
    rw = _pad_lanes(p["router_w"], LANES)
    w["rw_hi"] = rw.astype(BF16)
    w["rw_lo"] = (rw - w["rw_hi"].astype(F32)).astype(BF16)
    w["rb"] = _pad_lanes(p["router_b"].reshape(1, N_EXPERTS), LANES)
    return w


def _layer(x, c, ctx, c_ctx, p):
    b, l, _ = x.shape
    lc = ctx.shape[1]
    t = b * l
    w = _prep_weights(p)

    n_mod_rows = -(-(b + 1) // 8) * 8
    cc = jnp.zeros((n_mod_rows, D_MODEL), F32).at[:b].set(c).at[b].set(c_ctx)
    mod3 = _ada_mod(cc, p["ada_w"], p["ada_b"]).reshape(n_mod_rows, 6, D_MODEL)

    tables = _rope_tables(l)
    ident = (jnp.ones((lc, LANES), F32), jnp.zeros((lc, LANES), F32)) * 2
    tm = min(512, l)
    q_a, k_a, v_a, q_m, k_m, v_m, gates = _proj(x, mod3, lambda bi: bi, tables, w, True, tm)
    kc_a, vc_a, kc_m, vc_m = _proj(ctx, mod3, lambda bi: b, ident, w, False, min(256, lc))

    tq, tk = min(2048, l), min(2048, l)
    o_a = _attention(q_a, k_a, v_a, kc_a, vc_a, GQA_HEADS, GQA_GROUP, GQA_HEAD_DIM, GQA_HEAD_DIM, tq, tk, "attn_gqa",
                     _score_bound(p["gqa_q_norm"], p["gqa_k_norm"], GQA_HEAD_DIM))
    o_m = _attention(q_m, k_m, v_m, kc_m, vc_m, MLA_HEADS, 1, MLA_HEAD_PAD, MLA_V_DIM, tq, tk, "attn_mla",
                     _score_bound(p["mla_q_norm"], p["mla_k_norm"], MLA_QK_DIM))

    x1, h2, eidx, egate, rank, before_raw, tcnt, cnt = _merge(o_a, o_m, gates, x, mod3, w, tm)
    x1, h2 = x1.reshape(t, D_MODEL), h2.reshape(t, D_MODEL)
    eidx, egate, rank = eidx.reshape(t, LANES), egate.reshape(t, LANES), rank.reshape(t, LANES)

    assert t % MOE_TILE == 0
    counts = cnt[0, :N_EXPERTS].astype(jnp.int32)
    before = before_raw[:, 0, :N_EXPERTS].astype(jnp.int32)
    region = (counts + WIN + EXPERT_BLOCK - 1) // EXPERT_BLOCK * EXPERT_BLOCK
    region_ends = jnp.cumsum(region)
    region_starts = region_ends - region
    max_rows = t * TOP_K + N_EXPERTS * (WIN + EXPERT_BLOCK - 1)
    n_blocks = max_rows // EXPERT_BLOCK + 1
    n_rows = n_blocks * EXPERT_BLOCK
    block_row = jnp.arange(n_blocks, dtype=jnp.int32) * EXPERT_BLOCK
    block_e = jnp.minimum(jnp.sum(region_ends[None, :] <= block_row[:, None], axis=1), N_EXPERTS - 1).astype(jnp.int32)
    nused = (region_ends[-1:] // EXPERT_BLOCK).astype(jnp.int32)
    tile_start = (region_starts[None, :] + before // SEG_ALIGN * SEG_ALIGN).reshape(-1)
    tile_cnt = before % SEG_ALIGN + tcnt[:, 0, :N_EXPERTS].astype(jnp.int32)
    tile_ovf = jnp.sum(jnp.maximum((tile_cnt + WIN - 1) // WIN - 1, 0), axis=1).astype(jnp.int32)
    tile_cnt = tile_cnt.reshape(-1)
    start = jnp.zeros((8, LANES), F32).at[0, :N_EXPERTS].set(region_starts.astype(F32))
    region_ends = region_ends.astype(jnp.int32)
    buf, dest, col = _dispatch(tile_start, tile_cnt, tile_ovf, region_ends, h2, eidx, rank, start, before_raw, n_rows)
    y = _experts(block_e, nused, buf, p["expert_w1"], p["expert_b1"].reshape(N_EXPERTS, 1, -1),
                 p["expert_w2"], p["expert_b2"].reshape(N_EXPERTS, 1, -1))
    out = _combine(tile_start, tile_cnt, tile_ovf, region_ends, col, dest, egate, x1, mod3, y, l)
    return out.reshape(b, l, D_MODEL)


def kernel(x, c, ctx, c_ctx, ada_w, ada_b, norm_mix, norm_ffn, w_in, gqa_q_norm, gqa_k_norm, mla_q_a_norm, mla_kv_a_norm, mla_w_qb, mla_w_kvb, mla_q_norm, mla_k_norm, w_o_gqa, w_o_mla, w_out, router_w, router_b, expert_w1, expert_b1, expert_w2, expert_b2):
    assert ada_w.shape[0] == 1, "single-layer problem: the context stream is never updated"
    p = {
        "ada_w": ada_w[0], "ada_b": ada_b[0], "norm_mix": norm_mix[0], "norm_ffn": norm_ffn[0],
        "w_in": w_in[0], "gqa_q_norm": gqa_q_norm[0], "gqa_k_norm": gqa_k_norm[0],
        "mla_q_a_norm": mla_q_a_norm[0], "mla_kv_a_norm": mla_kv_a_norm[0],
        "mla_w_qb": mla_w_qb[0], "mla_w_kvb": mla_w_kvb[0],
        "mla_q_norm": mla_q_norm[0], "mla_k_norm": mla_k_norm[0],
        "w_o_gqa": w_o_gqa[0], "w_o_mla": w_o_mla[0], "w_out": w_out[0],
        "router_w": router_w[0], "router_b": router_b[0],
        "expert_w1": expert_w1[0], "expert_b1": expert_b1[0],
        "expert_w2": expert_w2[0], "expert_b2": expert_b2[0],
    }
    return _layer(x, c, ctx, c_ctx, p)
```
